```python
import jax
import jax.numpy as jnp
from jax import lax
import numpy as np

D_MODEL = 2048
BATCH = 2
SEQ = 4096
DEPTH = 4
DEC_BATCH = 8
DEC_SEQ = 1
PAST_LEN = 16384
PAGE_SIZE = 128

HEAD_DIM = 128
BRANCH_WIDTH = D_MODEL // 4
POOL_WIDTH = BRANCH_WIDTH
POOL_WINDOWS = (2, 4, 8, 16)
POOL_GDIM = POOL_WIDTH // len(POOL_WINDOWS)
POOL_BUF = max(POOL_WINDOWS) - 1
DIL_PAIRS = ((128, 1), (512, 4), (2048, 16))
N_DIL = len(DIL_PAIRS)
DIL_HEADS = BRANCH_WIDTH // HEAD_DIM
DIL_WIDTH = DIL_HEADS * HEAD_DIM
MEM_LEN = 256
MEM_HEADS = 4
MEM_WIDTH = MEM_HEADS * HEAD_DIM
MIX_WIDTH = POOL_WIDTH + DIL_WIDTH + MEM_WIDTH
IN_COLS = 2 * POOL_WIDTH + 3 * N_DIL * DIL_WIDTH + DIL_WIDTH + 2 * MEM_WIDTH
BAND_BLOCK = 128
EPS = 1e-6
ATTN_SCALE = HEAD_DIM ** -0.5

kernel_name = 'hybrid_pool_dilated_memory_step'


def rms_norm(x, g):
    xf = x.astype(jnp.float32)
    y = xf * lax.rsqrt(jnp.mean(xf * xf, axis=-1, keepdims=True) + EPS)
    return (y * g.astype(jnp.float32)).astype(x.dtype)


def pool_mix(u_ext, pos0, pool_w, pool_scale):
    B, E, C = u_ext.shape
    L = E - POOL_BUF
    uf = u_ext.astype(jnp.float32)
    cs = jnp.concatenate([jnp.zeros((B, 1, C), jnp.float32), jnp.cumsum(uf, axis=1)], axis=1)
    end = cs[:, POOL_BUF + 1:]
    u_new = uf[:, POOL_BUF:]
    pos = pos0 + jnp.arange(L)
    outs = []
    for gi, w in enumerate(POOL_WINDOWS):
        sl = slice(gi * POOL_GDIM, (gi + 1) * POOL_GDIM)
        start = cs[:, POOL_BUF + 1 - w: POOL_BUF + 1 - w + L, sl]
        cnt = jnp.minimum(w, pos + 1).astype(jnp.float32)[None, :, None]
        d = (end[..., sl] - start) / cnt - u_new[..., sl]
        outs.append(jnp.einsum('blc,cd->bld', d, pool_w[gi].astype(jnp.float32)))
    y = jnp.concatenate(outs, axis=-1) * pool_scale.astype(jnp.float32)
    return y.astype(u_ext.dtype)


def band_attention(q, k, v, back):
    N, T, H, Dh = q.shape
    nb = -(-T // BAND_BLOCK)
    Tp = nb * BAND_BLOCK
    qb = jnp.pad(q, ((0, 0), (0, Tp - T), (0, 0), (0, 0))).reshape(N, nb, BAND_BLOCK, H, Dh)
    kpad = ((0, 0), (BAND_BLOCK, Tp - T), (0, 0), (0, 0))
    kp = jnp.pad(k, kpad).reshape(N, nb + 1, BAND_BLOCK, H, Dh)
    vp = jnp.pad(v, kpad).reshape(N, nb + 1, BAND_BLOCK, H, Dh)
    kb = jnp.concatenate([kp[:, :-1], kp[:, 1:]], axis=2)
    vb = jnp.concatenate([vp[:, :-1], vp[:, 1:]], axis=2)
    s = jnp.einsum('nbqhd,nbkhd->nbhqk', qb, kb, preferred_element_type=jnp.float32) * ATTN_SCALE
    qi = BAND_BLOCK + jnp.arange(BAND_BLOCK)[:, None]
    kj = jnp.arange(2 * BAND_BLOCK)[None, :]
    dist = qi - kj
    kabs = (jnp.arange(nb)[:, None, None] - 1) * BAND_BLOCK + kj[None]
    valid = (dist >= 0) & (dist <= back) & (kabs >= 0)
    s = jnp.where(valid[None, :, None], s, -jnp.inf)
    lse = jax.nn.logsumexp(s, axis=-1)
    p = jnp.exp(s - lse[..., None])
    o = jnp.einsum('nbhqk,nbkhd->nbqhd', p.astype(vb.dtype), vb)
    o = o.reshape(N, Tp, H, Dh)[:, :T]
    lse = lse.transpose(0, 1, 3, 2).reshape(N, Tp, H)[:, :T]
    return o, lse


def dilated_prompt(q, k, v, window, dil):
    B, L, H, Dh = q.shape
    T = L // dil
    def to_sub(a):
        return a.reshape(B, T, dil, H, Dh).transpose(0, 2, 1, 3, 4).reshape(B * dil, T, H, Dh)
    o, lse = band_attention(to_sub(q), to_sub(k), to_sub(v), window // dil)
    o = o.reshape(B, dil, T, H, Dh).transpose(0, 2, 1, 3, 4).reshape(B, L, H, Dh)
    lse = lse.reshape(B, dil, T, H).transpose(0, 2, 1, 3).reshape(B, L, H)
    return o, lse


def dilated_sample(q, kv_ext, window, dil, pos0):
    S = q.shape[1]
    back = window // dil
    idx = window + jnp.arange(S)[:, None] - dil * jnp.arange(back + 1)[None, :]
    kg = kv_ext[:, :, 0][:, idx]
    vg = kv_ext[:, :, 1][:, idx]
    s = jnp.einsum('bshd,bskhd->bhsk', q, kg, preferred_element_type=jnp.float32) * ATTN_SCALE
    valid = (pos0 - window + idx) >= 0
    s = jnp.where(valid[None, None], s, -jnp.inf)
    lse = jax.nn.logsumexp(s, axis=-1)
    p = jnp.exp(s - lse[..., None])
    o = jnp.einsum('bhsk,bskhd->bshd', p.astype(vg.dtype), vg)
    return o, lse.transpose(0, 2, 1)


def combine_by_denominator(outs, lses):
    wgt = jax.nn.softmax(jnp.stack(lses, axis=0).astype(jnp.float32), axis=0)
    return jnp.einsum('gblh,gblhd->blhd', wgt, jnp.stack(outs, axis=0).astype(jnp.float32))


def memory_kv(mem, mem_norm_g, w_mem_kv, mem_k_norm):
    B, M, _ = mem.shape
    kv = jnp.einsum('bmd,dc->bmc', rms_norm(mem, mem_norm_g), w_mem_kv).reshape(B, M, 2, MEM_HEADS, HEAD_DIM)
    k = rms_norm(kv[:, :, 0], mem_k_norm)
    return jnp.stack([k, kv[:, :, 1]], axis=2)


def mem_attention(q, kv):
    s = jnp.einsum('blhd,bmhd->bhlm', q, kv[:, :, 0], preferred_element_type=jnp.float32) * ATTN_SCALE
    p = jax.nn.softmax(s, axis=-1)
    return jnp.einsum('bhlm,bmhd->blhd', p.astype(q.dtype), kv[:, :, 1])


def mix_layer(x, pool_prev, dil_prev, mem_kv, pos0, norm_g, w_in, pool_w, pool_scale,
              dil_q_norm, dil_k_norm, mem_q_norm, w_out):
    B, L, _ = x.shape
    z = jnp.einsum('bld,dc->blc', rms_norm(x, norm_g), w_in)
    c0 = 2 * POOL_WIDTH
    c1 = c0 + 3 * N_DIL * DIL_WIDTH
    c2 = c1 + DIL_WIDTH
    c3 = c2 + MEM_WIDTH
    u = z[..., :POOL_WIDTH]
    gate_pool = z[..., POOL_WIDTH:c0]
    qkv = z[..., c0:c1].reshape(B, L, N_DIL, 3, DIL_HEADS, HEAD_DIM)
    gate_dil = z[..., c1:c2]
    q_mem = z[..., c2:c3].reshape(B, L, MEM_HEADS, HEAD_DIM)
    gate_mem = z[..., c3:]

    u_ext = jnp.concatenate([pool_prev.astype(u.dtype), u], axis=1)
    y_pool = pool_mix(u_ext, pos0, pool_w, pool_scale)
    new_pool = u_ext[:, -POOL_BUF:]

    outs, lses, new_dil = [], [], []
    for gi, (win, dil) in enumerate(DIL_PAIRS):
        q = rms_norm(qkv[:, :, gi, 0], dil_q_norm[gi])
        k = rms_norm(qkv[:, :, gi, 1], dil_k_norm[gi])
        v = qkv[:, :, gi, 2]
        kv = jnp.stack([k, v], axis=2)
        if dil_prev is None:
            o, lse = dilated_prompt(q, k, v, win, dil)
            kv_ext = kv if L >= win else jnp.pad(kv, ((0, 0), (win - L, 0), (0, 0), (0, 0), (0, 0)))
        else:
            kv_ext = jnp.concatenate([dil_prev[gi].astype(kv.dtype), kv], axis=1)
            o, lse = dilated_sample(q, kv_ext, win, dil, pos0)
        outs.append(o)
        lses.append(lse)
        new_dil.append(kv_ext[:, -win:])
    y_dil = combine_by_denominator(outs, lses).reshape(B, L, DIL_WIDTH).astype(x.dtype)

    q_mem = rms_norm(q_mem, mem_q_norm)
    y_mem = mem_attention(q_mem, mem_kv.astype(q_mem.dtype)).reshape(B, L, MEM_WIDTH)

    y = jnp.concatenate([jax.nn.silu(gate_pool) * y_pool,
                         jax.nn.silu(gate_dil) * y_dil,
                         jax.nn.silu(gate_mem) * y_mem], axis=-1)
    x = x + jnp.einsum('blc,cd->bld', y, w_out).astype(x.dtype)
    return x, new_pool, new_dil


def setup_inputs(seed: int = 0) -> dict:
    key = jax.random.key(seed)
    ks = jax.random.split(key, 24)
    f32 = jnp.float32

    def nrm(k, shape, scale=1.0):
        return jax.random.normal(k, shape, f32) * scale

    def gain(k, shape):
        return 1.0 + 0.02 * jax.random.normal(k, shape, f32)

    inp = {}
    inp['x_prompt'] = nrm(ks[0], (BATCH, SEQ, D_MODEL))
    inp['x_sample'] = nrm(ks[1], (DEC_BATCH, DEC_SEQ, D_MODEL))
    inp['state_pool'] = nrm(ks[2], (DEPTH, DEC_BATCH, POOL_BUF, POOL_WIDTH))
    inp['cache_dil_w128'] = nrm(ks[3], (DEPTH, DEC_BATCH, DIL_PAIRS[0][0], 2, DIL_HEADS, HEAD_DIM))
    inp['cache_dil_w512'] = nrm(ks[4], (DEPTH, DEC_BATCH, DIL_PAIRS[1][0], 2, DIL_HEADS, HEAD_DIM))
    inp['cache_dil_w2048'] = nrm(ks[5], (DEPTH, DEC_BATCH, DIL_PAIRS[2][0], 2, DIL_HEADS, HEAD_DIM))
    inp['cache_mem_kv'] = nrm(ks[6], (DEPTH, DEC_BATCH, MEM_LEN, 2, MEM_HEADS, HEAD_DIM))
    inp['mem_prompt'] = nrm(ks[7], (BATCH, MEM_LEN, D_MODEL))
    inp['norm_g'] = gain(ks[8], (DEPTH, D_MODEL))
    inp['w_in'] = nrm(ks[9], (DEPTH, D_MODEL, IN_COLS), D_MODEL ** -0.5)
    inp['pool_w'] = nrm(ks[10], (DEPTH, len(POOL_WINDOWS), POOL_GDIM, POOL_GDIM), POOL_GDIM ** -0.5)
    inp['pool_scale'] = gain(ks[11], (DEPTH, POOL_WIDTH))
    inp['dil_q_norm'] = gain(ks[12], (DEPTH, N_DIL, HEAD_DIM))
    inp['dil_k_norm'] = gain(ks[13], (DEPTH, N_DIL, HEAD_DIM))
    inp['mem_norm_g'] = gain(ks[14], (DEPTH, D_MODEL))
    inp['w_mem_kv'] = nrm(ks[15], (DEPTH, D_MODEL, 2 * MEM_WIDTH), D_MODEL ** -0.5)
    inp['mem_q_norm'] = gain(ks[16], (DEPTH, HEAD_DIM))
    inp['mem_k_norm'] = gain(ks[17], (DEPTH, HEAD_DIM))
    inp['w_out'] = nrm(ks[18], (DEPTH, MIX_WIDTH, D_MODEL), MIX_WIDTH ** -0.5)
    return inp


def reference(x_prompt, x_sample, state_pool, cache_dil_w128, cache_dil_w512, cache_dil_w2048,
              cache_mem_kv, mem_prompt, norm_g, w_in, pool_w, pool_scale, dil_q_norm, dil_k_norm,
              mem_norm_g, w_mem_kv, mem_q_norm, mem_k_norm, w_out):
    dil_caches = (cache_dil_w128, cache_dil_w512, cache_dil_w2048)
    y_prompt, y_sample = x_prompt, x_sample
    pool_p, pool_s, mem_p = [], [], []
    dil_p = [[] for _ in DIL_PAIRS]
    dil_s = [[] for _ in DIL_PAIRS]
    pool_zero = jnp.zeros((x_prompt.shape[0], POOL_BUF, POOL_WIDTH), x_prompt.dtype)
    for l in range(DEPTH):
        lw = (norm_g[l], w_in[l], pool_w[l], pool_scale[l], dil_q_norm[l], dil_k_norm[l],
              mem_q_norm[l], w_out[l])
        mem_kv_p = memory_kv(mem_prompt, mem_norm_g[l], w_mem_kv[l], mem_k_norm[l])
        y_prompt, np_l, nd_l = mix_layer(y_prompt, pool_zero, None, mem_kv_p, 0, *lw)
        y_sample, ns_l, nds_l = mix_layer(y_sample, state_pool[l], [c[l] for c in dil_caches],
                                          cache_mem_kv[l], PAST_LEN, *lw)
        pool_p.append(np_l)
        pool_s.append(ns_l)
        mem_p.append(mem_kv_p)
        for gi in range(N_DIL):
            dil_p[gi].append(nd_l[gi])
            dil_s[gi].append(nds_l[gi])
    state_pool_prompt = jnp.stack(pool_p)
    cache_dil_w128_prompt = jnp.stack(dil_p[0])
    cache_dil_w512_prompt = jnp.stack(dil_p[1])
    cache_dil_w2048_prompt = jnp.stack(dil_p[2])
    cache_mem_kv_prompt = jnp.stack(mem_p)
    state_pool_sample = jnp.stack(pool_s)
    cache_dil_w128_sample = jnp.stack(dil_s[0])
    cache_dil_w512_sample = jnp.stack(dil_s[1])
    cache_dil_w2048_sample = jnp.stack(dil_s[2])
    return (y_prompt, y_sample, state_pool_prompt, cache_dil_w128_prompt, cache_dil_w512_prompt,
            cache_dil_w2048_prompt, cache_mem_kv_prompt, state_pool_sample, cache_dil_w128_sample,
            cache_dil_w512_sample, cache_dil_w2048_sample)
```

```python
import functools

import jax
import jax.numpy as jnp
from jax import lax
from jax.experimental import pallas as pl
from jax.experimental.pallas import tpu as pltpu

F32 = jnp.float32
BF16 = jnp.bfloat16

D_MODEL = 2048
HEAD_DIM = 128
N_HEADS = 4
SEG = N_HEADS * HEAD_DIM
N_SEG = 14
IN_COLS = N_SEG * SEG
POOL_WINDOWS = (2, 4, 8, 16)
POOL_BUF = 15
POOL_HALO = 16
DIL_PAIRS = ((128, 1), (512, 4), (2048, 16))
BAND = 128
MEM_LEN = 256
MIX_WIDTH = 3 * SEG
EPS = 1e-6
ATTN_SCALE = HEAD_DIM ** -0.5
PAST_LEN = 16384

SEG_U, SEG_GATE_POOL, SEG_QKV0, SEG_GATE_DIL, SEG_QMEM, SEG_GATE_MEM = 0, 1, 2, 11, 12, 13
NORM_SEGS = (2, 3, 5, 6, 8, 9, 12)
GATE_SEGS = (1, 11, 13)

VMEM_LIMIT_BYTES = 56 * 1024 * 1024

PROJ_ROWS = 1024
ATTN_ROWS = 2048
MIX_ROWS = 256
SAMPLE_ROWS = 16


def _any_of(j, ids):
    hit = j == ids[0]
    for c in ids[1:]:
        hit = jnp.logical_or(hit, j == c)
    return hit


def _proj_kernel(x_ref, g_ref, w_ref, eg_ref, o_ref, h_ref, *, norm_segs, gate_segs):
    j = pl.program_id(1)

    @pl.when(j == 0)
    def _():
        x = x_ref[...]
        ms = jnp.mean(x * x, axis=-1, keepdims=True)
        h_ref[...] = (x * lax.rsqrt(ms + EPS) * g_ref[...]).astype(BF16)

    acc = jnp.dot(h_ref[...], w_ref[...], preferred_element_type=F32)
    is_norm = _any_of(j, norm_segs)
    is_gate = _any_of(j, gate_segs) if gate_segs else None

    @pl.when(is_norm)
    def _():
        for h in range(N_HEADS):
            sl = slice(h * HEAD_DIM, (h + 1) * HEAD_DIM)
            a = acc[:, sl]
            ms = jnp.mean(a * a, axis=-1, keepdims=True)
            o_ref[:, sl] = a * lax.rsqrt(ms + EPS) * eg_ref[:, sl]

    if gate_segs:
        @pl.when(is_gate)
        def _():
            o_ref[...] = acc * jax.nn.sigmoid(acc)

        plain = jnp.logical_not(jnp.logical_or(is_norm, is_gate))
    else:
        plain = jnp.logical_not(is_norm)

    @pl.when(plain)
    def _():
        o_ref[...] = acc


def _project(x, gain, w, ep_gain, *, rows, norm_segs, gate_segs, name):
    m, k = x.shape
    n_seg = w.shape[1] // SEG
    return pl.pallas_call(
        functools.partial(_proj_kernel, norm_segs=norm_segs, gate_segs=gate_segs),
        out_shape=jax.ShapeDtypeStruct((m, n_seg * SEG), F32),
        grid=(m // rows, n_seg),
        in_specs=[
            pl.BlockSpec((rows, k), lambda i, j: (i, 0)),
            pl.BlockSpec((1, k), lambda i, j: (0, 0)),
            pl.BlockSpec((k, SEG), lambda i, j: (0, j)),
            pl.BlockSpec((None, 1, SEG), lambda i, j: (j, 0, 0)),
        ],
        out_specs=pl.BlockSpec((rows, SEG), lambda i, j: (i, j)),
        scratch_shapes=[pltpu.VMEM((rows, k), BF16)],
        compiler_params=pltpu.CompilerParams(
            dimension_semantics=("arbitrary", "arbitrary"),
            vmem_limit_bytes=VMEM_LIMIT_BYTES),
        name=name,
    )(x, gain, w, ep_gain)


def _dilattn_kernel(*refs, n_tiles):
    ins, outs, scratch = refs[:15], refs[15:22], refs[22:]
    y_ref = outs[0]
    tails = outs[1:]
    og_ref, lse_ref, kx_ref, vx_ref = scratch
    i = pl.program_id(2)

    row = lax.broadcasted_iota(jnp.int32, (BAND, BAND), 0)
    col = lax.broadcasted_iota(jnp.int32, (BAND, BAND), 1)
    mask_cur = col <= row
    mask_prev = col >= row
    nt_dims = (((1,), (1,)), ((), ()))

    for g, (win, d) in enumerate(DIL_PAIRS):
        q_ref, kc_ref, vc_ref, kp_ref, vp_ref = ins[5 * g:5 * g + 5]
        span = BAND * d
        n_units = ATTN_ROWS // BAND

        kx_ref[pl.ds(0, span), :] = kp_ref[...]
        vx_ref[pl.ds(0, span), :] = vp_ref[...]
        kx_ref[pl.ds(span, ATTN_ROWS), :] = kc_ref[...]
        vx_ref[pl.ds(span, ATTN_ROWS), :] = vc_ref[...]

        def unit(t, carry, g=g, d=d, span=span, q_ref=q_ref):
            u = t // d
            c = t % d
            base = u * span + c
            if d == 1:
                q_sl = pl.ds(base, BAND)
                p_sl = pl.ds(base, BAND)
                c_sl = pl.ds(base + span, BAND)
            else:
                q_sl = pl.ds(base, BAND, stride=d)
                p_sl = pl.ds(base, BAND, stride=d)
                c_sl = pl.ds(base + span, BAND, stride=d)
            q = q_ref[q_sl, :].astype(BF16)
            kp = kx_ref[p_sl, :].astype(BF16)
            kc = kx_ref[c_sl, :].astype(BF16)
            vp = vx_ref[p_sl, :].astype(BF16)
            vc = vx_ref[c_sl, :].astype(BF16)
            s_c = lax.dot_general(q, kc, nt_dims, preferred_element_type=F32)
            s_p = lax.dot_general(q, kp, nt_dims, preferred_element_type=F32)
            first = jnp.logical_and(i == 0, u == 0)
            s_c = jnp.where(mask_cur, s_c, -jnp.inf)
            s_p = jnp.where(col >= row + jnp.where(first, BAND, 0), s_p, -jnp.inf)
            m = jnp.max(jnp.maximum(s_c, s_p), axis=-1, keepdims=True)
            p_c = jnp.exp(s_c - m)
            p_p = jnp.exp(s_p - m)
            l = jnp.sum(p_c, axis=-1, keepdims=True) + jnp.sum(p_p, axis=-1, keepdims=True)
            o = (jnp.dot(p_c.astype(BF16), vc, preferred_element_type=F32)
                 + jnp.dot(p_p.astype(BF16), vp, preferred_element_type=F32))
            if d == 1:
                o_sl = pl.ds(g * ATTN_ROWS + base, BAND)
            else:
                o_sl = pl.ds(g * ATTN_ROWS + base, BAND, stride=d)
            og_ref[o_sl, :] = o / l
            lse_ref[o_sl, :] = jnp.broadcast_to(m + jnp.log(l), (BAND, HEAD_DIM))
            return carry

        lax.fori_loop(0, n_units, unit, 0, unroll=2)

        @pl.when(i == n_tiles - 1)
        def _(g=g, win=win, kc_ref=kc_ref, vc_ref=vc_ref):
            tails[2 * g][...] = kc_ref[pl.ds(ATTN_ROWS - win, win), :]
            tails[2 * g + 1][...] = vc_ref[pl.ds(ATTN_ROWS - win, win), :]

    chunk = 256

    def merge(r, carry):
        r0 = pl.multiple_of(r * chunk, chunk)
        sls = [pl.ds(g * ATTN_ROWS + r0, chunk) for g in range(len(DIL_PAIRS))]
        l0, l1, l2 = [lse_ref[sl, :] for sl in sls]
        mx = jnp.maximum(jnp.maximum(l0, l1), l2)
        e0, e1, e2 = jnp.exp(l0 - mx), jnp.exp(l1 - mx), jnp.exp(l2 - mx)
        num = e0 * og_ref[sls[0], :] + e1 * og_ref[sls[1], :] + e2 * og_ref[sls[2], :]
        y_ref[pl.ds(r0, chunk), :] = num / (e0 + e1 + e2)
        return carry

    lax.fori_loop(0, ATTN_ROWS // chunk, merge, 0)


def _dilated_attention(z3):
    bsz, seq, _ = z3.shape
    n_tiles = seq // ATTN_ROWS
    in_specs = []
    for g, (win, d) in enumerate(DIL_PAIRS):
        span = BAND * d
        qb = (SEG_QKV0 + 3 * g) * N_HEADS
        kb, vb = qb + N_HEADS, qb + 2 * N_HEADS
        per_tile = ATTN_ROWS // span

        def prev_map(col0, per_tile=per_tile):
            return lambda b, h, i: (b, jnp.maximum(i * per_tile - 1, 0), col0 + h)

        def cur_map(col0):
            return lambda b, h, i: (b, i, col0 + h)

        in_specs += [
            pl.BlockSpec((None, ATTN_ROWS, HEAD_DIM), cur_map(qb)),
            pl.BlockSpec((None, ATTN_ROWS, HEAD_DIM), cur_map(kb)),
            pl.BlockSpec((None, ATTN_ROWS, HEAD_DIM), cur_map(vb)),
            pl.BlockSpec((None, span, HEAD_DIM), prev_map(kb)),
            pl.BlockSpec((None, span, HEAD_DIM), prev_map(vb)),
        ]
    out_shape = [jax.ShapeDtypeStruct((bsz, seq, SEG), F32)]
    out_specs = [pl.BlockSpec((None, ATTN_ROWS, HEAD_DIM), lambda b, h, i: (b, i, h))]
    for win, _ in DIL_PAIRS:
        for _kv in range(2):
            out_shape.append(jax.ShapeDtypeStruct((bsz, win, SEG), F32))
            out_specs.append(pl.BlockSpec((None, win, HEAD_DIM), lambda b, h, i: (b, 0, h)))
    max_span = BAND * DIL_PAIRS[-1][1]
    return pl.pallas_call(
        functools.partial(_dilattn_kernel, n_tiles=n_tiles),
        out_shape=out_shape,
        grid=(bsz, N_HEADS, n_tiles),
        in_specs=in_specs,
        out_specs=out_specs,
        scratch_shapes=[
            pltpu.VMEM((len(DIL_PAIRS) * ATTN_ROWS, HEAD_DIM), F32),
            pltpu.VMEM((len(DIL_PAIRS) * ATTN_ROWS, HEAD_DIM), F32),
            pltpu.VMEM((max_span + ATTN_ROWS, HEAD_DIM), F32),
            pltpu.VMEM((max_span + ATTN_ROWS, HEAD_DIM), F32),
        ],
        compiler_params=pltpu.CompilerParams(
            dimension_semantics=("arbitrary", "arbitrary", "arbitrary"),
            vmem_limit_bytes=VMEM_LIMIT_BYTES),
        name="dilated_attention",
    )(*([z3] * 15))


def _mix_kernel(u_ref, up_ref, gp_ref, gd_ref, qm_ref, gm_ref, yd_ref, mkv_ref, pw_ref, ps_ref,
                wo_ref, x_ref, o_ref, ue_ref, y_ref):
    i = pl.program_id(1)
    rows = u_ref.shape[0]

    @pl.when(i > 0)
    def _():
        ue_ref[pl.ds(0, POOL_HALO), :] = up_ref[...]

    @pl.when(i == 0)
    def _():
        ue_ref[pl.ds(0, POOL_HALO), :] = jnp.zeros((POOL_HALO, SEG), F32)

    ue_ref[pl.ds(POOL_HALO, rows), :] = u_ref[...]
    pos = i * rows + lax.broadcasted_iota(jnp.int32, (rows, 1), 0)
    for gi, w in enumerate(POOL_WINDOWS):
        sl = slice(gi * HEAD_DIM, (gi + 1) * HEAD_DIM)
        tot = ue_ref[pl.ds(POOL_HALO, rows), sl]
        for back in range(1, w):
            tot = tot + ue_ref[pl.ds(POOL_HALO - back, rows), sl]
        cnt = jnp.minimum(w, pos + 1).astype(F32)
        dlt = tot / cnt - u_ref[:, sl]
        yp = jnp.dot(dlt.astype(BF16), pw_ref[gi], preferred_element_type=F32) * ps_ref[:, sl]
        y_ref[:, sl] = (gp_ref[:, sl] * yp).astype(BF16)

    y_ref[:, SEG:2 * SEG] = (gd_ref[...] * yd_ref[...]).astype(BF16)

    nt_dims = (((1,), (1,)), ((), ()))
    for h in range(N_HEADS):
        sl = slice(h * HEAD_DIM, (h + 1) * HEAD_DIM)
        q = qm_ref[:, sl].astype(BF16)
        k = mkv_ref[:, sl].astype(BF16)
        v = mkv_ref[:, SEG + h * HEAD_DIM:SEG + (h + 1) * HEAD_DIM].astype(BF16)
        s = lax.dot_general(q, k, nt_dims, preferred_element_type=F32)
        m = jnp.max(s, axis=-1, keepdims=True)
        p = jnp.exp(s - m)
        l = jnp.sum(p, axis=-1, keepdims=True)
        o = jnp.dot(p.astype(BF16), v, preferred_element_type=F32) / l
        y_ref[:, 2 * SEG + h * HEAD_DIM:2 * SEG + (h + 1) * HEAD_DIM] = (gm_ref[:, sl] * o).astype(BF16)

    o_ref[...] = x_ref[...] + jnp.dot(y_ref[...], wo_ref[...], preferred_element_type=F32)


def _mix(z3, y_dil, mem_kv, pool_w, pool_scale, w_out, x3):
    bsz, seq, _ = z3.shape
    rows = MIX_ROWS
    halo_per_tile = rows // POOL_HALO

    def seg_spec(seg):
        return pl.BlockSpec((None, rows, SEG), lambda b, i: (b, i, seg))

    return pl.pallas_call(
        _mix_kernel,
        out_shape=jax.ShapeDtypeStruct(x3.shape, F32),
        grid=(bsz, seq // rows),
        in_specs=[
            seg_spec(SEG_U),
            pl.BlockSpec((None, POOL_HALO, SEG),
                         lambda b, i: (b, jnp.maximum(i * halo_per_tile - 1, 0), SEG_U)),
            seg_spec(SEG_GATE_POOL),
            seg_spec(SEG_GATE_DIL),
            seg_spec(SEG_QMEM),
            seg_spec(SEG_GATE_MEM),
            pl.BlockSpec((None, rows, SEG), lambda b, i: (b, i, 0)),
            pl.BlockSpec((None, MEM_LEN, 2 * SEG), lambda b, i: (b, 0, 0)),
            pl.BlockSpec((len(POOL_WINDOWS), HEAD_DIM, HEAD_DIM), lambda b, i: (0, 0, 0)),
            pl.BlockSpec((1, SEG), lambda b, i: (0, 0)),
            pl.BlockSpec((MIX_WIDTH, D_MODEL), lambda b, i: (0, 0)),
            pl.BlockSpec((None, rows, D_MODEL), lambda b, i: (b, i, 0)),
        ],
        out_specs=pl.BlockSpec((None, rows, D_MODEL), lambda b, i: (b, i, 0)),
        scratch_shapes=[
            pltpu.VMEM((POOL_HALO + rows, SEG), F32),
            pltpu.VMEM((rows, MIX_WIDTH), BF16),
        ],
        compiler_params=pltpu.CompilerParams(
            dimension_semantics=("arbitrary", "arbitrary"),
            vmem_limit_bytes=VMEM_LIMIT_BYTES),
        name="mix",
    )(z3, z3, z3, z3, z3, z3, y_dil, mem_kv, pool_w, pool_scale, w_out, x3)


def _sample_attn_kernel(zs_ref, c0_ref, c1_ref, c2_ref, cm_ref, o_ref):
    outs, lses = [], []
    for g, c_ref in enumerate((c0_ref, c1_ref, c2_ref)):
        base = (SEG_QKV0 + 3 * g) * N_HEADS
        q = zs_ref[pl.ds(base, N_HEADS), :]
        k_new = zs_ref[pl.ds(base + N_HEADS, N_HEADS), :]
        v_new = zs_ref[pl.ds(base + 2 * N_HEADS, N_HEADS), :]
        k = c_ref[:, 0]
        v = c_ref[:, 1]
        s = jnp.sum(k * q[None], axis=-1, keepdims=True)
        s_new = jnp.sum(k_new * q, axis=-1, keepdims=True)
        m = jnp.maximum(jnp.max(s, axis=0), s_new)
        p = jnp.exp(s - m[None])
        p_new = jnp.exp(s_new - m)
        l = jnp.sum(p, axis=0) + p_new
        outs.append((jnp.sum(p * v, axis=0) + p_new * v_new) / l)
        lses.append(m + jnp.log(l))
    mx = jnp.maximum(jnp.maximum(lses[0], lses[1]), lses[2])
    es = [jnp.exp(x - mx) for x in lses]
    o_ref[pl.ds(0, N_HEADS), :] = (es[0] * outs[0] + es[1] * outs[1] + es[2] * outs[2]) / (es[0] + es[1] + es[2])

    q = zs_ref[pl.ds(SEG_QMEM * N_HEADS, N_HEADS), :]
    k = cm_ref[:, 0]
    v = cm_ref[:, 1]
    s = jnp.sum(k * q[None], axis=-1, keepdims=True)
    m = jnp.max(s, axis=0)
    p = jnp.exp(s - m[None])
    o_ref[pl.ds(N_HEADS, N_HEADS), :] = jnp.sum(p * v, axis=0) / jnp.sum(p, axis=0)


def _sample_attention(zs3, caches7, cache_mem, layer):
    bsz = cache_mem.shape[1]
    in_specs = [pl.BlockSpec((None, N_SEG * N_HEADS, HEAD_DIM), lambda b: (b, 0, 0))]
    for _ in caches7:
        in_specs.append(pl.BlockSpec((None, None, BAND, None, 2, N_HEADS, HEAD_DIM),
                                     lambda b: (layer, b, 0, 0, 0, 0, 0)))
    in_specs.append(pl.BlockSpec((None, None, MEM_LEN, 2, N_HEADS, HEAD_DIM),
                                 lambda b: (layer, b, 0, 0, 0, 0)))
    return pl.pallas_call(
        _sample_attn_kernel,
        out_shape=jax.ShapeDtypeStruct((bsz, 2 * N_HEADS, HEAD_DIM), F32),
        grid=(bsz,),
        in_specs=in_specs,
        out_specs=pl.BlockSpec((None, 2 * N_HEADS, HEAD_DIM), lambda b: (b, 0, 0)),
        compiler_params=pltpu.CompilerParams(dimension_semantics=("arbitrary",)),
        name="sample_attention",
    )(zs3, *caches7, cache_mem)


def _sample_out_kernel(zs_ref, st_ref, ydm_ref, pw_ref, ps_ref, wo_ref, x_ref, o_ref, ns_ref, y_ref):
    u = zs_ref[:, pl.ds(SEG_U * SEG, SEG)]
    for gi, w in enumerate(POOL_WINDOWS):
        sl = slice(gi * HEAD_DIM, (gi + 1) * HEAD_DIM)
        tot = u[:, sl]
        for back in range(1, w):
            tot = tot + st_ref[POOL_BUF - back, :, sl]
        cnt = float(min(w, PAST_LEN + 1))
        dlt = tot / cnt - u[:, sl]
        yp = jnp.dot(dlt.astype(BF16), pw_ref[gi], preferred_element_type=F32) * ps_ref[:, sl]
        y_ref[:, sl] = (zs_ref[:, pl.ds(SEG_GATE_POOL * SEG + gi * HEAD_DIM, HEAD_DIM)] * yp).astype(BF16)
    y_ref[:, SEG:2 * SEG] = (zs_ref[:, pl.ds(SEG_GATE_DIL * SEG, SEG)] * ydm_ref[:, pl.ds(0, SEG)]).astype(BF16)
    y_ref[:, 2 * SEG:] = (zs_ref[:, pl.ds(SEG_GATE_MEM * SEG, SEG)] * ydm_ref[:, pl.ds(SEG, SEG)]).astype(BF16)
    o_ref[...] = x_ref[...] + jnp.dot(y_ref[...], wo_ref[...], preferred_element_type=F32)
    for r in range(POOL_BUF - 1):
        ns_ref[r] = st_ref[r + 1]
    ns_ref[POOL_BUF - 1] = u


def _sample_out(zs, state_t, ydm, pool_w, pool_scale, w_out, xs):
    rows = zs.shape[0]
    full = lambda shape: pl.BlockSpec(shape, lambda: tuple(0 for _ in shape))
    return pl.pallas_call(
        _sample_out_kernel,
        out_shape=[jax.ShapeDtypeStruct(xs.shape, F32), jax.ShapeDtypeStruct(state_t.shape, F32)],
        in_specs=[full(zs.shape), full(state_t.shape), full(ydm.shape), full(pool_w.shape),
                  full(pool_scale.shape), full(w_out.shape), full(xs.shape)],
        out_specs=[full(xs.shape), full(state_t.shape)],
        scratch_shapes=[pltpu.VMEM((rows, MIX_WIDTH), BF16)],
        compiler_params=pltpu.CompilerParams(vmem_limit_bytes=VMEM_LIMIT_BYTES),
        name="sample_out",
    )(zs, state_t, ydm, pool_w, pool_scale, w_out, xs)


def _cache_shift_kernel(zs_ref, c0_ref, c1_ref, c2_ref, o0_ref, o1_ref, o2_ref, sems, *, depth, bsz):
    caches = ((c0_ref, o0_ref), (c1_ref, o1_ref), (c2_ref, o2_ref))

    def copies(n):
        layer = n // bsz
        b = n % bsz
        out = []
        for g, (c_ref, o_ref) in enumerate(caches):
            win = DIL_PAIRS[g][0]
            kseg = SEG_QKV0 + 3 * g + 1
            out.append(pltpu.make_async_copy(c_ref.at[layer, b, pl.ds(1, win - 1)],
                                             o_ref.at[layer, b, pl.ds(0, win - 1)], sems.at[2 * g]))
            out.append(pltpu.make_async_copy(zs_ref.at[layer, b, pl.ds(kseg, 2)],
                                             o_ref.at[layer, b, win - 1], sems.at[2 * g + 1]))
        return out

    def body(n, carry):
        cps = copies(n)
        for cp in cps:
            cp.start()
        for cp in cps:
            cp.wait()
        return carry

    lax.fori_loop(0, depth * bsz, body, 0)


def _cache_shift(zs_rows, caches):
    depth, bsz = zs_rows.shape[:2]
    any_spec = pl.BlockSpec(memory_space=pl.ANY)
    return pl.pallas_call(
        functools.partial(_cache_shift_kernel, depth=depth, bsz=bsz),
        out_shape=[jax.ShapeDtypeStruct(c.shape, c.dtype) for c in caches],
        in_specs=[any_spec] * 4,
        out_specs=[any_spec] * 3,
        scratch_shapes=[pltpu.SemaphoreType.DMA((6,))],
        name="cache_shift",
    )(zs_rows, *caches)


def _tile_heads(v):
    return jnp.tile(v, N_HEADS)


def kernel(x_prompt, x_sample, state_pool, cache_dil_w128, cache_dil_w512, cache_dil_w2048,
           cache_mem_kv, mem_prompt, norm_g, w_in, pool_w, pool_scale, dil_q_norm, dil_k_norm,
           mem_norm_g, w_mem_kv, mem_q_norm, mem_k_norm, w_out):
    depth = w_in.shape[0]
    bsz, seq, _ = x_prompt.shape
    dbsz = x_sample.shape[0]
    caches = (cache_dil_w128, cache_dil_w512, cache_dil_w2048)

    w_in_b = w_in.astype(BF16)
    w_out_b = w_out.astype(BF16)
    w_mem_b = w_mem_kv.astype(BF16)
    pool_w_b = pool_w.astype(BF16)

    ones = jnp.ones((SEG,), F32)
    caches7 = [c.reshape(depth, dbsz, win // d, d, 2, N_HEADS, HEAD_DIM)
               for c, (win, d) in zip(caches, DIL_PAIRS)]

    xp = x_prompt.reshape(bsz * seq, D_MODEL)
    xs = jnp.pad(x_sample.reshape(dbsz, D_MODEL), ((0, SAMPLE_ROWS - dbsz), (0, 0)))
    mem2 = mem_prompt.reshape(bsz * MEM_LEN, D_MODEL)

    pool_p, mem_p, pool_s, zs_rows = [], [], [], []
    tails = [[] for _ in range(6)]
    for l in range(depth):
        segs = [ones] * N_SEG
        for g in range(len(DIL_PAIRS)):
            segs[SEG_QKV0 + 3 * g] = _tile_heads(dil_q_norm[l, g]) * ATTN_SCALE
            segs[SEG_QKV0 + 3 * g + 1] = _tile_heads(dil_k_norm[l, g])
        segs[SEG_QMEM] = _tile_heads(mem_q_norm[l]) * ATTN_SCALE
        ep_gain = jnp.stack(segs)[:, None, :]
        mem_gain = jnp.stack([_tile_heads(mem_k_norm[l]), ones])[:, None, :]
        gain = norm_g[l][None, :]
        pscale = pool_scale[l][None, :]

        z = _project(xp, gain, w_in_b[l], ep_gain, rows=PROJ_ROWS,
                     norm_segs=NORM_SEGS, gate_segs=GATE_SEGS, name="proj_prompt")
        z3 = z.reshape(bsz, seq, IN_COLS)
        mkv = _project(mem2, mem_norm_g[l][None, :], w_mem_b[l], mem_gain, rows=bsz * MEM_LEN,
                       norm_segs=(0,), gate_segs=(), name="proj_mem")
        mkv3 = mkv.reshape(bsz, MEM_LEN, 2 * SEG)
        attn = _dilated_attention(z3)
        y_dil = attn[0]
        for t in range(6):
            tails[t].append(attn[1 + t])
        xp = _mix(z3, y_dil, mkv3, pool_w_b[l], pscale, w_out_b[l],
                  xp.reshape(bsz, seq, D_MODEL)).reshape(bsz * seq, D_MODEL)
        pool_p.append(z3[:, seq - POOL_BUF:, :SEG])
        mem_p.append(mkv3)

        zs = _project(xs, gain, w_in_b[l], ep_gain, rows=SAMPLE_ROWS,
                      norm_segs=NORM_SEGS, gate_segs=GATE_SEGS, name="proj_sample")
        zs3 = zs.reshape(SAMPLE_ROWS, N_SEG * N_HEADS, HEAD_DIM)
        ydm = _sample_attention(zs3, caches7, cache_mem_kv, l)
        ydm = jnp.pad(ydm.reshape(dbsz, 2 * SEG), ((0, SAMPLE_ROWS - dbsz), (0, 0)))
        state_t = jnp.pad(jnp.transpose(state_pool[l], (1, 0, 2)),
                          ((0, 0), (0, SAMPLE_ROWS - dbsz), (0, 0)))
        xs, new_state_t = _sample_out(zs, state_t, ydm, pool_w_b[l], pscale, w_out_b[l], xs)
        pool_s.append(jnp.transpose(new_state_t[:, :dbsz], (1, 0, 2)))
        zs_rows.append(zs[:dbsz].reshape(dbsz, N_SEG, N_HEADS, HEAD_DIM))

    new_caches = _cache_shift(jnp.stack(zs_rows), caches)

    def kv_cache(t):
        k = jnp.stack(tails[2 * t])
        v = jnp.stack(tails[2 * t + 1])
        win = k.shape[2]
        return jnp.stack([k, v], axis=3).reshape(depth, bsz, win, 2, N_HEADS, HEAD_DIM)

    y_prompt = xp.reshape(bsz, seq, D_MODEL)
    y_sample = xs[:dbsz].reshape(dbsz, 1, D_MODEL)
    cache_mem_prompt = jnp.stack(mem_p).reshape(depth, bsz, MEM_LEN, 2, N_HEADS, HEAD_DIM)
    return (y_prompt, y_sample, jnp.stack(pool_p), kv_cache(0), kv_cache(1), kv_cache(2),
            cache_mem_prompt, jnp.stack(pool_s), new_caches[0], new_caches[1], new_caches[2])
```

```python
import functools

import jax
import jax.numpy as jnp
from jax import lax
from jax.experimental import pallas as pl
from jax.experimental.pallas import tpu as pltpu

F32 = jnp.float32
BF16 = jnp.bfloat16

D_MODEL = 2048
HEAD_DIM = 128
N_HEADS = 4
SEG = N_HEADS * HEAD_DIM
N_SEG = 14
IN_COLS = N_SEG * SEG
POOL_WINDOWS = (2, 4, 8, 16)
POOL_BUF = 15
POOL_HALO = 16
DIL_PAIRS = ((128, 1), (512, 4), (2048, 16))
BAND = 128
MEM_LEN = 256
MIX_WIDTH = 3 * SEG
EPS = 1e-6
ATTN_SCALE = HEAD_DIM ** -0.5
PAST_LEN = 16384

SEG_U, SEG_GATE_POOL, SEG_QKV0, SEG_GATE_DIL, SEG_QMEM, SEG_GATE_MEM = 0, 1, 2, 11, 12, 13
NORM_SEGS = (2, 3, 5, 6, 8, 9, 12)
GATE_SEGS = (1, 11, 13)

VMEM_LIMIT_BYTES = 56 * 1024 * 1024

PROJ_ROWS = 1024
ATTN_ROWS = 2048
MIX_ROWS = 256
SAMPLE_ROWS = 16
SHIFT_ROWS = 1024
SHIFT_CHUNK = 64


def _any_of(j, ids):
    hit = j == ids[0]
    for c in ids[1:]:
        hit = jnp.logical_or(hit, j == c)
    return hit


def _proj_kernel(x_ref, g_ref, w_ref, eg_ref, o_ref, h_ref, *, norm_segs, gate_segs):
    j = pl.program_id(1)

    @pl.when(j == 0)
    def _():
        x = x_ref[...]
        ms = jnp.mean(x * x, axis=-1, keepdims=True)
        h_ref[...] = (x * lax.rsqrt(ms + EPS) * g_ref[...]).astype(BF16)

    acc = jnp.dot(h_ref[...], w_ref[...], preferred_element_type=F32)
    is_norm = _any_of(j, norm_segs)
    is_gate = _any_of(j, gate_segs) if gate_segs else None

    @pl.when(is_norm)
    def _():
        for h in range(N_HEADS):
            sl = slice(h * HEAD_DIM, (h + 1) * HEAD_DIM)
            a = acc[:, sl]
            ms = jnp.mean(a * a, axis=-1, keepdims=True)
            o_ref[:, sl] = a * lax.rsqrt(ms + EPS) * eg_ref[:, sl]

    if gate_segs:
        @pl.when(is_gate)
        def _():
            o_ref[...] = acc * jax.nn.sigmoid(acc)

        plain = jnp.logical_not(jnp.logical_or(is_norm, is_gate))
    else:
        plain = jnp.logical_not(is_norm)

    @pl.when(plain)
    def _():
        o_ref[...] = acc


def _project(x, gain, w, ep_gain, *, rows, norm_segs, gate_segs, name):
    m, k = x.shape
    n_seg = w.shape[1] // SEG
    return pl.pallas_call(
        functools.partial(_proj_kernel, norm_segs=norm_segs, gate_segs=gate_segs),
        out_shape=jax.ShapeDtypeStruct((m, n_seg * SEG), F32),
        grid=(m // rows, n_seg),
        in_specs=[
            pl.BlockSpec((rows, k), lambda i, j: (i, 0)),
            pl.BlockSpec((1, k), lambda i, j: (0, 0)),
            pl.BlockSpec((k, SEG), lambda i, j: (0, j)),
            pl.BlockSpec((None, 1, SEG), lambda i, j: (j, 0, 0)),
        ],
        out_specs=pl.BlockSpec((rows, SEG), lambda i, j: (i, j)),
        scratch_shapes=[pltpu.VMEM((rows, k), BF16)],
        compiler_params=pltpu.CompilerParams(
            dimension_semantics=("arbitrary", "arbitrary"),
            vmem_limit_bytes=VMEM_LIMIT_BYTES),
        name=name,
    )(x, gain, w, ep_gain)


def _dilattn_kernel(*refs, n_tiles):
    ins, outs, scratch = refs[:15], refs[15:22], refs[22:]
    y_ref = outs[0]
    tails = outs[1:]
    og_ref, lse_ref = scratch
    i = pl.program_id(2)

    row = lax.broadcasted_iota(jnp.int32, (BAND, 2 * BAND), 0)
    col = lax.broadcasted_iota(jnp.int32, (BAND, 2 * BAND), 1)
    not_future = col <= row + BAND
    band = jnp.logical_and(col >= row, not_future)
    first_lo = jnp.maximum(row, (i == 0).astype(jnp.int32) * BAND)
    band_first = jnp.logical_and(col >= first_lo, not_future)
    ones = jnp.ones((2 * BAND, HEAD_DIM), BF16)
    nt_dims = (((1,), (1,)), ((), ()))

    for g, (win, d) in enumerate(DIL_PAIRS):
        q_ref, kc_ref, vc_ref, kp_ref, vp_ref = ins[5 * g:5 * g + 5]
        span = BAND * d

        def rows_of(start, size, d=d):
            return pl.ds(start, size) if d == 1 else pl.ds(start, size, stride=d)

        for t in range(ATTN_ROWS // BAND):
            u, c = divmod(t, d)
            base = u * span + c
            q = q_ref[rows_of(base, BAND), :].astype(BF16)
            if u == 0:
                k = jnp.concatenate([kp_ref[rows_of(c, BAND), :], kc_ref[rows_of(c, BAND), :]], axis=0)
                v = jnp.concatenate([vp_ref[rows_of(c, BAND), :], vc_ref[rows_of(c, BAND), :]], axis=0)
            else:
                k = kc_ref[rows_of(base - span, 2 * BAND), :]
                v = vc_ref[rows_of(base - span, 2 * BAND), :]
            s = lax.dot_general(q, k.astype(BF16), nt_dims, preferred_element_type=F32)
            s = jnp.where(band_first if u == 0 else band, s, -jnp.inf)
            m = jnp.max(jnp.maximum(s[:, :BAND], s[:, BAND:]), axis=-1, keepdims=True)
            p = jnp.exp(s - m).astype(BF16)
            ov = jnp.dot(p, jnp.concatenate([v.astype(BF16), ones], axis=1), preferred_element_type=F32)
            l = ov[:, HEAD_DIM:]
            o_rows = rows_of(g * ATTN_ROWS + base, BAND)
            og_ref[o_rows, :] = ov[:, :HEAD_DIM] / l
            lse_ref[o_rows, :] = m + jnp.log(l)

        @pl.when(i == n_tiles - 1)
        def _(g=g, win=win, kc_ref=kc_ref, vc_ref=vc_ref):
            tails[2 * g][...] = kc_ref[pl.ds(ATTN_ROWS - win, win), :]
            tails[2 * g + 1][...] = vc_ref[pl.ds(ATTN_ROWS - win, win), :]

    chunk = 256

    def merge(r, carry):
        r0 = pl.multiple_of(r * chunk, chunk)
        sls = [pl.ds(g * ATTN_ROWS + r0, chunk) for g in range(len(DIL_PAIRS))]
        l0, l1, l2 = [lse_ref[sl, :] for sl in sls]
        mx = jnp.maximum(jnp.maximum(l0, l1), l2)
        e0, e1, e2 = jnp.exp(l0 - mx), jnp.exp(l1 - mx), jnp.exp(l2 - mx)
        num = e0 * og_ref[sls[0], :] + e1 * og_ref[sls[1], :] + e2 * og_ref[sls[2], :]
        y_ref[pl.ds(r0, chunk), :] = num / (e0 + e1 + e2)
        return carry

    lax.fori_loop(0, ATTN_ROWS // chunk, merge, 0)


def _dilated_attention(z3):
    bsz, seq, _ = z3.shape
    n_tiles = seq // ATTN_ROWS
    in_specs = []
    for g, (win, d) in enumerate(DIL_PAIRS):
        span = BAND * d
        qb = (SEG_QKV0 + 3 * g) * N_HEADS
        kb, vb = qb + N_HEADS, qb + 2 * N_HEADS
        per_tile = ATTN_ROWS // span

        def prev_map(col0, per_tile=per_tile):
            return lambda b, h, i: (b, jnp.maximum(i * per_tile - 1, 0), col0 + h)

        def cur_map(col0):
            return lambda b, h, i: (b, i, col0 + h)

        in_specs += [
            pl.BlockSpec((None, ATTN_ROWS, HEAD_DIM), cur_map(qb)),
            pl.BlockSpec((None, ATTN_ROWS, HEAD_DIM), cur_map(kb)),
            pl.BlockSpec((None, ATTN_ROWS, HEAD_DIM), cur_map(vb)),
            pl.BlockSpec((None, span, HEAD_DIM), prev_map(kb)),
            pl.BlockSpec((None, span, HEAD_DIM), prev_map(vb)),
        ]
    out_shape = [jax.ShapeDtypeStruct((bsz, seq, SEG), F32)]
    out_specs = [pl.BlockSpec((None, ATTN_ROWS, HEAD_DIM), lambda b, h, i: (b, i, h))]
    for win, _ in DIL_PAIRS:
        for _kv in range(2):
            out_shape.append(jax.ShapeDtypeStruct((bsz, win, SEG), F32))
            out_specs.append(pl.BlockSpec((None, win, HEAD_DIM), lambda b, h, i: (b, 0, h)))
    return pl.pallas_call(
        functools.partial(_dilattn_kernel, n_tiles=n_tiles),
        out_shape=out_shape,
        grid=(bsz, N_HEADS, n_tiles),
        in_specs=in_specs,
        out_specs=out_specs,
        scratch_shapes=[
            pltpu.VMEM((len(DIL_PAIRS) * ATTN_ROWS, HEAD_DIM), F32),
            pltpu.VMEM((len(DIL_PAIRS) * ATTN_ROWS, HEAD_DIM), F32),
        ],
        compiler_params=pltpu.CompilerParams(
            dimension_semantics=("arbitrary", "arbitrary", "arbitrary"),
            vmem_limit_bytes=VMEM_LIMIT_BYTES),
        name="dilated_attention",
    )(*([z3] * 15))


def _mix_kernel(u_ref, up_ref, gp_ref, gd_ref, qm_ref, gm_ref, yd_ref, mkv_ref, pw_ref, ps_ref,
                wo_ref, x_ref, o_ref, ue_ref, y_ref):
    i = pl.program_id(1)
    rows = u_ref.shape[0]

    @pl.when(i > 0)
    def _():
        ue_ref[pl.ds(0, POOL_HALO), :] = up_ref[...]

    @pl.when(i == 0)
    def _():
        ue_ref[pl.ds(0, POOL_HALO), :] = jnp.zeros((POOL_HALO, SEG), F32)

    ue_ref[pl.ds(POOL_HALO, rows), :] = u_ref[...]
    pos = i * rows + lax.broadcasted_iota(jnp.int32, (rows, 1), 0)
    for gi, w in enumerate(POOL_WINDOWS):
        sl = slice(gi * HEAD_DIM, (gi + 1) * HEAD_DIM)
        tot = ue_ref[pl.ds(POOL_HALO, rows), sl]
        for back in range(1, w):
            tot = tot + ue_ref[pl.ds(POOL_HALO - back, rows), sl]
        cnt = jnp.minimum(w, pos + 1).astype(F32)
        dlt = tot / cnt - u_ref[:, sl]
        yp = jnp.dot(dlt.astype(BF16), pw_ref[gi], preferred_element_type=F32) * ps_ref[:, sl]
        y_ref[:, sl] = (gp_ref[:, sl] * yp).astype(BF16)

    y_ref[:, SEG:2 * SEG] = (gd_ref[...] * yd_ref[...]).astype(BF16)

    nt_dims = (((1,), (1,)), ((), ()))
    for h in range(N_HEADS):
        sl = slice(h * HEAD_DIM, (h + 1) * HEAD_DIM)
        q = qm_ref[:, sl].astype(BF16)
        k = mkv_ref[:, sl].astype(BF16)
        v = mkv_ref[:, SEG + h * HEAD_DIM:SEG + (h + 1) * HEAD_DIM].astype(BF16)
        s = lax.dot_general(q, k, nt_dims, preferred_element_type=F32)
        m = jnp.max(s, axis=-1, keepdims=True)
        p = jnp.exp(s - m)
        l = jnp.sum(p, axis=-1, keepdims=True)
        o = jnp.dot(p.astype(BF16), v, preferred_element_type=F32) / l
        y_ref[:, 2 * SEG + h * HEAD_DIM:2 * SEG + (h + 1) * HEAD_DIM] = (gm_ref[:, sl] * o).astype(BF16)

    o_ref[...] = x_ref[...] + jnp.dot(y_ref[...], wo_ref[...], preferred_element_type=F32)


def _mix(z3, y_dil, mem_kv, pool_w, pool_scale, w_out, x3):
    bsz, seq, _ = z3.shape
    rows = MIX_ROWS
    halo_per_tile = rows // POOL_HALO

    def seg_spec(seg):
        return pl.BlockSpec((None, rows, SEG), lambda b, i: (b, i, seg))

    return pl.pallas_call(
        _mix_kernel,
        out_shape=jax.ShapeDtypeStruct(x3.shape, F32),
        grid=(bsz, seq // rows),
        in_specs=[
            seg_spec(SEG_U),
            pl.BlockSpec((None, POOL_HALO, SEG),
                         lambda b, i: (b, jnp.maximum(i * halo_per_tile - 1, 0), SEG_U)),
            seg_spec(SEG_GATE_POOL),
            seg_spec(SEG_GATE_DIL),
            seg_spec(SEG_QMEM),
            seg_spec(SEG_GATE_MEM),
            pl.BlockSpec((None, rows, SEG), lambda b, i: (b, i, 0)),
            pl.BlockSpec((None, MEM_LEN, 2 * SEG), lambda b, i: (b, 0, 0)),
            pl.BlockSpec((len(POOL_WINDOWS), HEAD_DIM, HEAD_DIM), lambda b, i: (0, 0, 0)),
            pl.BlockSpec((1, SEG), lambda b, i: (0, 0)),
            pl.BlockSpec((MIX_WIDTH, D_MODEL), lambda b, i: (0, 0)),
            pl.BlockSpec((None, rows, D_MODEL), lambda b, i: (b, i, 0)),
        ],
        out_specs=pl.BlockSpec((None, rows, D_MODEL), lambda b, i: (b, i, 0)),
        scratch_shapes=[
            pltpu.VMEM((POOL_HALO + rows, SEG), F32),
            pltpu.VMEM((rows, MIX_WIDTH), BF16),
        ],
        compiler_params=pltpu.CompilerParams(
            dimension_semantics=("arbitrary", "arbitrary"),
            vmem_limit_bytes=VMEM_LIMIT_BYTES),
        name="mix",
    )(z3, z3, z3, z3, z3, z3, y_dil, mem_kv, pool_w, pool_scale, w_out, x3)


def _sample_attn_kernel(zs_ref, c0_ref, c1_ref, c2_ref, cm_ref, o_ref):
    outs, lses = [], []
    for g, c_ref in enumerate((c0_ref, c1_ref, c2_ref)):
        base = (SEG_QKV0 + 3 * g) * N_HEADS
        q = zs_ref[pl.ds(base, N_HEADS), :]
        k_new = zs_ref[pl.ds(base + N_HEADS, N_HEADS), :]
        v_new = zs_ref[pl.ds(base + 2 * N_HEADS, N_HEADS), :]
        k = c_ref[:, 0]
        v = c_ref[:, 1]
        s = jnp.sum(k * q[None], axis=-1, keepdims=True)
        s_new = jnp.sum(k_new * q, axis=-1, keepdims=True)
        m = jnp.maximum(jnp.max(s, axis=0), s_new)
        p = jnp.exp(s - m[None])
        p_new = jnp.exp(s_new - m)
        l = jnp.sum(p, axis=0) + p_new
        outs.append((jnp.sum(p * v, axis=0) + p_new * v_new) / l)
        lses.append(m + jnp.log(l))
    mx = jnp.maximum(jnp.maximum(lses[0], lses[1]), lses[2])
    es = [jnp.exp(x - mx) for x in lses]
    o_ref[pl.ds(0, N_HEADS), :] = (es[0] * outs[0] + es[1] * outs[1] + es[2] * outs[2]) / (es[0] + es[1] + es[2])

    q = zs_ref[pl.ds(SEG_QMEM * N_HEADS, N_HEADS), :]
    k = cm_ref[:, 0]
    v = cm_ref[:, 1]
    s = jnp.sum(k * q[None], axis=-1, keepdims=True)
    m = jnp.max(s, axis=0)
    p = jnp.exp(s - m[None])
    o_ref[pl.ds(N_HEADS, N_HEADS), :] = jnp.sum(p * v, axis=0) / jnp.sum(p, axis=0)


def _sample_attention(zs3, caches7, cache_mem, layer):
    bsz = cache_mem.shape[1]
    in_specs = [pl.BlockSpec((None, N_SEG * N_HEADS, HEAD_DIM), lambda b: (b, 0, 0))]
    for _ in caches7:
        in_specs.append(pl.BlockSpec((None, None, BAND, None, 2, N_HEADS, HEAD_DIM),
                                     lambda b: (layer, b, 0, 0, 0, 0, 0)))
    in_specs.append(pl.BlockSpec((None, None, MEM_LEN, 2, N_HEADS, HEAD_DIM),
                                 lambda b: (layer, b, 0, 0, 0, 0)))
    return pl.pallas_call(
        _sample_attn_kernel,
        out_shape=jax.ShapeDtypeStruct((bsz, 2 * N_HEADS, HEAD_DIM), F32),
        grid=(bsz,),
        in_specs=in_specs,
        out_specs=pl.BlockSpec((None, 2 * N_HEADS, HEAD_DIM), lambda b: (b, 0, 0)),
        compiler_params=pltpu.CompilerParams(dimension_semantics=("arbitrary",)),
        name="sample_attention",
    )(zs3, *caches7, cache_mem)


def _sample_out_kernel(zs_ref, st_ref, ydm_ref, pw_ref, ps_ref, wo_ref, x_ref, o_ref, ns_ref, y_ref):
    u = zs_ref[:, pl.ds(SEG_U * SEG, SEG)]
    for gi, w in enumerate(POOL_WINDOWS):
        sl = slice(gi * HEAD_DIM, (gi + 1) * HEAD_DIM)
        tot = u[:, sl]
        for back in range(1, w):
            tot = tot + st_ref[POOL_BUF - back, :, sl]
        cnt = float(min(w, PAST_LEN + 1))
        dlt = tot / cnt - u[:, sl]
        yp = jnp.dot(dlt.astype(BF16), pw_ref[gi], preferred_element_type=F32) * ps_ref[:, sl]
        y_ref[:, sl] = (zs_ref[:, pl.ds(SEG_GATE_POOL * SEG + gi * HEAD_DIM, HEAD_DIM)] * yp).astype(BF16)
    y_ref[:, SEG:2 * SEG] = (zs_ref[:, pl.ds(SEG_GATE_DIL * SEG, SEG)] * ydm_ref[:, pl.ds(0, SEG)]).astype(BF16)
    y_ref[:, 2 * SEG:] = (zs_ref[:, pl.ds(SEG_GATE_MEM * SEG, SEG)] * ydm_ref[:, pl.ds(SEG, SEG)]).astype(BF16)
    o_ref[...] = x_ref[...] + jnp.dot(y_ref[...], wo_ref[...], preferred_element_type=F32)
    for r in range(POOL_BUF - 1):
        ns_ref[r] = st_ref[r + 1]
    ns_ref[POOL_BUF - 1] = u


def _sample_out(zs, state_t, ydm, pool_w, pool_scale, w_out, xs):
    rows = zs.shape[0]
    full = lambda shape: pl.BlockSpec(shape, lambda: tuple(0 for _ in shape))
    return pl.pallas_call(
        _sample_out_kernel,
        out_shape=[jax.ShapeDtypeStruct(xs.shape, F32), jax.ShapeDtypeStruct(state_t.shape, F32)],
        in_specs=[full(zs.shape), full(state_t.shape), full(ydm.shape), full(pool_w.shape),
                  full(pool_scale.shape), full(w_out.shape), full(xs.shape)],
        out_specs=[full(xs.shape), full(state_t.shape)],
        scratch_shapes=[pltpu.VMEM((rows, MIX_WIDTH), BF16)],
        compiler_params=pltpu.CompilerParams(vmem_limit_bytes=VMEM_LIMIT_BYTES),
        name="sample_out",
    )(zs, state_t, ydm, pool_w, pool_scale, w_out, xs)


def _cache_shift_kernel(c_ref, nxt_ref, new_ref, o_ref, *, n_blocks):
    i = pl.program_id(2)
    rows = c_ref.shape[0]
    chunk = min(rows, SHIFT_CHUNK)

    def copy(c, carry):
        r0 = pl.multiple_of(c * chunk, chunk)
        o_ref[pl.ds(r0, chunk)] = c_ref[pl.ds(r0 + 1, chunk)]
        return carry

    lax.fori_loop(0, rows // chunk - 1, copy, 0)
    o_ref[pl.ds(rows - chunk, chunk - 1)] = c_ref[pl.ds(rows - chunk + 1, chunk - 1)]

    @pl.when(i < n_blocks - 1)
    def _():
        o_ref[rows - 1] = nxt_ref[0]

    @pl.when(i == n_blocks - 1)
    def _():
        o_ref[rows - 1] = new_ref[...]


def _cache_shift(cache, new_rows):
    depth, bsz, win = cache.shape[:3]
    rows = min(win, SHIFT_ROWS)
    n_blocks = win // rows
    row_shape = cache.shape[3:]
    zeros = (0,) * len(row_shape)
    return pl.pallas_call(
        functools.partial(_cache_shift_kernel, n_blocks=n_blocks),
        out_shape=jax.ShapeDtypeStruct(cache.shape, cache.dtype),
        grid=(depth, bsz, n_blocks),
        in_specs=[
            pl.BlockSpec((None, None, rows) + row_shape, lambda l, b, i: (l, b, i) + zeros),
            pl.BlockSpec((None, None, 1) + row_shape,
                         lambda l, b, i: (l, b, jnp.minimum((i + 1) * rows, win - 1)) + zeros),
            pl.BlockSpec((None, None) + row_shape, lambda l, b, i: (l, b) + zeros),
        ],
        out_specs=pl.BlockSpec((None, None, rows) + row_shape, lambda l, b, i: (l, b, i) + zeros),
        compiler_params=pltpu.CompilerParams(
            dimension_semantics=("arbitrary", "arbitrary", "arbitrary"),
            vmem_limit_bytes=VMEM_LIMIT_BYTES),
        name="cache_shift",
    )(cache, cache, new_rows)


def _tile_heads(v):
    return jnp.tile(v, N_HEADS)


def kernel(x_prompt, x_sample, state_pool, cache_dil_w128, cache_dil_w512, cache_dil_w2048,
           cache_mem_kv, mem_prompt, norm_g, w_in, pool_w, pool_scale, dil_q_norm, dil_k_norm,
           mem_norm_g, w_mem_kv, mem_q_norm, mem_k_norm, w_out):
    depth = w_in.shape[0]
    bsz, seq, _ = x_prompt.shape
    dbsz = x_sample.shape[0]
    caches = (cache_dil_w128, cache_dil_w512, cache_dil_w2048)

    w_in_b = w_in.astype(BF16)
    w_out_b = w_out.astype(BF16)
    w_mem_b = w_mem_kv.astype(BF16)
    pool_w_b = pool_w.astype(BF16)

    ones = jnp.ones((SEG,), F32)
    caches7 = [c.reshape(depth, dbsz, win // d, d, 2, N_HEADS, HEAD_DIM)
               for c, (win, d) in zip(caches, DIL_PAIRS)]

    xp = x_prompt.reshape(bsz * seq, D_MODEL)
    xs = jnp.pad(x_sample.reshape(dbsz, D_MODEL), ((0, SAMPLE_ROWS - dbsz), (0, 0)))
    mem2 = mem_prompt.reshape(bsz * MEM_LEN, D_MODEL)

    pool_p, mem_p, pool_s, zs_rows = [], [], [], []
    tails = [[] for _ in range(6)]
    for l in range(depth):
        segs = [ones] * N_SEG
        for g in range(len(DIL_PAIRS)):
            segs[SEG_QKV0 + 3 * g] = _tile_heads(dil_q_norm[l, g]) * ATTN_SCALE
            segs[SEG_QKV0 + 3 * g + 1] = _tile_heads(dil_k_norm[l, g])
        segs[SEG_QMEM] = _tile_heads(mem_q_norm[l]) * ATTN_SCALE
        ep_gain = jnp.stack(segs)[:, None, :]
        mem_gain = jnp.stack([_tile_heads(mem_k_norm[l]), ones])[:, None, :]
        gain = norm_g[l][None, :]
        pscale = pool_scale[l][None, :]

        z = _project(xp, gain, w_in_b[l], ep_gain, rows=PROJ_ROWS,
                     norm_segs=NORM_SEGS, gate_segs=GATE_SEGS, name="proj_prompt")
        z3 = z.reshape(bsz, seq, IN_COLS)
        mkv = _project(mem2, mem_norm_g[l][None, :], w_mem_b[l], mem_gain, rows=bsz * MEM_LEN,
                       norm_segs=(0,), gate_segs=(), name="proj_mem")
        mkv3 = mkv.reshape(bsz, MEM_LEN, 2 * SEG)
        attn = _dilated_attention(z3)
        y_dil = attn[0]
        for t in range(6):
            tails[t].append(attn[1 + t])
        xp = _mix(z3, y_dil, mkv3, pool_w_b[l], pscale, w_out_b[l],
                  xp.reshape(bsz, seq, D_MODEL)).reshape(bsz * seq, D_MODEL)
        pool_p.append(z3[:, seq - POOL_BUF:, :SEG])
        mem_p.append(mkv3)

        zs = _project(xs, gain, w_in_b[l], ep_gain, rows=SAMPLE_ROWS,
                      norm_segs=NORM_SEGS, gate_segs=GATE_SEGS, name="proj_sample")
        zs3 = zs.reshape(SAMPLE_ROWS, N_SEG * N_HEADS, HEAD_DIM)
        ydm = _sample_attention(zs3, caches7, cache_mem_kv, l)
        ydm = jnp.pad(ydm.reshape(dbsz, 2 * SEG), ((0, SAMPLE_ROWS - dbsz), (0, 0)))
        state_t = jnp.pad(jnp.transpose(state_pool[l], (1, 0, 2)),
                          ((0, 0), (0, SAMPLE_ROWS - dbsz), (0, 0)))
        xs, new_state_t = _sample_out(zs, state_t, ydm, pool_w_b[l], pscale, w_out_b[l], xs)
        pool_s.append(jnp.transpose(new_state_t[:, :dbsz], (1, 0, 2)))
        zs_rows.append(zs[:dbsz].reshape(dbsz, N_SEG, N_HEADS, HEAD_DIM))

    zs_all = jnp.stack(zs_rows)
    new_caches = []
    for g, cache in enumerate(caches):
        kseg = SEG_QKV0 + 3 * g + 1
        new_caches.append(_cache_shift(cache, zs_all[:, :, kseg:kseg + 2]))

    def kv_cache(t):
        k = jnp.stack(tails[2 * t])
        v = jnp.stack(tails[2 * t + 1])
        win = k.shape[2]
        return jnp.stack([k, v], axis=3).reshape(depth, bsz, win, 2, N_HEADS, HEAD_DIM)

    y_prompt = xp.reshape(bsz, seq, D_MODEL)
    y_sample = xs[:dbsz].reshape(dbsz, 1, D_MODEL)
    cache_mem_prompt = jnp.stack(mem_p).reshape(depth, bsz, MEM_LEN, 2, N_HEADS, HEAD_DIM)
    return (y_prompt, y_sample, jnp.stack(pool_p), kv_cache(0), kv_cache(1), kv_cache(2),
            cache_mem_prompt, jnp.stack(pool_s), new_caches[0], new_caches[1], new_caches[2])
```

```python
import functools

import jax
import jax.numpy as jnp
from jax import lax
from jax.experimental import pallas as pl
from jax.experimental.pallas import tpu as pltpu

F32 = jnp.float32
BF16 = jnp.bfloat16

D_MODEL = 2048
HEAD_DIM = 128
N_HEADS = 4
SEG = N_HEADS * HEAD_DIM
N_SEG = 14
IN_COLS = N_SEG * SEG
POOL_WINDOWS = (2, 4, 8, 16)
POOL_BUF = 15
POOL_HALO = 16
DIL_PAIRS = ((128, 1), (512, 4), (2048, 16))
BAND = 128
MEM_LEN = 256
MIX_WIDTH = 3 * SEG
EPS = 1e-6
ATTN_SCALE = HEAD_DIM ** -0.5
PAST_LEN = 16384

SEG_U, SEG_GATE_POOL, SEG_QKV0, SEG_GATE_DIL, SEG_QMEM, SEG_GATE_MEM = 0, 1, 2, 11, 12, 13
NORM_SEGS = (2, 3, 5, 6, 8, 9, 12)
GATE_SEGS = (1, 11, 13)

VMEM_LIMIT_BYTES = 56 * 1024 * 1024

PROJ_ROWS = 1024
PROJ_COLS = 2 * SEG
ATTN_ROWS = 2048
MIX_ROWS = 256
SAMPLE_ROWS = 16
SHIFT_ROWS = 1024
SHIFT_CHUNK = 64


MODE_PLAIN, MODE_NORM, MODE_GATE = 0.0, 1.0, 2.0


def _proj_kernel(x_ref, g_ref, w_ref, eg_ref, em_ref, o_ref, h_ref):
    j = pl.program_id(1)

    @pl.when(j == 0)
    def _():
        x = x_ref[...]
        ms = jnp.mean(x * x, axis=-1, keepdims=True)
        h_ref[...] = (x * lax.rsqrt(ms + EPS) * g_ref[...]).astype(BF16)

    acc = jnp.dot(h_ref[...], w_ref[...], preferred_element_type=F32)
    for h in range(w_ref.shape[1] // HEAD_DIM):
        sl = slice(h * HEAD_DIM, (h + 1) * HEAD_DIM)
        a = acc[:, sl]
        mode = em_ref[:, sl]
        f_norm = lax.rsqrt(jnp.mean(a * a, axis=-1, keepdims=True) + EPS) * eg_ref[:, sl]
        f_gate = jax.nn.sigmoid(a)
        factor = jnp.where(mode == MODE_NORM, f_norm, jnp.where(mode == MODE_GATE, f_gate, 1.0))
        o_ref[:, sl] = a * factor


def _project(x, gain, w, layer, ep_gain, ep_mode, *, rows, cols, name):
    m, k = x.shape
    n = w.shape[2]
    return pl.pallas_call(
        _proj_kernel,
        out_shape=jax.ShapeDtypeStruct((m, n), F32),
        grid=(m // rows, n // cols),
        in_specs=[
            pl.BlockSpec((rows, k), lambda i, j: (i, 0)),
            pl.BlockSpec((1, k), lambda i, j: (0, 0)),
            pl.BlockSpec((None, k, cols), lambda i, j: (layer, 0, j)),
            pl.BlockSpec((1, cols), lambda i, j: (0, j)),
            pl.BlockSpec((1, cols), lambda i, j: (0, j)),
        ],
        out_specs=pl.BlockSpec((rows, cols), lambda i, j: (i, j)),
        scratch_shapes=[pltpu.VMEM((rows, k), BF16)],
        compiler_params=pltpu.CompilerParams(
            dimension_semantics=("arbitrary", "arbitrary"),
            vmem_limit_bytes=VMEM_LIMIT_BYTES),
        name=name,
    )(x, gain, w, ep_gain, ep_mode)


def _dilattn_kernel(*refs, n_tiles):
    ins, outs, scratch = refs[:15], refs[15:22], refs[22:]
    y_ref = outs[0]
    tails = outs[1:]
    og_ref, lse_ref = scratch
    i = pl.program_id(2)

    row = lax.broadcasted_iota(jnp.int32, (BAND, 2 * BAND), 0)
    col = lax.broadcasted_iota(jnp.int32, (BAND, 2 * BAND), 1)
    not_future = col <= row + BAND
    band = jnp.logical_and(col >= row, not_future)
    first_lo = jnp.maximum(row, (i == 0).astype(jnp.int32) * BAND)
    band_first = jnp.logical_and(col >= first_lo, not_future)
    ones = jnp.ones((2 * BAND, HEAD_DIM), BF16)
    nt_dims = (((1,), (1,)), ((), ()))

    for g, (win, d) in enumerate(DIL_PAIRS):
        q_ref, kc_ref, vc_ref, kp_ref, vp_ref = ins[5 * g:5 * g + 5]
        span = BAND * d

        def rows_of(start, size, d=d):
            return pl.ds(start, size) if d == 1 else pl.ds(start, size, stride=d)

        for t in range(ATTN_ROWS // BAND):
            u, c = divmod(t, d)
            base = u * span + c
            q = q_ref[rows_of(base, BAND), :].astype(BF16)
            if u == 0:
                k = jnp.concatenate([kp_ref[rows_of(c, BAND), :], kc_ref[rows_of(c, BAND), :]], axis=0)
                v = jnp.concatenate([vp_ref[rows_of(c, BAND), :], vc_ref[rows_of(c, BAND), :]], axis=0)
            else:
                k = kc_ref[rows_of(base - span, 2 * BAND), :]
                v = vc_ref[rows_of(base - span, 2 * BAND), :]
            s = lax.dot_general(q, k.astype(BF16), nt_dims, preferred_element_type=F32)
            s = jnp.where(band_first if u == 0 else band, s, -jnp.inf)
            m = jnp.max(jnp.maximum(s[:, :BAND], s[:, BAND:]), axis=-1, keepdims=True)
            p = jnp.exp(s - m).astype(BF16)
            ov = jnp.dot(p, jnp.concatenate([v.astype(BF16), ones], axis=1), preferred_element_type=F32)
            l = ov[:, HEAD_DIM:]
            o_rows = rows_of(g * ATTN_ROWS + base, BAND)
            og_ref[o_rows, :] = ov[:, :HEAD_DIM] / l
            lse_ref[o_rows, :] = m + jnp.log(l)

        @pl.when(i == n_tiles - 1)
        def _(g=g, win=win, kc_ref=kc_ref, vc_ref=vc_ref):
            tails[2 * g][...] = kc_ref[pl.ds(ATTN_ROWS - win, win), :]
            tails[2 * g + 1][...] = vc_ref[pl.ds(ATTN_ROWS - win, win), :]

    chunk = 256

    def merge(r, carry):
        r0 = pl.multiple_of(r * chunk, chunk)
        sls = [pl.ds(g * ATTN_ROWS + r0, chunk) for g in range(len(DIL_PAIRS))]
        l0, l1, l2 = [lse_ref[sl, :] for sl in sls]
        mx = jnp.maximum(jnp.maximum(l0, l1), l2)
        e0, e1, e2 = jnp.exp(l0 - mx), jnp.exp(l1 - mx), jnp.exp(l2 - mx)
        num = e0 * og_ref[sls[0], :] + e1 * og_ref[sls[1], :] + e2 * og_ref[sls[2], :]
        y_ref[pl.ds(r0, chunk), :] = num / (e0 + e1 + e2)
        return carry

    lax.fori_loop(0, ATTN_ROWS // chunk, merge, 0)


def _dilated_attention(z3):
    bsz, seq, _ = z3.shape
    n_tiles = seq // ATTN_ROWS
    in_specs = []
    for g, (win, d) in enumerate(DIL_PAIRS):
        span = BAND * d
        qb = (SEG_QKV0 + 3 * g) * N_HEADS
        kb, vb = qb + N_HEADS, qb + 2 * N_HEADS
        per_tile = ATTN_ROWS // span

        def prev_map(col0, per_tile=per_tile):
            return lambda b, h, i: (b, jnp.maximum(i * per_tile - 1, 0), col0 + h)

        def cur_map(col0):
            return lambda b, h, i: (b, i, col0 + h)

        in_specs += [
            pl.BlockSpec((None, ATTN_ROWS, HEAD_DIM), cur_map(qb)),
            pl.BlockSpec((None, ATTN_ROWS, HEAD_DIM), cur_map(kb)),
            pl.BlockSpec((None, ATTN_ROWS, HEAD_DIM), cur_map(vb)),
            pl.BlockSpec((None, span, HEAD_DIM), prev_map(kb)),
            pl.BlockSpec((None, span, HEAD_DIM), prev_map(vb)),
        ]
    out_shape = [jax.ShapeDtypeStruct((bsz, seq, SEG), F32)]
    out_specs = [pl.BlockSpec((None, ATTN_ROWS, HEAD_DIM), lambda b, h, i: (b, i, h))]
    for win, _ in DIL_PAIRS:
        for _kv in range(2):
            out_shape.append(jax.ShapeDtypeStruct((bsz, win, SEG), F32))
            out_specs.append(pl.BlockSpec((None, win, HEAD_DIM), lambda b, h, i: (b, 0, h)))
    return pl.pallas_call(
        functools.partial(_dilattn_kernel, n_tiles=n_tiles),
        out_shape=out_shape,
        grid=(bsz, N_HEADS, n_tiles),
        in_specs=in_specs,
        out_specs=out_specs,
        scratch_shapes=[
            pltpu.VMEM((len(DIL_PAIRS) * ATTN_ROWS, HEAD_DIM), F32),
            pltpu.VMEM((len(DIL_PAIRS) * ATTN_ROWS, HEAD_DIM), F32),
        ],
        compiler_params=pltpu.CompilerParams(
            dimension_semantics=("arbitrary", "arbitrary", "arbitrary"),
            vmem_limit_bytes=VMEM_LIMIT_BYTES),
        name="dilated_attention",
    )(*([z3] * 15))


def _mix_kernel(u_ref, up_ref, gp_ref, gd_ref, qm_ref, gm_ref, yd_ref, mkv_ref, pw_ref, ps_ref,
                wo_ref, x_ref, o_ref, ue_ref, y_ref):
    i = pl.program_id(1)
    rows = u_ref.shape[0]

    @pl.when(i > 0)
    def _():
        ue_ref[pl.ds(0, POOL_HALO), :] = up_ref[...]

    @pl.when(i == 0)
    def _():
        ue_ref[pl.ds(0, POOL_HALO), :] = jnp.zeros((POOL_HALO, SEG), F32)

    ue_ref[pl.ds(POOL_HALO, rows), :] = u_ref[...]
    pos = i * rows + lax.broadcasted_iota(jnp.int32, (rows, 1), 0)
    for gi, w in enumerate(POOL_WINDOWS):
        sl = slice(gi * HEAD_DIM, (gi + 1) * HEAD_DIM)
        tot = ue_ref[pl.ds(POOL_HALO, rows), sl]
        for back in range(1, w):
            tot = tot + ue_ref[pl.ds(POOL_HALO - back, rows), sl]
        cnt = jnp.minimum(w, pos + 1).astype(F32)
        dlt = tot / cnt - u_ref[:, sl]
        yp = jnp.dot(dlt.astype(BF16), pw_ref[gi], preferred_element_type=F32) * ps_ref[:, sl]
        y_ref[:, sl] = (gp_ref[:, sl] * yp).astype(BF16)

    y_ref[:, SEG:2 * SEG] = (gd_ref[...] * yd_ref[...]).astype(BF16)

    nt_dims = (((1,), (1,)), ((), ()))
    for h in range(N_HEADS):
        sl = slice(h * HEAD_DIM, (h + 1) * HEAD_DIM)
        q = qm_ref[:, sl].astype(BF16)
        k = mkv_ref[:, sl].astype(BF16)
        v = mkv_ref[:, SEG + h * HEAD_DIM:SEG + (h + 1) * HEAD_DIM].astype(BF16)
        s = lax.dot_general(q, k, nt_dims, preferred_element_type=F32)
        m = jnp.max(s, axis=-1, keepdims=True)
        p = jnp.exp(s - m)
        l = jnp.sum(p, axis=-1, keepdims=True)
        o = jnp.dot(p.astype(BF16), v, preferred_element_type=F32) / l
        y_ref[:, 2 * SEG + h * HEAD_DIM:2 * SEG + (h + 1) * HEAD_DIM] = (gm_ref[:, sl] * o).astype(BF16)

    o_ref[...] = x_ref[...] + jnp.dot(y_ref[...], wo_ref[...], preferred_element_type=F32)


def _mix(z3, y_dil, mem_kv, pool_w, pool_scale, w_out, x3, layer):
    bsz, seq, _ = z3.shape
    rows = MIX_ROWS
    halo_per_tile = rows // POOL_HALO

    def seg_spec(seg):
        return pl.BlockSpec((None, rows, SEG), lambda b, i: (b, i, seg))

    return pl.pallas_call(
        _mix_kernel,
        out_shape=jax.ShapeDtypeStruct(x3.shape, F32),
        grid=(bsz, seq // rows),
        in_specs=[
            seg_spec(SEG_U),
            pl.BlockSpec((None, POOL_HALO, SEG),
                         lambda b, i: (b, jnp.maximum(i * halo_per_tile - 1, 0), SEG_U)),
            seg_spec(SEG_GATE_POOL),
            seg_spec(SEG_GATE_DIL),
            seg_spec(SEG_QMEM),
            seg_spec(SEG_GATE_MEM),
            pl.BlockSpec((None, rows, SEG), lambda b, i: (b, i, 0)),
            pl.BlockSpec((None, MEM_LEN, 2 * SEG), lambda b, i: (b, 0, 0)),
            pl.BlockSpec((None, len(POOL_WINDOWS), HEAD_DIM, HEAD_DIM), lambda b, i: (layer, 0, 0, 0)),
            pl.BlockSpec((1, SEG), lambda b, i: (0, 0)),
            pl.BlockSpec((None, MIX_WIDTH, D_MODEL), lambda b, i: (layer, 0, 0)),
            pl.BlockSpec((None, rows, D_MODEL), lambda b, i: (b, i, 0)),
        ],
        out_specs=pl.BlockSpec((None, rows, D_MODEL), lambda b, i: (b, i, 0)),
        scratch_shapes=[
            pltpu.VMEM((POOL_HALO + rows, SEG), F32),
            pltpu.VMEM((rows, MIX_WIDTH), BF16),
        ],
        compiler_params=pltpu.CompilerParams(
            dimension_semantics=("arbitrary", "arbitrary"),
            vmem_limit_bytes=VMEM_LIMIT_BYTES),
        name="mix",
    )(z3, z3, z3, z3, z3, z3, y_dil, mem_kv, pool_w, pool_scale, w_out, x3)


def _sample_attn_kernel(zs_ref, c0_ref, c1_ref, c2_ref, cm_ref, o_ref):
    outs, lses = [], []
    for g, c_ref in enumerate((c0_ref, c1_ref, c2_ref)):
        base = (SEG_QKV0 + 3 * g) * N_HEADS
        q = zs_ref[pl.ds(base, N_HEADS), :]
        k_new = zs_ref[pl.ds(base + N_HEADS, N_HEADS), :]
        v_new = zs_ref[pl.ds(base + 2 * N_HEADS, N_HEADS), :]
        k = c_ref[:, 0]
        v = c_ref[:, 1]
        s = jnp.sum(k * q[None], axis=-1, keepdims=True)
        s_new = jnp.sum(k_new * q, axis=-1, keepdims=True)
        m = jnp.maximum(jnp.max(s, axis=0), s_new)
        p = jnp.exp(s - m[None])
        p_new = jnp.exp(s_new - m)
        l = jnp.sum(p, axis=0) + p_new
        outs.append((jnp.sum(p * v, axis=0) + p_new * v_new) / l)
        lses.append(m + jnp.log(l))
    mx = jnp.maximum(jnp.maximum(lses[0], lses[1]), lses[2])
    es = [jnp.exp(x - mx) for x in lses]
    o_ref[pl.ds(0, N_HEADS), :] = (es[0] * outs[0] + es[1] * outs[1] + es[2] * outs[2]) / (es[0] + es[1] + es[2])

    q = zs_ref[pl.ds(SEG_QMEM * N_HEADS, N_HEADS), :]
    k = cm_ref[:, 0]
    v = cm_ref[:, 1]
    s = jnp.sum(k * q[None], axis=-1, keepdims=True)
    m = jnp.max(s, axis=0)
    p = jnp.exp(s - m[None])
    o_ref[pl.ds(N_HEADS, N_HEADS), :] = jnp.sum(p * v, axis=0) / jnp.sum(p, axis=0)


def _sample_attention(zs3, caches7, cache_mem, layer):
    bsz = cache_mem.shape[1]
    in_specs = [pl.BlockSpec((None, N_SEG * N_HEADS, HEAD_DIM), lambda b: (b, 0, 0))]
    for _ in caches7:
        in_specs.append(pl.BlockSpec((None, None, BAND, None, 2, N_HEADS, HEAD_DIM),
                                     lambda b: (layer, b, 0, 0, 0, 0, 0)))
    in_specs.append(pl.BlockSpec((None, None, MEM_LEN, 2, N_HEADS, HEAD_DIM),
                                 lambda b: (layer, b, 0, 0, 0, 0)))
    return pl.pallas_call(
        _sample_attn_kernel,
        out_shape=jax.ShapeDtypeStruct((bsz, 2 * N_HEADS, HEAD_DIM), F32),
        grid=(bsz,),
        in_specs=in_specs,
        out_specs=pl.BlockSpec((None, 2 * N_HEADS, HEAD_DIM), lambda b: (b, 0, 0)),
        compiler_params=pltpu.CompilerParams(dimension_semantics=("arbitrary",)),
        name="sample_attention",
    )(zs3, *caches7, cache_mem)


def _sample_out_kernel(zs_ref, st_ref, ydm_ref, pw_ref, ps_ref, wo_ref, x_ref, o_ref, ns_ref, y_ref):
    u = zs_ref[:, pl.ds(SEG_U * SEG, SEG)]
    for gi, w in enumerate(POOL_WINDOWS):
        sl = slice(gi * HEAD_DIM, (gi + 1) * HEAD_DIM)
        tot = u[:, sl]
        for back in range(1, w):
            tot = tot + st_ref[POOL_BUF - back, :, sl]
        cnt = float(min(w, PAST_LEN + 1))
        dlt = tot / cnt - u[:, sl]
        yp = jnp.dot(dlt.astype(BF16), pw_ref[gi], preferred_element_type=F32) * ps_ref[:, sl]
        y_ref[:, sl] = (zs_ref[:, pl.ds(SEG_GATE_POOL * SEG + gi * HEAD_DIM, HEAD_DIM)] * yp).astype(BF16)
    y_ref[:, SEG:2 * SEG] = (zs_ref[:, pl.ds(SEG_GATE_DIL * SEG, SEG)] * ydm_ref[:, pl.ds(0, SEG)]).astype(BF16)
    y_ref[:, 2 * SEG:] = (zs_ref[:, pl.ds(SEG_GATE_MEM * SEG, SEG)] * ydm_ref[:, pl.ds(SEG, SEG)]).astype(BF16)
    o_ref[...] = x_ref[...] + jnp.dot(y_ref[...], wo_ref[...], preferred_element_type=F32)
    for r in range(POOL_BUF - 1):
        ns_ref[r] = st_ref[r + 1]
    ns_ref[POOL_BUF - 1] = u


def _sample_out(zs, state_t, ydm, pool_w, pool_scale, w_out, xs, layer):
    rows = zs.shape[0]
    full = lambda shape: pl.BlockSpec(shape, lambda i: tuple(0 for _ in shape))
    of_layer = lambda a: pl.BlockSpec((None,) + a.shape[1:], lambda i: (layer,) + (0,) * (a.ndim - 1))
    return pl.pallas_call(
        _sample_out_kernel,
        out_shape=[jax.ShapeDtypeStruct(xs.shape, F32), jax.ShapeDtypeStruct(state_t.shape, F32)],
        grid=(1,),
        in_specs=[full(zs.shape), full(state_t.shape), full(ydm.shape), of_layer(pool_w),
                  full(pool_scale.shape), of_layer(w_out), full(xs.shape)],
        out_specs=[full(xs.shape), full(state_t.shape)],
        scratch_shapes=[pltpu.VMEM((rows, MIX_WIDTH), BF16)],
        compiler_params=pltpu.CompilerParams(vmem_limit_bytes=VMEM_LIMIT_BYTES),
        name="sample_out",
    )(zs, state_t, ydm, pool_w, pool_scale, w_out, xs)


def _cache_shift_kernel(c_ref, nxt_ref, new_ref, o_ref, *, n_blocks):
    i = pl.program_id(2)
    rows = c_ref.shape[0]
    chunk = min(rows, SHIFT_CHUNK)

    def copy(c, carry):
        r0 = pl.multiple_of(c * chunk, chunk)
        o_ref[pl.ds(r0, chunk)] = c_ref[pl.ds(r0 + 1, chunk)]
        return carry

    lax.fori_loop(0, rows // chunk - 1, copy, 0)
    o_ref[pl.ds(rows - chunk, chunk - 1)] = c_ref[pl.ds(rows - chunk + 1, chunk - 1)]

    @pl.when(i < n_blocks - 1)
    def _():
        o_ref[rows - 1] = nxt_ref[0]

    @pl.when(i == n_blocks - 1)
    def _():
        o_ref[rows - 1] = new_ref[...]


def _cache_shift(cache, new_rows):
    depth, bsz, win = cache.shape[:3]
    rows = min(win, SHIFT_ROWS)
    n_blocks = win // rows
    row_shape = cache.shape[3:]
    zeros = (0,) * len(row_shape)
    return pl.pallas_call(
        functools.partial(_cache_shift_kernel, n_blocks=n_blocks),
        out_shape=jax.ShapeDtypeStruct(cache.shape, cache.dtype),
        grid=(depth, bsz, n_blocks),
        in_specs=[
            pl.BlockSpec((None, None, rows) + row_shape, lambda l, b, i: (l, b, i) + zeros),
            pl.BlockSpec((None, None, 1) + row_shape,
                         lambda l, b, i: (l, b, jnp.minimum((i + 1) * rows, win - 1)) + zeros),
            pl.BlockSpec((None, None) + row_shape, lambda l, b, i: (l, b) + zeros),
        ],
        out_specs=pl.BlockSpec((None, None, rows) + row_shape, lambda l, b, i: (l, b, i) + zeros),
        compiler_params=pltpu.CompilerParams(
            dimension_semantics=("arbitrary", "arbitrary", "arbitrary"),
            vmem_limit_bytes=VMEM_LIMIT_BYTES),
        name="cache_shift",
    )(cache, cache, new_rows)


def _tile_heads(v):
    return jnp.tile(v, N_HEADS)


def kernel(x_prompt, x_sample, state_pool, cache_dil_w128, cache_dil_w512, cache_dil_w2048,
           cache_mem_kv, mem_prompt, norm_g, w_in, pool_w, pool_scale, dil_q_norm, dil_k_norm,
           mem_norm_g, w_mem_kv, mem_q_norm, mem_k_norm, w_out):
    depth = w_in.shape[0]
    bsz, seq, _ = x_prompt.shape
    dbsz = x_sample.shape[0]
    caches = (cache_dil_w128, cache_dil_w512, cache_dil_w2048)

    w_in_b = w_in.astype(BF16)
    w_out_b = w_out.astype(BF16)
    w_mem_b = w_mem_kv.astype(BF16)
    pool_w_b = pool_w.astype(BF16)

    ones = jnp.ones((SEG,), F32)
    seg_modes = [MODE_PLAIN] * N_SEG
    for s in NORM_SEGS:
        seg_modes[s] = MODE_NORM
    for s in GATE_SEGS:
        seg_modes[s] = MODE_GATE
    ep_mode = jnp.repeat(jnp.array(seg_modes, F32), SEG)[None, :]
    mem_mode = jnp.repeat(jnp.array([MODE_NORM, MODE_PLAIN], F32), SEG)[None, :]
    caches7 =[c.reshape(depth, dbsz, win // d, d, 2, N_HEADS, HEAD_DIM)
               for c, (win, d) in zip(caches, DIL_PAIRS)]

    xp = x_prompt.reshape(bsz * seq, D_MODEL)
    xs = jnp.pad(x_sample.reshape(dbsz, D_MODEL), ((0, SAMPLE_ROWS - dbsz), (0, 0)))
    mem2 = mem_prompt.reshape(bsz * MEM_LEN, D_MODEL)

    pool_p, mem_p, pool_s, zs_rows = [], [], [], []
    tails = [[] for _ in range(6)]
    for l in range(depth):
        segs = [ones] * N_SEG
        for g in range(len(DIL_PAIRS)):
            segs[SEG_QKV0 + 3 * g] = _tile_heads(dil_q_norm[l, g]) * ATTN_SCALE
            segs[SEG_QKV0 + 3 * g + 1] = _tile_heads(dil_k_norm[l, g])
        segs[SEG_QMEM] = _tile_heads(mem_q_norm[l]) * ATTN_SCALE
        ep_gain = jnp.concatenate(segs)[None, :]
        mem_gain = jnp.concatenate([_tile_heads(mem_k_norm[l]), ones])[None, :]
        gain = norm_g[l][None, :]
        pscale = pool_scale[l][None, :]

        z = _project(xp, gain, w_in_b, l, ep_gain, ep_mode, rows=PROJ_ROWS, cols=PROJ_COLS,
                     name="proj_prompt")
        z3 = z.reshape(bsz, seq, IN_COLS)
        mkv = _project(mem2, mem_norm_g[l][None, :], w_mem_b, l, mem_gain, mem_mode,
                       rows=bsz * MEM_LEN, cols=2 * SEG, name="proj_mem")
        mkv3 = mkv.reshape(bsz, MEM_LEN, 2 * SEG)
        attn = _dilated_attention(z3)
        y_dil = attn[0]
        for t in range(6):
            tails[t].append(attn[1 + t])
        xp = _mix(z3, y_dil, mkv3, pool_w_b, pscale, w_out_b,
                  xp.reshape(bsz, seq, D_MODEL), l).reshape(bsz * seq, D_MODEL)
        pool_p.append(z3[:, seq - POOL_BUF:, :SEG])
        mem_p.append(mkv3)

        zs = _project(xs, gain, w_in_b, l, ep_gain, ep_mode, rows=SAMPLE_ROWS, cols=PROJ_COLS,
                      name="proj_sample")
        zs3 = zs.reshape(SAMPLE_ROWS, N_SEG * N_HEADS, HEAD_DIM)
        ydm = _sample_attention(zs3, caches7, cache_mem_kv, l)
        ydm = jnp.pad(ydm.reshape(dbsz, 2 * SEG), ((0, SAMPLE_ROWS - dbsz), (0, 0)))
        state_t = jnp.pad(jnp.transpose(state_pool[l], (1, 0, 2)),
                          ((0, 0), (0, SAMPLE_ROWS - dbsz), (0, 0)))
        xs, new_state_t = _sample_out(zs, state_t, ydm, pool_w_b, pscale, w_out_b, xs, l)
        pool_s.append(jnp.transpose(new_state_t[:, :dbsz], (1, 0, 2)))
        zs_rows.append(zs[:dbsz].reshape(dbsz, N_SEG, N_HEADS, HEAD_DIM))

    zs_all = jnp.stack(zs_rows)
    new_caches = []
    for g, cache in enumerate(caches):
        kseg = SEG_QKV0 + 3 * g + 1
        new_caches.append(_cache_shift(cache, zs_all[:, :, kseg:kseg + 2]))

    def kv_cache(t):
        k = jnp.stack(tails[2 * t])
        v = jnp.stack(tails[2 * t + 1])
        win = k.shape[2]
        return jnp.stack([k, v], axis=3).reshape(depth, bsz, win, 2, N_HEADS, HEAD_DIM)

    y_prompt = xp.reshape(bsz, seq, D_MODEL)
    y_sample = xs[:dbsz].reshape(dbsz, 1, D_MODEL)
    cache_mem_prompt = jnp.stack(mem_p).reshape(depth, bsz, MEM_LEN, 2, N_HEADS, HEAD_DIM)
    return (y_prompt, y_sample, jnp.stack(pool_p), kv_cache(0), kv_cache(1), kv_cache(2),
            cache_mem_prompt, jnp.stack(pool_s), new_caches[0], new_caches[1], new_caches[2])
```

```python
import functools

import jax
import jax.numpy as jnp
from jax import lax
from jax.experimental import pallas as pl
from jax.experimental.pallas import tpu as pltpu

F32 = jnp.float32
BF16 = jnp.bfloat16

D_MODEL = 2048
HEAD_DIM = 128
N_HEADS = 4
SEG = N_HEADS * HEAD_DIM
N_SEG = 14
IN_COLS = N_SEG * SEG
POOL_WINDOWS = (2, 4, 8, 16)
POOL_BUF = 15
POOL_HALO = 16
DIL_PAIRS = ((128, 1), (512, 4), (2048, 16))
BAND = 128
MEM_LEN = 256
MIX_WIDTH = 3 * SEG
EPS = 1e-6
ATTN_SCALE = HEAD_DIM ** -0.5
PAST_LEN = 16384

SEG_U, SEG_GATE_POOL, SEG_QKV0, SEG_GATE_DIL, SEG_QMEM, SEG_GATE_MEM = 0, 1, 2, 11, 12, 13
NORM_SEGS = (2, 3, 5, 6, 8, 9, 12)
GATE_SEGS = (1, 11, 13)

VMEM_LIMIT_BYTES = 56 * 1024 * 1024

PROJ_ROWS = 1024
PROJ_COLS = 2 * SEG
ATTN_ROWS = 2048
MIX_ROWS = 256
SAMPLE_ROWS = 16
SHIFT_ROWS = 512
TAIL_ROWS = 512
SHIFT_CHUNK = 64


MODE_PLAIN, MODE_NORM, MODE_GATE = 0.0, 1.0, 2.0


def _shift_rows(c_ref, nxt_ref, o_ref):
    rows = c_ref.shape[0]
    chunk = min(rows, SHIFT_CHUNK)
    for lo in range(0, rows, chunk):
        n = chunk if lo + chunk < rows else chunk - 1
        o_ref[pl.ds(lo, n)] = c_ref[pl.ds(lo + 1, n)]
    o_ref[rows - 1] = nxt_ref[0]


def _proj_kernel(x_ref, g_ref, w_ref, eg_ref, em_ref, *rest, with_shift, aliased):
    if with_shift:
        c_ref, nxt_ref = rest[:2]
        o_ref, co_ref, h_ref = rest[3:] if aliased else rest[2:]
    else:
        o_ref, h_ref = rest
    j = pl.program_id(1)

    @pl.when(j == 0)
    def _():
        x = x_ref[...]
        ms = jnp.mean(x * x, axis=-1, keepdims=True)
        h_ref[...] = (x * lax.rsqrt(ms + EPS) * g_ref[...]).astype(BF16)

    if with_shift:
        _shift_rows(c_ref, nxt_ref, co_ref)
    acc = jnp.dot(h_ref[...], w_ref[...], preferred_element_type=F32)
    for h in range(w_ref.shape[1] // HEAD_DIM):
        sl = slice(h * HEAD_DIM, (h + 1) * HEAD_DIM)
        a = acc[:, sl]
        mode = em_ref[:, sl]
        f_norm = lax.rsqrt(jnp.mean(a * a, axis=-1, keepdims=True) + EPS) * eg_ref[:, sl]
        f_gate = jax.nn.sigmoid(a)
        factor = jnp.where(mode == MODE_NORM, f_norm, jnp.where(mode == MODE_GATE, f_gate, 1.0))
        o_ref[:, sl] = a * factor


def _project(x, gain, w, layer, ep_gain, ep_mode, *, rows, cols, name, shift=None):
    m, k = x.shape
    n = w.shape[2]
    n_j = n // cols
    in_specs = [
        pl.BlockSpec((rows, k), lambda i, j: (i, 0)),
        pl.BlockSpec((1, k), lambda i, j: (0, 0)),
        pl.BlockSpec((None, k, cols), lambda i, j: (layer, 0, j)),
        pl.BlockSpec((1, cols), lambda i, j: (0, j)),
        pl.BlockSpec((1, cols), lambda i, j: (0, j)),
    ]
    args = [x, gain, w, ep_gain, ep_mode]
    out_shape = [jax.ShapeDtypeStruct((m, n), F32)]
    out_specs = [pl.BlockSpec((rows, cols), lambda i, j: (i, j))]
    aliases = {}
    if shift is not None:
        cache, buf = shift
        bsz, win = cache.shape[1:3]
        row_shape = cache.shape[3:]
        zeros = (0,) * len(row_shape)
        per_b = win // SHIFT_ROWS
        n_shift = bsz * per_b
        assert n_shift <= (m // rows) * n_j

        def block_of(i, j):
            t = jnp.minimum(i * n_j + j, n_shift - 1)
            return t // per_b, t % per_b

        def main_map(i, j):
            b, r = block_of(i, j)
            return (layer, b, r) + zeros

        def next_map(i, j):
            b, r = block_of(i, j)
            return (layer, b, jnp.minimum((r + 1) * SHIFT_ROWS, win - 1)) + zeros

        in_specs += [pl.BlockSpec((None, None, SHIFT_ROWS) + row_shape, main_map),
                     pl.BlockSpec((None, None, 1) + row_shape, next_map)]
        args += [cache, cache]
        if buf is not None:
            in_specs.append(pl.BlockSpec(memory_space=pl.ANY))
            args.append(buf)
            aliases = {len(args) - 1: 1}
        out_shape.append(jax.ShapeDtypeStruct(cache.shape, cache.dtype))
        out_specs.append(pl.BlockSpec((None, None, SHIFT_ROWS) + row_shape, main_map))
    res = pl.pallas_call(
        functools.partial(_proj_kernel, with_shift=shift is not None,
                          aliased=shift is not None and shift[1] is not None),
        out_shape=out_shape,
        grid=(m // rows, n_j),
        in_specs=in_specs,
        out_specs=out_specs,
        scratch_shapes=[pltpu.VMEM((rows, k), BF16)],
        input_output_aliases=aliases,
        compiler_params=pltpu.CompilerParams(
            dimension_semantics=("arbitrary", "arbitrary"),
            vmem_limit_bytes=VMEM_LIMIT_BYTES),
        name=name,
    )(*args)
    return res if shift is not None else res[0]


def _dilattn_kernel(*refs, n_tiles):
    ins, outs, scratch = refs[:15], refs[15:22], refs[22:]
    y_ref = outs[0]
    tails = outs[1:]
    og_ref, lse_ref = scratch
    i = pl.program_id(2)

    row = lax.broadcasted_iota(jnp.int32, (BAND, 2 * BAND), 0)
    col = lax.broadcasted_iota(jnp.int32, (BAND, 2 * BAND), 1)
    not_future = col <= row + BAND
    band = jnp.logical_and(col >= row, not_future)
    first_lo = jnp.maximum(row, (i == 0).astype(jnp.int32) * BAND)
    band_first = jnp.logical_and(col >= first_lo, not_future)
    ones = jnp.ones((2 * BAND, HEAD_DIM), BF16)
    nt_dims = (((1,), (1,)), ((), ()))

    for g, (win, d) in enumerate(DIL_PAIRS):
        q_ref, kc_ref, vc_ref, kp_ref, vp_ref = ins[5 * g:5 * g + 5]
        span = BAND * d

        def rows_of(start, size, d=d):
            return pl.ds(start, size) if d == 1 else pl.ds(start, size, stride=d)

        for t in range(ATTN_ROWS // BAND):
            u, c = divmod(t, d)
            base = u * span + c
            q = q_ref[rows_of(base, BAND), :].astype(BF16)
            if u == 0:
                k = jnp.concatenate([kp_ref[rows_of(c, BAND), :], kc_ref[rows_of(c, BAND), :]], axis=0)
                v = jnp.concatenate([vp_ref[rows_of(c, BAND), :], vc_ref[rows_of(c, BAND), :]], axis=0)
            else:
                k = kc_ref[rows_of(base - span, 2 * BAND), :]
                v = vc_ref[rows_of(base - span, 2 * BAND), :]
            s = lax.dot_general(q, k.astype(BF16), nt_dims, preferred_element_type=F32)
            s = jnp.where(band_first if u == 0 else band, s, -jnp.inf)
            m = jnp.max(jnp.maximum(s[:, :BAND], s[:, BAND:]), axis=-1, keepdims=True)
            p = jnp.exp(s - m).astype(BF16)
            ov = jnp.dot(p, jnp.concatenate([v.astype(BF16), ones], axis=1), preferred_element_type=F32)
            l = ov[:, HEAD_DIM:]
            o_rows = rows_of(g * ATTN_ROWS + base, BAND)
            og_ref[o_rows, :] = ov[:, :HEAD_DIM] / l
            lse_ref[o_rows, :] = m + jnp.log(l)

        @pl.when(i == n_tiles - 1)
        def _(g=g, win=win, kc_ref=kc_ref, vc_ref=vc_ref):
            tails[2 * g][...] = kc_ref[pl.ds(ATTN_ROWS - win, win), :]
            tails[2 * g + 1][...] = vc_ref[pl.ds(ATTN_ROWS - win, win), :]

    chunk = 256

    def merge(r, carry):
        r0 = pl.multiple_of(r * chunk, chunk)
        sls = [pl.ds(g * ATTN_ROWS + r0, chunk) for g in range(len(DIL_PAIRS))]
        l0, l1, l2 = [lse_ref[sl, :] for sl in sls]
        mx = jnp.maximum(jnp.maximum(l0, l1), l2)
        e0, e1, e2 = jnp.exp(l0 - mx), jnp.exp(l1 - mx), jnp.exp(l2 - mx)
        num = e0 * og_ref[sls[0], :] + e1 * og_ref[sls[1], :] + e2 * og_ref[sls[2], :]
        y_ref[pl.ds(r0, chunk), :] = num / (e0 + e1 + e2)
        return carry

    lax.fori_loop(0, ATTN_ROWS // chunk, merge, 0)


def _dilated_attention(z3):
    bsz, seq, _ = z3.shape
    n_tiles = seq // ATTN_ROWS
    in_specs = []
    for g, (win, d) in enumerate(DIL_PAIRS):
        span = BAND * d
        qb = (SEG_QKV0 + 3 * g) * N_HEADS
        kb, vb = qb + N_HEADS, qb + 2 * N_HEADS
        per_tile = ATTN_ROWS // span

        def prev_map(col0, per_tile=per_tile):
            return lambda b, h, i: (b, jnp.maximum(i * per_tile - 1, 0), col0 + h)

        def cur_map(col0):
            return lambda b, h, i: (b, i, col0 + h)

        in_specs += [
            pl.BlockSpec((None, ATTN_ROWS, HEAD_DIM), cur_map(qb)),
            pl.BlockSpec((None, ATTN_ROWS, HEAD_DIM), cur_map(kb)),
            pl.BlockSpec((None, ATTN_ROWS, HEAD_DIM), cur_map(vb)),
            pl.BlockSpec((None, span, HEAD_DIM), prev_map(kb)),
            pl.BlockSpec((None, span, HEAD_DIM), prev_map(vb)),
        ]
    out_shape = [jax.ShapeDtypeStruct((bsz, seq, SEG), F32)]
    out_specs = [pl.BlockSpec((None, ATTN_ROWS, HEAD_DIM), lambda b, h, i: (b, i, h))]
    for win, _ in DIL_PAIRS:
        for _kv in range(2):
            out_shape.append(jax.ShapeDtypeStruct((bsz, win, SEG), F32))
            out_specs.append(pl.BlockSpec((None, win, HEAD_DIM), lambda b, h, i: (b, 0, h)))
    return pl.pallas_call(
        functools.partial(_dilattn_kernel, n_tiles=n_tiles),
        out_shape=out_shape,
        grid=(bsz, N_HEADS, n_tiles),
        in_specs=in_specs,
        out_specs=out_specs,
        scratch_shapes=[
            pltpu.VMEM((len(DIL_PAIRS) * ATTN_ROWS, HEAD_DIM), F32),
            pltpu.VMEM((len(DIL_PAIRS) * ATTN_ROWS, HEAD_DIM), F32),
        ],
        compiler_params=pltpu.CompilerParams(
            dimension_semantics=("arbitrary", "arbitrary", "arbitrary"),
            vmem_limit_bytes=VMEM_LIMIT_BYTES),
        name="dilated_attention",
    )(*([z3] * 15))


def _mix_kernel(u_ref, up_ref, gp_ref, gd_ref, qm_ref, gm_ref, yd_ref, mkv_ref, pw_ref, ps_ref,
                wo_ref, x_ref, o_ref, ue_ref, y_ref):
    i = pl.program_id(1)
    rows = u_ref.shape[0]

    @pl.when(i > 0)
    def _():
        ue_ref[pl.ds(0, POOL_HALO), :] = up_ref[...]

    @pl.when(i == 0)
    def _():
        ue_ref[pl.ds(0, POOL_HALO), :] = jnp.zeros((POOL_HALO, SEG), F32)

    ue_ref[pl.ds(POOL_HALO, rows), :] = u_ref[...]
    pos = i * rows + lax.broadcasted_iota(jnp.int32, (rows, 1), 0)
    for gi, w in enumerate(POOL_WINDOWS):
        sl = slice(gi * HEAD_DIM, (gi + 1) * HEAD_DIM)
        tot = ue_ref[pl.ds(POOL_HALO, rows), sl]
        for back in range(1, w):
            tot = tot + ue_ref[pl.ds(POOL_HALO - back, rows), sl]
        cnt = jnp.minimum(w, pos + 1).astype(F32)
        dlt = tot / cnt - u_ref[:, sl]
        yp = jnp.dot(dlt.astype(BF16), pw_ref[gi], preferred_element_type=F32) * ps_ref[:, sl]
        y_ref[:, sl] = (gp_ref[:, sl] * yp).astype(BF16)

    y_ref[:, SEG:2 * SEG] = (gd_ref[...] * yd_ref[...]).astype(BF16)

    nt_dims = (((1,), (1,)), ((), ()))
    for h in range(N_HEADS):
        sl = slice(h * HEAD_DIM, (h + 1) * HEAD_DIM)
        q = qm_ref[:, sl].astype(BF16)
        k = mkv_ref[:, sl].astype(BF16)
        v = mkv_ref[:, SEG + h * HEAD_DIM:SEG + (h + 1) * HEAD_DIM].astype(BF16)
        s = lax.dot_general(q, k, nt_dims, preferred_element_type=F32)
        m = jnp.max(s, axis=-1, keepdims=True)
        p = jnp.exp(s - m)
        l = jnp.sum(p, axis=-1, keepdims=True)
        o = jnp.dot(p.astype(BF16), v, preferred_element_type=F32) / l
        y_ref[:, 2 * SEG + h * HEAD_DIM:2 * SEG + (h + 1) * HEAD_DIM] = (gm_ref[:, sl] * o).astype(BF16)

    o_ref[...] = x_ref[...] + jnp.dot(y_ref[...], wo_ref[...], preferred_element_type=F32)


def _mix(z3, y_dil, mem_kv, pool_w, pool_scale, w_out, x3, layer):
    bsz, seq, _ = z3.shape
    rows = MIX_ROWS
    halo_per_tile = rows // POOL_HALO

    def seg_spec(seg):
        return pl.BlockSpec((None, rows, SEG), lambda b, i: (b, i, seg))

    return pl.pallas_call(
        _mix_kernel,
        out_shape=jax.ShapeDtypeStruct(x3.shape, F32),
        grid=(bsz, seq // rows),
        in_specs=[
            seg_spec(SEG_U),
            pl.BlockSpec((None, POOL_HALO, SEG),
                         lambda b, i: (b, jnp.maximum(i * halo_per_tile - 1, 0), SEG_U)),
            seg_spec(SEG_GATE_POOL),
            seg_spec(SEG_GATE_DIL),
            seg_spec(SEG_QMEM),
            seg_spec(SEG_GATE_MEM),
            pl.BlockSpec((None, rows, SEG), lambda b, i: (b, i, 0)),
            pl.BlockSpec((None, MEM_LEN, 2 * SEG), lambda b, i: (b, 0, 0)),
            pl.BlockSpec((None, len(POOL_WINDOWS), HEAD_DIM, HEAD_DIM), lambda b, i: (layer, 0, 0, 0)),
            pl.BlockSpec((1, SEG), lambda b, i: (0, 0)),
            pl.BlockSpec((None, MIX_WIDTH, D_MODEL), lambda b, i: (layer, 0, 0)),
            pl.BlockSpec((None, rows, D_MODEL), lambda b, i: (b, i, 0)),
        ],
        out_specs=pl.BlockSpec((None, rows, D_MODEL), lambda b, i: (b, i, 0)),
        scratch_shapes=[
            pltpu.VMEM((POOL_HALO + rows, SEG), F32),
            pltpu.VMEM((rows, MIX_WIDTH), BF16),
        ],
        compiler_params=pltpu.CompilerParams(
            dimension_semantics=("arbitrary", "arbitrary"),
            vmem_limit_bytes=VMEM_LIMIT_BYTES),
        name="mix",
    )(z3, z3, z3, z3, z3, z3, y_dil, mem_kv, pool_w, pool_scale, w_out, x3)


def _sample_attn_kernel(zs_ref, c0_ref, c1_ref, c2_ref, cm_ref, o_ref):
    outs, lses = [], []
    for g, c_ref in enumerate((c0_ref, c1_ref, c2_ref)):
        base = (SEG_QKV0 + 3 * g) * N_HEADS
        q = zs_ref[pl.ds(base, N_HEADS), :]
        k_new = zs_ref[pl.ds(base + N_HEADS, N_HEADS), :]
        v_new = zs_ref[pl.ds(base + 2 * N_HEADS, N_HEADS), :]
        k = c_ref[:, 0]
        v = c_ref[:, 1]
        s = jnp.sum(k * q[None], axis=-1, keepdims=True)
        s_new = jnp.sum(k_new * q, axis=-1, keepdims=True)
        m = jnp.maximum(jnp.max(s, axis=0), s_new)
        p = jnp.exp(s - m[None])
        p_new = jnp.exp(s_new - m)
        l = jnp.sum(p, axis=0) + p_new
        outs.append((jnp.sum(p * v, axis=0) + p_new * v_new) / l)
        lses.append(m + jnp.log(l))
    mx = jnp.maximum(jnp.maximum(lses[0], lses[1]), lses[2])
    es = [jnp.exp(x - mx) for x in lses]
    o_ref[pl.ds(0, N_HEADS), :] = (es[0] * outs[0] + es[1] * outs[1] + es[2] * outs[2]) / (es[0] + es[1] + es[2])

    q = zs_ref[pl.ds(SEG_QMEM * N_HEADS, N_HEADS), :]
    k = cm_ref[:, 0]
    v = cm_ref[:, 1]
    s = jnp.sum(k * q[None], axis=-1, keepdims=True)
    m = jnp.max(s, axis=0)
    p = jnp.exp(s - m[None])
    o_ref[pl.ds(N_HEADS, N_HEADS), :] = jnp.sum(p * v, axis=0) / jnp.sum(p, axis=0)


def _sample_attention(zs3, caches7, cache_mem, layer):
    bsz = cache_mem.shape[1]
    in_specs = [pl.BlockSpec((None, N_SEG * N_HEADS, HEAD_DIM), lambda b: (b, 0, 0))]
    for _ in caches7:
        in_specs.append(pl.BlockSpec((None, None, BAND, None, 2, N_HEADS, HEAD_DIM),
                                     lambda b: (layer, b, 0, 0, 0, 0, 0)))
    in_specs.append(pl.BlockSpec((None, None, MEM_LEN, 2, N_HEADS, HEAD_DIM),
                                 lambda b: (layer, b, 0, 0, 0, 0)))
    return pl.pallas_call(
        _sample_attn_kernel,
        out_shape=jax.ShapeDtypeStruct((bsz, 2 * N_HEADS, HEAD_DIM), F32),
        grid=(bsz,),
        in_specs=in_specs,
        out_specs=pl.BlockSpec((None, 2 * N_HEADS, HEAD_DIM), lambda b: (b, 0, 0)),
        compiler_params=pltpu.CompilerParams(dimension_semantics=("arbitrary",)),
        name="sample_attention",
    )(zs3, *caches7, cache_mem)


def _sample_out_kernel(zs_ref, st_ref, ydm_ref, pw_ref, ps_ref, wo_ref, x_ref, o_ref, ns_ref, y_ref):
    u = zs_ref[:, pl.ds(SEG_U * SEG, SEG)]
    for gi, w in enumerate(POOL_WINDOWS):
        sl = slice(gi * HEAD_DIM, (gi + 1) * HEAD_DIM)
        tot = u[:, sl]
        for back in range(1, w):
            tot = tot + st_ref[POOL_BUF - back, :, sl]
        cnt = float(min(w, PAST_LEN + 1))
        dlt = tot / cnt - u[:, sl]
        yp = jnp.dot(dlt.astype(BF16), pw_ref[gi], preferred_element_type=F32) * ps_ref[:, sl]
        y_ref[:, sl] = (zs_ref[:, pl.ds(SEG_GATE_POOL * SEG + gi * HEAD_DIM, HEAD_DIM)] * yp).astype(BF16)
    y_ref[:, SEG:2 * SEG] = (zs_ref[:, pl.ds(SEG_GATE_DIL * SEG, SEG)] * ydm_ref[:, pl.ds(0, SEG)]).astype(BF16)
    y_ref[:, 2 * SEG:] = (zs_ref[:, pl.ds(SEG_GATE_MEM * SEG, SEG)] * ydm_ref[:, pl.ds(SEG, SEG)]).astype(BF16)
    o_ref[...] = x_ref[...] + jnp.dot(y_ref[...], wo_ref[...], preferred_element_type=F32)
    for r in range(POOL_BUF - 1):
        ns_ref[r] = st_ref[r + 1]
    ns_ref[POOL_BUF - 1] = u


def _sample_out(zs, state_t, ydm, pool_w, pool_scale, w_out, xs, layer):
    rows = zs.shape[0]
    full = lambda shape: pl.BlockSpec(shape, lambda i: tuple(0 for _ in shape))
    of_layer = lambda a: pl.BlockSpec((None,) + a.shape[1:], lambda i: (layer,) + (0,) * (a.ndim - 1))
    return pl.pallas_call(
        _sample_out_kernel,
        out_shape=[jax.ShapeDtypeStruct(xs.shape, F32), jax.ShapeDtypeStruct(state_t.shape, F32)],
        grid=(1,),
        in_specs=[full(zs.shape), full(state_t.shape), full(ydm.shape), of_layer(pool_w),
                  full(pool_scale.shape), of_layer(w_out), full(xs.shape)],
        out_specs=[full(xs.shape), full(state_t.shape)],
        scratch_shapes=[pltpu.VMEM((rows, MIX_WIDTH), BF16)],
        compiler_params=pltpu.CompilerParams(vmem_limit_bytes=VMEM_LIMIT_BYTES),
        name="sample_out",
    )(zs, state_t, ydm, pool_w, pool_scale, w_out, xs)


def _cache_shift_kernel(c_ref, nxt_ref, new_ref, o_ref, *, n_blocks):
    i = pl.program_id(2)
    rows = c_ref.shape[0]
    chunk = min(rows, SHIFT_CHUNK)

    def copy(c, carry):
        r0 = pl.multiple_of(c * chunk, chunk)
        o_ref[pl.ds(r0, chunk)] = c_ref[pl.ds(r0 + 1, chunk)]
        return carry

    lax.fori_loop(0, rows // chunk - 1, copy, 0)
    o_ref[pl.ds(rows - chunk, chunk - 1)] = c_ref[pl.ds(rows - chunk + 1, chunk - 1)]

    @pl.when(i < n_blocks - 1)
    def _():
        o_ref[rows - 1] = nxt_ref[0]

    @pl.when(i == n_blocks - 1)
    def _():
        o_ref[rows - 1] = new_ref[...]


def _cache_shift(cache, new_rows):
    depth, bsz, win = cache.shape[:3]
    rows = min(win, SHIFT_ROWS)
    n_blocks = win // rows
    row_shape = cache.shape[3:]
    zeros = (0,) * len(row_shape)
    return pl.pallas_call(
        functools.partial(_cache_shift_kernel, n_blocks=n_blocks),
        out_shape=jax.ShapeDtypeStruct(cache.shape, cache.dtype),
        grid=(depth, bsz, n_blocks),
        in_specs=[
            pl.BlockSpec((None, None, rows) + row_shape, lambda l, b, i: (l, b, i) + zeros),
            pl.BlockSpec((None, None, 1) + row_shape,
                         lambda l, b, i: (l, b, jnp.minimum((i + 1) * rows, win - 1)) + zeros),
            pl.BlockSpec((None, None) + row_shape, lambda l, b, i: (l, b) + zeros),
        ],
        out_specs=pl.BlockSpec((None, None, rows) + row_shape, lambda l, b, i: (l, b, i) + zeros),
        compiler_params=pltpu.CompilerParams(
            dimension_semantics=("arbitrary", "arbitrary", "arbitrary"),
            vmem_limit_bytes=VMEM_LIMIT_BYTES),
        name="cache_shift",
    )(cache, cache, new_rows)


def _set_last_kernel(new_ref, buf_ref, o_ref):
    del buf_ref
    o_ref[0] = new_ref[...]


def _cache_set_last(buf, new_rows):
    depth, bsz, win = buf.shape[:3]
    row_shape = buf.shape[3:]
    zeros = (0,) * len(row_shape)
    return pl.pallas_call(
        _set_last_kernel,
        out_shape=jax.ShapeDtypeStruct(buf.shape, buf.dtype),
        grid=(depth, bsz),
        in_specs=[pl.BlockSpec((None, None) + row_shape, lambda l, b: (l, b) + zeros),
                  pl.BlockSpec(memory_space=pl.ANY)],
        out_specs=pl.BlockSpec((None, None, 1) + row_shape, lambda l, b: (l, b, win - 1) + zeros),
        input_output_aliases={1: 0},
        compiler_params=pltpu.CompilerParams(dimension_semantics=("arbitrary", "arbitrary")),
        name="cache_set_last",
    )(new_rows, buf)


def _tail_kernel(k_ref, v_ref, *rest):
    o_ref = rest[-1]
    for h in range(N_HEADS):
        sl = slice(h * HEAD_DIM, (h + 1) * HEAD_DIM)
        o_ref[:, 0, h, :] = k_ref[:, sl]
        o_ref[:, 1, h, :] = v_ref[:, sl]


def _assemble_tail(buf, k_tail, v_tail, layer, depth):
    bsz, win, _ = k_tail.shape
    rows = min(win, TAIL_ROWS)
    shape = (depth, bsz, win, 2, N_HEADS, HEAD_DIM)
    in_specs = [pl.BlockSpec((None, rows, SEG), lambda b, r: (b, r, 0)),
                pl.BlockSpec((None, rows, SEG), lambda b, r: (b, r, 0))]
    args = [k_tail, v_tail]
    aliases = {}
    if buf is not None:
        in_specs.append(pl.BlockSpec(memory_space=pl.ANY))
        args.append(buf)
        aliases = {2: 0}
    return pl.pallas_call(
        _tail_kernel,
        out_shape=jax.ShapeDtypeStruct(shape, F32),
        grid=(bsz, win // rows),
        in_specs=in_specs,
        out_specs=pl.BlockSpec((None, None, rows, 2, N_HEADS, HEAD_DIM),
                               lambda b, r: (layer, b, r, 0, 0, 0)),
        input_output_aliases=aliases,
        compiler_params=pltpu.CompilerParams(dimension_semantics=("arbitrary", "arbitrary")),
        name="assemble_tail",
    )(*args)


def _tile_heads(v):
    return jnp.tile(v, N_HEADS)


def kernel(x_prompt, x_sample, state_pool, cache_dil_w128, cache_dil_w512, cache_dil_w2048,
           cache_mem_kv, mem_prompt, norm_g, w_in, pool_w, pool_scale, dil_q_norm, dil_k_norm,
           mem_norm_g, w_mem_kv, mem_q_norm, mem_k_norm, w_out):
    depth = w_in.shape[0]
    bsz, seq, _ = x_prompt.shape
    dbsz = x_sample.shape[0]
    caches = (cache_dil_w128, cache_dil_w512, cache_dil_w2048)

    w_in_b = w_in.astype(BF16)
    w_out_b = w_out.astype(BF16)
    w_mem_b = w_mem_kv.astype(BF16)
    pool_w_b = pool_w.astype(BF16)

    ones = jnp.ones((SEG,), F32)
    seg_modes = [MODE_PLAIN] * N_SEG
    for s in NORM_SEGS:
        seg_modes[s] = MODE_NORM
    for s in GATE_SEGS:
        seg_modes[s] = MODE_GATE
    ep_mode = jnp.repeat(jnp.array(seg_modes, F32), SEG)[None, :]
    mem_mode = jnp.repeat(jnp.array([MODE_NORM, MODE_PLAIN], F32), SEG)[None, :]
    caches7 =[c.reshape(depth, dbsz, win // d, d, 2, N_HEADS, HEAD_DIM)
               for c, (win, d) in zip(caches, DIL_PAIRS)]

    xp = x_prompt.reshape(bsz * seq, D_MODEL)
    xs = jnp.pad(x_sample.reshape(dbsz, D_MODEL), ((0, SAMPLE_ROWS - dbsz), (0, 0)))
    mem2 = mem_prompt.reshape(bsz * MEM_LEN, D_MODEL)

    pool_p, mem_p, pool_s, zs_rows = [], [], [], []
    prompt_caches = [None] * len(DIL_PAIRS)
    big_cache = None
    for l in range(depth):
        segs = [ones] * N_SEG
        for g in range(len(DIL_PAIRS)):
            segs[SEG_QKV0 + 3 * g] = _tile_heads(dil_q_norm[l, g]) * ATTN_SCALE
            segs[SEG_QKV0 + 3 * g + 1] = _tile_heads(dil_k_norm[l, g])
        segs[SEG_QMEM] = _tile_heads(mem_q_norm[l]) * ATTN_SCALE
        ep_gain = jnp.concatenate(segs)[None, :]
        mem_gain = jnp.concatenate([_tile_heads(mem_k_norm[l]), ones])[None, :]
        gain = norm_g[l][None, :]
        pscale = pool_scale[l][None, :]

        z, big_cache = _project(xp, gain, w_in_b, l, ep_gain, ep_mode, rows=PROJ_ROWS,
                                cols=PROJ_COLS, name="proj_prompt", shift=(caches[-1], big_cache))
        z3 = z.reshape(bsz, seq, IN_COLS)
        mkv = _project(mem2, mem_norm_g[l][None, :], w_mem_b, l, mem_gain, mem_mode,
                       rows=bsz * MEM_LEN, cols=2 * SEG, name="proj_mem")
        mkv3 = mkv.reshape(bsz, MEM_LEN, 2 * SEG)
        attn = _dilated_attention(z3)
        y_dil = attn[0]
        for g in range(len(DIL_PAIRS)):
            prompt_caches[g] = _assemble_tail(prompt_caches[g], attn[1 + 2 * g], attn[2 + 2 * g],
                                              l, depth)
        xp = _mix(z3, y_dil, mkv3, pool_w_b, pscale, w_out_b,
                  xp.reshape(bsz, seq, D_MODEL), l).reshape(bsz * seq, D_MODEL)
        pool_p.append(z3[:, seq - POOL_BUF:, :SEG])
        mem_p.append(mkv3)

        zs = _project(xs, gain, w_in_b, l, ep_gain, ep_mode, rows=SAMPLE_ROWS, cols=PROJ_COLS,
                      name="proj_sample")
        zs3 = zs.reshape(SAMPLE_ROWS, N_SEG * N_HEADS, HEAD_DIM)
        ydm = _sample_attention(zs3, caches7, cache_mem_kv, l)
        ydm = jnp.pad(ydm.reshape(dbsz, 2 * SEG), ((0, SAMPLE_ROWS - dbsz), (0, 0)))
        state_t = jnp.pad(jnp.transpose(state_pool[l], (1, 0, 2)),
                          ((0, 0), (0, SAMPLE_ROWS - dbsz), (0, 0)))
        xs, new_state_t = _sample_out(zs, state_t, ydm, pool_w_b, pscale, w_out_b, xs, l)
        pool_s.append(jnp.transpose(new_state_t[:, :dbsz], (1, 0, 2)))
        zs_rows.append(zs[:dbsz].reshape(dbsz, N_SEG, N_HEADS, HEAD_DIM))

    zs_all = jnp.stack(zs_rows)
    new_rows = [zs_all[:, :, SEG_QKV0 + 3 * g + 1:SEG_QKV0 + 3 * g + 3] for g in range(len(DIL_PAIRS))]
    new_caches = [_cache_shift(caches[0], new_rows[0]), _cache_shift(caches[1], new_rows[1]),
                  _cache_set_last(big_cache, new_rows[2])]

    y_prompt = xp.reshape(bsz, seq, D_MODEL)
    y_sample = xs[:dbsz].reshape(dbsz, 1, D_MODEL)
    cache_mem_prompt = jnp.stack(mem_p).reshape(depth, bsz, MEM_LEN, 2, N_HEADS, HEAD_DIM)
    return (y_prompt, y_sample, jnp.stack(pool_p), prompt_caches[0], prompt_caches[1],
            prompt_caches[2], cache_mem_prompt, jnp.stack(pool_s), new_caches[0], new_caches[1],
            new_caches[2])
```

```python
import functools

import jax
import jax.numpy as jnp
from jax import lax
from jax.experimental import pallas as pl
from jax.experimental.pallas import tpu as pltpu

F32 = jnp.float32
BF16 = jnp.bfloat16

D_MODEL = 2048
HEAD_DIM = 128
N_HEADS = 4
SEG = N_HEADS * HEAD_DIM
N_SEG = 14
IN_COLS = N_SEG * SEG
POOL_WINDOWS = (2, 4, 8, 16)
POOL_BUF = 15
POOL_HALO = 16
POOL_PAD = 8
DIL_PAIRS = ((128, 1), (512, 4), (2048, 16))
BAND = 128
MEM_LEN = 256
MIX_WIDTH = 3 * SEG
EPS = 1e-6
ATTN_SCALE = HEAD_DIM ** -0.5
PAST_LEN = 16384

Z_ORDER = (2, 0, 3, 4, 5, 1, 6, 7, 8, 11, 9, 10, 12, 13)
SEG_U, SEG_GATE_POOL, SEG_GATE_DIL, SEG_QMEM, SEG_GATE_MEM = 1, 5, 9, 12, 13
GATE_SEGS = (SEG_GATE_POOL, SEG_GATE_DIL, SEG_GATE_MEM)


def _seg_q(g):
    return 4 * g


def _seg_k(g):
    return 4 * g + 2

VMEM_LIMIT_BYTES = 56 * 1024 * 1024

PROJ_ROWS = 1024
PROJ_COLS = 2 * SEG
ATTN_ROWS = 2048
MIX_ROWS = 512
SAMPLE_ROWS = 16
SHIFT_ROWS = 512
TAIL_ROWS = 512
SHIFT_CHUNK = 64


MODE_PLAIN, MODE_GATE = 0.0, 1.0


def _shift_rows(c_ref, nxt_ref, o_ref):
    rows = c_ref.shape[0]
    chunk = min(rows, SHIFT_CHUNK)
    for lo in range(0, rows, chunk):
        n = chunk if lo + chunk < rows else chunk - 1
        o_ref[pl.ds(lo, n)] = c_ref[pl.ds(lo + 1, n)]
    o_ref[rows - 1] = nxt_ref[0]


def _proj_kernel(x_ref, g_ref, w_ref, eg_ref, em_ref, *rest, with_shift, aliased):
    if with_shift:
        c_ref, nxt_ref = rest[:2]
        o_ref, co_ref, h_ref = rest[3:] if aliased else rest[2:]
    else:
        o_ref, h_ref = rest
    j = pl.program_id(1)

    @pl.when(j == 0)
    def _():
        x = x_ref[...]
        ms = jnp.mean(x * x, axis=-1, keepdims=True)
        h_ref[...] = (x * lax.rsqrt(ms + EPS) * g_ref[...]).astype(BF16)

    if with_shift:
        _shift_rows(c_ref, nxt_ref, co_ref)
    acc = jnp.dot(h_ref[...], w_ref[...], preferred_element_type=F32)
    for h in range(N_HEADS):
        sl = slice(h * HEAD_DIM, (h + 1) * HEAD_DIM)
        a = acc[:, sl]
        o_ref[:, sl] = a * (lax.rsqrt(jnp.mean(a * a, axis=-1, keepdims=True) + EPS) * eg_ref[:, sl])
    for h in range(N_HEADS, 2 * N_HEADS):
        sl = slice(h * HEAD_DIM, (h + 1) * HEAD_DIM)
        a = acc[:, sl]
        o_ref[:, sl] = a * jnp.where(em_ref[:, sl] == MODE_GATE, jax.nn.sigmoid(a), 1.0)


def _project(x, gain, w, layer, ep_gain, ep_mode, *, rows, cols, name, shift=None):
    m, k = x.shape
    n = w.shape[2]
    assert cols == 2 * SEG
    n_j = n // cols
    in_specs = [
        pl.BlockSpec((rows, k), lambda i, j: (i, 0)),
        pl.BlockSpec((1, k), lambda i, j: (0, 0)),
        pl.BlockSpec((None, k, cols), lambda i, j: (layer, 0, j)),
        pl.BlockSpec((1, cols), lambda i, j: (0, j)),
        pl.BlockSpec((1, cols), lambda i, j: (0, j)),
    ]
    args = [x, gain, w, ep_gain, ep_mode]
    out_shape = [jax.ShapeDtypeStruct((m, n), F32)]
    out_specs = [pl.BlockSpec((rows, cols), lambda i, j: (i, j))]
    aliases = {}
    if shift is not None:
        cache, buf = shift
        bsz, win = cache.shape[1:3]
        row_shape = cache.shape[3:]
        zeros = (0,) * len(row_shape)
        per_b = win // SHIFT_ROWS
        n_shift = bsz * per_b
        assert n_shift <= (m // rows) * n_j

        def block_of(i, j):
            t = jnp.minimum(i * n_j + j, n_shift - 1)
            return t // per_b, t % per_b

        def main_map(i, j):
            b, r = block_of(i, j)
            return (layer, b, r) + zeros

        def next_map(i, j):
            b, r = block_of(i, j)
            return (layer, b, jnp.minimum((r + 1) * SHIFT_ROWS, win - 1)) + zeros

        in_specs += [pl.BlockSpec((None, None, SHIFT_ROWS) + row_shape, main_map),
                     pl.BlockSpec((None, None, 1) + row_shape, next_map)]
        args += [cache, cache]
        if buf is not None:
            in_specs.append(pl.BlockSpec(memory_space=pl.ANY))
            args.append(buf)
            aliases = {len(args) - 1: 1}
        out_shape.append(jax.ShapeDtypeStruct(cache.shape, cache.dtype))
        out_specs.append(pl.BlockSpec((None, None, SHIFT_ROWS) + row_shape, main_map))
    res = pl.pallas_call(
        functools.partial(_proj_kernel, with_shift=shift is not None,
                          aliased=shift is not None and shift[1] is not None),
        out_shape=out_shape,
        grid=(m // rows, n_j),
        in_specs=in_specs,
        out_specs=out_specs,
        scratch_shapes=[pltpu.VMEM((rows, k), BF16)],
        input_output_aliases=aliases,
        compiler_params=pltpu.CompilerParams(
            dimension_semantics=("arbitrary", "arbitrary"),
            vmem_limit_bytes=VMEM_LIMIT_BYTES),
        name=name,
    )(*args)
    return res if shift is not None else res[0]


def _dilattn_kernel(*refs, n_tiles):
    ins, outs, scratch = refs[:15], refs[15:22], refs[22:]
    y_ref = outs[0]
    tails = outs[1:]
    og_ref, lse_ref = scratch
    i = pl.program_id(2)

    row = lax.broadcasted_iota(jnp.int32, (BAND, 2 * BAND), 0)
    col = lax.broadcasted_iota(jnp.int32, (BAND, 2 * BAND), 1)
    not_future = col <= row + BAND
    band = jnp.logical_and(col >= row, not_future)
    first_lo = jnp.maximum(row, (i == 0).astype(jnp.int32) * BAND)
    band_first = jnp.logical_and(col >= first_lo, not_future)
    ones = jnp.ones((2 * BAND, HEAD_DIM), BF16)
    nt_dims = (((1,), (1,)), ((), ()))

    for g, (win, d) in enumerate(DIL_PAIRS):
        q_ref, kc_ref, vc_ref, kp_ref, vp_ref = ins[5 * g:5 * g + 5]
        span = BAND * d

        def rows_of(start, size, d=d):
            return pl.ds(start, size) if d == 1 else pl.ds(start, size, stride=d)

        for t in range(ATTN_ROWS // BAND):
            u, c = divmod(t, d)
            base = u * span + c
            q = q_ref[rows_of(base, BAND), :].astype(BF16)
            if u == 0:
                k = jnp.concatenate([kp_ref[rows_of(c, BAND), :], kc_ref[rows_of(c, BAND), :]], axis=0)
                v = jnp.concatenate([vp_ref[rows_of(c, BAND), :], vc_ref[rows_of(c, BAND), :]], axis=0)
            else:
                k = kc_ref[rows_of(base - span, 2 * BAND), :]
                v = vc_ref[rows_of(base - span, 2 * BAND), :]
            s = lax.dot_general(q, k.astype(BF16), nt_dims, preferred_element_type=F32)
            s = jnp.where(band_first if u == 0 else band, s, -jnp.inf)
            m = jnp.max(jnp.maximum(s[:, :BAND], s[:, BAND:]), axis=-1, keepdims=True)
            p = jnp.exp(s - m).astype(BF16)
            ov = jnp.dot(p, jnp.concatenate([v.astype(BF16), ones], axis=1), preferred_element_type=F32)
            l = ov[:, HEAD_DIM:]
            o_rows = rows_of(g * ATTN_ROWS + base, BAND)
            og_ref[o_rows, :] = ov[:, :HEAD_DIM] / l
            lse_ref[o_rows, :] = m + jnp.log(l)

        @pl.when(i == n_tiles - 1)
        def _(g=g, win=win, kc_ref=kc_ref, vc_ref=vc_ref):
            tails[2 * g][...] = kc_ref[pl.ds(ATTN_ROWS - win, win), :]
            tails[2 * g + 1][...] = vc_ref[pl.ds(ATTN_ROWS - win, win), :]

    chunk = 256

    def merge(r, carry):
        r0 = pl.multiple_of(r * chunk, chunk)
        sls = [pl.ds(g * ATTN_ROWS + r0, chunk) for g in range(len(DIL_PAIRS))]
        l0, l1, l2 = [lse_ref[sl, :] for sl in sls]
        mx = jnp.maximum(jnp.maximum(l0, l1), l2)
        e0, e1, e2 = jnp.exp(l0 - mx), jnp.exp(l1 - mx), jnp.exp(l2 - mx)
        num = e0 * og_ref[sls[0], :] + e1 * og_ref[sls[1], :] + e2 * og_ref[sls[2], :]
        y_ref[pl.ds(r0, chunk), :] = num / (e0 + e1 + e2)
        return carry

    lax.fori_loop(0, ATTN_ROWS // chunk, merge, 0)


def _dilated_attention(z3):
    bsz, seq, _ = z3.shape
    n_tiles = seq // ATTN_ROWS
    in_specs = []
    for g, (win, d) in enumerate(DIL_PAIRS):
        span = BAND * d
        qb = _seg_q(g) * N_HEADS
        kb = _seg_k(g) * N_HEADS
        vb = kb + N_HEADS
        per_tile = ATTN_ROWS // span

        def prev_map(col0, per_tile=per_tile):
            return lambda b, h, i: (b, jnp.maximum(i * per_tile - 1, 0), col0 + h)

        def cur_map(col0):
            return lambda b, h, i: (b, i, col0 + h)

        in_specs += [
            pl.BlockSpec((None, ATTN_ROWS, HEAD_DIM), cur_map(qb)),
            pl.BlockSpec((None, ATTN_ROWS, HEAD_DIM), cur_map(kb)),
            pl.BlockSpec((None, ATTN_ROWS, HEAD_DIM), cur_map(vb)),
            pl.BlockSpec((None, span, HEAD_DIM), prev_map(kb)),
            pl.BlockSpec((None, span, HEAD_DIM), prev_map(vb)),
        ]
    out_shape = [jax.ShapeDtypeStruct((bsz, seq, SEG), F32)]
    out_specs = [pl.BlockSpec((None, ATTN_ROWS, HEAD_DIM), lambda b, h, i: (b, i, h))]
    for win, _ in DIL_PAIRS:
        for _kv in range(2):
            out_shape.append(jax.ShapeDtypeStruct((bsz, win, SEG), F32))
            out_specs.append(pl.BlockSpec((None, win, HEAD_DIM), lambda b, h, i: (b, 0, h)))
    return pl.pallas_call(
        functools.partial(_dilattn_kernel, n_tiles=n_tiles),
        out_shape=out_shape,
        grid=(bsz, N_HEADS, n_tiles),
        in_specs=in_specs,
        out_specs=out_specs,
        scratch_shapes=[
            pltpu.VMEM((len(DIL_PAIRS) * ATTN_ROWS, HEAD_DIM), F32),
            pltpu.VMEM((len(DIL_PAIRS) * ATTN_ROWS, HEAD_DIM), F32),
        ],
        compiler_params=pltpu.CompilerParams(
            dimension_semantics=("arbitrary", "arbitrary", "arbitrary"),
            vmem_limit_bytes=VMEM_LIMIT_BYTES),
        name="dilated_attention",
    )(*([z3] * 15))


def _mix_kernel(u_ref, up_ref, gp_ref, gd_ref, qm_ref, gm_ref, yd_ref, mkv_ref, pw_ref, ps_ref,
                wo_ref, x_ref, o_ref, ue_ref, t_ref, y_ref):
    i = pl.program_id(1)
    rows = u_ref.shape[0]

    pad, halo = POOL_PAD, POOL_HALO
    n = halo + rows

    @pl.when(i > 0)
    def _():
        ue_ref[pl.ds(pad, halo), :] = up_ref[...]

    @pl.when(i == 0)
    def _():
        ue_ref[pl.ds(pad, halo), :] = jnp.zeros((halo, SEG), F32)

    ue_ref[pl.ds(0, pad), :] = jnp.zeros((pad, SEG), F32)
    t_ref[0, pl.ds(0, pad), :] = jnp.zeros((pad, HEAD_DIM), F32)
    t_ref[1, pl.ds(0, pad), :] = jnp.zeros((pad, HEAD_DIM), F32)
    ue_ref[pl.ds(pad + halo, rows), :] = u_ref[...]
    pos = i * rows + lax.broadcasted_iota(jnp.int32, (rows, 1), 0)
    for gi, w in enumerate(POOL_WINDOWS):
        sl = slice(gi * HEAD_DIM, (gi + 1) * HEAD_DIM)
        cur = ue_ref[pl.ds(pad, n), sl] + ue_ref[pl.ds(pad - 1, n), sl]
        shift, buf = 2, 0
        while shift < w:
            t_ref[buf, pl.ds(pad, n), :] = cur
            cur = cur + t_ref[buf, pl.ds(pad - shift, n), :]
            shift, buf = 2 * shift, 1 - buf
        cnt = jnp.minimum(w, pos + 1).astype(F32)
        dlt = cur[halo:] / cnt - u_ref[:, sl]
        yp = jnp.dot(dlt.astype(BF16), pw_ref[gi], preferred_element_type=F32) * ps_ref[:, sl]
        y_ref[:, sl] = (gp_ref[:, sl] * yp).astype(BF16)

    y_ref[:, SEG:2 * SEG] = (gd_ref[...] * yd_ref[...]).astype(BF16)

    nt_dims = (((1,), (1,)), ((), ()))
    ones = jnp.ones((MEM_LEN, HEAD_DIM), BF16)
    for h in range(N_HEADS):
        sl = slice(h * HEAD_DIM, (h + 1) * HEAD_DIM)
        q = qm_ref[:, sl].astype(BF16)
        k = mkv_ref[:, sl].astype(BF16)
        v = mkv_ref[:, SEG + h * HEAD_DIM:SEG + (h + 1) * HEAD_DIM].astype(BF16)
        s = lax.dot_general(q, k, nt_dims, preferred_element_type=F32)
        m = jnp.max(s, axis=-1, keepdims=True)
        p = jnp.exp(s - m).astype(BF16)
        ov = jnp.dot(p, jnp.concatenate([v, ones], axis=1), preferred_element_type=F32)
        o = ov[:, :HEAD_DIM] / ov[:, HEAD_DIM:]
        y_ref[:, 2 * SEG + h * HEAD_DIM:2 * SEG + (h + 1) * HEAD_DIM] = (gm_ref[:, sl] * o).astype(BF16)

    o_ref[...] = x_ref[...] + jnp.dot(y_ref[...], wo_ref[...], preferred_element_type=F32)


def _mix(z3, y_dil, mem_kv, pool_w, pool_scale, w_out, x3, layer):
    bsz, seq, _ = z3.shape
    rows = MIX_ROWS
    halo_per_tile = rows // POOL_HALO

    def seg_spec(seg):
        return pl.BlockSpec((None, rows, SEG), lambda b, i: (b, i, seg))

    return pl.pallas_call(
        _mix_kernel,
        out_shape=jax.ShapeDtypeStruct(x3.shape, F32),
        grid=(bsz, seq // rows),
        in_specs=[
            seg_spec(SEG_U),
            pl.BlockSpec((None, POOL_HALO, SEG),
                         lambda b, i: (b, jnp.maximum(i * halo_per_tile - 1, 0), SEG_U)),
            seg_spec(SEG_GATE_POOL),
            seg_spec(SEG_GATE_DIL),
            seg_spec(SEG_QMEM),
            seg_spec(SEG_GATE_MEM),
            pl.BlockSpec((None, rows, SEG), lambda b, i: (b, i, 0)),
            pl.BlockSpec((None, MEM_LEN, 2 * SEG), lambda b, i: (b, 0, 0)),
            pl.BlockSpec((None, len(POOL_WINDOWS), HEAD_DIM, HEAD_DIM), lambda b, i: (layer, 0, 0, 0)),
            pl.BlockSpec((1, SEG), lambda b, i: (0, 0)),
            pl.BlockSpec((None, MIX_WIDTH, D_MODEL), lambda b, i: (layer, 0, 0),
                         pipeline_mode=pl.Buffered(1)),
            pl.BlockSpec((None, rows, D_MODEL), lambda b, i: (b, i, 0)),
        ],
        out_specs=pl.BlockSpec((None, rows, D_MODEL), lambda b, i: (b, i, 0)),
        scratch_shapes=[
            pltpu.VMEM((POOL_PAD + POOL_HALO + rows, SEG), F32),
            pltpu.VMEM((2, POOL_PAD + POOL_HALO + rows, HEAD_DIM), F32),
            pltpu.VMEM((rows, MIX_WIDTH), BF16),
        ],
        compiler_params=pltpu.CompilerParams(
            dimension_semantics=("arbitrary", "arbitrary"),
            vmem_limit_bytes=VMEM_LIMIT_BYTES),
        name="mix",
    )(z3, z3, z3, z3, z3, z3, y_dil, mem_kv, pool_w, pool_scale, w_out, x3)


def _sample_attn_kernel(zs_ref, c0_ref, c1_ref, c2_ref, cm_ref, o_ref):
    outs, lses = [], []
    for g, c_ref in enumerate((c0_ref, c1_ref, c2_ref)):
        q = zs_ref[pl.ds(_seg_q(g) * N_HEADS, N_HEADS), :]
        k_new = zs_ref[pl.ds(_seg_k(g) * N_HEADS, N_HEADS), :]
        v_new = zs_ref[pl.ds((_seg_k(g) + 1) * N_HEADS, N_HEADS), :]
        k = c_ref[:, 0]
        v = c_ref[:, 1]
        s = jnp.sum(k * q[None], axis=-1, keepdims=True)
        s_new = jnp.sum(k_new * q, axis=-1, keepdims=True)
        m = jnp.maximum(jnp.max(s, axis=0), s_new)
        p = jnp.exp(s - m[None])
        p_new = jnp.exp(s_new - m)
        l = jnp.sum(p, axis=0) + p_new
        outs.append((jnp.sum(p * v, axis=0) + p_new * v_new) / l)
        lses.append(m + jnp.log(l))
    mx = jnp.maximum(jnp.maximum(lses[0], lses[1]), lses[2])
    es = [jnp.exp(x - mx) for x in lses]
    o_ref[pl.ds(0, N_HEADS), :] = (es[0] * outs[0] + es[1] * outs[1] + es[2] * outs[2]) / (es[0] + es[1] + es[2])

    q = zs_ref[pl.ds(SEG_QMEM * N_HEADS, N_HEADS), :]
    k = cm_ref[:, 0]
    v = cm_ref[:, 1]
    s = jnp.sum(k * q[None], axis=-1, keepdims=True)
    m = jnp.max(s, axis=0)
    p = jnp.exp(s - m[None])
    o_ref[pl.ds(N_HEADS, N_HEADS), :] = jnp.sum(p * v, axis=0) / jnp.sum(p, axis=0)


def _sample_attention(zs3, caches7, cache_mem, layer):
    bsz = cache_mem.shape[1]
    in_specs = [pl.BlockSpec((None, N_SEG * N_HEADS, HEAD_DIM), lambda b: (b, 0, 0))]
    for _ in caches7:
        in_specs.append(pl.BlockSpec((None, None, BAND, None, 2, N_HEADS, HEAD_DIM),
                                     lambda b: (layer, b, 0, 0, 0, 0, 0)))
    in_specs.append(pl.BlockSpec((None, None, MEM_LEN, 2, N_HEADS, HEAD_DIM),
                                 lambda b: (layer, b, 0, 0, 0, 0)))
    return pl.pallas_call(
        _sample_attn_kernel,
        out_shape=jax.ShapeDtypeStruct((bsz, 2 * N_HEADS, HEAD_DIM), F32),
        grid=(bsz,),
        in_specs=in_specs,
        out_specs=pl.BlockSpec((None, 2 * N_HEADS, HEAD_DIM), lambda b: (b, 0, 0)),
        compiler_params=pltpu.CompilerParams(dimension_semantics=("arbitrary",)),
        name="sample_attention",
    )(zs3, *caches7, cache_mem)


def _sample_out_kernel(zs_ref, st_ref, ydm_ref, pw_ref, ps_ref, wo_ref, x_ref, o_ref, ns_ref, y_ref):
    u = zs_ref[:, pl.ds(SEG_U * SEG, SEG)]
    for gi, w in enumerate(POOL_WINDOWS):
        sl = slice(gi * HEAD_DIM, (gi + 1) * HEAD_DIM)
        tot = u[:, sl]
        for back in range(1, w):
            tot = tot + st_ref[POOL_BUF - back, :, sl]
        cnt = float(min(w, PAST_LEN + 1))
        dlt = tot / cnt - u[:, sl]
        yp = jnp.dot(dlt.astype(BF16), pw_ref[gi], preferred_element_type=F32) * ps_ref[:, sl]
        y_ref[:, sl] = (zs_ref[:, pl.ds(SEG_GATE_POOL * SEG + gi * HEAD_DIM, HEAD_DIM)] * yp).astype(BF16)
    y_ref[:, SEG:2 * SEG] = (zs_ref[:, pl.ds(SEG_GATE_DIL * SEG, SEG)] * ydm_ref[:, pl.ds(0, SEG)]).astype(BF16)
    y_ref[:, 2 * SEG:] = (zs_ref[:, pl.ds(SEG_GATE_MEM * SEG, SEG)] * ydm_ref[:, pl.ds(SEG, SEG)]).astype(BF16)
    o_ref[...] = x_ref[...] + jnp.dot(y_ref[...], wo_ref[...], preferred_element_type=F32)
    for r in range(POOL_BUF - 1):
        ns_ref[r] = st_ref[r + 1]
    ns_ref[POOL_BUF - 1] = u


def _sample_out(zs, state_t, ydm, pool_w, pool_scale, w_out, xs, layer):
    rows = zs.shape[0]
    full = lambda shape: pl.BlockSpec(shape, lambda i: tuple(0 for _ in shape))
    of_layer = lambda a: pl.BlockSpec((None,) + a.shape[1:], lambda i: (layer,) + (0,) * (a.ndim - 1))
    return pl.pallas_call(
        _sample_out_kernel,
        out_shape=[jax.ShapeDtypeStruct(xs.shape, F32), jax.ShapeDtypeStruct(state_t.shape, F32)],
        grid=(1,),
        in_specs=[full(zs.shape), full(state_t.shape), full(ydm.shape), of_layer(pool_w),
                  full(pool_scale.shape), of_layer(w_out), full(xs.shape)],
        out_specs=[full(xs.shape), full(state_t.shape)],
        scratch_shapes=[pltpu.VMEM((rows, MIX_WIDTH), BF16)],
        compiler_params=pltpu.CompilerParams(vmem_limit_bytes=VMEM_LIMIT_BYTES),
        name="sample_out",
    )(zs, state_t, ydm, pool_w, pool_scale, w_out, xs)


def _cache_shift_kernel(c_ref, nxt_ref, new_ref, o_ref, *, n_blocks):
    i = pl.program_id(2)
    rows = c_ref.shape[0]
    chunk = min(rows, SHIFT_CHUNK)

    def copy(c, carry):
        r0 = pl.multiple_of(c * chunk, chunk)
        o_ref[pl.ds(r0, chunk)] = c_ref[pl.ds(r0 + 1, chunk)]
        return carry

    lax.fori_loop(0, rows // chunk - 1, copy, 0)
    o_ref[pl.ds(rows - chunk, chunk - 1)] = c_ref[pl.ds(rows - chunk + 1, chunk - 1)]

    @pl.when(i < n_blocks - 1)
    def _():
        o_ref[rows - 1] = nxt_ref[0]

    @pl.when(i == n_blocks - 1)
    def _():
        o_ref[rows - 1] = new_ref[...]


def _cache_shift(cache, new_rows):
    depth, bsz, win = cache.shape[:3]
    rows = min(win, SHIFT_ROWS)
    n_blocks = win // rows
    row_shape = cache.shape[3:]
    zeros = (0,) * len(row_shape)
    return pl.pallas_call(
        functools.partial(_cache_shift_kernel, n_blocks=n_blocks),
        out_shape=jax.ShapeDtypeStruct(cache.shape, cache.dtype),
        grid=(depth, bsz, n_blocks),
        in_specs=[
            pl.BlockSpec((None, None, rows) + row_shape, lambda l, b, i: (l, b, i) + zeros),
            pl.BlockSpec((None, None, 1) + row_shape,
                         lambda l, b, i: (l, b, jnp.minimum((i + 1) * rows, win - 1)) + zeros),
            pl.BlockSpec((None, None) + row_shape, lambda l, b, i: (l, b) + zeros),
        ],
        out_specs=pl.BlockSpec((None, None, rows) + row_shape, lambda l, b, i: (l, b, i) + zeros),
        compiler_params=pltpu.CompilerParams(
            dimension_semantics=("arbitrary", "arbitrary", "arbitrary"),
            vmem_limit_bytes=VMEM_LIMIT_BYTES),
        name="cache_shift",
    )(cache, cache, new_rows)


def _set_last_kernel(new_ref, buf_ref, o_ref):
    del buf_ref
    o_ref[0] = new_ref[...]


def _cache_set_last(buf, new_rows):
    depth, bsz, win = buf.shape[:3]
    row_shape = buf.shape[3:]
    zeros = (0,) * len(row_shape)
    return pl.pallas_call(
        _set_last_kernel,
        out_shape=jax.ShapeDtypeStruct(buf.shape, buf.dtype),
        grid=(depth, bsz),
        in_specs=[pl.BlockSpec((None, None) + row_shape, lambda l, b: (l, b) + zeros),
                  pl.BlockSpec(memory_space=pl.ANY)],
        out_specs=pl.BlockSpec((None, None, 1) + row_shape, lambda l, b: (l, b, win - 1) + zeros),
        input_output_aliases={1: 0},
        compiler_params=pltpu.CompilerParams(dimension_semantics=("arbitrary", "arbitrary")),
        name="cache_set_last",
    )(new_rows, buf)


def _tail_kernel(k_ref, v_ref, *rest):
    o_ref = rest[-1]
    for h in range(N_HEADS):
        sl = slice(h * HEAD_DIM, (h + 1) * HEAD_DIM)
        o_ref[:, 0, h, :] = k_ref[:, sl]
        o_ref[:, 1, h, :] = v_ref[:, sl]


def _assemble_tail(buf, k_tail, v_tail, layer, depth):
    bsz, win, _ = k_tail.shape
    rows = min(win, TAIL_ROWS)
    shape = (depth, bsz, win, 2, N_HEADS, HEAD_DIM)
    in_specs = [pl.BlockSpec((None, rows, SEG), lambda b, r: (b, r, 0)),
                pl.BlockSpec((None, rows, SEG), lambda b, r: (b, r, 0))]
    args = [k_tail, v_tail]
    aliases = {}
    if buf is not None:
        in_specs.append(pl.BlockSpec(memory_space=pl.ANY))
        args.append(buf)
        aliases = {2: 0}
    return pl.pallas_call(
        _tail_kernel,
        out_shape=jax.ShapeDtypeStruct(shape, F32),
        grid=(bsz, win // rows),
        in_specs=in_specs,
        out_specs=pl.BlockSpec((None, None, rows, 2, N_HEADS, HEAD_DIM),
                               lambda b, r: (layer, b, r, 0, 0, 0)),
        input_output_aliases=aliases,
        compiler_params=pltpu.CompilerParams(dimension_semantics=("arbitrary", "arbitrary")),
        name="assemble_tail",
    )(*args)


def _tile_heads(v):
    return jnp.tile(v, N_HEADS)


def kernel(x_prompt, x_sample, state_pool, cache_dil_w128, cache_dil_w512, cache_dil_w2048,
           cache_mem_kv, mem_prompt, norm_g, w_in, pool_w, pool_scale, dil_q_norm, dil_k_norm,
           mem_norm_g, w_mem_kv, mem_q_norm, mem_k_norm, w_out):
    depth = w_in.shape[0]
    bsz, seq, _ = x_prompt.shape
    dbsz = x_sample.shape[0]
    caches = (cache_dil_w128, cache_dil_w512, cache_dil_w2048)

    w_in_b = jnp.concatenate([w_in[:, :, s * SEG:(s + 1) * SEG] for s in Z_ORDER], axis=2).astype(BF16)
    w_out_b = w_out.astype(BF16)
    w_mem_b = w_mem_kv.astype(BF16)
    pool_w_b = pool_w.astype(BF16)

    ones = jnp.ones((SEG,), F32)
    seg_modes = [MODE_GATE if s in GATE_SEGS else MODE_PLAIN for s in range(N_SEG)]
    ep_mode = jnp.repeat(jnp.array(seg_modes, F32), SEG)[None, :]
    mem_mode = jnp.full((1, 2 * SEG), MODE_PLAIN, F32)
    caches7 =[c.reshape(depth, dbsz, win // d, d, 2, N_HEADS, HEAD_DIM)
               for c, (win, d) in zip(caches, DIL_PAIRS)]

    xp = x_prompt.reshape(bsz * seq, D_MODEL)
    xs = jnp.pad(x_sample.reshape(dbsz, D_MODEL), ((0, SAMPLE_ROWS - dbsz), (0, 0)))
    mem2 = mem_prompt.reshape(bsz * MEM_LEN, D_MODEL)

    pool_p, mem_p, pool_s, zs_rows = [], [], [], []
    prompt_caches = [None] * len(DIL_PAIRS)
    big_cache = None
    for l in range(depth):
        segs = [ones] * N_SEG
        for g in range(len(DIL_PAIRS)):
            segs[_seg_q(g)] = _tile_heads(dil_q_norm[l, g]) * ATTN_SCALE
            segs[_seg_k(g)] = _tile_heads(dil_k_norm[l, g])
        segs[SEG_QMEM] = _tile_heads(mem_q_norm[l]) * ATTN_SCALE
        ep_gain = jnp.concatenate(segs)[None, :]
        mem_gain = jnp.concatenate([_tile_heads(mem_k_norm[l]), ones])[None, :]
        gain = norm_g[l][None, :]
        pscale = pool_scale[l][None, :]

        z, big_cache = _project(xp, gain, w_in_b, l, ep_gain, ep_mode, rows=PROJ_ROWS,
                                cols=PROJ_COLS, name="proj_prompt", shift=(caches[-1], big_cache))
        z3 = z.reshape(bsz, seq, IN_COLS)
        mkv = _project(mem2, mem_norm_g[l][None, :], w_mem_b, l, mem_gain, mem_mode,
                       rows=bsz * MEM_LEN, cols=2 * SEG, name="proj_mem")
        mkv3 = mkv.reshape(bsz, MEM_LEN, 2 * SEG)
        attn = _dilated_attention(z3)
        y_dil = attn[0]
        for g in range(len(DIL_PAIRS)):
            prompt_caches[g] = _assemble_tail(prompt_caches[g], attn[1 + 2 * g], attn[2 + 2 * g],
                                              l, depth)
        xp = _mix(z3, y_dil, mkv3, pool_w_b, pscale, w_out_b,
                  xp.reshape(bsz, seq, D_MODEL), l).reshape(bsz * seq, D_MODEL)
        pool_p.append(z3[:, seq - POOL_BUF:, SEG_U * SEG:(SEG_U + 1) * SEG])
        mem_p.append(mkv3)

        zs = _project(xs, gain, w_in_b, l, ep_gain, ep_mode, rows=SAMPLE_ROWS, cols=PROJ_COLS,
                      name="proj_sample")
        zs3 = zs.reshape(SAMPLE_ROWS, N_SEG * N_HEADS, HEAD_DIM)
        ydm = _sample_attention(zs3, caches7, cache_mem_kv, l)
        ydm = jnp.pad(ydm.reshape(dbsz, 2 * SEG), ((0, SAMPLE_ROWS - dbsz), (0, 0)))
        state_t = jnp.pad(jnp.transpose(state_pool[l], (1, 0, 2)),
                          ((0, 0), (0, SAMPLE_ROWS - dbsz), (0, 0)))
        xs, new_state_t = _sample_out(zs, state_t, ydm, pool_w_b, pscale, w_out_b, xs, l)
        pool_s.append(jnp.transpose(new_state_t[:, :dbsz], (1, 0, 2)))
        zs_rows.append(zs[:dbsz].reshape(dbsz, N_SEG, N_HEADS, HEAD_DIM))

    zs_all = jnp.stack(zs_rows)
    new_rows = [zs_all[:, :, _seg_k(g):_seg_k(g) + 2] for g in range(len(DIL_PAIRS))]
    new_caches = [_cache_shift(caches[0], new_rows[0]), _cache_shift(caches[1], new_rows[1]),
                  _cache_set_last(big_cache, new_rows[2])]

    y_prompt = xp.reshape(bsz, seq, D_MODEL)
    y_sample = xs[:dbsz].reshape(dbsz, 1, D_MODEL)
    cache_mem_prompt = jnp.stack(mem_p).reshape(depth, bsz, MEM_LEN, 2, N_HEADS, HEAD_DIM)
    return (y_prompt, y_sample, jnp.stack(pool_p), prompt_caches[0], prompt_caches[1],
            prompt_caches[2], cache_mem_prompt, jnp.stack(pool_s), new_caches[0], new_caches[1],
            new_caches[2])
```

```python
import functools

import jax
import jax.numpy as jnp
from jax import lax
from jax.experimental import pallas as pl
from jax.experimental.pallas import tpu as pltpu

F32 = jnp.float32
BF16 = jnp.bfloat16

D_MODEL = 2048
HEAD_DIM = 128
N_HEADS = 4
SEG = N_HEADS * HEAD_DIM
N_SEG = 14
IN_COLS = N_SEG * SEG
POOL_WINDOWS = (2, 4, 8, 16)
POOL_BUF = 15
POOL_HALO = 16
POOL_PAD = 8
DIL_PAIRS = ((128, 1), (512, 4), (2048, 16))
BAND = 128
MEM_LEN = 256
MIX_WIDTH = 3 * SEG
EPS = 1e-6
ATTN_SCALE = HEAD_DIM ** -0.5
PAST_LEN = 16384

Z_ORDER = (2, 0, 3, 4, 5, 1, 6, 7, 8, 11, 9, 10, 12, 13)
SEG_U, SEG_GATE_POOL, SEG_GATE_DIL, SEG_QMEM, SEG_GATE_MEM = 1, 5, 9, 12, 13
GATE_SEGS = (SEG_GATE_POOL, SEG_GATE_DIL, SEG_GATE_MEM)


def _seg_q(g):
    return 4 * g


def _seg_k(g):
    return 4 * g + 2

VMEM_LIMIT_BYTES = 56 * 1024 * 1024

PROJ_ROWS = 1024
ATTN_ROWS = 2048
MIX_ROWS = 512
SAMPLE_ROWS = 16
SHIFT_ROWS = 512
TAIL_ROWS = 512
SHIFT_CHUNK = 64


MODE_PLAIN, MODE_GATE = 0.0, 1.0


def _shift_rows(c_ref, nxt_ref, o_ref):
    rows = c_ref.shape[0]
    chunk = min(rows, SHIFT_CHUNK)
    for lo in range(0, rows, chunk):
        n = chunk if lo + chunk < rows else chunk - 1
        o_ref[pl.ds(lo, n)] = c_ref[pl.ds(lo + 1, n)]
    o_ref[rows - 1] = nxt_ref[0]


def _shift_operands(cache, buf, layer, step_of, n_steps, first_step=0):
    bsz, win = cache.shape[1:3]
    rows = min(win, SHIFT_ROWS)
    row_shape = cache.shape[3:]
    zeros = (0,) * len(row_shape)
    per_b = win // rows
    n_shift = bsz * per_b
    assert first_step + n_shift <= n_steps

    def block_of(*idx):
        t = jnp.clip(step_of(*idx) - first_step, 0, n_shift - 1)
        return t // per_b, t % per_b

    def main_map(*idx):
        b, r = block_of(*idx)
        return (layer, b, r) + zeros

    def next_map(*idx):
        b, r = block_of(*idx)
        return (layer, b, jnp.minimum((r + 1) * rows, win - 1)) + zeros

    in_specs = [pl.BlockSpec((None, None, rows) + row_shape, main_map),
                pl.BlockSpec((None, None, 1) + row_shape, next_map)]
    args = [cache, cache]
    alias = None
    if buf is not None:
        in_specs.append(pl.BlockSpec(memory_space=pl.ANY))
        args.append(buf)
        alias = 2
    out_shape = jax.ShapeDtypeStruct(cache.shape, cache.dtype)
    out_spec = pl.BlockSpec((None, None, rows) + row_shape, main_map)
    return in_specs, args, out_shape, out_spec, alias


def _lookup(j, table):
    out = table[0]
    for t in range(1, len(table)):
        out = jnp.where(j == t, table[t], out)
    return out


def _proj_kernel(x_ref, g_ref, wa_ref, wb_ref, eg_ref, em_ref, *rest, with_shift, aliased):
    if with_shift:
        c_ref, nxt_ref = rest[:2]
        o_ref, co_ref, h_ref = rest[3:] if aliased else rest[2:]
    else:
        o_ref, h_ref = rest
    j = pl.program_id(1)

    @pl.when(j == 0)
    def _():
        x = x_ref[...]
        ms = jnp.mean(x * x, axis=-1, keepdims=True)
        h_ref[...] = (x * lax.rsqrt(ms + EPS) * g_ref[...]).astype(BF16)

    if with_shift:
        _shift_rows(c_ref, nxt_ref, co_ref)
    h_tile = h_ref[...]
    acc = jnp.dot(h_tile, wa_ref[...], preferred_element_type=F32)
    for h in range(N_HEADS):
        sl = slice(h * HEAD_DIM, (h + 1) * HEAD_DIM)
        a = acc[:, sl]
        o_ref[:, sl] = a * (lax.rsqrt(jnp.mean(a * a, axis=-1, keepdims=True) + EPS) * eg_ref[:, sl])
    acc = jnp.dot(h_tile, wb_ref[...], preferred_element_type=F32)
    for h in range(N_HEADS):
        sl = slice(h * HEAD_DIM, (h + 1) * HEAD_DIM)
        osl = slice(SEG + h * HEAD_DIM, SEG + (h + 1) * HEAD_DIM)
        a = acc[:, sl]
        o_ref[:, osl] = a * jnp.where(em_ref[:, osl] == MODE_GATE, jax.nn.sigmoid(a), 1.0)


def _project(x, gain, w, layer, seg_pairs, ep_gain, ep_mode, *, rows, name, shift=None):
    m, k = x.shape
    n_j = len(seg_pairs)
    cols = 2 * SEG
    n = n_j * cols
    firsts = tuple(p[0] for p in seg_pairs)
    seconds = tuple(p[1] for p in seg_pairs)
    in_specs = [
        pl.BlockSpec((rows, k), lambda i, j: (i, 0)),
        pl.BlockSpec((1, k), lambda i, j: (0, 0)),
        pl.BlockSpec((None, k, SEG), lambda i, j: (layer, 0, _lookup(j, firsts))),
        pl.BlockSpec((None, k, SEG), lambda i, j: (layer, 0, _lookup(j, seconds))),
        pl.BlockSpec((1, cols), lambda i, j: (0, j)),
        pl.BlockSpec((1, cols), lambda i, j: (0, j)),
    ]
    args = [x, gain, w, w, ep_gain, ep_mode]
    out_shape = [jax.ShapeDtypeStruct((m, n), F32)]
    out_specs = [pl.BlockSpec((rows, cols), lambda i, j: (i, j))]
    aliases = {}
    if shift is not None:
        s_specs, s_args, s_shape, s_spec, s_alias = _shift_operands(
            shift[0], shift[1], layer, lambda i, j: i * n_j + j, (m // rows) * n_j)
        if s_alias is not None:
            aliases = {len(args) + s_alias: 1}
        in_specs += s_specs
        args += s_args
        out_shape.append(s_shape)
        out_specs.append(s_spec)
    res = pl.pallas_call(
        functools.partial(_proj_kernel, with_shift=shift is not None,
                          aliased=shift is not None and shift[1] is not None),
        out_shape=out_shape,
        grid=(m // rows, n_j),
        in_specs=in_specs,
        out_specs=out_specs,
        scratch_shapes=[pltpu.VMEM((rows, k), BF16)],
        input_output_aliases=aliases,
        compiler_params=pltpu.CompilerParams(
            dimension_semantics=("arbitrary", "arbitrary"),
            vmem_limit_bytes=VMEM_LIMIT_BYTES),
        name=name,
    )(*args)
    return res if shift is not None else res[0]


def _dilattn_kernel(*refs, n_tiles):
    ins, outs, scratch = refs[:15], refs[15:22], refs[22:]
    y_ref = outs[0]
    tails = outs[1:]
    og_ref, lse_ref = scratch
    i = pl.program_id(2)

    row = lax.broadcasted_iota(jnp.int32, (BAND, 2 * BAND), 0)
    col = lax.broadcasted_iota(jnp.int32, (BAND, 2 * BAND), 1)
    not_future = col <= row + BAND
    band = jnp.logical_and(col >= row, not_future)
    first_lo = jnp.maximum(row, (i == 0).astype(jnp.int32) * BAND)
    band_first = jnp.logical_and(col >= first_lo, not_future)
    ones = jnp.ones((2 * BAND, HEAD_DIM), BF16)
    nt_dims = (((1,), (1,)), ((), ()))

    for g, (win, d) in enumerate(DIL_PAIRS):
        q_ref, kc_ref, vc_ref, kp_ref, vp_ref = ins[5 * g:5 * g + 5]
        span = BAND * d

        def rows_of(start, size, d=d):
            return pl.ds(start, size) if d == 1 else pl.ds(start, size, stride=d)

        for t in range(ATTN_ROWS // BAND):
            u, c = divmod(t, d)
            base = u * span + c
            q = q_ref[rows_of(base, BAND), :].astype(BF16)
            if u == 0:
                k = jnp.concatenate([kp_ref[rows_of(c, BAND), :], kc_ref[rows_of(c, BAND), :]], axis=0)
                v = jnp.concatenate([vp_ref[rows_of(c, BAND), :], vc_ref[rows_of(c, BAND), :]], axis=0)
            else:
                k = kc_ref[rows_of(base - span, 2 * BAND), :]
                v = vc_ref[rows_of(base - span, 2 * BAND), :]
            s = lax.dot_general(q, k.astype(BF16), nt_dims, preferred_element_type=F32)
            s = jnp.where(band_first if u == 0 else band, s, -jnp.inf)
            m = jnp.max(jnp.maximum(s[:, :BAND], s[:, BAND:]), axis=-1, keepdims=True)
            p = jnp.exp(s - m).astype(BF16)
            ov = jnp.dot(p, jnp.concatenate([v.astype(BF16), ones], axis=1), preferred_element_type=F32)
            l = ov[:, HEAD_DIM:]
            o_rows = rows_of(g * ATTN_ROWS + base, BAND)
            og_ref[o_rows, :] = ov[:, :HEAD_DIM] / l
            lse_ref[o_rows, :] = m + jnp.log(l)

        @pl.when(i == n_tiles - 1)
        def _(g=g, win=win, kc_ref=kc_ref, vc_ref=vc_ref):
            tails[2 * g][...] = kc_ref[pl.ds(ATTN_ROWS - win, win), :]
            tails[2 * g + 1][...] = vc_ref[pl.ds(ATTN_ROWS - win, win), :]

    chunk = 256

    def merge(r, carry):
        r0 = pl.multiple_of(r * chunk, chunk)
        sls = [pl.ds(g * ATTN_ROWS + r0, chunk) for g in range(len(DIL_PAIRS))]
        l0, l1, l2 = [lse_ref[sl, :] for sl in sls]
        mx = jnp.maximum(jnp.maximum(l0, l1), l2)
        e0, e1, e2 = jnp.exp(l0 - mx), jnp.exp(l1 - mx), jnp.exp(l2 - mx)
        num = e0 * og_ref[sls[0], :] + e1 * og_ref[sls[1], :] + e2 * og_ref[sls[2], :]
        y_ref[pl.ds(r0, chunk), :] = num / (e0 + e1 + e2)
        return carry

    lax.fori_loop(0, ATTN_ROWS // chunk, merge, 0)


def _dilated_attention(z3):
    bsz, seq, _ = z3.shape
    n_tiles = seq // ATTN_ROWS
    in_specs = []
    for g, (win, d) in enumerate(DIL_PAIRS):
        span = BAND * d
        qb = _seg_q(g) * N_HEADS
        kb = _seg_k(g) * N_HEADS
        vb = kb + N_HEADS
        per_tile = ATTN_ROWS // span

        def prev_map(col0, per_tile=per_tile):
            return lambda b, h, i: (b, jnp.maximum(i * per_tile - 1, 0), col0 + h)

        def cur_map(col0):
            return lambda b, h, i: (b, i, col0 + h)

        in_specs += [
            pl.BlockSpec((None, ATTN_ROWS, HEAD_DIM), cur_map(qb)),
            pl.BlockSpec((None, ATTN_ROWS, HEAD_DIM), cur_map(kb)),
            pl.BlockSpec((None, ATTN_ROWS, HEAD_DIM), cur_map(vb)),
            pl.BlockSpec((None, span, HEAD_DIM), prev_map(kb)),
            pl.BlockSpec((None, span, HEAD_DIM), prev_map(vb)),
        ]
    out_shape = [jax.ShapeDtypeStruct((bsz, seq, SEG), F32)]
    out_specs = [pl.BlockSpec((None, ATTN_ROWS, HEAD_DIM), lambda b, h, i: (b, i, h))]
    for win, _ in DIL_PAIRS:
        for _kv in range(2):
            out_shape.append(jax.ShapeDtypeStruct((bsz, win, SEG), F32))
            out_specs.append(pl.BlockSpec((None, win, HEAD_DIM), lambda b, h, i: (b, 0, h)))
    return pl.pallas_call(
        functools.partial(_dilattn_kernel, n_tiles=n_tiles),
        out_shape=out_shape,
        grid=(bsz, N_HEADS, n_tiles),
        in_specs=in_specs,
        out_specs=out_specs,
        scratch_shapes=[
            pltpu.VMEM((len(DIL_PAIRS) * ATTN_ROWS, HEAD_DIM), F32),
            pltpu.VMEM((len(DIL_PAIRS) * ATTN_ROWS, HEAD_DIM), F32),
        ],
        compiler_params=pltpu.CompilerParams(
            dimension_semantics=("arbitrary", "arbitrary", "arbitrary"),
            vmem_limit_bytes=VMEM_LIMIT_BYTES),
        name="dilated_attention",
    )(*([z3] * 15))


def _mix_kernel(u_ref, up_ref, gp_ref, gd_ref, qm_ref, gm_ref, yd_ref, mkv_ref, pw_ref, ps_ref,
                wo_ref, x_ref, *rest, n_aliased):
    shift_in, rest = rest[:4 + n_aliased], rest[4 + n_aliased:]
    o_ref, co0_ref, co1_ref, ue_ref, t_ref, y_ref = rest
    per = 2 + n_aliased // 2
    i = pl.program_id(1)
    rows = u_ref.shape[0]

    pad, halo = POOL_PAD, POOL_HALO
    n = halo + rows

    @pl.when(i > 0)
    def _():
        ue_ref[pl.ds(pad, halo), :] = up_ref[...]

    @pl.when(i == 0)
    def _():
        ue_ref[pl.ds(pad, halo), :] = jnp.zeros((halo, SEG), F32)

    ue_ref[pl.ds(0, pad), :] = jnp.zeros((pad, SEG), F32)
    t_ref[0, pl.ds(0, pad), :] = jnp.zeros((pad, HEAD_DIM), F32)
    t_ref[1, pl.ds(0, pad), :] = jnp.zeros((pad, HEAD_DIM), F32)
    ue_ref[pl.ds(pad + halo, rows), :] = u_ref[...]
    pos = i * rows + lax.broadcasted_iota(jnp.int32, (rows, 1), 0)
    for gi, w in enumerate(POOL_WINDOWS):
        sl = slice(gi * HEAD_DIM, (gi + 1) * HEAD_DIM)
        cur = ue_ref[pl.ds(pad, n), sl] + ue_ref[pl.ds(pad - 1, n), sl]
        shift, buf = 2, 0
        while shift < w:
            t_ref[buf, pl.ds(pad, n), :] = cur
            cur = cur + t_ref[buf, pl.ds(pad - shift, n), :]
            shift, buf = 2 * shift, 1 - buf
        cnt = jnp.minimum(w, pos + 1).astype(F32)
        dlt = cur[halo:] / cnt - u_ref[:, sl]
        yp = jnp.dot(dlt.astype(BF16), pw_ref[gi], preferred_element_type=F32) * ps_ref[:, sl]
        y_ref[:, sl] = (gp_ref[:, sl] * yp).astype(BF16)

    y_ref[:, SEG:2 * SEG] = (gd_ref[...] * yd_ref[...]).astype(BF16)

    nt_dims = (((1,), (1,)), ((), ()))
    ones = jnp.ones((MEM_LEN, HEAD_DIM), BF16)
    for h in range(N_HEADS):
        sl = slice(h * HEAD_DIM, (h + 1) * HEAD_DIM)
        q = qm_ref[:, sl].astype(BF16)
        k = mkv_ref[:, sl].astype(BF16)
        v = mkv_ref[:, SEG + h * HEAD_DIM:SEG + (h + 1) * HEAD_DIM].astype(BF16)
        s = lax.dot_general(q, k, nt_dims, preferred_element_type=F32)
        m = jnp.max(s, axis=-1, keepdims=True)
        p = jnp.exp(s - m).astype(BF16)
        ov = jnp.dot(p, jnp.concatenate([v, ones], axis=1), preferred_element_type=F32)
        o = ov[:, :HEAD_DIM] / ov[:, HEAD_DIM:]
        y_ref[:, 2 * SEG + h * HEAD_DIM:2 * SEG + (h + 1) * HEAD_DIM] = (gm_ref[:, sl] * o).astype(BF16)

    _shift_rows(shift_in[0], shift_in[1], co0_ref)
    _shift_rows(shift_in[per], shift_in[per + 1], co1_ref)
    o_ref[...] = x_ref[...] + jnp.dot(y_ref[...], wo_ref[...], preferred_element_type=F32)


def _mix(z3, y_dil, mem_kv, pool_w, pool_scale, w_out, x3, layer, shifts):
    bsz, seq, _ = z3.shape
    rows = MIX_ROWS
    n_tiles = seq // rows
    halo_per_tile = rows // POOL_HALO

    def seg_spec(seg):
        return pl.BlockSpec((None, rows, SEG), lambda b, i: (b, i, seg))

    n_main = 12
    shift_specs, shift_args, shift_shapes, shift_out_specs, aliases = [], [], [], [], {}
    first_step = 0
    for n, (cache, buf) in enumerate(shifts):
        s_specs, s_args, s_shape, s_spec, s_alias = _shift_operands(
            cache, buf, layer, lambda b, i: b * n_tiles + i, bsz * n_tiles, first_step)
        first_step += cache.shape[1] * (cache.shape[2] // min(cache.shape[2], SHIFT_ROWS))
        if s_alias is not None:
            aliases[n_main + len(shift_args) + s_alias] = 1 + n
        shift_specs += s_specs
        shift_args += s_args
        shift_shapes.append(s_shape)
        shift_out_specs.append(s_spec)

    return pl.pallas_call(
        functools.partial(_mix_kernel, n_aliased=len(aliases)),
        out_shape=[jax.ShapeDtypeStruct(x3.shape, F32)] + shift_shapes,
        grid=(bsz, n_tiles),
        in_specs=[
            seg_spec(SEG_U),
            pl.BlockSpec((None, POOL_HALO, SEG),
                         lambda b, i: (b, jnp.maximum(i * halo_per_tile - 1, 0), SEG_U)),
            seg_spec(SEG_GATE_POOL),
            seg_spec(SEG_GATE_DIL),
            seg_spec(SEG_QMEM),
            seg_spec(SEG_GATE_MEM),
            pl.BlockSpec((None, rows, SEG), lambda b, i: (b, i, 0)),
            pl.BlockSpec((None, MEM_LEN, 2 * SEG), lambda b, i: (b, 0, 0)),
            pl.BlockSpec((None, len(POOL_WINDOWS), HEAD_DIM, HEAD_DIM), lambda b, i: (layer, 0, 0, 0)),
            pl.BlockSpec((1, SEG), lambda b, i: (0, 0)),
            pl.BlockSpec((None, MIX_WIDTH, D_MODEL), lambda b, i: (layer, 0, 0),
                         pipeline_mode=pl.Buffered(1)),
            pl.BlockSpec((None, rows, D_MODEL), lambda b, i: (b, i, 0)),
        ] + shift_specs,
        out_specs=[pl.BlockSpec((None, rows, D_MODEL), lambda b, i: (b, i, 0))] + shift_out_specs,
        scratch_shapes=[
            pltpu.VMEM((POOL_PAD + POOL_HALO + rows, SEG), F32),
            pltpu.VMEM((2, POOL_PAD + POOL_HALO + rows, HEAD_DIM), F32),
            pltpu.VMEM((rows, MIX_WIDTH), BF16),
        ],
        input_output_aliases=aliases,
        compiler_params=pltpu.CompilerParams(
            dimension_semantics=("arbitrary", "arbitrary"),
            vmem_limit_bytes=VMEM_LIMIT_BYTES),
        name="mix",
    )(z3, z3, z3, z3, z3, z3, y_dil, mem_kv, pool_w, pool_scale, w_out, x3, *shift_args)


def _sample_attn_kernel(zs_ref, c0_ref, c1_ref, c2_ref, cm_ref, o_ref):
    outs, lses = [], []
    for g, c_ref in enumerate((c0_ref, c1_ref, c2_ref)):
        q = zs_ref[pl.ds(_seg_q(g) * N_HEADS, N_HEADS), :]
        k_new = zs_ref[pl.ds(_seg_k(g) * N_HEADS, N_HEADS), :]
        v_new = zs_ref[pl.ds((_seg_k(g) + 1) * N_HEADS, N_HEADS), :]
        k = c_ref[:, 0]
        v = c_ref[:, 1]
        s = jnp.sum(k * q[None], axis=-1, keepdims=True)
        s_new = jnp.sum(k_new * q, axis=-1, keepdims=True)
        m = jnp.maximum(jnp.max(s, axis=0), s_new)
        p = jnp.exp(s - m[None])
        p_new = jnp.exp(s_new - m)
        l = jnp.sum(p, axis=0) + p_new
        outs.append((jnp.sum(p * v, axis=0) + p_new * v_new) / l)
        lses.append(m + jnp.log(l))
    mx = jnp.maximum(jnp.maximum(lses[0], lses[1]), lses[2])
    es = [jnp.exp(x - mx) for x in lses]
    o_ref[pl.ds(0, N_HEADS), :] = (es[0] * outs[0] + es[1] * outs[1] + es[2] * outs[2]) / (es[0] + es[1] + es[2])

    q = zs_ref[pl.ds(SEG_QMEM * N_HEADS, N_HEADS), :]
    k = cm_ref[:, 0]
    v = cm_ref[:, 1]
    s = jnp.sum(k * q[None], axis=-1, keepdims=True)
    m = jnp.max(s, axis=0)
    p = jnp.exp(s - m[None])
    o_ref[pl.ds(N_HEADS, N_HEADS), :] = jnp.sum(p * v, axis=0) / jnp.sum(p, axis=0)


def _sample_attention(zs3, caches7, cache_mem, layer):
    bsz = cache_mem.shape[1]
    in_specs = [pl.BlockSpec((None, N_SEG * N_HEADS, HEAD_DIM), lambda b: (b, 0, 0))]
    for _ in caches7:
        in_specs.append(pl.BlockSpec((None, None, BAND, None, 2, N_HEADS, HEAD_DIM),
                                     lambda b: (layer, b, 0, 0, 0, 0, 0)))
    in_specs.append(pl.BlockSpec((None, None, MEM_LEN, 2, N_HEADS, HEAD_DIM),
                                 lambda b: (layer, b, 0, 0, 0, 0)))
    return pl.pallas_call(
        _sample_attn_kernel,
        out_shape=jax.ShapeDtypeStruct((bsz, 2 * N_HEADS, HEAD_DIM), F32),
        grid=(bsz,),
        in_specs=in_specs,
        out_specs=pl.BlockSpec((None, 2 * N_HEADS, HEAD_DIM), lambda b: (b, 0, 0)),
        compiler_params=pltpu.CompilerParams(dimension_semantics=("arbitrary",)),
        name="sample_attention",
    )(zs3, *caches7, cache_mem)


def _sample_out_kernel(zs_ref, st_ref, ydm_ref, pw_ref, ps_ref, wo_ref, x_ref, o_ref, ns_ref, y_ref):
    u = zs_ref[:, pl.ds(SEG_U * SEG, SEG)]
    for gi, w in enumerate(POOL_WINDOWS):
        sl = slice(gi * HEAD_DIM, (gi + 1) * HEAD_DIM)
        tot = u[:, sl]
        for back in range(1, w):
            tot = tot + st_ref[POOL_BUF - back, :, sl]
        cnt = float(min(w, PAST_LEN + 1))
        dlt = tot / cnt - u[:, sl]
        yp = jnp.dot(dlt.astype(BF16), pw_ref[gi], preferred_element_type=F32) * ps_ref[:, sl]
        y_ref[:, sl] = (zs_ref[:, pl.ds(SEG_GATE_POOL * SEG + gi * HEAD_DIM, HEAD_DIM)] * yp).astype(BF16)
    y_ref[:, SEG:2 * SEG] = (zs_ref[:, pl.ds(SEG_GATE_DIL * SEG, SEG)] * ydm_ref[:, pl.ds(0, SEG)]).astype(BF16)
    y_ref[:, 2 * SEG:] = (zs_ref[:, pl.ds(SEG_GATE_MEM * SEG, SEG)] * ydm_ref[:, pl.ds(SEG, SEG)]).astype(BF16)
    o_ref[...] = x_ref[...] + jnp.dot(y_ref[...], wo_ref[...], preferred_element_type=F32)
    for r in range(POOL_BUF - 1):
        ns_ref[r] = st_ref[r + 1]
    ns_ref[POOL_BUF - 1] = u


def _sample_out(zs, state_t, ydm, pool_w, pool_scale, w_out, xs, layer):
    rows = zs.shape[0]
    full = lambda shape: pl.BlockSpec(shape, lambda i: tuple(0 for _ in shape))
    of_layer = lambda a: pl.BlockSpec((None,) + a.shape[1:], lambda i: (layer,) + (0,) * (a.ndim - 1))
    return pl.pallas_call(
        _sample_out_kernel,
        out_shape=[jax.ShapeDtypeStruct(xs.shape, F32), jax.ShapeDtypeStruct(state_t.shape, F32)],
        grid=(1,),
        in_specs=[full(zs.shape), full(state_t.shape), full(ydm.shape), of_layer(pool_w),
                  full(pool_scale.shape), of_layer(w_out), full(xs.shape)],
        out_specs=[full(xs.shape), full(state_t.shape)],
        scratch_shapes=[pltpu.VMEM((rows, MIX_WIDTH), BF16)],
        compiler_params=pltpu.CompilerParams(vmem_limit_bytes=VMEM_LIMIT_BYTES),
        name="sample_out",
    )(zs, state_t, ydm, pool_w, pool_scale, w_out, xs)


def _set_last_kernel(new_ref, buf_ref, o_ref):
    del buf_ref
    o_ref[0] = new_ref[...]


def _cache_set_last(buf, new_rows):
    depth, bsz, win = buf.shape[:3]
    row_shape = buf.shape[3:]
    zeros = (0,) * len(row_shape)
    return pl.pallas_call(
        _set_last_kernel,
        out_shape=jax.ShapeDtypeStruct(buf.shape, buf.dtype),
        grid=(depth, bsz),
        in_specs=[pl.BlockSpec((None, None) + row_shape, lambda l, b: (l, b) + zeros),
                  pl.BlockSpec(memory_space=pl.ANY)],
        out_specs=pl.BlockSpec((None, None, 1) + row_shape, lambda l, b: (l, b, win - 1) + zeros),
        input_output_aliases={1: 0},
        compiler_params=pltpu.CompilerParams(dimension_semantics=("arbitrary", "arbitrary")),
        name="cache_set_last",
    )(new_rows, buf)


def _tail_kernel(k_ref, v_ref, *rest):
    o_ref = rest[-1]
    for h in range(N_HEADS):
        sl = slice(h * HEAD_DIM, (h + 1) * HEAD_DIM)
        o_ref[:, 0, h, :] = k_ref[:, sl]
        o_ref[:, 1, h, :] = v_ref[:, sl]


def _assemble_tail(buf, k_tail, v_tail, layer, depth):
    bsz, win, _ = k_tail.shape
    rows = min(win, TAIL_ROWS)
    shape = (depth, bsz, win, 2, N_HEADS, HEAD_DIM)
    in_specs = [pl.BlockSpec((None, rows, SEG), lambda b, r: (b, r, 0)),
                pl.BlockSpec((None, rows, SEG), lambda b, r: (b, r, 0))]
    args = [k_tail, v_tail]
    aliases = {}
    if buf is not None:
        in_specs.append(pl.BlockSpec(memory_space=pl.ANY))
        args.append(buf)
        aliases = {2: 0}
    return pl.pallas_call(
        _tail_kernel,
        out_shape=jax.ShapeDtypeStruct(shape, F32),
        grid=(bsz, win // rows),
        in_specs=in_specs,
        out_specs=pl.BlockSpec((None, None, rows, 2, N_HEADS, HEAD_DIM),
                               lambda b, r: (layer, b, r, 0, 0, 0)),
        input_output_aliases=aliases,
        compiler_params=pltpu.CompilerParams(dimension_semantics=("arbitrary", "arbitrary")),
        name="assemble_tail",
    )(*args)


def _tile_heads(v):
    return jnp.tile(v, N_HEADS)


def kernel(x_prompt, x_sample, state_pool, cache_dil_w128, cache_dil_w512, cache_dil_w2048,
           cache_mem_kv, mem_prompt, norm_g, w_in, pool_w, pool_scale, dil_q_norm, dil_k_norm,
           mem_norm_g, w_mem_kv, mem_q_norm, mem_k_norm, w_out):
    depth = w_in.shape[0]
    bsz, seq, _ = x_prompt.shape
    dbsz = x_sample.shape[0]
    caches = (cache_dil_w128, cache_dil_w512, cache_dil_w2048)

    w_in_b = w_in.astype(BF16)
    z_pairs = tuple(zip(Z_ORDER[0::2], Z_ORDER[1::2]))
    w_out_b = w_out.astype(BF16)
    w_mem_b = w_mem_kv.astype(BF16)
    pool_w_b = pool_w.astype(BF16)

    ones = jnp.ones((SEG,), F32)
    seg_modes = [MODE_GATE if s in GATE_SEGS else MODE_PLAIN for s in range(N_SEG)]
    ep_mode = jnp.repeat(jnp.array(seg_modes, F32), SEG)[None, :]
    mem_mode = jnp.full((1, 2 * SEG), MODE_PLAIN, F32)
    caches7 =[c.reshape(depth, dbsz, win // d, d, 2, N_HEADS, HEAD_DIM)
               for c, (win, d) in zip(caches, DIL_PAIRS)]

    xp = x_prompt.reshape(bsz * seq, D_MODEL)
    xs = jnp.pad(x_sample.reshape(dbsz, D_MODEL), ((0, SAMPLE_ROWS - dbsz), (0, 0)))
    mem2 = mem_prompt.reshape(bsz * MEM_LEN, D_MODEL)

    pool_p, mem_p, pool_s, zs_rows = [], [], [], []
    prompt_caches = [None] * len(DIL_PAIRS)
    big_cache = None
    small_caches = [None, None]
    for l in range(depth):
        segs = [ones] * N_SEG
        for g in range(len(DIL_PAIRS)):
            segs[_seg_q(g)] = _tile_heads(dil_q_norm[l, g]) * ATTN_SCALE
            segs[_seg_k(g)] = _tile_heads(dil_k_norm[l, g])
        segs[SEG_QMEM] = _tile_heads(mem_q_norm[l]) * ATTN_SCALE
        ep_gain = jnp.concatenate(segs)[None, :]
        mem_gain = jnp.concatenate([_tile_heads(mem_k_norm[l]), ones])[None, :]
        gain = norm_g[l][None, :]
        pscale = pool_scale[l][None, :]

        z, big_cache = _project(xp, gain, w_in_b, l, z_pairs, ep_gain, ep_mode, rows=PROJ_ROWS,
                                name="proj_prompt", shift=(caches[-1], big_cache))
        z3 = z.reshape(bsz, seq, IN_COLS)
        mkv = _project(mem2, mem_norm_g[l][None, :], w_mem_b, l, ((0, 1),), mem_gain, mem_mode,
                       rows=bsz * MEM_LEN, name="proj_mem")
        mkv3 = mkv.reshape(bsz, MEM_LEN, 2 * SEG)
        attn = _dilated_attention(z3)
        y_dil = attn[0]
        for g in range(len(DIL_PAIRS)):
            prompt_caches[g] = _assemble_tail(prompt_caches[g], attn[1 + 2 * g], attn[2 + 2 * g],
                                              l, depth)
        xp3, small_caches[1], small_caches[0] = _mix(
            z3, y_dil, mkv3, pool_w_b, pscale, w_out_b, xp.reshape(bsz, seq, D_MODEL), l,
            ((caches[1], small_caches[1]), (caches[0], small_caches[0])))
        xp = xp3.reshape(bsz * seq, D_MODEL)
        pool_p.append(z3[:, seq - POOL_BUF:, SEG_U * SEG:(SEG_U + 1) * SEG])
        mem_p.append(mkv3)

        zs = _project(xs, gain, w_in_b, l, z_pairs, ep_gain, ep_mode, rows=SAMPLE_ROWS,
                      name="proj_sample")
        zs3 = zs.reshape(SAMPLE_ROWS, N_SEG * N_HEADS, HEAD_DIM)
        ydm = _sample_attention(zs3, caches7, cache_mem_kv, l)
        ydm = jnp.pad(ydm.reshape(dbsz, 2 * SEG), ((0, SAMPLE_ROWS - dbsz), (0, 0)))
        state_t = jnp.pad(jnp.transpose(state_pool[l], (1, 0, 2)),
                          ((0, 0), (0, SAMPLE_ROWS - dbsz), (0, 0)))
        xs, new_state_t = _sample_out(zs, state_t, ydm, pool_w_b, pscale, w_out_b, xs, l)
        pool_s.append(jnp.transpose(new_state_t[:, :dbsz], (1, 0, 2)))
        zs_rows.append(zs[:dbsz].reshape(dbsz, N_SEG, N_HEADS, HEAD_DIM))

    zs_all = jnp.stack(zs_rows)
    new_rows = [zs_all[:, :, _seg_k(g):_seg_k(g) + 2] for g in range(len(DIL_PAIRS))]
    new_caches = [_cache_set_last(small_caches[0], new_rows[0]),
                  _cache_set_last(small_caches[1], new_rows[1]),
                  _cache_set_last(big_cache, new_rows[2])]

    y_prompt = xp.reshape(bsz, seq, D_MODEL)
    y_sample = xs[:dbsz].reshape(dbsz, 1, D_MODEL)
    cache_mem_prompt = jnp.stack(mem_p).reshape(depth, bsz, MEM_LEN, 2, N_HEADS, HEAD_DIM)
    return (y_prompt, y_sample, jnp.stack(pool_p), prompt_caches[0], prompt_caches[1],
            prompt_caches[2], cache_mem_prompt, jnp.stack(pool_s), new_caches[0], new_caches[1],
            new_caches[2])
```

```python
import functools

import jax
import jax.numpy as jnp
from jax import lax
from jax.experimental import pallas as pl
from jax.experimental.pallas import tpu as pltpu

F32 = jnp.float32
BF16 = jnp.bfloat16

D_MODEL = 2048
HEAD_DIM = 128
N_HEADS = 4
SEG = N_HEADS * HEAD_DIM
N_SEG = 14
IN_COLS = N_SEG * SEG
POOL_WINDOWS = (2, 4, 8, 16)
POOL_BUF = 15
POOL_HALO = 16
POOL_PAD = 8
DIL_PAIRS = ((128, 1), (512, 4), (2048, 16))
BAND = 128
MEM_LEN = 256
MIX_WIDTH = 3 * SEG
EPS = 1e-6
ATTN_SCALE = HEAD_DIM ** -0.5
PAST_LEN = 16384

Z_ORDER = (2, 0, 3, 4, 5, 1, 6, 7, 8, 11, 9, 10, 12, 13)
SEG_U, SEG_GATE_POOL, SEG_GATE_DIL, SEG_QMEM, SEG_GATE_MEM = 1, 5, 9, 12, 13
GATE_SEGS = (SEG_GATE_POOL, SEG_GATE_DIL, SEG_GATE_MEM)


def _seg_q(g):
    return 4 * g


def _seg_k(g):
    return 4 * g + 2

VMEM_LIMIT_BYTES = 56 * 1024 * 1024

PROJ_ROWS = 1024
ATTN_ROWS = 2048
MIX_ROWS = 512
SAMPLE_ROWS = 16
SHIFT_ROWS = 512
TAIL_ROWS = 512
SHIFT_CHUNK = 64


MODE_PLAIN, MODE_GATE = 0.0, 1.0


def _shift_rows(c_ref, nxt_ref, o_ref):
    rows = c_ref.shape[0]
    chunk = min(rows, SHIFT_CHUNK)
    for lo in range(0, rows, chunk):
        n = chunk if lo + chunk < rows else chunk - 1
        o_ref[pl.ds(lo, n)] = c_ref[pl.ds(lo + 1, n)]
    o_ref[rows - 1] = nxt_ref[0]


def _shift_operands(cache, buf, layer, step_of, n_steps, first_step=0):
    bsz, win = cache.shape[1:3]
    rows = min(win, SHIFT_ROWS)
    row_shape = cache.shape[3:]
    zeros = (0,) * len(row_shape)
    per_b = win // rows
    n_shift = bsz * per_b
    assert first_step + n_shift <= n_steps

    def block_of(*idx):
        t = jnp.clip(step_of(*idx) - first_step, 0, n_shift - 1)
        return t // per_b, t % per_b

    def main_map(*idx):
        b, r = block_of(*idx)
        return (layer, b, r) + zeros

    def next_map(*idx):
        b, r = block_of(*idx)
        return (layer, b, jnp.minimum((r + 1) * rows, win - 1)) + zeros

    in_specs = [pl.BlockSpec((None, None, rows) + row_shape, main_map),
                pl.BlockSpec((None, None, 1) + row_shape, next_map)]
    args = [cache, cache]
    alias = None
    if buf is not None:
        in_specs.append(pl.BlockSpec(memory_space=pl.ANY))
        args.append(buf)
        alias = 2
    out_shape = jax.ShapeDtypeStruct(cache.shape, cache.dtype)
    out_spec = pl.BlockSpec((None, None, rows) + row_shape, main_map)
    return in_specs, args, out_shape, out_spec, alias


def _lookup(j, table):
    out = table[0]
    for t in range(1, len(table)):
        out = jnp.where(j == t, table[t], out)
    return out


def _proj_kernel(x_ref, g_ref, wa_ref, wb_ref, eg_ref, em_ref, *rest, with_shift, aliased, emit_w):
    rest = list(rest)
    if with_shift:
        c_ref, nxt_ref = rest[:2]
        rest = rest[3:] if aliased else rest[2:]
    o_ref = rest.pop(0)
    if emit_w:
        wa_out_ref, wb_out_ref = rest.pop(0), rest.pop(0)
    if with_shift:
        co_ref = rest.pop(0)
    h_ref, = rest
    j = pl.program_id(1)
    ii = pl.program_id(2)

    @pl.when(j == 0)
    def _():
        x = x_ref[...]
        ms = jnp.mean(x * x, axis=-1, keepdims=True)
        h_ref[ii] = (x * lax.rsqrt(ms + EPS) * g_ref[...]).astype(BF16)

    if with_shift:
        _shift_rows(c_ref, nxt_ref, co_ref)
    h_tile = h_ref[ii]
    wa = wa_ref[...].astype(BF16)
    wb = wb_ref[...].astype(BF16)
    if emit_w:
        wa_out_ref[...] = wa
        wb_out_ref[...] = wb
    acc = jnp.dot(h_tile, wa, preferred_element_type=F32)
    for h in range(N_HEADS):
        sl = slice(h * HEAD_DIM, (h + 1) * HEAD_DIM)
        a = acc[:, sl]
        o_ref[:, sl] = a * (lax.rsqrt(jnp.mean(a * a, axis=-1, keepdims=True) + EPS) * eg_ref[:, sl])
    acc = jnp.dot(h_tile, wb, preferred_element_type=F32)
    for h in range(N_HEADS):
        sl = slice(h * HEAD_DIM, (h + 1) * HEAD_DIM)
        osl = slice(SEG + h * HEAD_DIM, SEG + (h + 1) * HEAD_DIM)
        a = acc[:, sl]
        o_ref[:, osl] = a * jnp.where(em_ref[:, osl] == MODE_GATE, jax.nn.sigmoid(a), 1.0)


def _project(x, gain, weights, ep_gain, ep_mode, *, rows, name, emit_w=False, shift=None):
    m, k = x.shape
    cols = 2 * SEG
    n_tiles = m // rows
    group = 2 if n_tiles % 2 == 0 else 1
    if len(weights) == 3:
        w, layer, seg_pairs = weights
        n_j = len(seg_pairs)
        firsts = tuple(p[0] for p in seg_pairs)
        seconds = tuple(p[1] for p in seg_pairs)
        w_specs = [pl.BlockSpec((None, k, SEG), lambda g, j, t: (layer, 0, _lookup(j, firsts))),
                   pl.BlockSpec((None, k, SEG), lambda g, j, t: (layer, 0, _lookup(j, seconds)))]
        w_args = [w, w]
    else:
        n_j = weights[0].shape[1] // SEG
        w_specs = [pl.BlockSpec((k, SEG), lambda g, j, t: (0, j))] * 2
        w_args = list(weights)
    n = n_j * cols

    def x_map(g, j, t):
        return (jnp.where(j == 0, g * group + t, g * group + group - 1), 0)

    in_specs = [
        pl.BlockSpec((rows, k), x_map),
        pl.BlockSpec((1, k), lambda g, j, t: (0, 0)),
        *w_specs,
        pl.BlockSpec((1, cols), lambda g, j, t: (0, j)),
        pl.BlockSpec((1, cols), lambda g, j, t: (0, j)),
    ]
    args = [x, gain, *w_args, ep_gain, ep_mode]
    out_shape = [jax.ShapeDtypeStruct((m, n), F32)]
    out_specs = [pl.BlockSpec((rows, cols), lambda g, j, t: (g * group + t, j))]
    if emit_w:
        assert n_tiles == 1
        out_shape += [jax.ShapeDtypeStruct((k, n_j * SEG), BF16)] * 2
        out_specs += [pl.BlockSpec((k, SEG), lambda g, j, t: (0, j))] * 2
    aliases = {}
    if shift is not None:
        s_specs, s_args, s_shape, s_spec, s_alias = _shift_operands(
            shift[0], shift[2], shift[1], lambda g, j, t: (g * n_j + j) * group + t, n_tiles * n_j)
        if s_alias is not None:
            aliases = {len(args) + s_alias: len(out_shape)}
        in_specs += s_specs
        args += s_args
        out_shape.append(s_shape)
        out_specs.append(s_spec)
    return pl.pallas_call(
        functools.partial(_proj_kernel, with_shift=shift is not None,
                          aliased=bool(aliases), emit_w=emit_w),
        out_shape=out_shape,
        grid=(n_tiles // group, n_j, group),
        in_specs=in_specs,
        out_specs=out_specs,
        scratch_shapes=[pltpu.VMEM((group, rows, k), BF16)],
        input_output_aliases=aliases,
        compiler_params=pltpu.CompilerParams(
            dimension_semantics=("arbitrary", "arbitrary", "arbitrary"),
            vmem_limit_bytes=VMEM_LIMIT_BYTES),
        name=name,
    )(*args)


def _dilattn_kernel(*refs, n_tiles):
    ins, outs, scratch = refs[:15], refs[15:22], refs[22:]
    y_ref = outs[0]
    tails = outs[1:]
    og_ref, lse_ref = scratch
    i = pl.program_id(2)

    row = lax.broadcasted_iota(jnp.int32, (BAND, 2 * BAND), 0)
    col = lax.broadcasted_iota(jnp.int32, (BAND, 2 * BAND), 1)
    not_future = col <= row + BAND
    band = jnp.logical_and(col >= row, not_future)
    first_lo = jnp.maximum(row, (i == 0).astype(jnp.int32) * BAND)
    band_first = jnp.logical_and(col >= first_lo, not_future)
    ones = jnp.ones((2 * BAND, HEAD_DIM), BF16)
    nt_dims = (((1,), (1,)), ((), ()))

    for g, (win, d) in enumerate(DIL_PAIRS):
        q_ref, kc_ref, vc_ref, kp_ref, vp_ref = ins[5 * g:5 * g + 5]
        span = BAND * d

        def rows_of(start, size, d=d):
            return pl.ds(start, size) if d == 1 else pl.ds(start, size, stride=d)

        for t in range(ATTN_ROWS // BAND):
            u, c = divmod(t, d)
            base = u * span + c
            q = q_ref[rows_of(base, BAND), :].astype(BF16)
            if u == 0:
                k = jnp.concatenate([kp_ref[rows_of(c, BAND), :], kc_ref[rows_of(c, BAND), :]], axis=0)
                v = jnp.concatenate([vp_ref[rows_of(c, BAND), :], vc_ref[rows_of(c, BAND), :]], axis=0)
            else:
                k = kc_ref[rows_of(base - span, 2 * BAND), :]
                v = vc_ref[rows_of(base - span, 2 * BAND), :]
            s = lax.dot_general(q, k.astype(BF16), nt_dims, preferred_element_type=F32)
            s = jnp.where(band_first if u == 0 else band, s, -jnp.inf)
            m = jnp.max(jnp.maximum(s[:, :BAND], s[:, BAND:]), axis=-1, keepdims=True)
            p = jnp.exp(s - m).astype(BF16)
            ov = jnp.dot(p, jnp.concatenate([v.astype(BF16), ones], axis=1), preferred_element_type=F32)
            l = ov[:, HEAD_DIM:]
            o_rows = rows_of(g * ATTN_ROWS + base, BAND)
            og_ref[o_rows, :] = ov[:, :HEAD_DIM] / l
            lse_ref[o_rows, :] = m + jnp.log(l)

        @pl.when(i == n_tiles - 1)
        def _(g=g, win=win, kc_ref=kc_ref, vc_ref=vc_ref):
            tails[2 * g][...] = kc_ref[pl.ds(ATTN_ROWS - win, win), :]
            tails[2 * g + 1][...] = vc_ref[pl.ds(ATTN_ROWS - win, win), :]

    chunk = 256

    def merge(r, carry):
        r0 = pl.multiple_of(r * chunk, chunk)
        sls = [pl.ds(g * ATTN_ROWS + r0, chunk) for g in range(len(DIL_PAIRS))]
        l0, l1, l2 = [lse_ref[sl, :] for sl in sls]
        mx = jnp.maximum(jnp.maximum(l0, l1), l2)
        e0, e1, e2 = jnp.exp(l0 - mx), jnp.exp(l1 - mx), jnp.exp(l2 - mx)
        num = e0 * og_ref[sls[0], :] + e1 * og_ref[sls[1], :] + e2 * og_ref[sls[2], :]
        y_ref[pl.ds(r0, chunk), :] = num / (e0 + e1 + e2)
        return carry

    lax.fori_loop(0, ATTN_ROWS // chunk, merge, 0)


def _dilated_attention(z3):
    bsz, seq, _ = z3.shape
    n_tiles = seq // ATTN_ROWS
    in_specs = []
    for g, (win, d) in enumerate(DIL_PAIRS):
        span = BAND * d
        qb = _seg_q(g) * N_HEADS
        kb = _seg_k(g) * N_HEADS
        vb = kb + N_HEADS
        per_tile = ATTN_ROWS // span

        def prev_map(col0, per_tile=per_tile):
            return lambda b, h, i: (b, jnp.maximum(i * per_tile - 1, 0), col0 + h)

        def cur_map(col0):
            return lambda b, h, i: (b, i, col0 + h)

        in_specs += [
            pl.BlockSpec((None, ATTN_ROWS, HEAD_DIM), cur_map(qb)),
            pl.BlockSpec((None, ATTN_ROWS, HEAD_DIM), cur_map(kb)),
            pl.BlockSpec((None, ATTN_ROWS, HEAD_DIM), cur_map(vb)),
            pl.BlockSpec((None, span, HEAD_DIM), prev_map(kb)),
            pl.BlockSpec((None, span, HEAD_DIM), prev_map(vb)),
        ]
    out_shape = [jax.ShapeDtypeStruct((bsz, seq, SEG), F32)]
    out_specs = [pl.BlockSpec((None, ATTN_ROWS, HEAD_DIM), lambda b, h, i: (b, i, h))]
    for win, _ in DIL_PAIRS:
        for _kv in range(2):
            out_shape.append(jax.ShapeDtypeStruct((bsz, win, SEG), F32))
            out_specs.append(pl.BlockSpec((None, win, HEAD_DIM), lambda b, h, i: (b, 0, h)))
    return pl.pallas_call(
        functools.partial(_dilattn_kernel, n_tiles=n_tiles),
        out_shape=out_shape,
        grid=(bsz, N_HEADS, n_tiles),
        in_specs=in_specs,
        out_specs=out_specs,
        scratch_shapes=[
            pltpu.VMEM((len(DIL_PAIRS) * ATTN_ROWS, HEAD_DIM), F32),
            pltpu.VMEM((len(DIL_PAIRS) * ATTN_ROWS, HEAD_DIM), F32),
        ],
        compiler_params=pltpu.CompilerParams(
            dimension_semantics=("arbitrary", "arbitrary", "arbitrary"),
            vmem_limit_bytes=VMEM_LIMIT_BYTES),
        name="dilated_attention",
    )(*([z3] * 15))


def _mix_kernel(u_ref, up_ref, gp_ref, gd_ref, qm_ref, gm_ref, yd_ref, mkv_ref, pw_ref, ps_ref,
                wo_ref, x_ref, *rest, n_aliased):
    shift_in, rest = rest[:4 + n_aliased], rest[4 + n_aliased:]
    o_ref, co0_ref, co1_ref, ue_ref, t_ref, y_ref = rest
    per = 2 + n_aliased // 2
    i = pl.program_id(1)
    rows = u_ref.shape[0]

    pad, halo = POOL_PAD, POOL_HALO
    n = halo + rows

    @pl.when(i > 0)
    def _():
        ue_ref[pl.ds(pad, halo), :] = up_ref[...]

    @pl.when(i == 0)
    def _():
        ue_ref[pl.ds(pad, halo), :] = jnp.zeros((halo, SEG), F32)

    ue_ref[pl.ds(0, pad), :] = jnp.zeros((pad, SEG), F32)
    t_ref[0, pl.ds(0, pad), :] = jnp.zeros((pad, HEAD_DIM), F32)
    t_ref[1, pl.ds(0, pad), :] = jnp.zeros((pad, HEAD_DIM), F32)
    ue_ref[pl.ds(pad + halo, rows), :] = u_ref[...]
    pos = i * rows + lax.broadcasted_iota(jnp.int32, (rows, 1), 0)
    for gi, w in enumerate(POOL_WINDOWS):
        sl = slice(gi * HEAD_DIM, (gi + 1) * HEAD_DIM)
        cur = ue_ref[pl.ds(pad, n), sl] + ue_ref[pl.ds(pad - 1, n), sl]
        shift, buf = 2, 0
        while shift < w:
            t_ref[buf, pl.ds(pad, n), :] = cur
            cur = cur + t_ref[buf, pl.ds(pad - shift, n), :]
            shift, buf = 2 * shift, 1 - buf
        cnt = jnp.minimum(w, pos + 1).astype(F32)
        dlt = cur[halo:] / cnt - u_ref[:, sl]
        yp = jnp.dot(dlt.astype(BF16), pw_ref[gi], preferred_element_type=F32) * ps_ref[:, sl]
        y_ref[:, sl] = (gp_ref[:, sl] * yp).astype(BF16)

    y_ref[:, SEG:2 * SEG] = (gd_ref[...] * yd_ref[...]).astype(BF16)

    nt_dims = (((1,), (1,)), ((), ()))
    ones = jnp.ones((MEM_LEN, HEAD_DIM), BF16)
    for h in range(N_HEADS):
        sl = slice(h * HEAD_DIM, (h + 1) * HEAD_DIM)
        q = qm_ref[:, sl].astype(BF16)
        k = mkv_ref[:, sl].astype(BF16)
        v = mkv_ref[:, SEG + h * HEAD_DIM:SEG + (h + 1) * HEAD_DIM].astype(BF16)
        s = lax.dot_general(q, k, nt_dims, preferred_element_type=F32)
        m = jnp.max(s, axis=-1, keepdims=True)
        p = jnp.exp(s - m).astype(BF16)
        ov = jnp.dot(p, jnp.concatenate([v, ones], axis=1), preferred_element_type=F32)
        o = ov[:, :HEAD_DIM] / ov[:, HEAD_DIM:]
        y_ref[:, 2 * SEG + h * HEAD_DIM:2 * SEG + (h + 1) * HEAD_DIM] = (gm_ref[:, sl] * o).astype(BF16)

    _shift_rows(shift_in[0], shift_in[1], co0_ref)
    _shift_rows(shift_in[per], shift_in[per + 1], co1_ref)
    o_ref[...] = x_ref[...] + jnp.dot(y_ref[...], wo_ref[...], preferred_element_type=F32)


def _mix(z3, y_dil, mem_kv, pool_w, pool_scale, w_out, x3, layer, shifts):
    bsz, seq, _ = z3.shape
    rows = MIX_ROWS
    n_tiles = seq // rows
    halo_per_tile = rows // POOL_HALO

    def seg_spec(seg):
        return pl.BlockSpec((None, rows, SEG), lambda b, i: (b, i, seg))

    n_main = 12
    shift_specs, shift_args, shift_shapes, shift_out_specs, aliases = [], [], [], [], {}
    first_step = 0
    for n, (cache, buf) in enumerate(shifts):
        s_specs, s_args, s_shape, s_spec, s_alias = _shift_operands(
            cache, buf, layer, lambda b, i: b * n_tiles + i, bsz * n_tiles, first_step)
        first_step += cache.shape[1] * (cache.shape[2] // min(cache.shape[2], SHIFT_ROWS))
        if s_alias is not None:
            aliases[n_main + len(shift_args) + s_alias] = 1 + n
        shift_specs += s_specs
        shift_args += s_args
        shift_shapes.append(s_shape)
        shift_out_specs.append(s_spec)

    return pl.pallas_call(
        functools.partial(_mix_kernel, n_aliased=len(aliases)),
        out_shape=[jax.ShapeDtypeStruct(x3.shape, F32)] + shift_shapes,
        grid=(bsz, n_tiles),
        in_specs=[
            seg_spec(SEG_U),
            pl.BlockSpec((None, POOL_HALO, SEG),
                         lambda b, i: (b, jnp.maximum(i * halo_per_tile - 1, 0), SEG_U)),
            seg_spec(SEG_GATE_POOL),
            seg_spec(SEG_GATE_DIL),
            seg_spec(SEG_QMEM),
            seg_spec(SEG_GATE_MEM),
            pl.BlockSpec((None, rows, SEG), lambda b, i: (b, i, 0)),
            pl.BlockSpec((None, MEM_LEN, 2 * SEG), lambda b, i: (b, 0, 0)),
            pl.BlockSpec((None, len(POOL_WINDOWS), HEAD_DIM, HEAD_DIM), lambda b, i: (layer, 0, 0, 0)),
            pl.BlockSpec((1, SEG), lambda b, i: (0, 0)),
            pl.BlockSpec((None, MIX_WIDTH, D_MODEL), lambda b, i: (layer, 0, 0),
                         pipeline_mode=pl.Buffered(1)),
            pl.BlockSpec((None, rows, D_MODEL), lambda b, i: (b, i, 0)),
        ] + shift_specs,
        out_specs=[pl.BlockSpec((None, rows, D_MODEL), lambda b, i: (b, i, 0))] + shift_out_specs,
        scratch_shapes=[
            pltpu.VMEM((POOL_PAD + POOL_HALO + rows, SEG), F32),
            pltpu.VMEM((2, POOL_PAD + POOL_HALO + rows, HEAD_DIM), F32),
            pltpu.VMEM((rows, MIX_WIDTH), BF16),
        ],
        input_output_aliases=aliases,
        compiler_params=pltpu.CompilerParams(
            dimension_semantics=("arbitrary", "arbitrary"),
            vmem_limit_bytes=VMEM_LIMIT_BYTES),
        name="mix",
    )(z3, z3, z3, z3, z3, z3, y_dil, mem_kv, pool_w, pool_scale, w_out, x3, *shift_args)


def _sample_attn_kernel(zs_ref, c0_ref, c1_ref, c2_ref, cm_ref, o_ref):
    outs, lses = [], []
    for g, c_ref in enumerate((c0_ref, c1_ref, c2_ref)):
        q = zs_ref[pl.ds(_seg_q(g) * N_HEADS, N_HEADS), :]
        k_new = zs_ref[pl.ds(_seg_k(g) * N_HEADS, N_HEADS), :]
        v_new = zs_ref[pl.ds((_seg_k(g) + 1) * N_HEADS, N_HEADS), :]
        k = c_ref[:, 0]
        v = c_ref[:, 1]
        s = jnp.sum(k * q[None], axis=-1, keepdims=True)
        s_new = jnp.sum(k_new * q, axis=-1, keepdims=True)
        m = jnp.maximum(jnp.max(s, axis=0), s_new)
        p = jnp.exp(s - m[None])
        p_new = jnp.exp(s_new - m)
        l = jnp.sum(p, axis=0) + p_new
        outs.append((jnp.sum(p * v, axis=0) + p_new * v_new) / l)
        lses.append(m + jnp.log(l))
    mx = jnp.maximum(jnp.maximum(lses[0], lses[1]), lses[2])
    es = [jnp.exp(x - mx) for x in lses]
    o_ref[pl.ds(0, N_HEADS), :] = (es[0] * outs[0] + es[1] * outs[1] + es[2] * outs[2]) / (es[0] + es[1] + es[2])

    q = zs_ref[pl.ds(SEG_QMEM * N_HEADS, N_HEADS), :]
    k = cm_ref[:, 0]
    v = cm_ref[:, 1]
    s = jnp.sum(k * q[None], axis=-1, keepdims=True)
    m = jnp.max(s, axis=0)
    p = jnp.exp(s - m[None])
    o_ref[pl.ds(N_HEADS, N_HEADS), :] = jnp.sum(p * v, axis=0) / jnp.sum(p, axis=0)


def _sample_attention(zs3, caches7, cache_mem, layer):
    bsz = cache_mem.shape[1]
    in_specs = [pl.BlockSpec((None, N_SEG * N_HEADS, HEAD_DIM), lambda b: (b, 0, 0))]
    for _ in caches7:
        in_specs.append(pl.BlockSpec((None, None, BAND, None, 2, N_HEADS, HEAD_DIM),
                                     lambda b: (layer, b, 0, 0, 0, 0, 0)))
    in_specs.append(pl.BlockSpec((None, None, MEM_LEN, 2, N_HEADS, HEAD_DIM),
                                 lambda b: (layer, b, 0, 0, 0, 0)))
    return pl.pallas_call(
        _sample_attn_kernel,
        out_shape=jax.ShapeDtypeStruct((bsz, 2 * N_HEADS, HEAD_DIM), F32),
        grid=(bsz,),
        in_specs=in_specs,
        out_specs=pl.BlockSpec((None, 2 * N_HEADS, HEAD_DIM), lambda b: (b, 0, 0)),
        compiler_params=pltpu.CompilerParams(dimension_semantics=("arbitrary",)),
        name="sample_attention",
    )(zs3, *caches7, cache_mem)


def _sample_out_kernel(zs_ref, st_ref, ydm_ref, pw_ref, ps_ref, wo_ref, x_ref, o_ref, ns_ref, y_ref):
    u = zs_ref[:, pl.ds(SEG_U * SEG, SEG)]
    for gi, w in enumerate(POOL_WINDOWS):
        sl = slice(gi * HEAD_DIM, (gi + 1) * HEAD_DIM)
        tot = u[:, sl]
        for back in range(1, w):
            tot = tot + st_ref[POOL_BUF - back, :, sl]
        cnt = float(min(w, PAST_LEN + 1))
        dlt = tot / cnt - u[:, sl]
        yp = jnp.dot(dlt.astype(BF16), pw_ref[gi], preferred_element_type=F32) * ps_ref[:, sl]
        y_ref[:, sl] = (zs_ref[:, pl.ds(SEG_GATE_POOL * SEG + gi * HEAD_DIM, HEAD_DIM)] * yp).astype(BF16)
    y_ref[:, SEG:2 * SEG] = (zs_ref[:, pl.ds(SEG_GATE_DIL * SEG, SEG)] * ydm_ref[:, pl.ds(0, SEG)]).astype(BF16)
    y_ref[:, 2 * SEG:] = (zs_ref[:, pl.ds(SEG_GATE_MEM * SEG, SEG)] * ydm_ref[:, pl.ds(SEG, SEG)]).astype(BF16)
    o_ref[...] = x_ref[...] + jnp.dot(y_ref[...], wo_ref[...], preferred_element_type=F32)
    for r in range(POOL_BUF - 1):
        ns_ref[r] = st_ref[r + 1]
    ns_ref[POOL_BUF - 1] = u


def _sample_out(zs, state_t, ydm, pool_w, pool_scale, w_out, xs, layer):
    rows = zs.shape[0]
    full = lambda shape: pl.BlockSpec(shape, lambda i: tuple(0 for _ in shape))
    of_layer = lambda a: pl.BlockSpec((None,) + a.shape[1:], lambda i: (layer,) + (0,) * (a.ndim - 1))
    return pl.pallas_call(
        _sample_out_kernel,
        out_shape=[jax.ShapeDtypeStruct(xs.shape, F32), jax.ShapeDtypeStruct(state_t.shape, F32)],
        grid=(1,),
        in_specs=[full(zs.shape), full(state_t.shape), full(ydm.shape), of_layer(pool_w),
                  full(pool_scale.shape), of_layer(w_out), full(xs.shape)],
        out_specs=[full(xs.shape), full(state_t.shape)],
        scratch_shapes=[pltpu.VMEM((rows, MIX_WIDTH), BF16)],
        compiler_params=pltpu.CompilerParams(vmem_limit_bytes=VMEM_LIMIT_BYTES),
        name="sample_out",
    )(zs, state_t, ydm, pool_w, pool_scale, w_out, xs)


def _set_last_kernel(*refs):
    n = len(refs) // 3
    for new_ref, o_ref in zip(refs[:n], refs[2 * n:]):
        o_ref[0] = new_ref[...]


def _cache_set_last(bufs, new_rows):
    n = len(bufs)
    depth, bsz = bufs[0].shape[:2]
    row_shape = bufs[0].shape[3:]
    zeros = (0,) * len(row_shape)

    def last_row_spec(win):
        return pl.BlockSpec((None, None, 1) + row_shape, lambda l, b: (l, b, win - 1) + zeros)

    return pl.pallas_call(
        _set_last_kernel,
        out_shape=[jax.ShapeDtypeStruct(buf.shape, buf.dtype) for buf in bufs],
        grid=(depth, bsz),
        in_specs=[pl.BlockSpec((None, None) + row_shape, lambda l, b: (l, b) + zeros)] * n
        + [pl.BlockSpec(memory_space=pl.ANY)] * n,
        out_specs=[last_row_spec(buf.shape[2]) for buf in bufs],
        input_output_aliases={n + g: g for g in range(n)},
        compiler_params=pltpu.CompilerParams(dimension_semantics=("arbitrary", "arbitrary")),
        name="cache_set_last",
    )(*new_rows, *bufs)


def _tail_kernel(k_ref, v_ref, *rest):
    o_ref = rest[-1]
    for h in range(N_HEADS):
        sl = slice(h * HEAD_DIM, (h + 1) * HEAD_DIM)
        o_ref[:, 0, h, :] = k_ref[:, sl]
        o_ref[:, 1, h, :] = v_ref[:, sl]


def _assemble_tail(buf, k_tail, v_tail, layer, depth):
    bsz, win, _ = k_tail.shape
    rows = min(win, TAIL_ROWS)
    shape = (depth, bsz, win, 2, N_HEADS, HEAD_DIM)
    in_specs = [pl.BlockSpec((None, rows, SEG), lambda b, r: (b, r, 0)),
                pl.BlockSpec((None, rows, SEG), lambda b, r: (b, r, 0))]
    args = [k_tail, v_tail]
    aliases = {}
    if buf is not None:
        in_specs.append(pl.BlockSpec(memory_space=pl.ANY))
        args.append(buf)
        aliases = {2: 0}
    return pl.pallas_call(
        _tail_kernel,
        out_shape=jax.ShapeDtypeStruct(shape, F32),
        grid=(bsz, win // rows),
        in_specs=in_specs,
        out_specs=pl.BlockSpec((None, None, rows, 2, N_HEADS, HEAD_DIM),
                               lambda b, r: (layer, b, r, 0, 0, 0)),
        input_output_aliases=aliases,
        compiler_params=pltpu.CompilerParams(dimension_semantics=("arbitrary", "arbitrary")),
        name="assemble_tail",
    )(*args)


def _tile_heads(v):
    return jnp.tile(v, N_HEADS)


def kernel(x_prompt, x_sample, state_pool, cache_dil_w128, cache_dil_w512, cache_dil_w2048,
           cache_mem_kv, mem_prompt, norm_g, w_in, pool_w, pool_scale, dil_q_norm, dil_k_norm,
           mem_norm_g, w_mem_kv, mem_q_norm, mem_k_norm, w_out):
    depth = w_in.shape[0]
    bsz, seq, _ = x_prompt.shape
    dbsz = x_sample.shape[0]
    caches = (cache_dil_w128, cache_dil_w512, cache_dil_w2048)

    z_pairs = tuple(zip(Z_ORDER[0::2], Z_ORDER[1::2]))
    w_out_b = w_out.astype(BF16)
    pool_w_b = pool_w.astype(BF16)

    ones = jnp.ones((SEG,), F32)
    seg_modes = [MODE_GATE if s in GATE_SEGS else MODE_PLAIN for s in range(N_SEG)]
    ep_mode = jnp.repeat(jnp.array(seg_modes, F32), SEG)[None, :]
    mem_mode = jnp.full((1, 2 * SEG), MODE_PLAIN, F32)
    caches7 =[c.reshape(depth, dbsz, win // d, d, 2, N_HEADS, HEAD_DIM)
               for c, (win, d) in zip(caches, DIL_PAIRS)]

    xp = x_prompt.reshape(bsz * seq, D_MODEL)
    xs = jnp.pad(x_sample.reshape(dbsz, D_MODEL), ((0, SAMPLE_ROWS - dbsz), (0, 0)))
    mem2 = mem_prompt.reshape(bsz * MEM_LEN, D_MODEL)

    pool_p, mem_p, pool_s, zs_rows = [], [], [], []
    prompt_caches = [None] * len(DIL_PAIRS)
    big_cache = None
    small_caches = [None, None]
    for l in range(depth):
        segs = [ones] * N_SEG
        for g in range(len(DIL_PAIRS)):
            segs[_seg_q(g)] = _tile_heads(dil_q_norm[l, g]) * ATTN_SCALE
            segs[_seg_k(g)] = _tile_heads(dil_k_norm[l, g])
        segs[SEG_QMEM] = _tile_heads(mem_q_norm[l]) * ATTN_SCALE
        ep_gain = jnp.concatenate(segs)[None, :]
        mem_gain = jnp.concatenate([_tile_heads(mem_k_norm[l]), ones])[None, :]
        gain = norm_g[l][None, :]
        pscale = pool_scale[l][None, :]

        zs, w_first, w_second = _project(xs, gain, (w_in, l, z_pairs), ep_gain, ep_mode,
                                         rows=SAMPLE_ROWS, name="proj_sample", emit_w=True)
        zs3 = zs.reshape(SAMPLE_ROWS, N_SEG * N_HEADS, HEAD_DIM)
        ydm = _sample_attention(zs3, caches7, cache_mem_kv, l)
        ydm = jnp.pad(ydm.reshape(dbsz, 2 * SEG), ((0, SAMPLE_ROWS - dbsz), (0, 0)))
        state_t = jnp.pad(jnp.transpose(state_pool[l], (1, 0, 2)),
                          ((0, 0), (0, SAMPLE_ROWS - dbsz), (0, 0)))
        xs, new_state_t = _sample_out(zs, state_t, ydm, pool_w_b, pscale, w_out_b, xs, l)
        pool_s.append(jnp.transpose(new_state_t[:, :dbsz], (1, 0, 2)))
        zs_rows.append(zs[:dbsz].reshape(dbsz, N_SEG, N_HEADS, HEAD_DIM))

        z, big_cache = _project(xp, gain, (w_first, w_second), ep_gain, ep_mode, rows=PROJ_ROWS,
                                name="proj_prompt", shift=(caches[-1], l, big_cache))
        z3 = z.reshape(bsz, seq, IN_COLS)
        mkv, = _project(mem2, mem_norm_g[l][None, :], (w_mem_kv, l, ((0, 1),)), mem_gain, mem_mode,
                        rows=bsz * MEM_LEN, name="proj_mem")
        mkv3 = mkv.reshape(bsz, MEM_LEN, 2 * SEG)
        attn = _dilated_attention(z3)
        y_dil = attn[0]
        for g in range(len(DIL_PAIRS)):
            prompt_caches[g] = _assemble_tail(prompt_caches[g], attn[1 + 2 * g], attn[2 + 2 * g],
                                              l, depth)
        xp3, small_caches[1], small_caches[0] = _mix(
            z3, y_dil, mkv3, pool_w_b, pscale, w_out_b, xp.reshape(bsz, seq, D_MODEL), l,
            ((caches[1], small_caches[1]), (caches[0], small_caches[0])))
        xp = xp3.reshape(bsz * seq, D_MODEL)
        pool_p.append(z3[:, seq - POOL_BUF:, SEG_U * SEG:(SEG_U + 1) * SEG])
        mem_p.append(mkv3)

    zs_all = jnp.stack(zs_rows)
    new_rows = [zs_all[:, :, _seg_k(g):_seg_k(g) + 2] for g in range(len(DIL_PAIRS))]
    new_caches = _cache_set_last([small_caches[0], small_caches[1], big_cache], new_rows)

    y_prompt = xp.reshape(bsz, seq, D_MODEL)
    y_sample = xs[:dbsz].reshape(dbsz, 1, D_MODEL)
    cache_mem_prompt = jnp.stack(mem_p).reshape(depth, bsz, MEM_LEN, 2, N_HEADS, HEAD_DIM)
    return (y_prompt, y_sample, jnp.stack(pool_p), prompt_caches[0], prompt_caches[1],
            prompt_caches[2], cache_mem_prompt, jnp.stack(pool_s), new_caches[0], new_caches[1],
            new_caches[2])
```

```python
import functools

import jax
import jax.numpy as jnp
from jax import lax
from jax.experimental import pallas as pl
from jax.experimental.pallas import tpu as pltpu

F32 = jnp.float32
BF16 = jnp.bfloat16

D_MODEL = 2048
HEAD_DIM = 128
N_HEADS = 4
SEG = N_HEADS * HEAD_DIM
N_SEG = 14
IN_COLS = N_SEG * SEG
POOL_WINDOWS = (2, 4, 8, 16)
POOL_BUF = 15
POOL_HALO = 16
POOL_PAD = 8
DIL_PAIRS = ((128, 1), (512, 4), (2048, 16))
BAND = 128
MEM_LEN = 256
MIX_WIDTH = 3 * SEG
EPS = 1e-6
ATTN_SCALE = HEAD_DIM ** -0.5
PAST_LEN = 16384

Z_ORDER = (2, 0, 3, 4, 5, 1, 6, 7, 8, 11, 9, 10, 12, 13)
SEG_U, SEG_GATE_POOL, SEG_GATE_DIL, SEG_QMEM, SEG_GATE_MEM = 1, 5, 9, 12, 13
GATE_SEGS = (SEG_GATE_POOL, SEG_GATE_DIL, SEG_GATE_MEM)


def _seg_q(g):
    return 4 * g


def _seg_k(g):
    return 4 * g + 2

VMEM_LIMIT_BYTES = 56 * 1024 * 1024

PROJ_ROWS = 1024
PROJ_ROW_PARTS = 4
PROJ_MIN_PART = 256
ATTN_ROWS = 2048
MIX_ROWS = 512
SAMPLE_ROWS = 16
SHIFT_ROWS = 512
SHIFT_CHUNK = 64


MODE_PLAIN, MODE_GATE = 0.0, 1.0


def _shift_rows(c_ref, nxt_ref, o_ref):
    rows = c_ref.shape[0]
    chunk = min(rows, SHIFT_CHUNK)
    for lo in range(0, rows, chunk):
        n = chunk if lo + chunk < rows else chunk - 1
        o_ref[pl.ds(lo, n)] = c_ref[pl.ds(lo + 1, n)]
    o_ref[rows - 1] = nxt_ref[0]


def _shift_operands(cache, buf, layer, step_of, n_steps, first_step=0):
    bsz, win = cache.shape[1:3]
    rows = min(win, SHIFT_ROWS)
    row_shape = cache.shape[3:]
    zeros = (0,) * len(row_shape)
    per_b = win // rows
    n_shift = bsz * per_b
    assert first_step + n_shift <= n_steps

    def block_of(*idx):
        t = jnp.clip(step_of(*idx) - first_step, 0, n_shift - 1)
        return t // per_b, t % per_b

    def main_map(*idx):
        b, r = block_of(*idx)
        return (layer, b, r) + zeros

    def next_map(*idx):
        b, r = block_of(*idx)
        return (layer, b, jnp.minimum((r + 1) * rows, win - 1)) + zeros

    in_specs = [pl.BlockSpec((None, None, rows) + row_shape, main_map),
                pl.BlockSpec((None, None, 1) + row_shape, next_map)]
    args = [cache, cache]
    alias = None
    if buf is not None:
        in_specs.append(pl.BlockSpec(memory_space=pl.ANY))
        args.append(buf)
        alias = 2
    out_shape = jax.ShapeDtypeStruct(cache.shape, cache.dtype)
    out_spec = pl.BlockSpec((None, None, rows) + row_shape, main_map)
    return in_specs, args, out_shape, out_spec, alias


def _lookup(j, table):
    out = table[0]
    for t in range(1, len(table)):
        out = jnp.where(j == t, table[t], out)
    return out


def _proj_kernel(x_ref, g_ref, wa_ref, wb_ref, eg_ref, em_ref, *rest, with_shift, aliased, emit_w):
    rest = list(rest)
    if with_shift:
        c_ref, nxt_ref = rest[:2]
        rest = rest[3:] if aliased else rest[2:]
    o_ref = rest.pop(0)
    if emit_w:
        wa_out_ref, wb_out_ref = rest.pop(0), rest.pop(0)
    if with_shift:
        co_ref = rest.pop(0)
    h_ref, = rest
    j = pl.program_id(1)
    ii = pl.program_id(2)

    @pl.when(j == 0)
    def _():
        x = x_ref[...]
        ms = jnp.mean(x * x, axis=-1, keepdims=True)
        h_ref[ii] = (x * lax.rsqrt(ms + EPS) * g_ref[...]).astype(BF16)

    if with_shift:
        _shift_rows(c_ref, nxt_ref, co_ref)
    h_tile = h_ref[ii]
    wa = wa_ref[...].astype(BF16)
    wb = wb_ref[...].astype(BF16)
    if emit_w:
        wa_out_ref[...] = wa
        wb_out_ref[...] = wb
    rows = h_tile.shape[0]
    part = max(rows // PROJ_ROW_PARTS, min(rows, PROJ_MIN_PART))
    for r0 in range(0, rows, part):
        rsl = pl.ds(r0, part)
        acc = jnp.dot(h_tile[r0:r0 + part], wa, preferred_element_type=F32)
        for h in range(N_HEADS):
            sl = slice(h * HEAD_DIM, (h + 1) * HEAD_DIM)
            a = acc[:, sl]
            o_ref[rsl, sl] = a * (lax.rsqrt(jnp.mean(a * a, axis=-1, keepdims=True) + EPS) * eg_ref[:, sl])
    for r0 in range(0, rows, part):
        rsl = pl.ds(r0, part)
        acc = jnp.dot(h_tile[r0:r0 + part], wb, preferred_element_type=F32)
        for h in range(N_HEADS):
            sl = slice(h * HEAD_DIM, (h + 1) * HEAD_DIM)
            osl = slice(SEG + h * HEAD_DIM, SEG + (h + 1) * HEAD_DIM)
            a = acc[:, sl]
            o_ref[rsl, osl] = a * jnp.where(em_ref[:, osl] == MODE_GATE, jax.nn.sigmoid(a), 1.0)


def _project(x, gain, weights, ep_gain, ep_mode, *, rows, name, emit_w=False, shift=None):
    m, k = x.shape
    cols = 2 * SEG
    n_tiles = m // rows
    group = 2 if n_tiles % 2 == 0 else 1
    if len(weights) == 3:
        w, layer, seg_pairs = weights
        n_j = len(seg_pairs)
        firsts = tuple(p[0] for p in seg_pairs)
        seconds = tuple(p[1] for p in seg_pairs)
        w_specs = [pl.BlockSpec((None, k, SEG), lambda g, j, t: (layer, 0, _lookup(j, firsts))),
                   pl.BlockSpec((None, k, SEG), lambda g, j, t: (layer, 0, _lookup(j, seconds)))]
        w_args = [w, w]
    else:
        n_j = weights[0].shape[1] // SEG
        w_specs = [pl.BlockSpec((k, SEG), lambda g, j, t: (0, j))] * 2
        w_args = list(weights)
    n = n_j * cols

    def x_map(g, j, t):
        return (jnp.where(j == 0, g * group + t, g * group + group - 1), 0)

    in_specs = [
        pl.BlockSpec((rows, k), x_map),
        pl.BlockSpec((1, k), lambda g, j, t: (0, 0)),
        *w_specs,
        pl.BlockSpec((1, cols), lambda g, j, t: (0, j)),
        pl.BlockSpec((1, cols), lambda g, j, t: (0, j)),
    ]
    args = [x, gain, *w_args, ep_gain, ep_mode]
    out_shape = [jax.ShapeDtypeStruct((m, n), F32)]
    out_specs = [pl.BlockSpec((rows, cols), lambda g, j, t: (g * group + t, j))]
    if emit_w:
        assert n_tiles == 1
        out_shape += [jax.ShapeDtypeStruct((k, n_j * SEG), BF16)] * 2
        out_specs += [pl.BlockSpec((k, SEG), lambda g, j, t: (0, j))] * 2
    aliases = {}
    if shift is not None:
        s_specs, s_args, s_shape, s_spec, s_alias = _shift_operands(
            shift[0], shift[2], shift[1], lambda g, j, t: (g * n_j + j) * group + t, n_tiles * n_j)
        if s_alias is not None:
            aliases = {len(args) + s_alias: len(out_shape)}
        in_specs += s_specs
        args += s_args
        out_shape.append(s_shape)
        out_specs.append(s_spec)
    return pl.pallas_call(
        functools.partial(_proj_kernel, with_shift=shift is not None,
                          aliased=bool(aliases), emit_w=emit_w),
        out_shape=out_shape,
        grid=(n_tiles // group, n_j, group),
        in_specs=in_specs,
        out_specs=out_specs,
        scratch_shapes=[pltpu.VMEM((group, rows, k), BF16)],
        input_output_aliases=aliases,
        compiler_params=pltpu.CompilerParams(
            dimension_semantics=("arbitrary", "arbitrary", "arbitrary"),
            vmem_limit_bytes=VMEM_LIMIT_BYTES),
        name=name,
    )(*args)


def _dilattn_kernel(*refs, n_tiles):
    n_groups = len(DIL_PAIRS)
    ins = refs[:5 * n_groups]
    y_ref, *tails, og_ref, lse_ref = refs[len(refs) - 3 - n_groups:]
    head = pl.program_id(1)
    i = pl.program_id(2)

    row = lax.broadcasted_iota(jnp.int32, (BAND, 2 * BAND), 0)
    col = lax.broadcasted_iota(jnp.int32, (BAND, 2 * BAND), 1)
    not_future = col <= row + BAND
    band = jnp.logical_and(col >= row, not_future)
    first_lo = jnp.maximum(row, (i == 0).astype(jnp.int32) * BAND)
    band_first = jnp.logical_and(col >= first_lo, not_future)
    ones = jnp.ones((2 * BAND, HEAD_DIM), BF16)
    nt_dims = (((1,), (1,)), ((), ()))

    for g, (win, d) in enumerate(DIL_PAIRS):
        q_ref, kc_ref, vc_ref, kp_ref, vp_ref = ins[5 * g:5 * g + 5]
        span = BAND * d

        def rows_of(start, size, d=d):
            return pl.ds(start, size) if d == 1 else pl.ds(start, size, stride=d)

        for t in range(ATTN_ROWS // BAND):
            u, c = divmod(t, d)
            base = u * span + c
            q = q_ref[rows_of(base, BAND), :].astype(BF16)
            if u == 0:
                k = jnp.concatenate([kp_ref[rows_of(c, BAND), :], kc_ref[rows_of(c, BAND), :]], axis=0)
                v = jnp.concatenate([vp_ref[rows_of(c, BAND), :], vc_ref[rows_of(c, BAND), :]], axis=0)
            else:
                k = kc_ref[rows_of(base - span, 2 * BAND), :]
                v = vc_ref[rows_of(base - span, 2 * BAND), :]
            s = lax.dot_general(q, k.astype(BF16), nt_dims, preferred_element_type=F32)
            s = jnp.where(band_first if u == 0 else band, s, -jnp.inf)
            m = jnp.max(jnp.maximum(s[:, :BAND], s[:, BAND:]), axis=-1, keepdims=True)
            p = jnp.exp(s - m).astype(BF16)
            ov = jnp.dot(p, jnp.concatenate([v.astype(BF16), ones], axis=1), preferred_element_type=F32)
            l = ov[:, HEAD_DIM:]
            o_rows = rows_of(g * ATTN_ROWS + base, BAND)
            og_ref[o_rows, :] = ov[:, :HEAD_DIM] / l
            lse_ref[o_rows, :] = m + jnp.log(l)

        for hh in range(N_HEADS):
            @pl.when(jnp.logical_and(i == n_tiles - 1, head == hh))
            def _(g=g, win=win, hh=hh, kc_ref=kc_ref, vc_ref=vc_ref):
                tails[g][:, 0, hh, :] = kc_ref[pl.ds(ATTN_ROWS - win, win), :]
                tails[g][:, 1, hh, :] = vc_ref[pl.ds(ATTN_ROWS - win, win), :]

    chunk = 256

    def merge(r, carry):
        r0 = pl.multiple_of(r * chunk, chunk)
        sls = [pl.ds(g * ATTN_ROWS + r0, chunk) for g in range(len(DIL_PAIRS))]
        l0, l1, l2 = [lse_ref[sl, :] for sl in sls]
        mx = jnp.maximum(jnp.maximum(l0, l1), l2)
        e0, e1, e2 = jnp.exp(l0 - mx), jnp.exp(l1 - mx), jnp.exp(l2 - mx)
        num = e0 * og_ref[sls[0], :] + e1 * og_ref[sls[1], :] + e2 * og_ref[sls[2], :]
        y_ref[pl.ds(r0, chunk), :] = num / (e0 + e1 + e2)
        return carry

    lax.fori_loop(0, ATTN_ROWS // chunk, merge, 0)


def _dilated_attention(z3, layer, depth, cache_bufs):
    bsz, seq, _ = z3.shape
    n_tiles = seq // ATTN_ROWS
    in_specs = []
    for g, (win, d) in enumerate(DIL_PAIRS):
        span = BAND * d
        qb = _seg_q(g) * N_HEADS
        kb = _seg_k(g) * N_HEADS
        vb = kb + N_HEADS
        per_tile = ATTN_ROWS // span

        def prev_map(col0, per_tile=per_tile):
            return lambda b, h, i: (b, jnp.maximum(i * per_tile - 1, 0), col0 + h)

        def cur_map(col0):
            return lambda b, h, i: (b, i, col0 + h)

        in_specs += [
            pl.BlockSpec((None, ATTN_ROWS, HEAD_DIM), cur_map(qb)),
            pl.BlockSpec((None, ATTN_ROWS, HEAD_DIM), cur_map(kb)),
            pl.BlockSpec((None, ATTN_ROWS, HEAD_DIM), cur_map(vb)),
            pl.BlockSpec((None, span, HEAD_DIM), prev_map(kb)),
            pl.BlockSpec((None, span, HEAD_DIM), prev_map(vb)),
        ]
    args = [z3] * len(in_specs)
    out_shape = [jax.ShapeDtypeStruct((bsz, seq, SEG), F32)]
    out_specs = [pl.BlockSpec((None, ATTN_ROWS, HEAD_DIM), lambda b, h, i: (b, i, h))]
    aliases = {}
    for g, (win, _) in enumerate(DIL_PAIRS):
        out_shape.append(jax.ShapeDtypeStruct((depth, bsz, win, 2, N_HEADS, HEAD_DIM), F32))
        out_specs.append(pl.BlockSpec((None, None, win, 2, N_HEADS, HEAD_DIM),
                                      lambda b, h, i: (layer, b, 0, 0, 0, 0)))
        if cache_bufs is not None:
            in_specs.append(pl.BlockSpec(memory_space=pl.ANY))
            args.append(cache_bufs[g])
            aliases[len(args) - 1] = 1 + g
    return pl.pallas_call(
        functools.partial(_dilattn_kernel, n_tiles=n_tiles),
        out_shape=out_shape,
        grid=(bsz, N_HEADS, n_tiles),
        in_specs=in_specs,
        out_specs=out_specs,
        scratch_shapes=[
            pltpu.VMEM((len(DIL_PAIRS) * ATTN_ROWS, HEAD_DIM), F32),
            pltpu.VMEM((len(DIL_PAIRS) * ATTN_ROWS, HEAD_DIM), F32),
        ],
        input_output_aliases=aliases,
        compiler_params=pltpu.CompilerParams(
            dimension_semantics=("arbitrary", "arbitrary", "arbitrary"),
            vmem_limit_bytes=VMEM_LIMIT_BYTES),
        name="dilated_attention",
    )(*args)


def _mix_kernel(u_ref, up_ref, gp_ref, gd_ref, qm_ref, gm_ref, yd_ref, mkv_ref, pw_ref, ps_ref,
                wo_ref, x_ref, *rest, n_aliased):
    shift_in, rest = rest[:4 + n_aliased], rest[4 + n_aliased:]
    o_ref, co0_ref, co1_ref, ue_ref, t_ref, y_ref = rest
    per = 2 + n_aliased // 2
    i = pl.program_id(1)
    rows = u_ref.shape[0]

    pad, halo = POOL_PAD, POOL_HALO
    n = halo + rows

    @pl.when(i > 0)
    def _():
        ue_ref[pl.ds(pad, halo), :] = up_ref[...]

    @pl.when(i == 0)
    def _():
        ue_ref[pl.ds(pad, halo), :] = jnp.zeros((halo, SEG), F32)

    ue_ref[pl.ds(0, pad), :] = jnp.zeros((pad, SEG), F32)
    t_ref[0, pl.ds(0, pad), :] = jnp.zeros((pad, HEAD_DIM), F32)
    t_ref[1, pl.ds(0, pad), :] = jnp.zeros((pad, HEAD_DIM), F32)
    ue_ref[pl.ds(pad + halo, rows), :] = u_ref[...]
    pos = i * rows + lax.broadcasted_iota(jnp.int32, (rows, 1), 0)
    for gi, w in enumerate(POOL_WINDOWS):
        sl = slice(gi * HEAD_DIM, (gi + 1) * HEAD_DIM)
        cur = ue_ref[pl.ds(pad, n), sl] + ue_ref[pl.ds(pad - 1, n), sl]
        shift, buf = 2, 0
        while shift < w:
            t_ref[buf, pl.ds(pad, n), :] = cur
            cur = cur + t_ref[buf, pl.ds(pad - shift, n), :]
            shift, buf = 2 * shift, 1 - buf
        cnt = jnp.minimum(w, pos + 1).astype(F32)
        dlt = cur[halo:] / cnt - u_ref[:, sl]
        yp = jnp.dot(dlt.astype(BF16), pw_ref[gi], preferred_element_type=F32) * ps_ref[:, sl]
        y_ref[:, sl] = (gp_ref[:, sl] * yp).astype(BF16)

    y_ref[:, SEG:2 * SEG] = (gd_ref[...] * yd_ref[...]).astype(BF16)

    nt_dims = (((1,), (1,)), ((), ()))
    ones = jnp.ones((MEM_LEN, HEAD_DIM), BF16)
    for h in range(N_HEADS):
        sl = slice(h * HEAD_DIM, (h + 1) * HEAD_DIM)
        q = qm_ref[:, sl].astype(BF16)
        k = mkv_ref[:, sl].astype(BF16)
        v = mkv_ref[:, SEG + h * HEAD_DIM:SEG + (h + 1) * HEAD_DIM].astype(BF16)
        s = lax.dot_general(q, k, nt_dims, preferred_element_type=F32)
        m = jnp.max(s, axis=-1, keepdims=True)
        p = jnp.exp(s - m).astype(BF16)
        ov = jnp.dot(p, jnp.concatenate([v, ones], axis=1), preferred_element_type=F32)
        o = ov[:, :HEAD_DIM] / ov[:, HEAD_DIM:]
        y_ref[:, 2 * SEG + h * HEAD_DIM:2 * SEG + (h + 1) * HEAD_DIM] = (gm_ref[:, sl] * o).astype(BF16)

    _shift_rows(shift_in[0], shift_in[1], co0_ref)
    _shift_rows(shift_in[per], shift_in[per + 1], co1_ref)
    o_ref[...] = x_ref[...] + jnp.dot(y_ref[...], wo_ref[...], preferred_element_type=F32)


def _mix(z3, y_dil, mem_kv, pool_w, pool_scale, w_out, x3, layer, shifts):
    bsz, seq, _ = z3.shape
    rows = MIX_ROWS
    n_tiles = seq // rows
    halo_per_tile = rows // POOL_HALO

    def seg_spec(seg):
        return pl.BlockSpec((None, rows, SEG), lambda b, i: (b, i, seg))

    n_main = 12
    shift_specs, shift_args, shift_shapes, shift_out_specs, aliases = [], [], [], [], {}
    first_step = 0
    for n, (cache, buf) in enumerate(shifts):
        s_specs, s_args, s_shape, s_spec, s_alias = _shift_operands(
            cache, buf, layer, lambda b, i: b * n_tiles + i, bsz * n_tiles, first_step)
        first_step += cache.shape[1] * (cache.shape[2] // min(cache.shape[2], SHIFT_ROWS))
        if s_alias is not None:
            aliases[n_main + len(shift_args) + s_alias] = 1 + n
        shift_specs += s_specs
        shift_args += s_args
        shift_shapes.append(s_shape)
        shift_out_specs.append(s_spec)

    return pl.pallas_call(
        functools.partial(_mix_kernel, n_aliased=len(aliases)),
        out_shape=[jax.ShapeDtypeStruct(x3.shape, F32)] + shift_shapes,
        grid=(bsz, n_tiles),
        in_specs=[
            seg_spec(SEG_U),
            pl.BlockSpec((None, POOL_HALO, SEG),
                         lambda b, i: (b, jnp.maximum(i * halo_per_tile - 1, 0), SEG_U)),
            seg_spec(SEG_GATE_POOL),
            seg_spec(SEG_GATE_DIL),
            seg_spec(SEG_QMEM),
            seg_spec(SEG_GATE_MEM),
            pl.BlockSpec((None, rows, SEG), lambda b, i: (b, i, 0)),
            pl.BlockSpec((None, MEM_LEN, 2 * SEG), lambda b, i: (b, 0, 0)),
            pl.BlockSpec((None, len(POOL_WINDOWS), HEAD_DIM, HEAD_DIM), lambda b, i: (layer, 0, 0, 0)),
            pl.BlockSpec((1, SEG), lambda b, i: (0, 0)),
            pl.BlockSpec((None, MIX_WIDTH, D_MODEL), lambda b, i: (layer, 0, 0),
                         pipeline_mode=pl.Buffered(1)),
            pl.BlockSpec((None, rows, D_MODEL), lambda b, i: (b, i, 0)),
        ] + shift_specs,
        out_specs=[pl.BlockSpec((None, rows, D_MODEL), lambda b, i: (b, i, 0))] + shift_out_specs,
        scratch_shapes=[
            pltpu.VMEM((POOL_PAD + POOL_HALO + rows, SEG), F32),
            pltpu.VMEM((2, POOL_PAD + POOL_HALO + rows, HEAD_DIM), F32),
            pltpu.VMEM((rows, MIX_WIDTH), BF16),
        ],
        input_output_aliases=aliases,
        compiler_params=pltpu.CompilerParams(
            dimension_semantics=("arbitrary", "arbitrary"),
            vmem_limit_bytes=VMEM_LIMIT_BYTES),
        name="mix",
    )(z3, z3, z3, z3, z3, z3, y_dil, mem_kv, pool_w, pool_scale, w_out, x3, *shift_args)


def _sample_attn_kernel(zs_ref, c0_ref, c1_ref, c2_ref, cm_ref, o_ref):
    outs, lses = [], []
    for g, c_ref in enumerate((c0_ref, c1_ref, c2_ref)):
        q = zs_ref[pl.ds(_seg_q(g) * N_HEADS, N_HEADS), :]
        k_new = zs_ref[pl.ds(_seg_k(g) * N_HEADS, N_HEADS), :]
        v_new = zs_ref[pl.ds((_seg_k(g) + 1) * N_HEADS, N_HEADS), :]
        k = c_ref[:, 0]
        v = c_ref[:, 1]
        s = jnp.sum(k * q[None], axis=-1, keepdims=True)
        s_new = jnp.sum(k_new * q, axis=-1, keepdims=True)
        m = jnp.maximum(jnp.max(s, axis=0), s_new)
        p = jnp.exp(s - m[None])
        p_new = jnp.exp(s_new - m)
        l = jnp.sum(p, axis=0) + p_new
        outs.append((jnp.sum(p * v, axis=0) + p_new * v_new) / l)
        lses.append(m + jnp.log(l))
    mx = jnp.maximum(jnp.maximum(lses[0], lses[1]), lses[2])
    es = [jnp.exp(x - mx) for x in lses]
    o_ref[pl.ds(0, N_HEADS), :] = (es[0] * outs[0] + es[1] * outs[1] + es[2] * outs[2]) / (es[0] + es[1] + es[2])

    q = zs_ref[pl.ds(SEG_QMEM * N_HEADS, N_HEADS), :]
    k = cm_ref[:, 0]
    v = cm_ref[:, 1]
    s = jnp.sum(k * q[None], axis=-1, keepdims=True)
    m = jnp.max(s, axis=0)
    p = jnp.exp(s - m[None])
    o_ref[pl.ds(N_HEADS, N_HEADS), :] = jnp.sum(p * v, axis=0) / jnp.sum(p, axis=0)


def _sample_attention(zs3, caches7, cache_mem, layer):
    bsz = cache_mem.shape[1]
    in_specs = [pl.BlockSpec((None, N_SEG * N_HEADS, HEAD_DIM), lambda b: (b, 0, 0))]
    for _ in caches7:
        in_specs.append(pl.BlockSpec((None, None, BAND, None, 2, N_HEADS, HEAD_DIM),
                                     lambda b: (layer, b, 0, 0, 0, 0, 0)))
    in_specs.append(pl.BlockSpec((None, None, MEM_LEN, 2, N_HEADS, HEAD_DIM),
                                 lambda b: (layer, b, 0, 0, 0, 0)))
    return pl.pallas_call(
        _sample_attn_kernel,
        out_shape=jax.ShapeDtypeStruct((bsz, 2 * N_HEADS, HEAD_DIM), F32),
        grid=(bsz,),
        in_specs=in_specs,
        out_specs=pl.BlockSpec((None, 2 * N_HEADS, HEAD_DIM), lambda b: (b, 0, 0)),
        compiler_params=pltpu.CompilerParams(dimension_semantics=("arbitrary",)),
        name="sample_attention",
    )(zs3, *caches7, cache_mem)


def _sample_out_kernel(zs_ref, st_ref, ydm_ref, pw_ref, ps_ref, wo_ref, x_ref, o_ref, ns_ref, y_ref):
    u = zs_ref[:, pl.ds(SEG_U * SEG, SEG)]
    for gi, w in enumerate(POOL_WINDOWS):
        sl = slice(gi * HEAD_DIM, (gi + 1) * HEAD_DIM)
        tot = u[:, sl]
        for back in range(1, w):
            tot = tot + st_ref[POOL_BUF - back, :, sl]
        cnt = float(min(w, PAST_LEN + 1))
        dlt = tot / cnt - u[:, sl]
        yp = jnp.dot(dlt.astype(BF16), pw_ref[gi], preferred_element_type=F32) * ps_ref[:, sl]
        y_ref[:, sl] = (zs_ref[:, pl.ds(SEG_GATE_POOL * SEG + gi * HEAD_DIM, HEAD_DIM)] * yp).astype(BF16)
    y_ref[:, SEG:2 * SEG] = (zs_ref[:, pl.ds(SEG_GATE_DIL * SEG, SEG)] * ydm_ref[:, pl.ds(0, SEG)]).astype(BF16)
    y_ref[:, 2 * SEG:] = (zs_ref[:, pl.ds(SEG_GATE_MEM * SEG, SEG)] * ydm_ref[:, pl.ds(SEG, SEG)]).astype(BF16)
    o_ref[...] = x_ref[...] + jnp.dot(y_ref[...], wo_ref[...], preferred_element_type=F32)
    for r in range(POOL_BUF - 1):
        ns_ref[r] = st_ref[r + 1]
    ns_ref[POOL_BUF - 1] = u


def _sample_out(zs, state_t, ydm, pool_w, pool_scale, w_out, xs, layer):
    rows = zs.shape[0]
    full = lambda shape: pl.BlockSpec(shape, lambda i: tuple(0 for _ in shape))
    of_layer = lambda a: pl.BlockSpec((None,) + a.shape[1:], lambda i: (layer,) + (0,) * (a.ndim - 1))
    return pl.pallas_call(
        _sample_out_kernel,
        out_shape=[jax.ShapeDtypeStruct(xs.shape, F32), jax.ShapeDtypeStruct(state_t.shape, F32)],
        grid=(1,),
        in_specs=[full(zs.shape), full(state_t.shape), full(ydm.shape), of_layer(pool_w),
                  full(pool_scale.shape), of_layer(w_out), full(xs.shape)],
        out_specs=[full(xs.shape), full(state_t.shape)],
        scratch_shapes=[pltpu.VMEM((rows, MIX_WIDTH), BF16)],
        compiler_params=pltpu.CompilerParams(vmem_limit_bytes=VMEM_LIMIT_BYTES),
        name="sample_out",
    )(zs, state_t, ydm, pool_w, pool_scale, w_out, xs)


def _set_last_kernel(*refs):
    n = len(refs) // 3
    for new_ref, o_ref in zip(refs[:n], refs[2 * n:]):
        o_ref[0] = new_ref[...]


def _cache_set_last(bufs, new_rows):
    n = len(bufs)
    depth, bsz = bufs[0].shape[:2]
    row_shape = bufs[0].shape[3:]
    zeros = (0,) * len(row_shape)

    def last_row_spec(win):
        return pl.BlockSpec((None, None, 1) + row_shape, lambda l, b: (l, b, win - 1) + zeros)

    return pl.pallas_call(
        _set_last_kernel,
        out_shape=[jax.ShapeDtypeStruct(buf.shape, buf.dtype) for buf in bufs],
        grid=(depth, bsz),
        in_specs=[pl.BlockSpec((None, None) + row_shape, lambda l, b: (l, b) + zeros)] * n
        + [pl.BlockSpec(memory_space=pl.ANY)] * n,
        out_specs=[last_row_spec(buf.shape[2]) for buf in bufs],
        input_output_aliases={n + g: g for g in range(n)},
        compiler_params=pltpu.CompilerParams(dimension_semantics=("arbitrary", "arbitrary")),
        name="cache_set_last",
    )(*new_rows, *bufs)


def _tile_heads(v):
    return jnp.tile(v, N_HEADS)


def kernel(x_prompt, x_sample, state_pool, cache_dil_w128, cache_dil_w512, cache_dil_w2048,
           cache_mem_kv, mem_prompt, norm_g, w_in, pool_w, pool_scale, dil_q_norm, dil_k_norm,
           mem_norm_g, w_mem_kv, mem_q_norm, mem_k_norm, w_out):
    depth = w_in.shape[0]
    bsz, seq, _ = x_prompt.shape
    dbsz = x_sample.shape[0]
    caches = (cache_dil_w128, cache_dil_w512, cache_dil_w2048)

    z_pairs = tuple(zip(Z_ORDER[0::2], Z_ORDER[1::2]))
    w_out_b = w_out.astype(BF16)
    pool_w_b = pool_w.astype(BF16)

    ones = jnp.ones((SEG,), F32)
    seg_modes = [MODE_GATE if s in GATE_SEGS else MODE_PLAIN for s in range(N_SEG)]
    ep_mode = jnp.repeat(jnp.array(seg_modes, F32), SEG)[None, :]
    mem_mode = jnp.full((1, 2 * SEG), MODE_PLAIN, F32)
    caches7 =[c.reshape(depth, dbsz, win // d, d, 2, N_HEADS, HEAD_DIM)
               for c, (win, d) in zip(caches, DIL_PAIRS)]

    xp = x_prompt.reshape(bsz * seq, D_MODEL)
    xs = jnp.pad(x_sample.reshape(dbsz, D_MODEL), ((0, SAMPLE_ROWS - dbsz), (0, 0)))
    mem2 = mem_prompt.reshape(bsz * MEM_LEN, D_MODEL)

    pool_p, mem_p, pool_s, zs_rows = [], [], [], []
    prompt_caches = None
    big_cache = None
    small_caches = [None, None]
    for l in range(depth):
        segs = [ones] * N_SEG
        for g in range(len(DIL_PAIRS)):
            segs[_seg_q(g)] = _tile_heads(dil_q_norm[l, g]) * ATTN_SCALE
            segs[_seg_k(g)] = _tile_heads(dil_k_norm[l, g])
        segs[SEG_QMEM] = _tile_heads(mem_q_norm[l]) * ATTN_SCALE
        ep_gain = jnp.concatenate(segs)[None, :]
        mem_gain = jnp.concatenate([_tile_heads(mem_k_norm[l]), ones])[None, :]
        gain = norm_g[l][None, :]
        pscale = pool_scale[l][None, :]

        zs, w_first, w_second = _project(xs, gain, (w_in, l, z_pairs), ep_gain, ep_mode,
                                         rows=SAMPLE_ROWS, name="proj_sample", emit_w=True)
        zs3 = zs.reshape(SAMPLE_ROWS, N_SEG * N_HEADS, HEAD_DIM)
        ydm = _sample_attention(zs3, caches7, cache_mem_kv, l)
        ydm = jnp.pad(ydm.reshape(dbsz, 2 * SEG), ((0, SAMPLE_ROWS - dbsz), (0, 0)))
        state_t = jnp.pad(jnp.transpose(state_pool[l], (1, 0, 2)),
                          ((0, 0), (0, SAMPLE_ROWS - dbsz), (0, 0)))
        xs, new_state_t = _sample_out(zs, state_t, ydm, pool_w_b, pscale, w_out_b, xs, l)
        pool_s.append(jnp.transpose(new_state_t[:, :dbsz], (1, 0, 2)))
        zs_rows.append(zs[:dbsz].reshape(dbsz, N_SEG, N_HEADS, HEAD_DIM))

        z, big_cache = _project(xp, gain, (w_first, w_second), ep_gain, ep_mode, rows=PROJ_ROWS,
                                name="proj_prompt", shift=(caches[-1], l, big_cache))
        z3 = z.reshape(bsz, seq, IN_COLS)
        mkv, = _project(mem2, mem_norm_g[l][None, :], (w_mem_kv, l, ((0, 1),)), mem_gain, mem_mode,
                        rows=bsz * MEM_LEN, name="proj_mem")
        mkv3 = mkv.reshape(bsz, MEM_LEN, 2 * SEG)
        y_dil, *prompt_caches = _dilated_attention(z3, l, depth, prompt_caches)
        xp3, small_caches[1], small_caches[0] = _mix(
            z3, y_dil, mkv3, pool_w_b, pscale, w_out_b, xp.reshape(bsz, seq, D_MODEL), l,
            ((caches[1], small_caches[1]), (caches[0], small_caches[0])))
        xp = xp3.reshape(bsz * seq, D_MODEL)
        pool_p.append(z3[:, seq - POOL_BUF:, SEG_U * SEG:(SEG_U + 1) * SEG])
        mem_p.append(mkv3)

    zs_all = jnp.stack(zs_rows)
    new_rows = [zs_all[:, :, _seg_k(g):_seg_k(g) + 2] for g in range(len(DIL_PAIRS))]
    new_caches = _cache_set_last([small_caches[0], small_caches[1], big_cache], new_rows)

    y_prompt = xp.reshape(bsz, seq, D_MODEL)
    y_sample = xs[:dbsz].reshape(dbsz, 1, D_MODEL)
    cache_mem_prompt = jnp.stack(mem_p).reshape(depth, bsz, MEM_LEN, 2, N_HEADS, HEAD_DIM)
    return (y_prompt, y_sample, jnp.stack(pool_p), prompt_caches[0], prompt_caches[1],
            prompt_caches[2], cache_mem_prompt, jnp.stack(pool_s), new_caches[0], new_caches[1],
            new_caches[2])
```

```python
import functools

import jax
import jax.numpy as jnp
from jax import lax
from jax.experimental import pallas as pl
from jax.experimental.pallas import tpu as pltpu

F32 = jnp.float32
BF16 = jnp.bfloat16

D_MODEL = 2048
HEAD_DIM = 128
N_HEADS = 4
SEG = N_HEADS * HEAD_DIM
N_SEG = 14
IN_COLS = N_SEG * SEG
POOL_WINDOWS = (2, 4, 8, 16)
POOL_BUF = 15
POOL_HALO = 16
POOL_PAD = 8
DIL_PAIRS = ((128, 1), (512, 4), (2048, 16))
BAND = 128
MEM_LEN = 256
MIX_WIDTH = 3 * SEG
EPS = 1e-6
ATTN_SCALE = HEAD_DIM ** -0.5
PAST_LEN = 16384

Z_ORDER = (2, 0, 3, 4, 5, 1, 6, 7, 8, 11, 9, 10, 12, 13)
SEG_U, SEG_GATE_POOL, SEG_GATE_DIL, SEG_QMEM, SEG_GATE_MEM = 1, 5, 9, 12, 13
GATE_SEGS = (SEG_GATE_POOL, SEG_GATE_DIL, SEG_GATE_MEM)


def _seg_q(g):
    return 4 * g


def _seg_k(g):
    return 4 * g + 2

VMEM_LIMIT_BYTES = 56 * 1024 * 1024

PROJ_ROWS = 1024
PROJ_ROW_PARTS = 4
PROJ_MIN_PART = 256
ATTN_ROWS = 2048
MIX_ROWS = 512
SAMPLE_ROWS = 16
SHIFT_ROWS = 512
SHIFT_CHUNK = 64


MODE_PLAIN, MODE_GATE = 0.0, 1.0


def _shift_rows(c_ref, nxt_ref, o_ref):
    rows = c_ref.shape[0]
    chunk = min(rows, SHIFT_CHUNK)
    for lo in range(0, rows, chunk):
        n = chunk if lo + chunk < rows else chunk - 1
        o_ref[pl.ds(lo, n)] = c_ref[pl.ds(lo + 1, n)]
    o_ref[rows - 1] = nxt_ref[0]


def _shift_operands(cache, buf, layer, step_of, n_steps, first_step=0):
    bsz, win = cache.shape[1:3]
    rows = min(win, SHIFT_ROWS)
    row_shape = cache.shape[3:]
    zeros = (0,) * len(row_shape)
    per_b = win // rows
    n_shift = bsz * per_b
    assert first_step + n_shift <= n_steps

    def block_of(*idx):
        t = jnp.clip(step_of(*idx) - first_step, 0, n_shift - 1)
        return t // per_b, t % per_b

    def main_map(*idx):
        b, r = block_of(*idx)
        return (layer, b, r) + zeros

    def next_map(*idx):
        b, r = block_of(*idx)
        return (layer, b, jnp.minimum((r + 1) * rows, win - 1)) + zeros

    in_specs = [pl.BlockSpec((None, None, rows) + row_shape, main_map),
                pl.BlockSpec((None, None, 1) + row_shape, next_map)]
    args = [cache, cache]
    alias = None
    if buf is not None:
        in_specs.append(pl.BlockSpec(memory_space=pl.ANY))
        args.append(buf)
        alias = 2
    out_shape = jax.ShapeDtypeStruct(cache.shape, cache.dtype)
    out_spec = pl.BlockSpec((None, None, rows) + row_shape, main_map)
    return in_specs, args, out_shape, out_spec, alias


def _lookup(j, table):
    out = table[0]
    for t in range(1, len(table)):
        out = jnp.where(j == t, table[t], out)
    return out


def _proj_kernel(x_ref, g_ref, wa_ref, wb_ref, eg_ref, em_ref, *rest, with_shift, aliased, emit_w):
    rest = list(rest)
    if with_shift:
        c_ref, nxt_ref = rest[:2]
        rest = rest[3:] if aliased else rest[2:]
    o_ref = rest.pop(0)
    if emit_w:
        wa_out_ref, wb_out_ref = rest.pop(0), rest.pop(0)
    if with_shift:
        co_ref = rest.pop(0)
    h_ref, = rest
    j = pl.program_id(1)
    ii = pl.program_id(2)

    @pl.when(j == 0)
    def _():
        x = x_ref[...]
        ms = jnp.mean(x * x, axis=-1, keepdims=True)
        h_ref[ii] = (x * lax.rsqrt(ms + EPS) * g_ref[...]).astype(BF16)

    if with_shift:
        _shift_rows(c_ref, nxt_ref, co_ref)
    h_tile = h_ref[ii]
    wa = wa_ref[...].astype(BF16)
    wb = wb_ref[...].astype(BF16)
    if emit_w:
        wa_out_ref[...] = wa
        wb_out_ref[...] = wb
    rows = h_tile.shape[0]
    part = max(rows // PROJ_ROW_PARTS, min(rows, PROJ_MIN_PART))
    for r0 in range(0, rows, part):
        rsl = pl.ds(r0, part)
        acc = jnp.dot(h_tile[r0:r0 + part], wa, preferred_element_type=F32)
        for h in range(N_HEADS):
            sl = slice(h * HEAD_DIM, (h + 1) * HEAD_DIM)
            a = acc[:, sl]
            o_ref[rsl, sl] = a * (lax.rsqrt(jnp.mean(a * a, axis=-1, keepdims=True) + EPS) * eg_ref[:, sl])
    for r0 in range(0, rows, part):
        rsl = pl.ds(r0, part)
        acc = jnp.dot(h_tile[r0:r0 + part], wb, preferred_element_type=F32)
        for h in range(N_HEADS):
            sl = slice(h * HEAD_DIM, (h + 1) * HEAD_DIM)
            osl = slice(SEG + h * HEAD_DIM, SEG + (h + 1) * HEAD_DIM)
            a = acc[:, sl]
            o_ref[rsl, osl] = a * jnp.where(em_ref[:, osl] == MODE_GATE, jax.nn.sigmoid(a), 1.0)


def _project(x, gain, weights, ep_gain, ep_mode, *, gain_layer, rows, name, emit_w=False,
             shift=None):
    m, k = x.shape
    cols = 2 * SEG
    n_tiles = m // rows
    group = 2 if n_tiles % 2 == 0 else 1
    if len(weights) == 3:
        w, layer, seg_pairs = weights
        n_j = len(seg_pairs)
        firsts = tuple(p[0] for p in seg_pairs)
        seconds = tuple(p[1] for p in seg_pairs)
        w_specs = [pl.BlockSpec((None, k, SEG), lambda g, j, t: (layer, 0, _lookup(j, firsts))),
                   pl.BlockSpec((None, k, SEG), lambda g, j, t: (layer, 0, _lookup(j, seconds)))]
        w_args = [w, w]
    else:
        n_j = weights[0].shape[1] // SEG
        w_specs = [pl.BlockSpec((k, SEG), lambda g, j, t: (0, j))] * 2
        w_args = list(weights)
    n = n_j * cols

    def x_map(g, j, t):
        return (jnp.where(j == 0, g * group + t, g * group + group - 1), 0)

    in_specs = [
        pl.BlockSpec((rows, k), x_map),
        pl.BlockSpec((None, 1, k), lambda g, j, t: (gain_layer, 0, 0)),
        *w_specs,
        pl.BlockSpec((None, 1, cols), lambda g, j, t: (gain_layer, 0, j)),
        pl.BlockSpec((1, cols), lambda g, j, t: (0, j)),
    ]
    args = [x, gain, *w_args, ep_gain, ep_mode]
    out_shape = [jax.ShapeDtypeStruct((m, n), F32)]
    out_specs = [pl.BlockSpec((rows, cols), lambda g, j, t: (g * group + t, j))]
    if emit_w:
        assert n_tiles == 1
        out_shape += [jax.ShapeDtypeStruct((k, n_j * SEG), BF16)] * 2
        out_specs += [pl.BlockSpec((k, SEG), lambda g, j, t: (0, j))] * 2
    aliases = {}
    if shift is not None:
        s_specs, s_args, s_shape, s_spec, s_alias = _shift_operands(
            shift[0], shift[2], shift[1], lambda g, j, t: (g * n_j + j) * group + t, n_tiles * n_j)
        if s_alias is not None:
            aliases = {len(args) + s_alias: len(out_shape)}
        in_specs += s_specs
        args += s_args
        out_shape.append(s_shape)
        out_specs.append(s_spec)
    return pl.pallas_call(
        functools.partial(_proj_kernel, with_shift=shift is not None,
                          aliased=bool(aliases), emit_w=emit_w),
        out_shape=out_shape,
        grid=(n_tiles // group, n_j, group),
        in_specs=in_specs,
        out_specs=out_specs,
        scratch_shapes=[pltpu.VMEM((group, rows, k), BF16)],
        input_output_aliases=aliases,
        compiler_params=pltpu.CompilerParams(
            dimension_semantics=("arbitrary", "arbitrary", "arbitrary"),
            vmem_limit_bytes=VMEM_LIMIT_BYTES),
        name=name,
    )(*args)


def _dilattn_kernel(*refs, n_tiles):
    n_groups = len(DIL_PAIRS)
    ins = refs[:5 * n_groups]
    y_ref, *tails, og_ref, lse_ref = refs[len(refs) - 3 - n_groups:]
    head = pl.program_id(1)
    i = pl.program_id(2)

    row = lax.broadcasted_iota(jnp.int32, (BAND, 2 * BAND), 0)
    col = lax.broadcasted_iota(jnp.int32, (BAND, 2 * BAND), 1)
    not_future = col <= row + BAND
    band = jnp.logical_and(col >= row, not_future)
    first_lo = jnp.maximum(row, (i == 0).astype(jnp.int32) * BAND)
    band_first = jnp.logical_and(col >= first_lo, not_future)
    ones = jnp.ones((2 * BAND, HEAD_DIM), BF16)
    nt_dims = (((1,), (1,)), ((), ()))

    for g, (win, d) in enumerate(DIL_PAIRS):
        q_ref, kc_ref, vc_ref, kp_ref, vp_ref = ins[5 * g:5 * g + 5]
        span = BAND * d

        def rows_of(start, size, d=d):
            return pl.ds(start, size) if d == 1 else pl.ds(start, size, stride=d)

        for t in range(ATTN_ROWS // BAND):
            u, c = divmod(t, d)
            base = u * span + c
            q = q_ref[rows_of(base, BAND), :].astype(BF16)
            if u == 0:
                k = jnp.concatenate([kp_ref[rows_of(c, BAND), :], kc_ref[rows_of(c, BAND), :]], axis=0)
                v = jnp.concatenate([vp_ref[rows_of(c, BAND), :], vc_ref[rows_of(c, BAND), :]], axis=0)
            else:
                k = kc_ref[rows_of(base - span, 2 * BAND), :]
                v = vc_ref[rows_of(base - span, 2 * BAND), :]
            s = lax.dot_general(q, k.astype(BF16), nt_dims, preferred_element_type=F32)
            s = jnp.where(band_first if u == 0 else band, s, -jnp.inf)
            m = jnp.max(jnp.maximum(s[:, :BAND], s[:, BAND:]), axis=-1, keepdims=True)
            p = jnp.exp(s - m).astype(BF16)
            ov = jnp.dot(p, jnp.concatenate([v.astype(BF16), ones], axis=1), preferred_element_type=F32)
            l = ov[:, HEAD_DIM:]
            o_rows = rows_of(g * ATTN_ROWS + base, BAND)
            og_ref[o_rows, :] = ov[:, :HEAD_DIM] / l
            lse_ref[o_rows, :] = m + jnp.log(l)

        for hh in range(N_HEADS):
            @pl.when(jnp.logical_and(i == n_tiles - 1, head == hh))
            def _(g=g, win=win, hh=hh, kc_ref=kc_ref, vc_ref=vc_ref):
                tails[g][:, 0, hh, :] = kc_ref[pl.ds(ATTN_ROWS - win, win), :]
                tails[g][:, 1, hh, :] = vc_ref[pl.ds(ATTN_ROWS - win, win), :]

    chunk = 256

    def merge(r, carry):
        r0 = pl.multiple_of(r * chunk, chunk)
        sls = [pl.ds(g * ATTN_ROWS + r0, chunk) for g in range(len(DIL_PAIRS))]
        l0, l1, l2 = [lse_ref[sl, :] for sl in sls]
        mx = jnp.maximum(jnp.maximum(l0, l1), l2)
        e0, e1, e2 = jnp.exp(l0 - mx), jnp.exp(l1 - mx), jnp.exp(l2 - mx)
        num = e0 * og_ref[sls[0], :] + e1 * og_ref[sls[1], :] + e2 * og_ref[sls[2], :]
        y_ref[pl.ds(r0, chunk), :] = num / (e0 + e1 + e2)
        return carry

    lax.fori_loop(0, ATTN_ROWS // chunk, merge, 0)


def _dilated_attention(z3, layer, depth, cache_bufs):
    bsz, seq, _ = z3.shape
    n_tiles = seq // ATTN_ROWS
    in_specs = []
    for g, (win, d) in enumerate(DIL_PAIRS):
        span = BAND * d
        qb = _seg_q(g) * N_HEADS
        kb = _seg_k(g) * N_HEADS
        vb = kb + N_HEADS
        per_tile = ATTN_ROWS // span

        def prev_map(col0, per_tile=per_tile):
            return lambda b, h, i: (b, jnp.maximum(i * per_tile - 1, 0), col0 + h)

        def cur_map(col0):
            return lambda b, h, i: (b, i, col0 + h)

        in_specs += [
            pl.BlockSpec((None, ATTN_ROWS, HEAD_DIM), cur_map(qb)),
            pl.BlockSpec((None, ATTN_ROWS, HEAD_DIM), cur_map(kb)),
            pl.BlockSpec((None, ATTN_ROWS, HEAD_DIM), cur_map(vb)),
            pl.BlockSpec((None, span, HEAD_DIM), prev_map(kb)),
            pl.BlockSpec((None, span, HEAD_DIM), prev_map(vb)),
        ]
    args = [z3] * len(in_specs)
    out_shape = [jax.ShapeDtypeStruct((bsz, seq, SEG), F32)]
    out_specs = [pl.BlockSpec((None, ATTN_ROWS, HEAD_DIM), lambda b, h, i: (b, i, h))]
    aliases = {}
    for g, (win, _) in enumerate(DIL_PAIRS):
        out_shape.append(jax.ShapeDtypeStruct((depth, bsz, win, 2, N_HEADS, HEAD_DIM), F32))
        out_specs.append(pl.BlockSpec((None, None, win, 2, N_HEADS, HEAD_DIM),
                                      lambda b, h, i: (layer, b, 0, 0, 0, 0)))
        if cache_bufs is not None:
            in_specs.append(pl.BlockSpec(memory_space=pl.ANY))
            args.append(cache_bufs[g])
            aliases[len(args) - 1] = 1 + g
    return pl.pallas_call(
        functools.partial(_dilattn_kernel, n_tiles=n_tiles),
        out_shape=out_shape,
        grid=(bsz, N_HEADS, n_tiles),
        in_specs=in_specs,
        out_specs=out_specs,
        scratch_shapes=[
            pltpu.VMEM((len(DIL_PAIRS) * ATTN_ROWS, HEAD_DIM), F32),
            pltpu.VMEM((len(DIL_PAIRS) * ATTN_ROWS, HEAD_DIM), F32),
        ],
        input_output_aliases=aliases,
        compiler_params=pltpu.CompilerParams(
            dimension_semantics=("arbitrary", "arbitrary", "arbitrary"),
            vmem_limit_bytes=VMEM_LIMIT_BYTES),
        name="dilated_attention",
    )(*args)


def _mix_kernel(u_ref, up_ref, gp_ref, gd_ref, qm_ref, gm_ref, yd_ref, mkv_ref, pw_ref, ps_ref,
                wo_ref, x_ref, *rest, n_aliased):
    shift_in, rest = rest[:4 + n_aliased], rest[4 + n_aliased:]
    o_ref, co0_ref, co1_ref, ue_ref, t_ref, y_ref = rest
    per = 2 + n_aliased // 2
    i = pl.program_id(1)
    rows = u_ref.shape[0]

    pad, halo = POOL_PAD, POOL_HALO
    n = halo + rows

    @pl.when(i > 0)
    def _():
        ue_ref[pl.ds(pad, halo), :] = up_ref[...]

    @pl.when(i == 0)
    def _():
        ue_ref[pl.ds(pad, halo), :] = jnp.zeros((halo, SEG), F32)

    ue_ref[pl.ds(0, pad), :] = jnp.zeros((pad, SEG), F32)
    t_ref[0, pl.ds(0, pad), :] = jnp.zeros((pad, HEAD_DIM), F32)
    t_ref[1, pl.ds(0, pad), :] = jnp.zeros((pad, HEAD_DIM), F32)
    ue_ref[pl.ds(pad + halo, rows), :] = u_ref[...]
    pos = i * rows + lax.broadcasted_iota(jnp.int32, (rows, 1), 0)
    for gi, w in enumerate(POOL_WINDOWS):
        sl = slice(gi * HEAD_DIM, (gi + 1) * HEAD_DIM)
        cur = ue_ref[pl.ds(pad, n), sl] + ue_ref[pl.ds(pad - 1, n), sl]
        shift, buf = 2, 0
        while shift < w:
            t_ref[buf, pl.ds(pad, n), :] = cur
            cur = cur + t_ref[buf, pl.ds(pad - shift, n), :]
            shift, buf = 2 * shift, 1 - buf
        cnt = jnp.minimum(w, pos + 1).astype(F32)
        dlt = cur[halo:] / cnt - u_ref[:, sl]
        yp = jnp.dot(dlt.astype(BF16), pw_ref[gi], preferred_element_type=F32) * ps_ref[:, sl]
        y_ref[:, sl] = (gp_ref[:, sl] * yp).astype(BF16)

    y_ref[:, SEG:2 * SEG] = (gd_ref[...] * yd_ref[...]).astype(BF16)

    nt_dims = (((1,), (1,)), ((), ()))
    ones = jnp.ones((MEM_LEN, HEAD_DIM), BF16)
    for h in range(N_HEADS):
        sl = slice(h * HEAD_DIM, (h + 1) * HEAD_DIM)
        q = qm_ref[:, sl].astype(BF16)
        k = mkv_ref[:, sl].astype(BF16)
        v = mkv_ref[:, SEG + h * HEAD_DIM:SEG + (h + 1) * HEAD_DIM].astype(BF16)
        s = lax.dot_general(q, k, nt_dims, preferred_element_type=F32)
        m = jnp.max(s, axis=-1, keepdims=True)
        p = jnp.exp(s - m).astype(BF16)
        ov = jnp.dot(p, jnp.concatenate([v, ones], axis=1), preferred_element_type=F32)
        o = ov[:, :HEAD_DIM] / ov[:, HEAD_DIM:]
        y_ref[:, 2 * SEG + h * HEAD_DIM:2 * SEG + (h + 1) * HEAD_DIM] = (gm_ref[:, sl] * o).astype(BF16)

    _shift_rows(shift_in[0], shift_in[1], co0_ref)
    _shift_rows(shift_in[per], shift_in[per + 1], co1_ref)
    o_ref[...] = x_ref[...] + jnp.dot(y_ref[...], wo_ref[...], preferred_element_type=F32)


def _mix(z3, y_dil, mem_kv, pool_w, pool_scale, w_out, x3, layer, shifts):
    bsz, seq, _ = z3.shape
    rows = MIX_ROWS
    n_tiles = seq // rows
    halo_per_tile = rows // POOL_HALO

    def seg_spec(seg):
        return pl.BlockSpec((None, rows, SEG), lambda b, i: (b, i, seg))

    n_main = 12
    shift_specs, shift_args, shift_shapes, shift_out_specs, aliases = [], [], [], [], {}
    first_step = 0
    for n, (cache, buf) in enumerate(shifts):
        s_specs, s_args, s_shape, s_spec, s_alias = _shift_operands(
            cache, buf, layer, lambda b, i: b * n_tiles + i, bsz * n_tiles, first_step)
        first_step += cache.shape[1] * (cache.shape[2] // min(cache.shape[2], SHIFT_ROWS))
        if s_alias is not None:
            aliases[n_main + len(shift_args) + s_alias] = 1 + n
        shift_specs += s_specs
        shift_args += s_args
        shift_shapes.append(s_shape)
        shift_out_specs.append(s_spec)

    return pl.pallas_call(
        functools.partial(_mix_kernel, n_aliased=len(aliases)),
        out_shape=[jax.ShapeDtypeStruct(x3.shape, F32)] + shift_shapes,
        grid=(bsz, n_tiles),
        in_specs=[
            seg_spec(SEG_U),
            pl.BlockSpec((None, POOL_HALO, SEG),
                         lambda b, i: (b, jnp.maximum(i * halo_per_tile - 1, 0), SEG_U)),
            seg_spec(SEG_GATE_POOL),
            seg_spec(SEG_GATE_DIL),
            seg_spec(SEG_QMEM),
            seg_spec(SEG_GATE_MEM),
            pl.BlockSpec((None, rows, SEG), lambda b, i: (b, i, 0)),
            pl.BlockSpec((None, MEM_LEN, 2 * SEG), lambda b, i: (b, 0, 0)),
            pl.BlockSpec((None, len(POOL_WINDOWS), HEAD_DIM, HEAD_DIM), lambda b, i: (layer, 0, 0, 0)),
            pl.BlockSpec((None, 1, SEG), lambda b, i: (layer, 0, 0)),
            pl.BlockSpec((MIX_WIDTH, D_MODEL), lambda b, i: (0, 0), pipeline_mode=pl.Buffered(1)),
            pl.BlockSpec((None, rows, D_MODEL), lambda b, i: (b, i, 0)),
        ] + shift_specs,
        out_specs=[pl.BlockSpec((None, rows, D_MODEL), lambda b, i: (b, i, 0))] + shift_out_specs,
        scratch_shapes=[
            pltpu.VMEM((POOL_PAD + POOL_HALO + rows, SEG), F32),
            pltpu.VMEM((2, POOL_PAD + POOL_HALO + rows, HEAD_DIM), F32),
            pltpu.VMEM((rows, MIX_WIDTH), BF16),
        ],
        input_output_aliases=aliases,
        compiler_params=pltpu.CompilerParams(
            dimension_semantics=("arbitrary", "arbitrary"),
            vmem_limit_bytes=VMEM_LIMIT_BYTES),
        name="mix",
    )(z3, z3, z3, z3, z3, z3, y_dil, mem_kv, pool_w, pool_scale, w_out, x3, *shift_args)


def _sample_attn_kernel(zs_ref, c0_ref, c1_ref, c2_ref, cm_ref, o_ref):
    outs, lses = [], []
    for g, c_ref in enumerate((c0_ref, c1_ref, c2_ref)):
        q = zs_ref[pl.ds(_seg_q(g) * N_HEADS, N_HEADS), :]
        k_new = zs_ref[pl.ds(_seg_k(g) * N_HEADS, N_HEADS), :]
        v_new = zs_ref[pl.ds((_seg_k(g) + 1) * N_HEADS, N_HEADS), :]
        k = c_ref[:, 0]
        v = c_ref[:, 1]
        s = jnp.sum(k * q[None], axis=-1, keepdims=True)
        s_new = jnp.sum(k_new * q, axis=-1, keepdims=True)
        m = jnp.maximum(jnp.max(s, axis=0), s_new)
        p = jnp.exp(s - m[None])
        p_new = jnp.exp(s_new - m)
        l = jnp.sum(p, axis=0) + p_new
        outs.append((jnp.sum(p * v, axis=0) + p_new * v_new) / l)
        lses.append(m + jnp.log(l))
    mx = jnp.maximum(jnp.maximum(lses[0], lses[1]), lses[2])
    es = [jnp.exp(x - mx) for x in lses]
    o_ref[pl.ds(0, N_HEADS), :] = (es[0] * outs[0] + es[1] * outs[1] + es[2] * outs[2]) / (es[0] + es[1] + es[2])

    q = zs_ref[pl.ds(SEG_QMEM * N_HEADS, N_HEADS), :]
    k = cm_ref[:, 0]
    v = cm_ref[:, 1]
    s = jnp.sum(k * q[None], axis=-1, keepdims=True)
    m = jnp.max(s, axis=0)
    p = jnp.exp(s - m[None])
    o_ref[pl.ds(N_HEADS, N_HEADS), :] = jnp.sum(p * v, axis=0) / jnp.sum(p, axis=0)


def _sample_attention(zs3, caches7, cache_mem, layer):
    bsz = cache_mem.shape[1]
    in_specs = [pl.BlockSpec((None, N_SEG * N_HEADS, HEAD_DIM), lambda b: (b, 0, 0))]
    for _ in caches7:
        in_specs.append(pl.BlockSpec((None, None, BAND, None, 2, N_HEADS, HEAD_DIM),
                                     lambda b: (layer, b, 0, 0, 0, 0, 0)))
    in_specs.append(pl.BlockSpec((None, None, MEM_LEN, 2, N_HEADS, HEAD_DIM),
                                 lambda b: (layer, b, 0, 0, 0, 0)))
    return pl.pallas_call(
        _sample_attn_kernel,
        out_shape=jax.ShapeDtypeStruct((bsz, 2 * N_HEADS, HEAD_DIM), F32),
        grid=(bsz,),
        in_specs=in_specs,
        out_specs=pl.BlockSpec((None, 2 * N_HEADS, HEAD_DIM), lambda b: (b, 0, 0)),
        compiler_params=pltpu.CompilerParams(dimension_semantics=("arbitrary",)),
        name="sample_attention",
    )(zs3, *caches7, cache_mem)


def _sample_out_kernel(zs_ref, st_ref, ydm_ref, pw_ref, ps_ref, wo_ref, x_ref, o_ref, ns_ref,
                       wob_ref, y_ref):
    wob_ref[...] = wo_ref[...].astype(BF16)
    u = zs_ref[:, pl.ds(SEG_U * SEG, SEG)]
    for gi, w in enumerate(POOL_WINDOWS):
        sl = slice(gi * HEAD_DIM, (gi + 1) * HEAD_DIM)
        tot = u[:, sl]
        for back in range(1, w):
            tot = tot + st_ref[POOL_BUF - back, :, sl]
        cnt = float(min(w, PAST_LEN + 1))
        dlt = tot / cnt - u[:, sl]
        yp = jnp.dot(dlt.astype(BF16), pw_ref[gi], preferred_element_type=F32) * ps_ref[:, sl]
        y_ref[:, sl] = (zs_ref[:, pl.ds(SEG_GATE_POOL * SEG + gi * HEAD_DIM, HEAD_DIM)] * yp).astype(BF16)
    y_ref[:, SEG:2 * SEG] = (zs_ref[:, pl.ds(SEG_GATE_DIL * SEG, SEG)] * ydm_ref[:, pl.ds(0, SEG)]).astype(BF16)
    y_ref[:, 2 * SEG:] = (zs_ref[:, pl.ds(SEG_GATE_MEM * SEG, SEG)] * ydm_ref[:, pl.ds(SEG, SEG)]).astype(BF16)
    o_ref[...] = x_ref[...] + jnp.dot(y_ref[...], wob_ref[...], preferred_element_type=F32)
    for r in range(POOL_BUF - 1):
        ns_ref[r] = st_ref[r + 1]
    ns_ref[POOL_BUF - 1] = u


def _sample_out(zs, state_t, ydm, pool_w, pool_scale, w_out, xs, layer):
    rows = zs.shape[0]
    full = lambda shape: pl.BlockSpec(shape, lambda i: tuple(0 for _ in shape))
    of_layer = lambda a: pl.BlockSpec((None,) + a.shape[1:], lambda i: (layer,) + (0,) * (a.ndim - 1))
    return pl.pallas_call(
        _sample_out_kernel,
        out_shape=[jax.ShapeDtypeStruct(xs.shape, F32),
                   jax.ShapeDtypeStruct(state_t.shape[1:], F32),
                   jax.ShapeDtypeStruct(w_out.shape[1:], BF16)],
        grid=(1,),
        in_specs=[full(zs.shape), of_layer(state_t), full(ydm.shape), of_layer(pool_w),
                  of_layer(pool_scale), of_layer(w_out), full(xs.shape)],
        out_specs=[full(xs.shape), full(state_t.shape[1:]), full(w_out.shape[1:])],
        scratch_shapes=[pltpu.VMEM((rows, MIX_WIDTH), BF16)],
        compiler_params=pltpu.CompilerParams(
            dimension_semantics=("arbitrary",), vmem_limit_bytes=VMEM_LIMIT_BYTES),
        name="sample_out",
    )(zs, state_t, ydm, pool_w, pool_scale, w_out, xs)


def _set_last_kernel(*refs):
    n = len(refs) // 3
    for new_ref, o_ref in zip(refs[:n], refs[2 * n:]):
        o_ref[0] = new_ref[...]


def _cache_set_last(bufs, new_rows):
    n = len(bufs)
    depth, bsz = bufs[0].shape[:2]
    row_shape = bufs[0].shape[3:]
    zeros = (0,) * len(row_shape)

    def last_row_spec(win):
        return pl.BlockSpec((None, None, 1) + row_shape, lambda l, b: (l, b, win - 1) + zeros)

    return pl.pallas_call(
        _set_last_kernel,
        out_shape=[jax.ShapeDtypeStruct(buf.shape, buf.dtype) for buf in bufs],
        grid=(depth, bsz),
        in_specs=[pl.BlockSpec((None, None) + row_shape, lambda l, b: (l, b) + zeros)] * n
        + [pl.BlockSpec(memory_space=pl.ANY)] * n,
        out_specs=[last_row_spec(buf.shape[2]) for buf in bufs],
        input_output_aliases={n + g: g for g in range(n)},
        compiler_params=pltpu.CompilerParams(dimension_semantics=("arbitrary", "arbitrary")),
        name="cache_set_last",
    )(*new_rows, *bufs)


def _tile_heads(v):
    return jnp.tile(v, N_HEADS)


def kernel(x_prompt, x_sample, state_pool, cache_dil_w128, cache_dil_w512, cache_dil_w2048,
           cache_mem_kv, mem_prompt, norm_g, w_in, pool_w, pool_scale, dil_q_norm, dil_k_norm,
           mem_norm_g, w_mem_kv, mem_q_norm, mem_k_norm, w_out):
    depth = w_in.shape[0]
    bsz, seq, _ = x_prompt.shape
    dbsz = x_sample.shape[0]
    caches = (cache_dil_w128, cache_dil_w512, cache_dil_w2048)

    z_pairs = tuple(zip(Z_ORDER[0::2], Z_ORDER[1::2]))
    pool_w_b = pool_w.astype(BF16)

    ones = jnp.ones((depth, SEG), F32)
    segs = [ones] * N_SEG
    for g in range(len(DIL_PAIRS)):
        segs[_seg_q(g)] = _tile_heads(dil_q_norm[:, g]) * ATTN_SCALE
        segs[_seg_k(g)] = _tile_heads(dil_k_norm[:, g])
    segs[SEG_QMEM] = _tile_heads(mem_q_norm) * ATTN_SCALE
    ep_gain = jnp.concatenate(segs, axis=1)[:, None, :]
    mem_gain = jnp.concatenate([_tile_heads(mem_k_norm), ones], axis=1)[:, None, :]
    seg_modes = [MODE_GATE if s in GATE_SEGS else MODE_PLAIN for s in range(N_SEG)]
    ep_mode = jnp.repeat(jnp.array(seg_modes, F32), SEG)[None, :]
    mem_mode = jnp.full((1, 2 * SEG), MODE_PLAIN, F32)
    gain = norm_g[:, None, :]
    mem_norm = mem_norm_g[:, None, :]
    pscale = pool_scale[:, None, :]

    caches7 = [c.reshape(depth, dbsz, win // d, d, 2, N_HEADS, HEAD_DIM)
               for c, (win, d) in zip(caches, DIL_PAIRS)]
    state_t = jnp.pad(jnp.transpose(state_pool, (0, 2, 1, 3)),
                      ((0, 0), (0, 0), (0, SAMPLE_ROWS - dbsz), (0, 0)))

    xp = x_prompt.reshape(bsz * seq, D_MODEL)
    xs = jnp.pad(x_sample.reshape(dbsz, D_MODEL), ((0, SAMPLE_ROWS - dbsz), (0, 0)))
    mem2 = mem_prompt.reshape(bsz * MEM_LEN, D_MODEL)

    pool_p, mem_p, pool_s, zs_rows = [], [], [], []
    prompt_caches = None
    big_cache = None
    small_caches = [None, None]
    for l in range(depth):
        zs, w_first, w_second = _project(xs, gain, (w_in, l, z_pairs), ep_gain, ep_mode,
                                         gain_layer=l, rows=SAMPLE_ROWS, name="proj_sample",
                                         emit_w=True)
        zs3 = zs.reshape(SAMPLE_ROWS, N_SEG * N_HEADS, HEAD_DIM)
        ydm = _sample_attention(zs3, caches7, cache_mem_kv, l)
        ydm = jnp.pad(ydm.reshape(dbsz, 2 * SEG), ((0, SAMPLE_ROWS - dbsz), (0, 0)))
        xs, new_state_t, w_out_b = _sample_out(zs, state_t, ydm, pool_w_b, pscale, w_out, xs, l)
        pool_s.append(new_state_t)
        zs_rows.append(zs)

        z, big_cache = _project(xp, gain, (w_first, w_second), ep_gain, ep_mode, gain_layer=l,
                                rows=PROJ_ROWS, name="proj_prompt",
                                shift=(caches[-1], l, big_cache))
        z3 = z.reshape(bsz, seq, IN_COLS)
        mkv, = _project(mem2, mem_norm, (w_mem_kv, l, ((0, 1),)), mem_gain, mem_mode,
                        gain_layer=l, rows=bsz * MEM_LEN, name="proj_mem")
        mkv3 = mkv.reshape(bsz, MEM_LEN, 2 * SEG)
        y_dil, *prompt_caches = _dilated_attention(z3, l, depth, prompt_caches)
        xp3, small_caches[1], small_caches[0] = _mix(
            z3, y_dil, mkv3, pool_w_b, pscale, w_out_b, xp.reshape(bsz, seq, D_MODEL), l,
            ((caches[1], small_caches[1]), (caches[0], small_caches[0])))
        xp = xp3.reshape(bsz * seq, D_MODEL)
        pool_p.append(z3[:, seq - POOL_BUF:, SEG_U * SEG:(SEG_U + 1) * SEG])
        mem_p.append(mkv3)

    zs_all = jnp.stack(zs_rows)[:, :dbsz].reshape(depth, dbsz, N_SEG, N_HEADS, HEAD_DIM)
    new_rows = [zs_all[:, :, _seg_k(g):_seg_k(g) + 2] for g in range(len(DIL_PAIRS))]
    new_caches = _cache_set_last([small_caches[0], small_caches[1], big_cache], new_rows)

    y_prompt = xp.reshape(bsz, seq, D_MODEL)
    y_sample = xs[:dbsz].reshape(dbsz, 1, D_MODEL)
    cache_mem_prompt = jnp.stack(mem_p).reshape(depth, bsz, MEM_LEN, 2, N_HEADS, HEAD_DIM)
    state_pool_sample = jnp.transpose(jnp.stack(pool_s)[:, :, :dbsz], (0, 2, 1, 3))
    return (y_prompt, y_sample, jnp.stack(pool_p), prompt_caches[0], prompt_caches[1],
            prompt_caches[2], cache_mem_prompt, state_pool_sample, new_caches[0], new_caches[1],
            new_caches[2])
```

```python
import functools

import jax
import jax.numpy as jnp
from jax import lax
from jax.experimental import pallas as pl
from jax.experimental.pallas import tpu as pltpu

F32 = jnp.float32
BF16 = jnp.bfloat16

D_MODEL = 2048
HEAD_DIM = 128
N_HEADS = 4
SEG = N_HEADS * HEAD_DIM
N_SEG = 14
IN_COLS = N_SEG * SEG
POOL_WINDOWS = (2, 4, 8, 16)
POOL_BUF = 15
POOL_HALO = 16
POOL_PAD = 8
DIL_PAIRS = ((128, 1), (512, 4), (2048, 16))
BAND = 128
MEM_LEN = 256
MIX_WIDTH = 3 * SEG
EPS = 1e-6
ATTN_SCALE = HEAD_DIM ** -0.5
PAST_LEN = 16384

Z_ORDER = (2, 0, 3, 4, 5, 1, 6, 7, 8, 11, 9, 10, 12, 13)
SEG_U, SEG_GATE_POOL, SEG_GATE_DIL, SEG_QMEM, SEG_GATE_MEM = 1, 5, 9, 12, 13
GATE_SEGS = (SEG_GATE_POOL, SEG_GATE_DIL, SEG_GATE_MEM)


def _seg_q(g):
    return 4 * g


def _seg_k(g):
    return 4 * g + 2

VMEM_LIMIT_BYTES = 56 * 1024 * 1024

PROJ_ROWS = 1024
PROJ_ROW_PARTS = 4
PROJ_MIN_PART = 256
ATTN_ROWS = 2048
MIX_ROWS = 512
SAMPLE_ROWS = 16
KV_ROW = 2 * N_HEADS
SHIFT_ROWS = 512
SHIFT_CHUNK = 64


MODE_PLAIN, MODE_GATE = 0.0, 1.0


def _shift_rows(c_ref, nxt_ref, o_ref):
    n = c_ref.shape[0]
    chunk = min(n, SHIFT_CHUNK * KV_ROW)
    for lo in range(0, n - KV_ROW, chunk):
        size = min(chunk, n - KV_ROW - lo)
        o_ref[pl.ds(lo, size), :] = c_ref[pl.ds(lo + KV_ROW, size), :]
    o_ref[pl.ds(n - KV_ROW, KV_ROW), :] = nxt_ref[...]


def _shift_operands(cache, buf, layer, step_of, n_steps, first_step=0):
    bsz = cache.shape[1]
    win = cache.shape[2] // KV_ROW
    rows = min(win, SHIFT_ROWS)
    per_b = win // rows
    n_shift = bsz * per_b
    assert first_step + n_shift <= n_steps

    def block_of(*idx):
        t = jnp.clip(step_of(*idx) - first_step, 0, n_shift - 1)
        return t // per_b, t % per_b

    def main_map(*idx):
        b, r = block_of(*idx)
        return (layer, b, r, 0)

    def next_map(*idx):
        b, r = block_of(*idx)
        return (layer, b, jnp.minimum((r + 1) * rows, win - 1), 0)

    in_specs = [pl.BlockSpec((None, None, rows * KV_ROW, HEAD_DIM), main_map),
                pl.BlockSpec((None, None, KV_ROW, HEAD_DIM), next_map)]
    args = [cache, cache]
    alias = None
    if buf is not None:
        in_specs.append(pl.BlockSpec(memory_space=pl.ANY))
        args.append(buf)
        alias = 2
    out_shape = jax.ShapeDtypeStruct(cache.shape, cache.dtype)
    out_spec = pl.BlockSpec((None, None, rows * KV_ROW, HEAD_DIM), main_map)
    return in_specs, args, out_shape, out_spec, alias


def _lookup(j, table):
    out = table[0]
    for t in range(1, len(table)):
        out = jnp.where(j == t, table[t], out)
    return out


def _proj_kernel(x_ref, g_ref, wa_ref, wb_ref, eg_ref, em_ref, *rest, with_shift, aliased, emit_w):
    rest = list(rest)
    if with_shift:
        c_ref, nxt_ref = rest[:2]
        rest = rest[3:] if aliased else rest[2:]
    o_ref = rest.pop(0)
    if emit_w:
        wa_out_ref, wb_out_ref = rest.pop(0), rest.pop(0)
    if with_shift:
        co_ref = rest.pop(0)
    h_ref, = rest
    j = pl.program_id(1)
    ii = pl.program_id(2)

    @pl.when(j == 0)
    def _():
        x = x_ref[...]
        ms = jnp.mean(x * x, axis=-1, keepdims=True)
        h_ref[ii] = (x * lax.rsqrt(ms + EPS) * g_ref[...]).astype(BF16)

    if with_shift:
        _shift_rows(c_ref, nxt_ref, co_ref)
    h_tile = h_ref[ii]
    wa = wa_ref[...].astype(BF16)
    wb = wb_ref[...].astype(BF16)
    if emit_w:
        wa_out_ref[...] = wa
        wb_out_ref[...] = wb
    rows = h_tile.shape[0]
    part = max(rows // PROJ_ROW_PARTS, min(rows, PROJ_MIN_PART))
    for r0 in range(0, rows, part):
        rsl = pl.ds(r0, part)
        acc = jnp.dot(h_tile[r0:r0 + part], wa, preferred_element_type=F32)
        for h in range(N_HEADS):
            sl = slice(h * HEAD_DIM, (h + 1) * HEAD_DIM)
            a = acc[:, sl]
            o_ref[rsl, sl] = a * (lax.rsqrt(jnp.mean(a * a, axis=-1, keepdims=True) + EPS) * eg_ref[:, sl])
    for r0 in range(0, rows, part):
        rsl = pl.ds(r0, part)
        acc = jnp.dot(h_tile[r0:r0 + part], wb, preferred_element_type=F32)
        for h in range(N_HEADS):
            sl = slice(h * HEAD_DIM, (h + 1) * HEAD_DIM)
            osl = slice(SEG + h * HEAD_DIM, SEG + (h + 1) * HEAD_DIM)
            a = acc[:, sl]
            o_ref[rsl, osl] = a * jnp.where(em_ref[:, osl] == MODE_GATE, jax.nn.sigmoid(a), 1.0)


def _project(x, gain, weights, ep_gain, ep_mode, *, gain_layer, rows, name, emit_w=False,
             shift=None):
    m, k = x.shape
    cols = 2 * SEG
    n_tiles = m // rows
    group = 2 if n_tiles % 2 == 0 else 1
    if len(weights) == 3:
        w, layer, seg_pairs = weights
        n_j = len(seg_pairs)
        firsts = tuple(p[0] for p in seg_pairs)
        seconds = tuple(p[1] for p in seg_pairs)
        w_specs = [pl.BlockSpec((None, k, SEG), lambda g, j, t: (layer, 0, _lookup(j, firsts))),
                   pl.BlockSpec((None, k, SEG), lambda g, j, t: (layer, 0, _lookup(j, seconds)))]
        w_args = [w, w]
    else:
        n_j = weights[0].shape[1] // SEG
        w_specs = [pl.BlockSpec((k, SEG), lambda g, j, t: (0, j))] * 2
        w_args = list(weights)
    n = n_j * cols

    def x_map(g, j, t):
        return (jnp.where(j == 0, g * group + t, g * group + group - 1), 0)

    in_specs = [
        pl.BlockSpec((rows, k), x_map),
        pl.BlockSpec((None, 1, k), lambda g, j, t: (gain_layer, 0, 0)),
        *w_specs,
        pl.BlockSpec((None, 1, cols), lambda g, j, t: (gain_layer, 0, j)),
        pl.BlockSpec((1, cols), lambda g, j, t: (0, j)),
    ]
    args = [x, gain, *w_args, ep_gain, ep_mode]
    out_shape = [jax.ShapeDtypeStruct((m, n), F32)]
    out_specs = [pl.BlockSpec((rows, cols), lambda g, j, t: (g * group + t, j))]
    if emit_w:
        assert n_tiles == 1
        out_shape += [jax.ShapeDtypeStruct((k, n_j * SEG), BF16)] * 2
        out_specs += [pl.BlockSpec((k, SEG), lambda g, j, t: (0, j))] * 2
    aliases = {}
    if shift is not None:
        s_specs, s_args, s_shape, s_spec, s_alias = _shift_operands(
            shift[0], shift[2], shift[1], lambda g, j, t: (g * n_j + j) * group + t, n_tiles * n_j)
        if s_alias is not None:
            aliases = {len(args) + s_alias: len(out_shape)}
        in_specs += s_specs
        args += s_args
        out_shape.append(s_shape)
        out_specs.append(s_spec)
    return pl.pallas_call(
        functools.partial(_proj_kernel, with_shift=shift is not None,
                          aliased=bool(aliases), emit_w=emit_w),
        out_shape=out_shape,
        grid=(n_tiles // group, n_j, group),
        in_specs=in_specs,
        out_specs=out_specs,
        scratch_shapes=[pltpu.VMEM((group, rows, k), BF16)],
        input_output_aliases=aliases,
        compiler_params=pltpu.CompilerParams(
            dimension_semantics=("arbitrary", "arbitrary", "arbitrary"),
            vmem_limit_bytes=VMEM_LIMIT_BYTES),
        name=name,
    )(*args)


def _dilattn_kernel(*refs, n_tiles):
    n_groups = len(DIL_PAIRS)
    ins = refs[:5 * n_groups]
    y_ref, *tails, og_ref, lse_ref = refs[len(refs) - 3 - n_groups:]
    head = pl.program_id(1)
    i = pl.program_id(2)

    row = lax.broadcasted_iota(jnp.int32, (BAND, 2 * BAND), 0)
    col = lax.broadcasted_iota(jnp.int32, (BAND, 2 * BAND), 1)
    not_future = col <= row + BAND
    band = jnp.logical_and(col >= row, not_future)
    first_lo = jnp.maximum(row, (i == 0).astype(jnp.int32) * BAND)
    band_first = jnp.logical_and(col >= first_lo, not_future)
    ones = jnp.ones((2 * BAND, HEAD_DIM), BF16)
    nt_dims = (((1,), (1,)), ((), ()))

    for g, (win, d) in enumerate(DIL_PAIRS):
        q_ref, kc_ref, vc_ref, kp_ref, vp_ref = ins[5 * g:5 * g + 5]
        span = BAND * d

        def rows_of(start, size, d=d):
            return pl.ds(start, size) if d == 1 else pl.ds(start, size, stride=d)

        for t in range(ATTN_ROWS // BAND):
            u, c = divmod(t, d)
            base = u * span + c
            q = q_ref[rows_of(base, BAND), :].astype(BF16)
            if u == 0:
                k = jnp.concatenate([kp_ref[rows_of(c, BAND), :], kc_ref[rows_of(c, BAND), :]], axis=0)
                v = jnp.concatenate([vp_ref[rows_of(c, BAND), :], vc_ref[rows_of(c, BAND), :]], axis=0)
            else:
                k = kc_ref[rows_of(base - span, 2 * BAND), :]
                v = vc_ref[rows_of(base - span, 2 * BAND), :]
            s = lax.dot_general(q, k.astype(BF16), nt_dims, preferred_element_type=F32)
            s = jnp.where(band_first if u == 0 else band, s, -jnp.inf)
            m = jnp.max(jnp.maximum(s[:, :BAND], s[:, BAND:]), axis=-1, keepdims=True)
            p = jnp.exp(s - m).astype(BF16)
            ov = jnp.dot(p, jnp.concatenate([v.astype(BF16), ones], axis=1), preferred_element_type=F32)
            l = ov[:, HEAD_DIM:]
            o_rows = rows_of(g * ATTN_ROWS + base, BAND)
            og_ref[o_rows, :] = ov[:, :HEAD_DIM] / l
            lse_ref[o_rows, :] = m + jnp.log(l)

        for hh in range(N_HEADS):
            @pl.when(jnp.logical_and(i == n_tiles - 1, head == hh))
            def _(g=g, win=win, hh=hh, kc_ref=kc_ref, vc_ref=vc_ref):
                tails[g][pl.ds(hh, win, stride=KV_ROW), :] = kc_ref[pl.ds(ATTN_ROWS - win, win), :]
                tails[g][pl.ds(N_HEADS + hh, win, stride=KV_ROW), :] = (
                    vc_ref[pl.ds(ATTN_ROWS - win, win), :])

    chunk = 256

    def merge(r, carry):
        r0 = pl.multiple_of(r * chunk, chunk)
        sls = [pl.ds(g * ATTN_ROWS + r0, chunk) for g in range(len(DIL_PAIRS))]
        l0, l1, l2 = [lse_ref[sl, :] for sl in sls]
        mx = jnp.maximum(jnp.maximum(l0, l1), l2)
        e0, e1, e2 = jnp.exp(l0 - mx), jnp.exp(l1 - mx), jnp.exp(l2 - mx)
        num = e0 * og_ref[sls[0], :] + e1 * og_ref[sls[1], :] + e2 * og_ref[sls[2], :]
        y_ref[pl.ds(r0, chunk), :] = num / (e0 + e1 + e2)
        return carry

    lax.fori_loop(0, ATTN_ROWS // chunk, merge, 0)


def _dilated_attention(z3, layer, depth, cache_bufs):
    bsz, seq, _ = z3.shape
    n_tiles = seq // ATTN_ROWS
    in_specs = []
    for g, (win, d) in enumerate(DIL_PAIRS):
        span = BAND * d
        qb = _seg_q(g) * N_HEADS
        kb = _seg_k(g) * N_HEADS
        vb = kb + N_HEADS
        per_tile = ATTN_ROWS // span

        def prev_map(col0, per_tile=per_tile):
            return lambda b, h, i: (b, jnp.maximum(i * per_tile - 1, 0), col0 + h)

        def cur_map(col0):
            return lambda b, h, i: (b, i, col0 + h)

        in_specs += [
            pl.BlockSpec((None, ATTN_ROWS, HEAD_DIM), cur_map(qb)),
            pl.BlockSpec((None, ATTN_ROWS, HEAD_DIM), cur_map(kb)),
            pl.BlockSpec((None, ATTN_ROWS, HEAD_DIM), cur_map(vb)),
            pl.BlockSpec((None, span, HEAD_DIM), prev_map(kb)),
            pl.BlockSpec((None, span, HEAD_DIM), prev_map(vb)),
        ]
    args = [z3] * len(in_specs)
    out_shape = [jax.ShapeDtypeStruct((bsz, seq, SEG), F32)]
    out_specs = [pl.BlockSpec((None, ATTN_ROWS, HEAD_DIM), lambda b, h, i: (b, i, h))]
    aliases = {}
    for g, (win, _) in enumerate(DIL_PAIRS):
        out_shape.append(jax.ShapeDtypeStruct((depth, bsz, win * KV_ROW, HEAD_DIM), F32))
        out_specs.append(pl.BlockSpec((None, None, win * KV_ROW, HEAD_DIM),
                                      lambda b, h, i: (layer, b, 0, 0)))
        if cache_bufs is not None:
            in_specs.append(pl.BlockSpec(memory_space=pl.ANY))
            args.append(cache_bufs[g])
            aliases[len(args) - 1] = 1 + g
    return pl.pallas_call(
        functools.partial(_dilattn_kernel, n_tiles=n_tiles),
        out_shape=out_shape,
        grid=(bsz, N_HEADS, n_tiles),
        in_specs=in_specs,
        out_specs=out_specs,
        scratch_shapes=[
            pltpu.VMEM((len(DIL_PAIRS) * ATTN_ROWS, HEAD_DIM), F32),
            pltpu.VMEM((len(DIL_PAIRS) * ATTN_ROWS, HEAD_DIM), F32),
        ],
        input_output_aliases=aliases,
        compiler_params=pltpu.CompilerParams(
            dimension_semantics=("arbitrary", "arbitrary", "arbitrary"),
            vmem_limit_bytes=VMEM_LIMIT_BYTES),
        name="dilated_attention",
    )(*args)


def _mix_kernel(u_ref, up_ref, gp_ref, gd_ref, qm_ref, gm_ref, yd_ref, mkv_ref, pw_ref, ps_ref,
                wo_ref, x_ref, *rest, n_aliased):
    shift_in, rest = rest[:4 + n_aliased], rest[4 + n_aliased:]
    o_ref, co0_ref, co1_ref, ue_ref, t_ref, y_ref = rest
    per = 2 + n_aliased // 2
    i = pl.program_id(1)
    rows = u_ref.shape[0]

    pad, halo = POOL_PAD, POOL_HALO
    n = halo + rows

    @pl.when(i > 0)
    def _():
        ue_ref[pl.ds(pad, halo), :] = up_ref[...]

    @pl.when(i == 0)
    def _():
        ue_ref[pl.ds(pad, halo), :] = jnp.zeros((halo, SEG), F32)

    ue_ref[pl.ds(0, pad), :] = jnp.zeros((pad, SEG), F32)
    t_ref[0, pl.ds(0, pad), :] = jnp.zeros((pad, HEAD_DIM), F32)
    t_ref[1, pl.ds(0, pad), :] = jnp.zeros((pad, HEAD_DIM), F32)
    ue_ref[pl.ds(pad + halo, rows), :] = u_ref[...]
    pos = i * rows + lax.broadcasted_iota(jnp.int32, (rows, 1), 0)
    for gi, w in enumerate(POOL_WINDOWS):
        sl = slice(gi * HEAD_DIM, (gi + 1) * HEAD_DIM)
        cur = ue_ref[pl.ds(pad, n), sl] + ue_ref[pl.ds(pad - 1, n), sl]
        shift, buf = 2, 0
        while shift < w:
            t_ref[buf, pl.ds(pad, n), :] = cur
            cur = cur + t_ref[buf, pl.ds(pad - shift, n), :]
            shift, buf = 2 * shift, 1 - buf
        cnt = jnp.minimum(w, pos + 1).astype(F32)
        dlt = cur[halo:] / cnt - u_ref[:, sl]
        yp = jnp.dot(dlt.astype(BF16), pw_ref[gi], preferred_element_type=F32) * ps_ref[:, sl]
        y_ref[:, sl] = (gp_ref[:, sl] * yp).astype(BF16)

    y_ref[:, SEG:2 * SEG] = (gd_ref[...] * yd_ref[...]).astype(BF16)

    nt_dims = (((1,), (1,)), ((), ()))
    ones = jnp.ones((MEM_LEN, HEAD_DIM), BF16)
    for h in range(N_HEADS):
        sl = slice(h * HEAD_DIM, (h + 1) * HEAD_DIM)
        q = qm_ref[:, sl].astype(BF16)
        k = mkv_ref[:, sl].astype(BF16)
        v = mkv_ref[:, SEG + h * HEAD_DIM:SEG + (h + 1) * HEAD_DIM].astype(BF16)
        s = lax.dot_general(q, k, nt_dims, preferred_element_type=F32)
        m = jnp.max(s, axis=-1, keepdims=True)
        p = jnp.exp(s - m).astype(BF16)
        ov = jnp.dot(p, jnp.concatenate([v, ones], axis=1), preferred_element_type=F32)
        o = ov[:, :HEAD_DIM] / ov[:, HEAD_DIM:]
        y_ref[:, 2 * SEG + h * HEAD_DIM:2 * SEG + (h + 1) * HEAD_DIM] = (gm_ref[:, sl] * o).astype(BF16)

    _shift_rows(shift_in[0], shift_in[1], co0_ref)
    _shift_rows(shift_in[per], shift_in[per + 1], co1_ref)
    o_ref[...] = x_ref[...] + jnp.dot(y_ref[...], wo_ref[...], preferred_element_type=F32)


def _mix(z3, y_dil, mem_kv, pool_w, pool_scale, w_out, x3, layer, shifts):
    bsz, seq, _ = z3.shape
    rows = MIX_ROWS
    n_tiles = seq // rows
    halo_per_tile = rows // POOL_HALO

    def seg_spec(seg):
        return pl.BlockSpec((None, rows, SEG), lambda b, i: (b, i, seg))

    n_main = 12
    shift_specs, shift_args, shift_shapes, shift_out_specs, aliases = [], [], [], [], {}
    first_step = 0
    for n, (cache, buf) in enumerate(shifts):
        s_specs, s_args, s_shape, s_spec, s_alias = _shift_operands(
            cache, buf, layer, lambda b, i: b * n_tiles + i, bsz * n_tiles, first_step)
        win = cache.shape[2] // KV_ROW
        first_step += cache.shape[1] * (win // min(win, SHIFT_ROWS))
        if s_alias is not None:
            aliases[n_main + len(shift_args) + s_alias] = 1 + n
        shift_specs += s_specs
        shift_args += s_args
        shift_shapes.append(s_shape)
        shift_out_specs.append(s_spec)

    return pl.pallas_call(
        functools.partial(_mix_kernel, n_aliased=len(aliases)),
        out_shape=[jax.ShapeDtypeStruct(x3.shape, F32)] + shift_shapes,
        grid=(bsz, n_tiles),
        in_specs=[
            seg_spec(SEG_U),
            pl.BlockSpec((None, POOL_HALO, SEG),
                         lambda b, i: (b, jnp.maximum(i * halo_per_tile - 1, 0), SEG_U)),
            seg_spec(SEG_GATE_POOL),
            seg_spec(SEG_GATE_DIL),
            seg_spec(SEG_QMEM),
            seg_spec(SEG_GATE_MEM),
            pl.BlockSpec((None, rows, SEG), lambda b, i: (b, i, 0)),
            pl.BlockSpec((None, MEM_LEN, 2 * SEG), lambda b, i: (b, 0, 0)),
            pl.BlockSpec((None, len(POOL_WINDOWS), HEAD_DIM, HEAD_DIM), lambda b, i: (layer, 0, 0, 0)),
            pl.BlockSpec((None, 1, SEG), lambda b, i: (layer, 0, 0)),
            pl.BlockSpec((MIX_WIDTH, D_MODEL), lambda b, i: (0, 0), pipeline_mode=pl.Buffered(1)),
            pl.BlockSpec((None, rows, D_MODEL), lambda b, i: (b, i, 0)),
        ] + shift_specs,
        out_specs=[pl.BlockSpec((None, rows, D_MODEL), lambda b, i: (b, i, 0))] + shift_out_specs,
        scratch_shapes=[
            pltpu.VMEM((POOL_PAD + POOL_HALO + rows, SEG), F32),
            pltpu.VMEM((2, POOL_PAD + POOL_HALO + rows, HEAD_DIM), F32),
            pltpu.VMEM((rows, MIX_WIDTH), BF16),
        ],
        input_output_aliases=aliases,
        compiler_params=pltpu.CompilerParams(
            dimension_semantics=("arbitrary", "arbitrary"),
            vmem_limit_bytes=VMEM_LIMIT_BYTES),
        name="mix",
    )(z3, z3, z3, z3, z3, z3, y_dil, mem_kv, pool_w, pool_scale, w_out, x3, *shift_args)


def _sample_attn_kernel(zs_ref, c0_ref, c1_ref, c2_ref, cm_ref, o_ref):
    outs, lses = [], []
    for g, c_ref in enumerate((c0_ref, c1_ref, c2_ref)):
        q = zs_ref[pl.ds(_seg_q(g) * N_HEADS, N_HEADS), :]
        k_new = zs_ref[pl.ds(_seg_k(g) * N_HEADS, N_HEADS), :]
        v_new = zs_ref[pl.ds((_seg_k(g) + 1) * N_HEADS, N_HEADS), :]
        k = c_ref[:, 0]
        v = c_ref[:, 1]
        s = jnp.sum(k * q[None], axis=-1, keepdims=True)
        s_new = jnp.sum(k_new * q, axis=-1, keepdims=True)
        m = jnp.maximum(jnp.max(s, axis=0), s_new)
        p = jnp.exp(s - m[None])
        p_new = jnp.exp(s_new - m)
        l = jnp.sum(p, axis=0) + p_new
        outs.append((jnp.sum(p * v, axis=0) + p_new * v_new) / l)
        lses.append(m + jnp.log(l))
    mx = jnp.maximum(jnp.maximum(lses[0], lses[1]), lses[2])
    es = [jnp.exp(x - mx) for x in lses]
    o_ref[pl.ds(0, N_HEADS), :] = (es[0] * outs[0] + es[1] * outs[1] + es[2] * outs[2]) / (es[0] + es[1] + es[2])

    q = zs_ref[pl.ds(SEG_QMEM * N_HEADS, N_HEADS), :]
    k = cm_ref[:, 0]
    v = cm_ref[:, 1]
    s = jnp.sum(k * q[None], axis=-1, keepdims=True)
    m = jnp.max(s, axis=0)
    p = jnp.exp(s - m[None])
    o_ref[pl.ds(N_HEADS, N_HEADS), :] = jnp.sum(p * v, axis=0) / jnp.sum(p, axis=0)


def _sample_attention(zs3, caches7, cache_mem, layer):
    bsz = cache_mem.shape[1]
    in_specs = [pl.BlockSpec((None, N_SEG * N_HEADS, HEAD_DIM), lambda b: (b, 0, 0))]
    for _ in caches7:
        in_specs.append(pl.BlockSpec((None, None, BAND, None, 2, N_HEADS, HEAD_DIM),
                                     lambda b: (layer, b, 0, 0, 0, 0, 0)))
    in_specs.append(pl.BlockSpec((None, None, MEM_LEN, 2, N_HEADS, HEAD_DIM),
                                 lambda b: (layer, b, 0, 0, 0, 0)))
    return pl.pallas_call(
        _sample_attn_kernel,
        out_shape=jax.ShapeDtypeStruct((bsz, 2 * N_HEADS, HEAD_DIM), F32),
        grid=(bsz,),
        in_specs=in_specs,
        out_specs=pl.BlockSpec((None, 2 * N_HEADS, HEAD_DIM), lambda b: (b, 0, 0)),
        compiler_params=pltpu.CompilerParams(dimension_semantics=("arbitrary",)),
        name="sample_attention",
    )(zs3, *caches7, cache_mem)


def _sample_out_kernel(zs_ref, st_ref, ydm_ref, pw_ref, ps_ref, wo_ref, x_ref, o_ref, ns_ref,
                       wob_ref, y_ref):
    wob_ref[...] = wo_ref[...].astype(BF16)
    u = zs_ref[:, pl.ds(SEG_U * SEG, SEG)]
    for gi, w in enumerate(POOL_WINDOWS):
        sl = slice(gi * HEAD_DIM, (gi + 1) * HEAD_DIM)
        tot = u[:, sl]
        for back in range(1, w):
            tot = tot + st_ref[POOL_BUF - back, :, sl]
        cnt = float(min(w, PAST_LEN + 1))
        dlt = tot / cnt - u[:, sl]
        yp = jnp.dot(dlt.astype(BF16), pw_ref[gi], preferred_element_type=F32) * ps_ref[:, sl]
        y_ref[:, sl] = (zs_ref[:, pl.ds(SEG_GATE_POOL * SEG + gi * HEAD_DIM, HEAD_DIM)] * yp).astype(BF16)
    y_ref[:, SEG:2 * SEG] = (zs_ref[:, pl.ds(SEG_GATE_DIL * SEG, SEG)] * ydm_ref[:, pl.ds(0, SEG)]).astype(BF16)
    y_ref[:, 2 * SEG:] = (zs_ref[:, pl.ds(SEG_GATE_MEM * SEG, SEG)] * ydm_ref[:, pl.ds(SEG, SEG)]).astype(BF16)
    o_ref[...] = x_ref[...] + jnp.dot(y_ref[...], wob_ref[...], preferred_element_type=F32)
    for r in range(POOL_BUF - 1):
        ns_ref[r] = st_ref[r + 1]
    ns_ref[POOL_BUF - 1] = u


def _sample_out(zs, state_t, ydm, pool_w, pool_scale, w_out, xs, layer):
    rows = zs.shape[0]
    full = lambda shape: pl.BlockSpec(shape, lambda i: tuple(0 for _ in shape))
    of_layer = lambda a: pl.BlockSpec((None,) + a.shape[1:], lambda i: (layer,) + (0,) * (a.ndim - 1))
    return pl.pallas_call(
        _sample_out_kernel,
        out_shape=[jax.ShapeDtypeStruct(xs.shape, F32),
                   jax.ShapeDtypeStruct(state_t.shape[1:], F32),
                   jax.ShapeDtypeStruct(w_out.shape[1:], BF16)],
        grid=(1,),
        in_specs=[full(zs.shape), of_layer(state_t), full(ydm.shape), of_layer(pool_w),
                  of_layer(pool_scale), of_layer(w_out), full(xs.shape)],
        out_specs=[full(xs.shape), full(state_t.shape[1:]), full(w_out.shape[1:])],
        scratch_shapes=[pltpu.VMEM((rows, MIX_WIDTH), BF16)],
        compiler_params=pltpu.CompilerParams(
            dimension_semantics=("arbitrary",), vmem_limit_bytes=VMEM_LIMIT_BYTES),
        name="sample_out",
    )(zs, state_t, ydm, pool_w, pool_scale, w_out, xs)


def _set_last_kernel(*refs):
    n = len(refs) // 3
    for new_ref, o_ref in zip(refs[:n], refs[2 * n:]):
        o_ref[...] = new_ref[...]


def _cache_set_last(bufs, new_rows):
    n = len(bufs)
    depth, bsz = bufs[0].shape[:2]

    def last_row_spec(win):
        return pl.BlockSpec((None, None, KV_ROW, HEAD_DIM), lambda l, b: (l, b, win - 1, 0))

    return pl.pallas_call(
        _set_last_kernel,
        out_shape=[jax.ShapeDtypeStruct(buf.shape, buf.dtype) for buf in bufs],
        grid=(depth, bsz),
        in_specs=[pl.BlockSpec((None, None, KV_ROW, HEAD_DIM), lambda l, b: (l, b, 0, 0))] * n
        + [pl.BlockSpec(memory_space=pl.ANY)] * n,
        out_specs=[last_row_spec(buf.shape[2] // KV_ROW) for buf in bufs],
        input_output_aliases={n + g: g for g in range(n)},
        compiler_params=pltpu.CompilerParams(dimension_semantics=("arbitrary", "arbitrary")),
        name="cache_set_last",
    )(*new_rows, *bufs)


def _tile_heads(v):
    return jnp.tile(v, N_HEADS)


def kernel(x_prompt, x_sample, state_pool, cache_dil_w128, cache_dil_w512, cache_dil_w2048,
           cache_mem_kv, mem_prompt, norm_g, w_in, pool_w, pool_scale, dil_q_norm, dil_k_norm,
           mem_norm_g, w_mem_kv, mem_q_norm, mem_k_norm, w_out):
    depth = w_in.shape[0]
    bsz, seq, _ = x_prompt.shape
    dbsz = x_sample.shape[0]
    caches = (cache_dil_w128, cache_dil_w512, cache_dil_w2048)

    z_pairs = tuple(zip(Z_ORDER[0::2], Z_ORDER[1::2]))
    pool_w_b = pool_w.astype(BF16)

    ones = jnp.ones((depth, SEG), F32)
    segs = [ones] * N_SEG
    for g in range(len(DIL_PAIRS)):
        segs[_seg_q(g)] = _tile_heads(dil_q_norm[:, g]) * ATTN_SCALE
        segs[_seg_k(g)] = _tile_heads(dil_k_norm[:, g])
    segs[SEG_QMEM] = _tile_heads(mem_q_norm) * ATTN_SCALE
    ep_gain = jnp.concatenate(segs, axis=1)[:, None, :]
    mem_gain = jnp.concatenate([_tile_heads(mem_k_norm), ones], axis=1)[:, None, :]
    seg_modes = [MODE_GATE if s in GATE_SEGS else MODE_PLAIN for s in range(N_SEG)]
    ep_mode = jnp.repeat(jnp.array(seg_modes, F32), SEG)[None, :]
    mem_mode = jnp.full((1, 2 * SEG), MODE_PLAIN, F32)
    gain = norm_g[:, None, :]
    mem_norm = mem_norm_g[:, None, :]
    pscale = pool_scale[:, None, :]

    caches7 = [c.reshape(depth, dbsz, win // d, d, 2, N_HEADS, HEAD_DIM)
               for c, (win, d) in zip(caches, DIL_PAIRS)]
    caches_2d = [c.reshape(depth, dbsz, win * KV_ROW, HEAD_DIM)
                 for c, (win, _) in zip(caches, DIL_PAIRS)]
    state_t = jnp.pad(jnp.transpose(state_pool, (0, 2, 1, 3)),
                      ((0, 0), (0, 0), (0, SAMPLE_ROWS - dbsz), (0, 0)))

    xp = x_prompt.reshape(bsz * seq, D_MODEL)
    xs = jnp.pad(x_sample.reshape(dbsz, D_MODEL), ((0, SAMPLE_ROWS - dbsz), (0, 0)))
    mem2 = mem_prompt.reshape(bsz * MEM_LEN, D_MODEL)

    pool_p, mem_p, pool_s, zs_rows = [], [], [], []
    prompt_caches = None
    big_cache = None
    small_caches = [None, None]
    for l in range(depth):
        zs, w_first, w_second = _project(xs, gain, (w_in, l, z_pairs), ep_gain, ep_mode,
                                         gain_layer=l, rows=SAMPLE_ROWS, name="proj_sample",
                                         emit_w=True)
        zs3 = zs.reshape(SAMPLE_ROWS, N_SEG * N_HEADS, HEAD_DIM)
        ydm = _sample_attention(zs3, caches7, cache_mem_kv, l)
        ydm = jnp.pad(ydm.reshape(dbsz, 2 * SEG), ((0, SAMPLE_ROWS - dbsz), (0, 0)))
        xs, new_state_t, w_out_b = _sample_out(zs, state_t, ydm, pool_w_b, pscale, w_out, xs, l)
        pool_s.append(new_state_t)
        zs_rows.append(zs)

        z, big_cache = _project(xp, gain, (w_first, w_second), ep_gain, ep_mode, gain_layer=l,
                                rows=PROJ_ROWS, name="proj_prompt",
                                shift=(caches_2d[-1], l, big_cache))
        z3 = z.reshape(bsz, seq, IN_COLS)
        mkv, = _project(mem2, mem_norm, (w_mem_kv, l, ((0, 1),)), mem_gain, mem_mode,
                        gain_layer=l, rows=bsz * MEM_LEN, name="proj_mem")
        mkv3 = mkv.reshape(bsz, MEM_LEN, 2 * SEG)
        y_dil, *prompt_caches = _dilated_attention(z3, l, depth, prompt_caches)
        xp3, small_caches[1], small_caches[0] = _mix(
            z3, y_dil, mkv3, pool_w_b, pscale, w_out_b, xp.reshape(bsz, seq, D_MODEL), l,
            ((caches_2d[1], small_caches[1]), (caches_2d[0], small_caches[0])))
        xp = xp3.reshape(bsz * seq, D_MODEL)
        pool_p.append(z3[:, seq - POOL_BUF:, SEG_U * SEG:(SEG_U + 1) * SEG])
        mem_p.append(mkv3)

    zs_all = jnp.stack(zs_rows)[:, :dbsz].reshape(depth, dbsz, N_SEG, N_HEADS, HEAD_DIM)
    new_rows = [zs_all[:, :, _seg_k(g):_seg_k(g) + 2].reshape(depth, dbsz, KV_ROW, HEAD_DIM)
                for g in range(len(DIL_PAIRS))]
    new_caches = _cache_set_last([small_caches[0], small_caches[1], big_cache], new_rows)
    new_caches = [c.reshape(depth, dbsz, win, 2, N_HEADS, HEAD_DIM)
                  for c, (win, _) in zip(new_caches, DIL_PAIRS)]

    y_prompt = xp.reshape(bsz, seq, D_MODEL)
    y_sample = xs[:dbsz].reshape(dbsz, 1, D_MODEL)
    cache_mem_prompt = jnp.stack(mem_p).reshape(depth, bsz, MEM_LEN, 2, N_HEADS, HEAD_DIM)
    state_pool_sample = jnp.transpose(jnp.stack(pool_s)[:, :, :dbsz], (0, 2, 1, 3))
    prompt_caches = [c.reshape(depth, bsz, win, 2, N_HEADS, HEAD_DIM)
                     for c, (win, _) in zip(prompt_caches, DIL_PAIRS)]
    return (y_prompt, y_sample, jnp.stack(pool_p), prompt_caches[0], prompt_caches[1],
            prompt_caches[2], cache_mem_prompt, state_pool_sample, new_caches[0], new_caches[1],
            new_caches[2])
```

```python
import functools

import jax
import jax.numpy as jnp
from jax import lax
from jax.experimental import pallas as pl
from jax.experimental.pallas import tpu as pltpu

F32 = jnp.float32
BF16 = jnp.bfloat16

D_MODEL = 2048
HEAD_DIM = 128
N_HEADS = 4
SEG = N_HEADS * HEAD_DIM
N_SEG = 14
IN_COLS = N_SEG * SEG
POOL_WINDOWS = (2, 4, 8, 16)
POOL_BUF = 15
POOL_HALO = 16
POOL_PAD = 8
DIL_PAIRS = ((128, 1), (512, 4), (2048, 16))
BAND = 128
MEM_LEN = 256
MIX_WIDTH = 3 * SEG
EPS = 1e-6
ATTN_SCALE = HEAD_DIM ** -0.5
PAST_LEN = 16384

Z_ORDER = (2, 0, 3, 4, 5, 1, 6, 7, 8, 11, 9, 10, 12, 13)
SEG_U, SEG_GATE_POOL, SEG_GATE_DIL, SEG_QMEM, SEG_GATE_MEM = 1, 5, 9, 12, 13
GATE_SEGS = (SEG_GATE_POOL, SEG_GATE_DIL, SEG_GATE_MEM)


def _seg_q(g):
    return 4 * g


def _seg_k(g):
    return 4 * g + 2

VMEM_LIMIT_BYTES = 56 * 1024 * 1024

PROJ_ROWS = 1024
PROJ_ROW_PARTS = 4
PROJ_MIN_PART = 256
ATTN_ROWS = 2048
MIX_ROWS = 512
SAMPLE_ROWS = 16
KV_ROW = 2 * N_HEADS
SHIFT_ROWS = 512
SHIFT_CHUNK = 64


MODE_PLAIN, MODE_GATE = 0.0, 1.0


def _shift_rows(c_ref, nxt_ref, o_ref):
    n = c_ref.shape[0]
    chunk = min(n, SHIFT_CHUNK * KV_ROW)
    for lo in range(0, n - KV_ROW, chunk):
        size = min(chunk, n - KV_ROW - lo)
        o_ref[pl.ds(lo, size), :] = c_ref[pl.ds(lo + KV_ROW, size), :]
    o_ref[pl.ds(n - KV_ROW, KV_ROW), :] = nxt_ref[...]


def _shift_operands(cache, buf, layer, step_of, n_steps, first_step=0):
    bsz = cache.shape[1]
    win = cache.shape[2] // KV_ROW
    rows = min(win, SHIFT_ROWS)
    per_b = win // rows
    n_shift = bsz * per_b
    assert first_step + n_shift <= n_steps

    def block_of(*idx):
        t = jnp.clip(step_of(*idx) - first_step, 0, n_shift - 1)
        return t // per_b, t % per_b

    def main_map(*idx):
        b, r = block_of(*idx)
        return (layer, b, r, 0)

    def next_map(*idx):
        b, r = block_of(*idx)
        return (layer, b, jnp.minimum((r + 1) * rows, win - 1), 0)

    in_specs = [pl.BlockSpec((None, None, rows * KV_ROW, HEAD_DIM), main_map),
                pl.BlockSpec((None, None, KV_ROW, HEAD_DIM), next_map)]
    args = [cache, cache]
    alias = None
    if buf is not None:
        in_specs.append(pl.BlockSpec(memory_space=pl.ANY))
        args.append(buf)
        alias = 2
    out_shape = jax.ShapeDtypeStruct(cache.shape, cache.dtype)
    out_spec = pl.BlockSpec((None, None, rows * KV_ROW, HEAD_DIM), main_map)
    return in_specs, args, out_shape, out_spec, alias


def _lookup(j, table):
    out = table[0]
    for t in range(1, len(table)):
        out = jnp.where(j == t, table[t], out)
    return out


def _proj_kernel(x_ref, g_ref, wa_ref, wb_ref, eg_ref, em_ref, *rest, with_shift, aliased, emit_w):
    rest = list(rest)
    if with_shift:
        c_ref, nxt_ref = rest[:2]
        rest = rest[3:] if aliased else rest[2:]
    o_ref = rest.pop(0)
    if emit_w:
        wa_out_ref, wb_out_ref = rest.pop(0), rest.pop(0)
    if with_shift:
        co_ref = rest.pop(0)
    h_ref, = rest
    j = pl.program_id(1)
    ii = pl.program_id(2)

    @pl.when(j == 0)
    def _():
        x = x_ref[...]
        ms = jnp.mean(x * x, axis=-1, keepdims=True)
        h_ref[ii] = (x * lax.rsqrt(ms + EPS) * g_ref[...]).astype(BF16)

    if with_shift:
        _shift_rows(c_ref, nxt_ref, co_ref)
    h_tile = h_ref[ii]
    wa = wa_ref[...].astype(BF16)
    wb = wb_ref[...].astype(BF16)
    if emit_w:
        wa_out_ref[...] = wa
        wb_out_ref[...] = wb
    rows = h_tile.shape[0]
    part = max(rows // PROJ_ROW_PARTS, min(rows, PROJ_MIN_PART))
    for r0 in range(0, rows, part):
        rsl = pl.ds(r0, part)
        acc = jnp.dot(h_tile[r0:r0 + part], wa, preferred_element_type=F32)
        for h in range(N_HEADS):
            sl = slice(h * HEAD_DIM, (h + 1) * HEAD_DIM)
            a = acc[:, sl]
            o_ref[rsl, sl] = a * (lax.rsqrt(jnp.mean(a * a, axis=-1, keepdims=True) + EPS) * eg_ref[:, sl])
    for r0 in range(0, rows, part):
        rsl = pl.ds(r0, part)
        acc = jnp.dot(h_tile[r0:r0 + part], wb, preferred_element_type=F32)
        for h in range(N_HEADS):
            sl = slice(h * HEAD_DIM, (h + 1) * HEAD_DIM)
            osl = slice(SEG + h * HEAD_DIM, SEG + (h + 1) * HEAD_DIM)
            a = acc[:, sl]
            o_ref[rsl, osl] = a * jnp.where(em_ref[:, osl] == MODE_GATE, jax.nn.sigmoid(a), 1.0)


def _project(x, gain, weights, ep_gain, ep_mode, *, gain_layer, rows, name, emit_w=False,
             shift=None):
    m, k = x.shape
    cols = 2 * SEG
    n_tiles = m // rows
    group = 2 if n_tiles % 2 == 0 else 1
    if len(weights) == 3:
        w, layer, seg_pairs = weights
        n_j = len(seg_pairs)
        firsts = tuple(p[0] for p in seg_pairs)
        seconds = tuple(p[1] for p in seg_pairs)
        w_specs = [pl.BlockSpec((None, k, SEG), lambda g, j, t: (layer, 0, _lookup(j, firsts))),
                   pl.BlockSpec((None, k, SEG), lambda g, j, t: (layer, 0, _lookup(j, seconds)))]
        w_args = [w, w]
    else:
        n_j = weights[0].shape[1] // SEG
        w_specs = [pl.BlockSpec((k, SEG), lambda g, j, t: (0, j))] * 2
        w_args = list(weights)
    n = n_j * cols

    def x_map(g, j, t):
        return (jnp.where(j == 0, g * group + t, g * group + group - 1), 0)

    in_specs = [
        pl.BlockSpec((rows, k), x_map),
        pl.BlockSpec((None, 1, k), lambda g, j, t: (gain_layer, 0, 0)),
        *w_specs,
        pl.BlockSpec((None, 1, cols), lambda g, j, t: (gain_layer, 0, j)),
        pl.BlockSpec((1, cols), lambda g, j, t: (0, j)),
    ]
    args = [x, gain, *w_args, ep_gain, ep_mode]
    out_shape = [jax.ShapeDtypeStruct((m, n), F32)]
    out_specs = [pl.BlockSpec((rows, cols), lambda g, j, t: (g * group + t, j))]
    if emit_w:
        assert n_tiles == 1
        out_shape += [jax.ShapeDtypeStruct((k, n_j * SEG), BF16)] * 2
        out_specs += [pl.BlockSpec((k, SEG), lambda g, j, t: (0, j))] * 2
    aliases = {}
    if shift is not None:
        s_specs, s_args, s_shape, s_spec, s_alias = _shift_operands(
            shift[0], shift[2], shift[1], lambda g, j, t: (g * n_j + j) * group + t, n_tiles * n_j)
        if s_alias is not None:
            aliases = {len(args) + s_alias: len(out_shape)}
        in_specs += s_specs
        args += s_args
        out_shape.append(s_shape)
        out_specs.append(s_spec)
    return pl.pallas_call(
        functools.partial(_proj_kernel, with_shift=shift is not None,
                          aliased=bool(aliases), emit_w=emit_w),
        out_shape=out_shape,
        grid=(n_tiles // group, n_j, group),
        in_specs=in_specs,
        out_specs=out_specs,
        scratch_shapes=[pltpu.VMEM((group, rows, k), BF16)],
        input_output_aliases=aliases,
        compiler_params=pltpu.CompilerParams(
            dimension_semantics=("arbitrary", "arbitrary", "arbitrary"),
            vmem_limit_bytes=VMEM_LIMIT_BYTES),
        name=name,
    )(*args)


def _dilattn_kernel(*refs, n_tiles):
    n_groups = len(DIL_PAIRS)
    ins = refs[:5 * n_groups]
    y_ref, *tails, og_ref, lse_ref = refs[len(refs) - 3 - n_groups:]
    head = pl.program_id(1)
    i = pl.program_id(2)

    row = lax.broadcasted_iota(jnp.int32, (BAND, 2 * BAND), 0)
    col = lax.broadcasted_iota(jnp.int32, (BAND, 2 * BAND), 1)
    not_future = col <= row + BAND
    band = jnp.logical_and(col >= row, not_future)
    first_lo = jnp.maximum(row, (i == 0).astype(jnp.int32) * BAND)
    band_first = jnp.logical_and(col >= first_lo, not_future)
    ones = jnp.ones((2 * BAND, HEAD_DIM), BF16)
    nt_dims = (((1,), (1,)), ((), ()))

    for g in range(n_groups - 1, -1, -1):
        win, d = DIL_PAIRS[g]
        q_ref, kc_ref, vc_ref, kp_ref, vp_ref = ins[5 * g:5 * g + 5]
        span = BAND * d

        def rows_of(start, size, d=d):
            return pl.ds(start, size) if d == 1 else pl.ds(start, size, stride=d)

        for t in range(ATTN_ROWS // BAND):
            u, c = divmod(t, d)
            base = u * span + c
            q = q_ref[rows_of(base, BAND), :].astype(BF16)
            if u == 0:
                k = jnp.concatenate([kp_ref[rows_of(c, BAND), :], kc_ref[rows_of(c, BAND), :]], axis=0)
                v = jnp.concatenate([vp_ref[rows_of(c, BAND), :], vc_ref[rows_of(c, BAND), :]], axis=0)
            else:
                k = kc_ref[rows_of(base - span, 2 * BAND), :]
                v = vc_ref[rows_of(base - span, 2 * BAND), :]
            s = lax.dot_general(q, k.astype(BF16), nt_dims, preferred_element_type=F32)
            s = jnp.where(band_first if u == 0 else band, s, -jnp.inf)
            m = jnp.max(jnp.maximum(s[:, :BAND], s[:, BAND:]), axis=-1, keepdims=True)
            p = jnp.exp(s - m).astype(BF16)
            ov = jnp.dot(p, jnp.concatenate([v.astype(BF16), ones], axis=1), preferred_element_type=F32)
            l = ov[:, HEAD_DIM:]
            o = ov[:, :HEAD_DIM] / l
            lse = m + jnp.log(l)
            if g > 0:
                o_rows = rows_of((g - 1) * ATTN_ROWS + base, BAND)
                og_ref[o_rows, :] = o
                lse_ref[o_rows, :] = lse
            else:
                outs, lses = [o], [lse]
                for other in range(n_groups - 1):
                    sl = pl.ds(other * ATTN_ROWS + base, BAND)
                    outs.append(og_ref[sl, :])
                    lses.append(lse_ref[sl, :])
                mx = functools.reduce(jnp.maximum, lses)
                es = [jnp.exp(x - mx) for x in lses]
                num = sum(e * x for e, x in zip(es, outs))
                y_ref[pl.ds(base, BAND), :] = num / sum(es)

        for hh in range(N_HEADS):
            @pl.when(jnp.logical_and(i == n_tiles - 1, head == hh))
            def _(g=g, win=win, hh=hh, kc_ref=kc_ref, vc_ref=vc_ref):
                tails[g][pl.ds(hh, win, stride=KV_ROW), :] = kc_ref[pl.ds(ATTN_ROWS - win, win), :]
                tails[g][pl.ds(N_HEADS + hh, win, stride=KV_ROW), :] = (
                    vc_ref[pl.ds(ATTN_ROWS - win, win), :])


def _dilated_attention(z3, layer, depth, cache_bufs):
    bsz, seq, _ = z3.shape
    n_tiles = seq // ATTN_ROWS
    in_specs = []
    for g, (win, d) in enumerate(DIL_PAIRS):
        span = BAND * d
        qb = _seg_q(g) * N_HEADS
        kb = _seg_k(g) * N_HEADS
        vb = kb + N_HEADS
        per_tile = ATTN_ROWS // span

        def prev_map(col0, per_tile=per_tile):
            return lambda b, h, i: (b, jnp.maximum(i * per_tile - 1, 0), col0 + h)

        def cur_map(col0):
            return lambda b, h, i: (b, i, col0 + h)

        in_specs += [
            pl.BlockSpec((None, ATTN_ROWS, HEAD_DIM), cur_map(qb)),
            pl.BlockSpec((None, ATTN_ROWS, HEAD_DIM), cur_map(kb)),
            pl.BlockSpec((None, ATTN_ROWS, HEAD_DIM), cur_map(vb)),
            pl.BlockSpec((None, span, HEAD_DIM), prev_map(kb)),
            pl.BlockSpec((None, span, HEAD_DIM), prev_map(vb)),
        ]
    args = [z3] * len(in_specs)
    out_shape = [jax.ShapeDtypeStruct((bsz, seq, SEG), F32)]
    out_specs = [pl.BlockSpec((None, ATTN_ROWS, HEAD_DIM), lambda b, h, i: (b, i, h))]
    aliases = {}
    for g, (win, _) in enumerate(DIL_PAIRS):
        out_shape.append(jax.ShapeDtypeStruct((depth, bsz, win * KV_ROW, HEAD_DIM), F32))
        out_specs.append(pl.BlockSpec((None, None, win * KV_ROW, HEAD_DIM),
                                      lambda b, h, i: (layer, b, 0, 0)))
        if cache_bufs is not None:
            in_specs.append(pl.BlockSpec(memory_space=pl.ANY))
            args.append(cache_bufs[g])
            aliases[len(args) - 1] = 1 + g
    return pl.pallas_call(
        functools.partial(_dilattn_kernel, n_tiles=n_tiles),
        out_shape=out_shape,
        grid=(bsz, N_HEADS, n_tiles),
        in_specs=in_specs,
        out_specs=out_specs,
        scratch_shapes=[
            pltpu.VMEM(((len(DIL_PAIRS) - 1) * ATTN_ROWS, HEAD_DIM), F32),
            pltpu.VMEM(((len(DIL_PAIRS) - 1) * ATTN_ROWS, HEAD_DIM), F32),
        ],
        input_output_aliases=aliases,
        compiler_params=pltpu.CompilerParams(
            dimension_semantics=("arbitrary", "arbitrary", "arbitrary"),
            vmem_limit_bytes=VMEM_LIMIT_BYTES),
        name="dilated_attention",
    )(*args)


def _mix_kernel(u_ref, up_ref, gp_ref, gd_ref, qm_ref, gm_ref, yd_ref, mkv_ref, pw_ref, ps_ref,
                wo_ref, x_ref, *rest, n_aliased):
    shift_in, rest = rest[:4 + n_aliased], rest[4 + n_aliased:]
    o_ref, co0_ref, co1_ref, ue_ref, t_ref, y_ref = rest
    per = 2 + n_aliased // 2
    i = pl.program_id(1)
    rows = u_ref.shape[0]

    pad, halo = POOL_PAD, POOL_HALO
    n = halo + rows

    @pl.when(i > 0)
    def _():
        ue_ref[pl.ds(pad, halo), :] = up_ref[...]

    @pl.when(i == 0)
    def _():
        ue_ref[pl.ds(pad, halo), :] = jnp.zeros((halo, SEG), F32)

    ue_ref[pl.ds(0, pad), :] = jnp.zeros((pad, SEG), F32)
    t_ref[0, pl.ds(0, pad), :] = jnp.zeros((pad, HEAD_DIM), F32)
    t_ref[1, pl.ds(0, pad), :] = jnp.zeros((pad, HEAD_DIM), F32)
    ue_ref[pl.ds(pad + halo, rows), :] = u_ref[...]
    pos = i * rows + lax.broadcasted_iota(jnp.int32, (rows, 1), 0)
    for gi, w in enumerate(POOL_WINDOWS):
        sl = slice(gi * HEAD_DIM, (gi + 1) * HEAD_DIM)
        cur = ue_ref[pl.ds(pad, n), sl] + ue_ref[pl.ds(pad - 1, n), sl]
        shift, buf = 2, 0
        while shift < w:
            t_ref[buf, pl.ds(pad, n), :] = cur
            cur = cur + t_ref[buf, pl.ds(pad - shift, n), :]
            shift, buf = 2 * shift, 1 - buf
        cnt = jnp.minimum(w, pos + 1).astype(F32)
        dlt = cur[halo:] / cnt - u_ref[:, sl]
        yp = jnp.dot(dlt.astype(BF16), pw_ref[gi], preferred_element_type=F32) * ps_ref[:, sl]
        y_ref[:, sl] = (gp_ref[:, sl] * yp).astype(BF16)

    y_ref[:, SEG:2 * SEG] = (gd_ref[...] * yd_ref[...]).astype(BF16)

    nt_dims = (((1,), (1,)), ((), ()))
    ones = jnp.ones((MEM_LEN, HEAD_DIM), BF16)
    for h in range(N_HEADS):
        sl = slice(h * HEAD_DIM, (h + 1) * HEAD_DIM)
        q = qm_ref[:, sl].astype(BF16)
        k = mkv_ref[:, sl].astype(BF16)
        v = mkv_ref[:, SEG + h * HEAD_DIM:SEG + (h + 1) * HEAD_DIM].astype(BF16)
        s = lax.dot_general(q, k, nt_dims, preferred_element_type=F32)
        m = jnp.max(s, axis=-1, keepdims=True)
        p = jnp.exp(s - m).astype(BF16)
        ov = jnp.dot(p, jnp.concatenate([v, ones], axis=1), preferred_element_type=F32)
        o = ov[:, :HEAD_DIM] / ov[:, HEAD_DIM:]
        y_ref[:, 2 * SEG + h * HEAD_DIM:2 * SEG + (h + 1) * HEAD_DIM] = (gm_ref[:, sl] * o).astype(BF16)

    _shift_rows(shift_in[0], shift_in[1], co0_ref)
    _shift_rows(shift_in[per], shift_in[per + 1], co1_ref)
    o_ref[...] = x_ref[...] + jnp.dot(y_ref[...], wo_ref[...], preferred_element_type=F32)


def _mix(z3, y_dil, mem_kv, pool_w, pool_scale, w_out, x3, layer, shifts):
    bsz, seq, _ = z3.shape
    rows = MIX_ROWS
    n_tiles = seq // rows
    halo_per_tile = rows // POOL_HALO

    def seg_spec(seg):
        return pl.BlockSpec((None, rows, SEG), lambda b, i: (b, i, seg))

    n_main = 12
    shift_specs, shift_args, shift_shapes, shift_out_specs, aliases = [], [], [], [], {}
    first_step = 0
    for n, (cache, buf) in enumerate(shifts):
        s_specs, s_args, s_shape, s_spec, s_alias = _shift_operands(
            cache, buf, layer, lambda b, i: b * n_tiles + i, bsz * n_tiles, first_step)
        win = cache.shape[2] // KV_ROW
        first_step += cache.shape[1] * (win // min(win, SHIFT_ROWS))
        if s_alias is not None:
            aliases[n_main + len(shift_args) + s_alias] = 1 + n
        shift_specs += s_specs
        shift_args += s_args
        shift_shapes.append(s_shape)
        shift_out_specs.append(s_spec)

    return pl.pallas_call(
        functools.partial(_mix_kernel, n_aliased=len(aliases)),
        out_shape=[jax.ShapeDtypeStruct(x3.shape, F32)] + shift_shapes,
        grid=(bsz, n_tiles),
        in_specs=[
            seg_spec(SEG_U),
            pl.BlockSpec((None, POOL_HALO, SEG),
                         lambda b, i: (b, jnp.maximum(i * halo_per_tile - 1, 0), SEG_U)),
            seg_spec(SEG_GATE_POOL),
            seg_spec(SEG_GATE_DIL),
            seg_spec(SEG_QMEM),
            seg_spec(SEG_GATE_MEM),
            pl.BlockSpec((None, rows, SEG), lambda b, i: (b, i, 0)),
            pl.BlockSpec((None, MEM_LEN, 2 * SEG), lambda b, i: (b, 0, 0)),
            pl.BlockSpec((None, len(POOL_WINDOWS), HEAD_DIM, HEAD_DIM), lambda b, i: (layer, 0, 0, 0)),
            pl.BlockSpec((None, 1, SEG), lambda b, i: (layer, 0, 0)),
            pl.BlockSpec((MIX_WIDTH, D_MODEL), lambda b, i: (0, 0), pipeline_mode=pl.Buffered(1)),
            pl.BlockSpec((None, rows, D_MODEL), lambda b, i: (b, i, 0)),
        ] + shift_specs,
        out_specs=[pl.BlockSpec((None, rows, D_MODEL), lambda b, i: (b, i, 0))] + shift_out_specs,
        scratch_shapes=[
            pltpu.VMEM((POOL_PAD + POOL_HALO + rows, SEG), F32),
            pltpu.VMEM((2, POOL_PAD + POOL_HALO + rows, HEAD_DIM), F32),
            pltpu.VMEM((rows, MIX_WIDTH), BF16),
        ],
        input_output_aliases=aliases,
        compiler_params=pltpu.CompilerParams(
            dimension_semantics=("arbitrary", "arbitrary"),
            vmem_limit_bytes=VMEM_LIMIT_BYTES),
        name="mix",
    )(z3, z3, z3, z3, z3, z3, y_dil, mem_kv, pool_w, pool_scale, w_out, x3, *shift_args)


def _sample_attn_kernel(zs_ref, c0_ref, c1_ref, c2_ref, cm_ref, o_ref):
    outs, lses = [], []
    for g, c_ref in enumerate((c0_ref, c1_ref, c2_ref)):
        q = zs_ref[pl.ds(_seg_q(g) * N_HEADS, N_HEADS), :]
        k_new = zs_ref[pl.ds(_seg_k(g) * N_HEADS, N_HEADS), :]
        v_new = zs_ref[pl.ds((_seg_k(g) + 1) * N_HEADS, N_HEADS), :]
        k = c_ref[:, 0]
        v = c_ref[:, 1]
        s = jnp.sum(k * q[None], axis=-1, keepdims=True)
        s_new = jnp.sum(k_new * q, axis=-1, keepdims=True)
        m = jnp.maximum(jnp.max(s, axis=0), s_new)
        p = jnp.exp(s - m[None])
        p_new = jnp.exp(s_new - m)
        l = jnp.sum(p, axis=0) + p_new
        outs.append((jnp.sum(p * v, axis=0) + p_new * v_new) / l)
        lses.append(m + jnp.log(l))
    mx = jnp.maximum(jnp.maximum(lses[0], lses[1]), lses[2])
    es = [jnp.exp(x - mx) for x in lses]
    o_ref[pl.ds(0, N_HEADS), :] = (es[0] * outs[0] + es[1] * outs[1] + es[2] * outs[2]) / (es[0] + es[1] + es[2])

    q = zs_ref[pl.ds(SEG_QMEM * N_HEADS, N_HEADS), :]
    k = cm_ref[:, 0]
    v = cm_ref[:, 1]
    s = jnp.sum(k * q[None], axis=-1, keepdims=True)
    m = jnp.max(s, axis=0)
    p = jnp.exp(s - m[None])
    o_ref[pl.ds(N_HEADS, N_HEADS), :] = jnp.sum(p * v, axis=0) / jnp.sum(p, axis=0)


def _sample_attention(zs3, caches7, cache_mem, layer):
    bsz = cache_mem.shape[1]
    in_specs = [pl.BlockSpec((None, N_SEG * N_HEADS, HEAD_DIM), lambda b: (b, 0, 0))]
    for _ in caches7:
        in_specs.append(pl.BlockSpec((None, None, BAND, None, 2, N_HEADS, HEAD_DIM),
                                     lambda b: (layer, b, 0, 0, 0, 0, 0)))
    in_specs.append(pl.BlockSpec((None, None, MEM_LEN, 2, N_HEADS, HEAD_DIM),
                                 lambda b: (layer, b, 0, 0, 0, 0)))
    return pl.pallas_call(
        _sample_attn_kernel,
        out_shape=jax.ShapeDtypeStruct((bsz, 2 * N_HEADS, HEAD_DIM), F32),
        grid=(bsz,),
        in_specs=in_specs,
        out_specs=pl.BlockSpec((None, 2 * N_HEADS, HEAD_DIM), lambda b: (b, 0, 0)),
        compiler_params=pltpu.CompilerParams(dimension_semantics=("arbitrary",)),
        name="sample_attention",
    )(zs3, *caches7, cache_mem)


def _sample_out_kernel(zs_ref, st_ref, ydm_ref, pw_ref, ps_ref, wo_ref, x_ref, o_ref, ns_ref,
                       wob_ref, y_ref):
    wob_ref[...] = wo_ref[...].astype(BF16)
    u = zs_ref[:, pl.ds(SEG_U * SEG, SEG)]
    for gi, w in enumerate(POOL_WINDOWS):
        sl = slice(gi * HEAD_DIM, (gi + 1) * HEAD_DIM)
        tot = u[:, sl]
        for back in range(1, w):
            tot = tot + st_ref[POOL_BUF - back, :, sl]
        cnt = float(min(w, PAST_LEN + 1))
        dlt = tot / cnt - u[:, sl]
        yp = jnp.dot(dlt.astype(BF16), pw_ref[gi], preferred_element_type=F32) * ps_ref[:, sl]
        y_ref[:, sl] = (zs_ref[:, pl.ds(SEG_GATE_POOL * SEG + gi * HEAD_DIM, HEAD_DIM)] * yp).astype(BF16)
    y_ref[:, SEG:2 * SEG] = (zs_ref[:, pl.ds(SEG_GATE_DIL * SEG, SEG)] * ydm_ref[:, pl.ds(0, SEG)]).astype(BF16)
    y_ref[:, 2 * SEG:] = (zs_ref[:, pl.ds(SEG_GATE_MEM * SEG, SEG)] * ydm_ref[:, pl.ds(SEG, SEG)]).astype(BF16)
    o_ref[...] = x_ref[...] + jnp.dot(y_ref[...], wob_ref[...], preferred_element_type=F32)
    for r in range(POOL_BUF - 1):
        ns_ref[r] = st_ref[r + 1]
    ns_ref[POOL_BUF - 1] = u


def _sample_out(zs, state_t, ydm, pool_w, pool_scale, w_out, xs, layer):
    rows = zs.shape[0]
    full = lambda shape: pl.BlockSpec(shape, lambda i: tuple(0 for _ in shape))
    of_layer = lambda a: pl.BlockSpec((None,) + a.shape[1:], lambda i: (layer,) + (0,) * (a.ndim - 1))
    return pl.pallas_call(
        _sample_out_kernel,
        out_shape=[jax.ShapeDtypeStruct(xs.shape, F32),
                   jax.ShapeDtypeStruct(state_t.shape[1:], F32),
                   jax.ShapeDtypeStruct(w_out.shape[1:], BF16)],
        grid=(1,),
        in_specs=[full(zs.shape), of_layer(state_t), full(ydm.shape), of_layer(pool_w),
                  of_layer(pool_scale), of_layer(w_out), full(xs.shape)],
        out_specs=[full(xs.shape), full(state_t.shape[1:]), full(w_out.shape[1:])],
        scratch_shapes=[pltpu.VMEM((rows, MIX_WIDTH), BF16)],
        compiler_params=pltpu.CompilerParams(
            dimension_semantics=("arbitrary",), vmem_limit_bytes=VMEM_LIMIT_BYTES),
        name="sample_out",
    )(zs, state_t, ydm, pool_w, pool_scale, w_out, xs)


def _set_last_kernel(*refs):
    n = len(refs) // 3
    for new_ref, o_ref in zip(refs[:n], refs[2 * n:]):
        o_ref[...] = new_ref[...]


def _cache_set_last(bufs, new_rows):
    n = len(bufs)
    depth, bsz = bufs[0].shape[:2]

    def last_row_spec(win):
        return pl.BlockSpec((None, bsz, KV_ROW, HEAD_DIM), lambda l: (l, 0, win - 1, 0))

    return pl.pallas_call(
        _set_last_kernel,
        out_shape=[jax.ShapeDtypeStruct(buf.shape, buf.dtype) for buf in bufs],
        grid=(depth,),
        in_specs=[pl.BlockSpec((None, bsz, KV_ROW, HEAD_DIM), lambda l: (l, 0, 0, 0))] * n
        + [pl.BlockSpec(memory_space=pl.ANY)] * n,
        out_specs=[last_row_spec(buf.shape[2] // KV_ROW) for buf in bufs],
        input_output_aliases={n + g: g for g in range(n)},
        compiler_params=pltpu.CompilerParams(dimension_semantics=("arbitrary",)),
        name="cache_set_last",
    )(*new_rows, *bufs)


def _tile_heads(v):
    return jnp.tile(v, N_HEADS)


def kernel(x_prompt, x_sample, state_pool, cache_dil_w128, cache_dil_w512, cache_dil_w2048,
           cache_mem_kv, mem_prompt, norm_g, w_in, pool_w, pool_scale, dil_q_norm, dil_k_norm,
           mem_norm_g, w_mem_kv, mem_q_norm, mem_k_norm, w_out):
    depth = w_in.shape[0]
    bsz, seq, _ = x_prompt.shape
    dbsz = x_sample.shape[0]
    caches = (cache_dil_w128, cache_dil_w512, cache_dil_w2048)

    z_pairs = tuple(zip(Z_ORDER[0::2], Z_ORDER[1::2]))
    pool_w_b = pool_w.astype(BF16)

    ones = jnp.ones((depth, SEG), F32)
    segs = [ones] * N_SEG
    for g in range(len(DIL_PAIRS)):
        segs[_seg_q(g)] = _tile_heads(dil_q_norm[:, g]) * ATTN_SCALE
        segs[_seg_k(g)] = _tile_heads(dil_k_norm[:, g])
    segs[SEG_QMEM] = _tile_heads(mem_q_norm) * ATTN_SCALE
    ep_gain = jnp.concatenate(segs, axis=1)[:, None, :]
    mem_gain = jnp.concatenate([_tile_heads(mem_k_norm), ones], axis=1)[:, None, :]
    seg_modes = [MODE_GATE if s in GATE_SEGS else MODE_PLAIN for s in range(N_SEG)]
    ep_mode = jnp.repeat(jnp.array(seg_modes, F32), SEG)[None, :]
    mem_mode = jnp.full((1, 2 * SEG), MODE_PLAIN, F32)
    gain = norm_g[:, None, :]
    mem_norm = mem_norm_g[:, None, :]
    pscale = pool_scale[:, None, :]

    caches7 = [c.reshape(depth, dbsz, win // d, d, 2, N_HEADS, HEAD_DIM)
               for c, (win, d) in zip(caches, DIL_PAIRS)]
    caches_2d = [c.reshape(depth, dbsz, win * KV_ROW, HEAD_DIM)
                 for c, (win, _) in zip(caches, DIL_PAIRS)]
    state_t = jnp.pad(jnp.transpose(state_pool, (0, 2, 1, 3)),
                      ((0, 0), (0, 0), (0, SAMPLE_ROWS - dbsz), (0, 0)))

    xp = x_prompt.reshape(bsz * seq, D_MODEL)
    xs = jnp.pad(x_sample.reshape(dbsz, D_MODEL), ((0, SAMPLE_ROWS - dbsz), (0, 0)))
    mem2 = mem_prompt.reshape(bsz * MEM_LEN, D_MODEL)

    pool_p, mem_p, pool_s, zs_rows = [], [], [], []
    prompt_caches = None
    big_cache = None
    small_caches = [None, None]
    for l in range(depth):
        zs, w_first, w_second = _project(xs, gain, (w_in, l, z_pairs), ep_gain, ep_mode,
                                         gain_layer=l, rows=SAMPLE_ROWS, name="proj_sample",
                                         emit_w=True)
        zs3 = zs.reshape(SAMPLE_ROWS, N_SEG * N_HEADS, HEAD_DIM)
        ydm = _sample_attention(zs3, caches7, cache_mem_kv, l)
        ydm = jnp.pad(ydm.reshape(dbsz, 2 * SEG), ((0, SAMPLE_ROWS - dbsz), (0, 0)))
        xs, new_state_t, w_out_b = _sample_out(zs, state_t, ydm, pool_w_b, pscale, w_out, xs, l)
        pool_s.append(new_state_t)
        zs_rows.append(zs)

        z, big_cache = _project(xp, gain, (w_first, w_second), ep_gain, ep_mode, gain_layer=l,
                                rows=PROJ_ROWS, name="proj_prompt",
                                shift=(caches_2d[-1], l, big_cache))
        z3 = z.reshape(bsz, seq, IN_COLS)
        mkv, = _project(mem2, mem_norm, (w_mem_kv, l, ((0, 1),)), mem_gain, mem_mode,
                        gain_layer=l, rows=bsz * MEM_LEN, name="proj_mem")
        mkv3 = mkv.reshape(bsz, MEM_LEN, 2 * SEG)
        y_dil, *prompt_caches = _dilated_attention(z3, l, depth, prompt_caches)
        xp3, small_caches[1], small_caches[0] = _mix(
            z3, y_dil, mkv3, pool_w_b, pscale, w_out_b, xp.reshape(bsz, seq, D_MODEL), l,
            ((caches_2d[1], small_caches[1]), (caches_2d[0], small_caches[0])))
        xp = xp3.reshape(bsz * seq, D_MODEL)
        pool_p.append(z3[:, seq - POOL_BUF:, SEG_U * SEG:(SEG_U + 1) * SEG])
        mem_p.append(mkv3)

    zs_all = jnp.stack(zs_rows)[:, :dbsz].reshape(depth, dbsz, N_SEG, N_HEADS, HEAD_DIM)
    new_rows = [zs_all[:, :, _seg_k(g):_seg_k(g) + 2].reshape(depth, dbsz, KV_ROW, HEAD_DIM)
                for g in range(len(DIL_PAIRS))]
    new_caches = _cache_set_last([small_caches[0], small_caches[1], big_cache], new_rows)
    new_caches = [c.reshape(depth, dbsz, win, 2, N_HEADS, HEAD_DIM)
                  for c, (win, _) in zip(new_caches, DIL_PAIRS)]

    y_prompt = xp.reshape(bsz, seq, D_MODEL)
    y_sample = xs[:dbsz].reshape(dbsz, 1, D_MODEL)
    cache_mem_prompt = jnp.stack(mem_p).reshape(depth, bsz, MEM_LEN, 2, N_HEADS, HEAD_DIM)
    state_pool_sample = jnp.transpose(jnp.stack(pool_s)[:, :, :dbsz], (0, 2, 1, 3))
    prompt_caches = [c.reshape(depth, bsz, win, 2, N_HEADS, HEAD_DIM)
                     for c, (win, _) in zip(prompt_caches, DIL_PAIRS)]
    return (y_prompt, y_sample, jnp.stack(pool_p), prompt_caches[0], prompt_caches[1],
            prompt_caches[2], cache_mem_prompt, state_pool_sample, new_caches[0], new_caches[1],
            new_caches[2])
```

```python
import functools

import jax
import jax.numpy as jnp
from jax import lax
from jax.experimental import pallas as pl
from jax.experimental.pallas import tpu as pltpu

F32 = jnp.float32
BF16 = jnp.bfloat16

D_MODEL = 2048
HEAD_DIM = 128
N_HEADS = 4
SEG = N_HEADS * HEAD_DIM
N_SEG = 14
IN_COLS = N_SEG * SEG
POOL_WINDOWS = (2, 4, 8, 16)
POOL_BUF = 15
POOL_HALO = 16
POOL_PAD = 8
DIL_PAIRS = ((128, 1), (512, 4), (2048, 16))
BAND = 128
MEM_LEN = 256
MIX_WIDTH = 3 * SEG
EPS = 1e-6
ATTN_SCALE = HEAD_DIM ** -0.5
PAST_LEN = 16384

Z_ORDER = (2, 0, 3, 4, 5, 1, 6, 7, 8, 11, 9, 10, 12, 13)
SEG_U, SEG_GATE_POOL, SEG_GATE_DIL, SEG_QMEM, SEG_GATE_MEM = 1, 5, 9, 12, 13
GATE_SEGS = (SEG_GATE_POOL, SEG_GATE_DIL, SEG_GATE_MEM)


def _seg_q(g):
    return 4 * g


def _seg_k(g):
    return 4 * g + 2

VMEM_LIMIT_BYTES = 56 * 1024 * 1024

PROJ_ROWS = 1024
PROJ_ROW_PARTS = 4
PROJ_MIN_PART = 256
ATTN_ROWS = 2048
MIX_ROWS = 512
SAMPLE_ROWS = 16
KV_ROW = 2 * N_HEADS
SHIFT_ROWS = 512
SHIFT_CHUNK = 64


MODE_PLAIN, MODE_GATE = 0.0, 1.0


def _shift_rows(c_ref, nxt_ref, o_ref):
    n = c_ref.shape[0]
    chunk = min(n, SHIFT_CHUNK * KV_ROW)
    for lo in range(0, n - KV_ROW, chunk):
        size = min(chunk, n - KV_ROW - lo)
        o_ref[pl.ds(lo, size), :] = c_ref[pl.ds(lo + KV_ROW, size), :]
    o_ref[pl.ds(n - KV_ROW, KV_ROW), :] = nxt_ref[...]


def _shift_jobs(jobs, layer, step_of, n_steps, n_in, n_out):
    in_specs, args, out_shapes, out_specs, aliases = [], [], [], [], {}
    first_step = 0
    for cache, buf, rows in jobs:
        specs, a, shape, spec, alias, n_shift = _shift_operands(
            cache, buf, layer, step_of, n_steps, first_step, rows)
        if alias is not None:
            aliases[n_in + len(args) + alias] = n_out + len(out_shapes)
        in_specs += specs
        args += a
        out_shapes.append(shape)
        out_specs.append(spec)
        first_step += n_shift
    return in_specs, args, out_shapes, out_specs, aliases


def _run_shifts(in_refs, out_refs):
    per = len(in_refs) // len(out_refs)
    for n, o_ref in enumerate(out_refs):
        _shift_rows(in_refs[n * per], in_refs[n * per + 1], o_ref)


def _shift_operands(cache, buf, layer, step_of, n_steps, first_step, rows):
    bsz = cache.shape[1]
    win = cache.shape[2] // KV_ROW
    rows = min(win, rows)
    per_b = win // rows
    n_shift = bsz * per_b
    assert first_step + n_shift <= n_steps

    def block_of(*idx):
        t = jnp.clip(step_of(*idx) - first_step, 0, n_shift - 1)
        return t // per_b, t % per_b

    def main_map(*idx):
        b, r = block_of(*idx)
        return (layer, b, r, 0)

    def next_map(*idx):
        b, r = block_of(*idx)
        return (layer, b, jnp.minimum((r + 1) * rows, win - 1), 0)

    in_specs = [pl.BlockSpec((None, None, rows * KV_ROW, HEAD_DIM), main_map),
                pl.BlockSpec((None, None, KV_ROW, HEAD_DIM), next_map)]
    args = [cache, cache]
    alias = None
    if buf is not None:
        in_specs.append(pl.BlockSpec(memory_space=pl.ANY))
        args.append(buf)
        alias = 2
    out_shape = jax.ShapeDtypeStruct(cache.shape, cache.dtype)
    out_spec = pl.BlockSpec((None, None, rows * KV_ROW, HEAD_DIM), main_map)
    return in_specs, args, out_shape, out_spec, alias, n_shift


def _lookup(j, table):
    out = table[0]
    for t in range(1, len(table)):
        out = jnp.where(j == t, table[t], out)
    return out


def _proj_kernel(x_ref, g_ref, wa_ref, wb_ref, eg_ref, em_ref, *rest, n_shift_in, n_shifts,
                 emit_w):
    rest = list(rest)
    shift_in, rest = rest[:n_shift_in], rest[n_shift_in:]
    o_ref = rest.pop(0)
    if emit_w:
        wa_out_ref, wb_out_ref = rest.pop(0), rest.pop(0)
    shift_out, rest = rest[:n_shifts], rest[n_shifts:]
    h_ref, = rest
    j = pl.program_id(1)
    ii = pl.program_id(2)

    @pl.when(j == 0)
    def _():
        x = x_ref[...]
        ms = jnp.mean(x * x, axis=-1, keepdims=True)
        h_ref[ii] = (x * lax.rsqrt(ms + EPS) * g_ref[...]).astype(BF16)

    if n_shifts:
        _run_shifts(shift_in, shift_out)
    h_tile = h_ref[ii]
    wa = wa_ref[...].astype(BF16)
    wb = wb_ref[...].astype(BF16)
    if emit_w:
        wa_out_ref[...] = wa
        wb_out_ref[...] = wb
    rows = h_tile.shape[0]
    part = max(rows // PROJ_ROW_PARTS, min(rows, PROJ_MIN_PART))
    for r0 in range(0, rows, part):
        rsl = pl.ds(r0, part)
        acc = jnp.dot(h_tile[r0:r0 + part], wa, preferred_element_type=F32)
        for h in range(N_HEADS):
            sl = slice(h * HEAD_DIM, (h + 1) * HEAD_DIM)
            a = acc[:, sl]
            o_ref[rsl, sl] = a * (lax.rsqrt(jnp.mean(a * a, axis=-1, keepdims=True) + EPS) * eg_ref[:, sl])
    for r0 in range(0, rows, part):
        rsl = pl.ds(r0, part)
        acc = jnp.dot(h_tile[r0:r0 + part], wb, preferred_element_type=F32)
        for h in range(N_HEADS):
            sl = slice(h * HEAD_DIM, (h + 1) * HEAD_DIM)
            osl = slice(SEG + h * HEAD_DIM, SEG + (h + 1) * HEAD_DIM)
            a = acc[:, sl]
            o_ref[rsl, osl] = a * jnp.where(em_ref[:, osl] == MODE_GATE, jax.nn.sigmoid(a), 1.0)


def _project(x, gain, weights, ep_gain, ep_mode, *, gain_layer, rows, name, emit_w=False,
             shift=None):
    m, k = x.shape
    cols = 2 * SEG
    n_tiles = m // rows
    group = 2 if n_tiles % 2 == 0 else 1
    if len(weights) == 3:
        w, layer, seg_pairs = weights
        n_j = len(seg_pairs)
        firsts = tuple(p[0] for p in seg_pairs)
        seconds = tuple(p[1] for p in seg_pairs)
        w_specs = [pl.BlockSpec((None, k, SEG), lambda g, j, t: (layer, 0, _lookup(j, firsts))),
                   pl.BlockSpec((None, k, SEG), lambda g, j, t: (layer, 0, _lookup(j, seconds)))]
        w_args = [w, w]
    else:
        n_j = weights[0].shape[1] // SEG
        w_specs = [pl.BlockSpec((k, SEG), lambda g, j, t: (0, j))] * 2
        w_args = list(weights)
    n = n_j * cols

    def x_map(g, j, t):
        return (jnp.where(j == 0, g * group + t, g * group + group - 1), 0)

    in_specs = [
        pl.BlockSpec((rows, k), x_map),
        pl.BlockSpec((None, 1, k), lambda g, j, t: (gain_layer, 0, 0)),
        *w_specs,
        pl.BlockSpec((None, 1, cols), lambda g, j, t: (gain_layer, 0, j)),
        pl.BlockSpec((1, cols), lambda g, j, t: (0, j)),
    ]
    args = [x, gain, *w_args, ep_gain, ep_mode]
    out_shape = [jax.ShapeDtypeStruct((m, n), F32)]
    out_specs = [pl.BlockSpec((rows, cols), lambda g, j, t: (g * group + t, j))]
    if emit_w:
        assert n_tiles == 1
        out_shape += [jax.ShapeDtypeStruct((k, n_j * SEG), BF16)] * 2
        out_specs += [pl.BlockSpec((k, SEG), lambda g, j, t: (0, j))] * 2
    aliases, s_args, s_shapes = {}, [], []
    if shift is not None:
        s_specs, s_args, s_shapes, s_out_specs, aliases = _shift_jobs(
            shift[1], shift[0], lambda g, j, t: (g * n_j + j) * group + t, n_tiles * n_j,
            len(args), len(out_shape))
        in_specs += s_specs
        args += s_args
        out_shape += s_shapes
        out_specs += s_out_specs
    return pl.pallas_call(
        functools.partial(_proj_kernel, n_shift_in=len(s_args), n_shifts=len(s_shapes),
                          emit_w=emit_w),
        out_shape=out_shape,
        grid=(n_tiles // group, n_j, group),
        in_specs=in_specs,
        out_specs=out_specs,
        scratch_shapes=[pltpu.VMEM((group, rows, k), BF16)],
        input_output_aliases=aliases,
        compiler_params=pltpu.CompilerParams(
            dimension_semantics=("arbitrary", "arbitrary", "arbitrary"),
            vmem_limit_bytes=VMEM_LIMIT_BYTES),
        name=name,
    )(*args)


def _dilattn_kernel(*refs, n_tiles):
    n_groups = len(DIL_PAIRS)
    ins = refs[:5 * n_groups]
    gate_ref = refs[5 * n_groups]
    y_ref, *tails, og_ref, lse_ref = refs[len(refs) - 3 - n_groups:]
    head = pl.program_id(1)
    i = pl.program_id(2)

    row = lax.broadcasted_iota(jnp.int32, (BAND, 2 * BAND), 0)
    col = lax.broadcasted_iota(jnp.int32, (BAND, 2 * BAND), 1)
    not_future = col <= row + BAND
    band = jnp.logical_and(col >= row, not_future)
    first_lo = jnp.maximum(row, (i == 0).astype(jnp.int32) * BAND)
    band_first = jnp.logical_and(col >= first_lo, not_future)
    ones = jnp.ones((2 * BAND, HEAD_DIM), BF16)
    nt_dims = (((1,), (1,)), ((), ()))

    for g in range(n_groups - 1, -1, -1):
        win, d = DIL_PAIRS[g]
        q_ref, kc_ref, vc_ref, kp_ref, vp_ref = ins[5 * g:5 * g + 5]
        span = BAND * d

        def rows_of(start, size, d=d):
            return pl.ds(start, size) if d == 1 else pl.ds(start, size, stride=d)

        for t in range(ATTN_ROWS // BAND):
            u, c = divmod(t, d)
            base = u * span + c
            q = q_ref[rows_of(base, BAND), :].astype(BF16)
            if u == 0:
                k = jnp.concatenate([kp_ref[rows_of(c, BAND), :], kc_ref[rows_of(c, BAND), :]], axis=0)
                v = jnp.concatenate([vp_ref[rows_of(c, BAND), :], vc_ref[rows_of(c, BAND), :]], axis=0)
            else:
                k = kc_ref[rows_of(base - span, 2 * BAND), :]
                v = vc_ref[rows_of(base - span, 2 * BAND), :]
            s = lax.dot_general(q, k.astype(BF16), nt_dims, preferred_element_type=F32)
            s = jnp.where(band_first if u == 0 else band, s, -jnp.inf)
            m = jnp.max(jnp.maximum(s[:, :BAND], s[:, BAND:]), axis=-1, keepdims=True)
            p = jnp.exp(s - m).astype(BF16)
            ov = jnp.dot(p, jnp.concatenate([v.astype(BF16), ones], axis=1), preferred_element_type=F32)
            l = ov[:, HEAD_DIM:]
            o = ov[:, :HEAD_DIM] / l
            lse = m + jnp.log(l)
            if g > 0:
                o_rows = rows_of((g - 1) * ATTN_ROWS + base, BAND)
                og_ref[o_rows, :] = o
                lse_ref[o_rows, :] = lse
            else:
                outs, lses = [o], [lse]
                for other in range(n_groups - 1):
                    sl = pl.ds(other * ATTN_ROWS + base, BAND)
                    outs.append(og_ref[sl, :])
                    lses.append(lse_ref[sl, :])
                mx = functools.reduce(jnp.maximum, lses)
                es = [jnp.exp(x - mx) for x in lses]
                num = sum(e * x for e, x in zip(es, outs))
                gated = gate_ref[pl.ds(base, BAND), :] * (num / sum(es))
                y_ref[pl.ds(base, BAND), :] = gated.astype(BF16)

        for hh in range(N_HEADS):
            @pl.when(jnp.logical_and(i == n_tiles - 1, head == hh))
            def _(g=g, win=win, hh=hh, kc_ref=kc_ref, vc_ref=vc_ref):
                tails[g][pl.ds(hh, win, stride=KV_ROW), :] = kc_ref[pl.ds(ATTN_ROWS - win, win), :]
                tails[g][pl.ds(N_HEADS + hh, win, stride=KV_ROW), :] = (
                    vc_ref[pl.ds(ATTN_ROWS - win, win), :])


def _dilated_attention(z3, layer, depth, cache_bufs):
    bsz, seq, _ = z3.shape
    n_tiles = seq // ATTN_ROWS
    in_specs = []
    for g, (win, d) in enumerate(DIL_PAIRS):
        span = BAND * d
        qb = _seg_q(g) * N_HEADS
        kb = _seg_k(g) * N_HEADS
        vb = kb + N_HEADS
        per_tile = ATTN_ROWS // span

        def prev_map(col0, per_tile=per_tile):
            return lambda b, h, i: (b, jnp.maximum(i * per_tile - 1, 0), col0 + h)

        def cur_map(col0):
            return lambda b, h, i: (b, i, col0 + h)

        in_specs += [
            pl.BlockSpec((None, ATTN_ROWS, HEAD_DIM), cur_map(qb)),
            pl.BlockSpec((None, ATTN_ROWS, HEAD_DIM), cur_map(kb)),
            pl.BlockSpec((None, ATTN_ROWS, HEAD_DIM), cur_map(vb)),
            pl.BlockSpec((None, span, HEAD_DIM), prev_map(kb)),
            pl.BlockSpec((None, span, HEAD_DIM), prev_map(vb)),
        ]
    in_specs.append(pl.BlockSpec((None, ATTN_ROWS, HEAD_DIM),
                                 lambda b, h, i: (b, i, SEG_GATE_DIL * N_HEADS + h)))
    args = [z3] * len(in_specs)
    out_shape = [jax.ShapeDtypeStruct((bsz, seq, SEG), BF16)]
    out_specs = [pl.BlockSpec((None, ATTN_ROWS, HEAD_DIM), lambda b, h, i: (b, i, h))]
    aliases = {}
    for g, (win, _) in enumerate(DIL_PAIRS):
        out_shape.append(jax.ShapeDtypeStruct((depth, bsz, win * KV_ROW, HEAD_DIM), F32))
        out_specs.append(pl.BlockSpec((None, None, win * KV_ROW, HEAD_DIM),
                                      lambda b, h, i: (layer, b, 0, 0)))
        if cache_bufs is not None:
            in_specs.append(pl.BlockSpec(memory_space=pl.ANY))
            args.append(cache_bufs[g])
            aliases[len(args) - 1] = 1 + g
    return pl.pallas_call(
        functools.partial(_dilattn_kernel, n_tiles=n_tiles),
        out_shape=out_shape,
        grid=(bsz, N_HEADS, n_tiles),
        in_specs=in_specs,
        out_specs=out_specs,
        scratch_shapes=[
            pltpu.VMEM(((len(DIL_PAIRS) - 1) * ATTN_ROWS, HEAD_DIM), F32),
            pltpu.VMEM(((len(DIL_PAIRS) - 1) * ATTN_ROWS, HEAD_DIM), F32),
        ],
        input_output_aliases=aliases,
        compiler_params=pltpu.CompilerParams(
            dimension_semantics=("arbitrary", "arbitrary", "arbitrary"),
            vmem_limit_bytes=VMEM_LIMIT_BYTES),
        name="dilated_attention",
    )(*args)


def _mix_kernel(u_ref, up_ref, gp_ref, qm_ref, gm_ref, yd_ref, mkv_ref, pw_ref, ps_ref,
                wo_ref, x_ref, *rest, n_shift_in, n_shifts):
    shift_in, rest = rest[:n_shift_in], rest[n_shift_in:]
    o_ref, rest = rest[0], rest[1:]
    shift_out, (ue_ref, t_ref, y_ref) = rest[:n_shifts], rest[n_shifts:]
    i = pl.program_id(1)
    rows = u_ref.shape[0]

    pad, halo = POOL_PAD, POOL_HALO
    n = halo + rows

    @pl.when(i > 0)
    def _():
        ue_ref[pl.ds(pad, halo), :] = up_ref[...]

    @pl.when(i == 0)
    def _():
        ue_ref[pl.ds(pad, halo), :] = jnp.zeros((halo, SEG), F32)

    ue_ref[pl.ds(0, pad), :] = jnp.zeros((pad, SEG), F32)
    t_ref[0, pl.ds(0, pad), :] = jnp.zeros((pad, HEAD_DIM), F32)
    t_ref[1, pl.ds(0, pad), :] = jnp.zeros((pad, HEAD_DIM), F32)
    ue_ref[pl.ds(pad + halo, rows), :] = u_ref[...]
    pos = i * rows + lax.broadcasted_iota(jnp.int32, (rows, 1), 0)
    for gi, w in enumerate(POOL_WINDOWS):
        sl = slice(gi * HEAD_DIM, (gi + 1) * HEAD_DIM)
        cur = ue_ref[pl.ds(pad, n), sl] + ue_ref[pl.ds(pad - 1, n), sl]
        shift, buf = 2, 0
        while shift < w:
            t_ref[buf, pl.ds(pad, n), :] = cur
            cur = cur + t_ref[buf, pl.ds(pad - shift, n), :]
            shift, buf = 2 * shift, 1 - buf
        cnt = jnp.minimum(w, pos + 1).astype(F32)
        dlt = cur[halo:] / cnt - u_ref[:, sl]
        yp = jnp.dot(dlt.astype(BF16), pw_ref[gi], preferred_element_type=F32) * ps_ref[:, sl]
        y_ref[:, sl] = (gp_ref[:, sl] * yp).astype(BF16)

    y_ref[:, SEG:2 * SEG] = yd_ref[...]

    nt_dims = (((1,), (1,)), ((), ()))
    ones = jnp.ones((MEM_LEN, HEAD_DIM), BF16)
    for h in range(N_HEADS):
        sl = slice(h * HEAD_DIM, (h + 1) * HEAD_DIM)
        q = qm_ref[:, sl].astype(BF16)
        k = mkv_ref[:, sl].astype(BF16)
        v = mkv_ref[:, SEG + h * HEAD_DIM:SEG + (h + 1) * HEAD_DIM].astype(BF16)
        s = lax.dot_general(q, k, nt_dims, preferred_element_type=F32)
        m = jnp.max(s, axis=-1, keepdims=True)
        p = jnp.exp(s - m).astype(BF16)
        ov = jnp.dot(p, jnp.concatenate([v, ones], axis=1), preferred_element_type=F32)
        o = ov[:, :HEAD_DIM] / ov[:, HEAD_DIM:]
        y_ref[:, 2 * SEG + h * HEAD_DIM:2 * SEG + (h + 1) * HEAD_DIM] = (gm_ref[:, sl] * o).astype(BF16)

    _run_shifts(shift_in, shift_out)
    o_ref[...] = x_ref[...] + jnp.dot(y_ref[...], wo_ref[...], preferred_element_type=F32)


def _mix(z3, y_dil, mem_kv, pool_w, pool_scale, w_out, x3, layer, shift_jobs):
    bsz, seq, _ = z3.shape
    rows = MIX_ROWS
    n_tiles = seq // rows
    halo_per_tile = rows // POOL_HALO

    def seg_spec(seg):
        return pl.BlockSpec((None, rows, SEG), lambda b, i: (b, i, seg))

    s_specs, s_args, s_shapes, s_out_specs, aliases = _shift_jobs(
        shift_jobs, layer, lambda b, i: b * n_tiles + i, bsz * n_tiles, 11, 1)
    return pl.pallas_call(
        functools.partial(_mix_kernel, n_shift_in=len(s_args), n_shifts=len(s_shapes)),
        out_shape=[jax.ShapeDtypeStruct(x3.shape, F32)] + s_shapes,
        grid=(bsz, n_tiles),
        in_specs=[
            seg_spec(SEG_U),
            pl.BlockSpec((None, POOL_HALO, SEG),
                         lambda b, i: (b, jnp.maximum(i * halo_per_tile - 1, 0), SEG_U)),
            seg_spec(SEG_GATE_POOL),
            seg_spec(SEG_QMEM),
            seg_spec(SEG_GATE_MEM),
            pl.BlockSpec((None, rows, SEG), lambda b, i: (b, i, 0)),
            pl.BlockSpec((None, MEM_LEN, 2 * SEG), lambda b, i: (b, 0, 0)),
            pl.BlockSpec((None, len(POOL_WINDOWS), HEAD_DIM, HEAD_DIM), lambda b, i: (layer, 0, 0, 0)),
            pl.BlockSpec((None, 1, SEG), lambda b, i: (layer, 0, 0)),
            pl.BlockSpec((MIX_WIDTH, D_MODEL), lambda b, i: (0, 0), pipeline_mode=pl.Buffered(1)),
            pl.BlockSpec((None, rows, D_MODEL), lambda b, i: (b, i, 0)),
        ] + s_specs,
        out_specs=[pl.BlockSpec((None, rows, D_MODEL), lambda b, i: (b, i, 0))] + s_out_specs,
        scratch_shapes=[
            pltpu.VMEM((POOL_PAD + POOL_HALO + rows, SEG), F32),
            pltpu.VMEM((2, POOL_PAD + POOL_HALO + rows, HEAD_DIM), F32),
            pltpu.VMEM((rows, MIX_WIDTH), BF16),
        ],
        input_output_aliases=aliases,
        compiler_params=pltpu.CompilerParams(
            dimension_semantics=("arbitrary", "arbitrary"),
            vmem_limit_bytes=VMEM_LIMIT_BYTES),
        name="mix",
    )(z3, z3, z3, z3, z3, y_dil, mem_kv, pool_w, pool_scale, w_out, x3, *s_args)


def _sample_attn_kernel(zs_ref, c0_ref, c1_ref, c2_ref, cm_ref, o_ref):
    outs, lses = [], []
    for g, c_ref in enumerate((c0_ref, c1_ref, c2_ref)):
        q = zs_ref[pl.ds(_seg_q(g) * N_HEADS, N_HEADS), :]
        k_new = zs_ref[pl.ds(_seg_k(g) * N_HEADS, N_HEADS), :]
        v_new = zs_ref[pl.ds((_seg_k(g) + 1) * N_HEADS, N_HEADS), :]
        k = c_ref[:, 0]
        v = c_ref[:, 1]
        s = jnp.sum(k * q[None], axis=-1, keepdims=True)
        s_new = jnp.sum(k_new * q, axis=-1, keepdims=True)
        m = jnp.maximum(jnp.max(s, axis=0), s_new)
        p = jnp.exp(s - m[None])
        p_new = jnp.exp(s_new - m)
        l = jnp.sum(p, axis=0) + p_new
        outs.append((jnp.sum(p * v, axis=0) + p_new * v_new) / l)
        lses.append(m + jnp.log(l))
    mx = jnp.maximum(jnp.maximum(lses[0], lses[1]), lses[2])
    es = [jnp.exp(x - mx) for x in lses]
    o_ref[pl.ds(0, N_HEADS), :] = (es[0] * outs[0] + es[1] * outs[1] + es[2] * outs[2]) / (es[0] + es[1] + es[2])

    q = zs_ref[pl.ds(SEG_QMEM * N_HEADS, N_HEADS), :]
    k = cm_ref[:, 0]
    v = cm_ref[:, 1]
    s = jnp.sum(k * q[None], axis=-1, keepdims=True)
    m = jnp.max(s, axis=0)
    p = jnp.exp(s - m[None])
    o_ref[pl.ds(N_HEADS, N_HEADS), :] = jnp.sum(p * v, axis=0) / jnp.sum(p, axis=0)


def _sample_attention(zs3, caches7, cache_mem, layer):
    bsz = cache_mem.shape[1]
    in_specs = [pl.BlockSpec((None, N_SEG * N_HEADS, HEAD_DIM), lambda b: (b, 0, 0))]
    for _ in caches7:
        in_specs.append(pl.BlockSpec((None, None, BAND, None, 2, N_HEADS, HEAD_DIM),
                                     lambda b: (layer, b, 0, 0, 0, 0, 0)))
    in_specs.append(pl.BlockSpec((None, None, MEM_LEN, 2, N_HEADS, HEAD_DIM),
                                 lambda b: (layer, b, 0, 0, 0, 0)))
    return pl.pallas_call(
        _sample_attn_kernel,
        out_shape=jax.ShapeDtypeStruct((bsz, 2 * N_HEADS, HEAD_DIM), F32),
        grid=(bsz,),
        in_specs=in_specs,
        out_specs=pl.BlockSpec((None, 2 * N_HEADS, HEAD_DIM), lambda b: (b, 0, 0)),
        compiler_params=pltpu.CompilerParams(dimension_semantics=("arbitrary",)),
        name="sample_attention",
    )(zs3, *caches7, cache_mem)


def _sample_out_kernel(zs_ref, st_ref, ydm_ref, pw_ref, ps_ref, wo_ref, x_ref, o_ref, ns_ref,
                       wob_ref, y_ref):
    wob_ref[...] = wo_ref[...].astype(BF16)
    u = zs_ref[:, pl.ds(SEG_U * SEG, SEG)]
    for gi, w in enumerate(POOL_WINDOWS):
        sl = slice(gi * HEAD_DIM, (gi + 1) * HEAD_DIM)
        tot = u[:, sl]
        for back in range(1, w):
            tot = tot + st_ref[POOL_BUF - back, :, sl]
        cnt = float(min(w, PAST_LEN + 1))
        dlt = tot / cnt - u[:, sl]
        yp = jnp.dot(dlt.astype(BF16), pw_ref[gi], preferred_element_type=F32) * ps_ref[:, sl]
        y_ref[:, sl] = (zs_ref[:, pl.ds(SEG_GATE_POOL * SEG + gi * HEAD_DIM, HEAD_DIM)] * yp).astype(BF16)
    y_ref[:, SEG:2 * SEG] = (zs_ref[:, pl.ds(SEG_GATE_DIL * SEG, SEG)] * ydm_ref[:, pl.ds(0, SEG)]).astype(BF16)
    y_ref[:, 2 * SEG:] = (zs_ref[:, pl.ds(SEG_GATE_MEM * SEG, SEG)] * ydm_ref[:, pl.ds(SEG, SEG)]).astype(BF16)
    o_ref[...] = x_ref[...] + jnp.dot(y_ref[...], wob_ref[...], preferred_element_type=F32)
    for r in range(POOL_BUF - 1):
        ns_ref[r] = st_ref[r + 1]
    ns_ref[POOL_BUF - 1] = u


def _sample_out(zs, state_t, ydm, pool_w, pool_scale, w_out, xs, layer):
    rows = zs.shape[0]
    full = lambda shape: pl.BlockSpec(shape, lambda i: tuple(0 for _ in shape))
    of_layer = lambda a: pl.BlockSpec((None,) + a.shape[1:], lambda i: (layer,) + (0,) * (a.ndim - 1))
    return pl.pallas_call(
        _sample_out_kernel,
        out_shape=[jax.ShapeDtypeStruct(xs.shape, F32),
                   jax.ShapeDtypeStruct(state_t.shape[1:], F32),
                   jax.ShapeDtypeStruct(w_out.shape[1:], BF16)],
        grid=(1,),
        in_specs=[full(zs.shape), of_layer(state_t), full(ydm.shape), of_layer(pool_w),
                  of_layer(pool_scale), of_layer(w_out), full(xs.shape)],
        out_specs=[full(xs.shape), full(state_t.shape[1:]), full(w_out.shape[1:])],
        scratch_shapes=[pltpu.VMEM((rows, MIX_WIDTH), BF16)],
        compiler_params=pltpu.CompilerParams(
            dimension_semantics=("arbitrary",), vmem_limit_bytes=VMEM_LIMIT_BYTES),
        name="sample_out",
    )(zs, state_t, ydm, pool_w, pool_scale, w_out, xs)


def _set_last_kernel(*refs):
    n = len(refs) // 3
    for new_ref, o_ref in zip(refs[:n], refs[2 * n:]):
        o_ref[...] = new_ref[...]


def _cache_set_last(bufs, new_rows):
    n = len(bufs)
    depth, bsz = bufs[0].shape[:2]

    def last_row_spec(win):
        return pl.BlockSpec((None, bsz, KV_ROW, HEAD_DIM), lambda l: (l, 0, win - 1, 0))

    return pl.pallas_call(
        _set_last_kernel,
        out_shape=[jax.ShapeDtypeStruct(buf.shape, buf.dtype) for buf in bufs],
        grid=(depth,),
        in_specs=[pl.BlockSpec((None, bsz, KV_ROW, HEAD_DIM), lambda l: (l, 0, 0, 0))] * n
        + [pl.BlockSpec(memory_space=pl.ANY)] * n,
        out_specs=[last_row_spec(buf.shape[2] // KV_ROW) for buf in bufs],
        input_output_aliases={n + g: g for g in range(n)},
        compiler_params=pltpu.CompilerParams(dimension_semantics=("arbitrary",)),
        name="cache_set_last",
    )(*new_rows, *bufs)


def _tile_heads(v):
    return jnp.tile(v, N_HEADS)


def kernel(x_prompt, x_sample, state_pool, cache_dil_w128, cache_dil_w512, cache_dil_w2048,
           cache_mem_kv, mem_prompt, norm_g, w_in, pool_w, pool_scale, dil_q_norm, dil_k_norm,
           mem_norm_g, w_mem_kv, mem_q_norm, mem_k_norm, w_out):
    depth = w_in.shape[0]
    bsz, seq, _ = x_prompt.shape
    dbsz = x_sample.shape[0]
    caches = (cache_dil_w128, cache_dil_w512, cache_dil_w2048)

    z_pairs = tuple(zip(Z_ORDER[0::2], Z_ORDER[1::2]))
    pool_w_b = pool_w.astype(BF16)

    ones = jnp.ones((depth, SEG), F32)
    segs = [ones] * N_SEG
    for g in range(len(DIL_PAIRS)):
        segs[_seg_q(g)] = _tile_heads(dil_q_norm[:, g]) * ATTN_SCALE
        segs[_seg_k(g)] = _tile_heads(dil_k_norm[:, g])
    segs[SEG_QMEM] = _tile_heads(mem_q_norm) * ATTN_SCALE
    ep_gain = jnp.concatenate(segs, axis=1)[:, None, :]
    mem_gain = jnp.concatenate([_tile_heads(mem_k_norm), ones], axis=1)[:, None, :]
    seg_modes = [MODE_GATE if s in GATE_SEGS else MODE_PLAIN for s in range(N_SEG)]
    ep_mode = jnp.repeat(jnp.array(seg_modes, F32), SEG)[None, :]
    mem_mode = jnp.full((1, 2 * SEG), MODE_PLAIN, F32)
    gain = norm_g[:, None, :]
    mem_norm = mem_norm_g[:, None, :]
    pscale = pool_scale[:, None, :]

    caches7 = [c.reshape(depth, dbsz, win // d, d, 2, N_HEADS, HEAD_DIM)
               for c, (win, d) in zip(caches, DIL_PAIRS)]
    caches_2d = [c.reshape(depth, dbsz, win * KV_ROW, HEAD_DIM)
                 for c, (win, _) in zip(caches, DIL_PAIRS)]
    state_t = jnp.pad(jnp.transpose(state_pool, (0, 2, 1, 3)),
                      ((0, 0), (0, 0), (0, SAMPLE_ROWS - dbsz), (0, 0)))

    xp = x_prompt.reshape(bsz * seq, D_MODEL)
    xs = jnp.pad(x_sample.reshape(dbsz, D_MODEL), ((0, SAMPLE_ROWS - dbsz), (0, 0)))
    mem2 = mem_prompt.reshape(bsz * MEM_LEN, D_MODEL)

    pool_p, mem_p, pool_s, zs_rows = [], [], [], []
    prompt_caches = None
    sample_caches = [None] * len(DIL_PAIRS)
    for l in range(depth):
        zs, w_first, w_second = _project(xs, gain, (w_in, l, z_pairs), ep_gain, ep_mode,
                                         gain_layer=l, rows=SAMPLE_ROWS, name="proj_sample",
                                         emit_w=True)
        zs3 = zs.reshape(SAMPLE_ROWS, N_SEG * N_HEADS, HEAD_DIM)
        ydm = _sample_attention(zs3, caches7, cache_mem_kv, l)
        ydm = jnp.pad(ydm.reshape(dbsz, 2 * SEG), ((0, SAMPLE_ROWS - dbsz), (0, 0)))
        xs, new_state_t, w_out_b = _sample_out(zs, state_t, ydm, pool_w_b, pscale, w_out, xs, l)
        pool_s.append(new_state_t)
        zs_rows.append(zs)

        z, sample_caches[2] = _project(
            xp, gain, (w_first, w_second), ep_gain, ep_mode, gain_layer=l, rows=PROJ_ROWS,
            name="proj_prompt", shift=(l, [(caches_2d[2], sample_caches[2], SHIFT_ROWS)]))
        z3 = z.reshape(bsz, seq, IN_COLS)
        mkv, = _project(mem2, mem_norm, (w_mem_kv, l, ((0, 1),)), mem_gain, mem_mode,
                        gain_layer=l, rows=bsz * MEM_LEN, name="proj_mem")
        mkv3 = mkv.reshape(bsz, MEM_LEN, 2 * SEG)
        y_dil, *prompt_caches = _dilated_attention(z3, l, depth, prompt_caches)
        xp3, sample_caches[1], sample_caches[0] = _mix(
            z3, y_dil, mkv3, pool_w_b, pscale, w_out_b, xp.reshape(bsz, seq, D_MODEL), l,
            [(caches_2d[1], sample_caches[1], SHIFT_ROWS), (caches_2d[0], sample_caches[0], SHIFT_ROWS)])
        xp = xp3.reshape(bsz * seq, D_MODEL)
        pool_p.append(z3[:, seq - POOL_BUF:, SEG_U * SEG:(SEG_U + 1) * SEG])
        mem_p.append(mkv3)

    zs_all = jnp.stack(zs_rows)[:, :dbsz].reshape(depth, dbsz, N_SEG, N_HEADS, HEAD_DIM)
    new_rows = [zs_all[:, :, _seg_k(g):_seg_k(g) + 2].reshape(depth, dbsz, KV_ROW, HEAD_DIM)
                for g in range(len(DIL_PAIRS))]
    new_caches = _cache_set_last(sample_caches, new_rows)
    new_caches = [c.reshape(depth, dbsz, win, 2, N_HEADS, HEAD_DIM)
                  for c, (win, _) in zip(new_caches, DIL_PAIRS)]

    y_prompt = xp.reshape(bsz, seq, D_MODEL)
    y_sample = xs[:dbsz].reshape(dbsz, 1, D_MODEL)
    cache_mem_prompt = jnp.stack(mem_p).reshape(depth, bsz, MEM_LEN, 2, N_HEADS, HEAD_DIM)
    state_pool_sample = jnp.transpose(jnp.stack(pool_s)[:, :, :dbsz], (0, 2, 1, 3))
    prompt_caches = [c.reshape(depth, bsz, win, 2, N_HEADS, HEAD_DIM)
                     for c, (win, _) in zip(prompt_caches, DIL_PAIRS)]
    return (y_prompt, y_sample, jnp.stack(pool_p), prompt_caches[0], prompt_caches[1],
            prompt_caches[2], cache_mem_prompt, state_pool_sample, new_caches[0], new_caches[1],
            new_caches[2])
```

```python
import functools

import jax
import jax.numpy as jnp
from jax import lax
from jax.experimental import pallas as pl
from jax.experimental.pallas import tpu as pltpu

F32 = jnp.float32
BF16 = jnp.bfloat16

D_MODEL = 2048
HEAD_DIM = 128
N_HEADS = 4
SEG = N_HEADS * HEAD_DIM
N_SEG = 14
IN_COLS = N_SEG * SEG
POOL_WINDOWS = (2, 4, 8, 16)
POOL_BUF = 15
POOL_HALO = 16
POOL_PAD = 8
DIL_PAIRS = ((128, 1), (512, 4), (2048, 16))
BAND = 128
MEM_LEN = 256
MIX_WIDTH = 3 * SEG
EPS = 1e-6
ATTN_SCALE = HEAD_DIM ** -0.5
PAST_LEN = 16384

Z_ORDER = (2, 0, 3, 4, 5, 1, 6, 7, 8, 11, 9, 10, 12, 13)
SEG_U, SEG_GATE_POOL, SEG_GATE_DIL, SEG_QMEM, SEG_GATE_MEM = 1, 5, 9, 12, 13
GATE_SEGS = (SEG_GATE_POOL, SEG_GATE_DIL, SEG_GATE_MEM)


def _seg_q(g):
    return 4 * g


def _seg_k(g):
    return 4 * g + 2

VMEM_LIMIT_BYTES = 56 * 1024 * 1024

PROJ_ROWS = 1024
PROJ_ROW_PARTS = 4
PROJ_MIN_PART = 256
ATTN_ROWS = 2048
MIX_ROWS = 512
SAMPLE_ROWS = 16
KV_ROW = 2 * N_HEADS
SHIFT_ROWS = 512
SHIFT_CHUNK = 64


MODE_PLAIN, MODE_GATE = 0.0, 1.0


def _shift_rows(c_ref, nxt_ref, o_ref):
    n = c_ref.shape[0]
    chunk = min(n, SHIFT_CHUNK * KV_ROW)
    for lo in range(0, n - KV_ROW, chunk):
        size = min(chunk, n - KV_ROW - lo)
        o_ref[pl.ds(lo, size), :] = c_ref[pl.ds(lo + KV_ROW, size), :]
    o_ref[pl.ds(n - KV_ROW, KV_ROW), :] = nxt_ref[...]


def _shift_jobs(jobs, layer, step_of, n_steps, n_in, n_out):
    in_specs, args, out_shapes, out_specs, aliases = [], [], [], [], {}
    first_step = 0
    for cache, buf, rows in jobs:
        specs, a, shape, spec, alias, n_shift = _shift_operands(
            cache, buf, layer, step_of, n_steps, first_step, rows)
        if alias is not None:
            aliases[n_in + len(args) + alias] = n_out + len(out_shapes)
        in_specs += specs
        args += a
        out_shapes.append(shape)
        out_specs.append(spec)
        first_step += n_shift
    return in_specs, args, out_shapes, out_specs, aliases


def _run_shifts(in_refs, out_refs):
    per = len(in_refs) // len(out_refs)
    for n, o_ref in enumerate(out_refs):
        _shift_rows(in_refs[n * per], in_refs[n * per + 1], o_ref)


def _shift_operands(cache, buf, layer, step_of, n_steps, first_step, rows):
    bsz = cache.shape[1]
    win = cache.shape[2] // KV_ROW
    rows = min(win, rows)
    per_b = win // rows
    n_shift = bsz * per_b
    assert first_step + n_shift <= n_steps

    def block_of(*idx):
        t = jnp.clip(step_of(*idx) - first_step, 0, n_shift - 1)
        return t // per_b, t % per_b

    def main_map(*idx):
        b, r = block_of(*idx)
        return (layer, b, r, 0)

    def next_map(*idx):
        b, r = block_of(*idx)
        return (layer, b, jnp.minimum((r + 1) * rows, win - 1), 0)

    in_specs = [pl.BlockSpec((None, None, rows * KV_ROW, HEAD_DIM), main_map),
                pl.BlockSpec((None, None, KV_ROW, HEAD_DIM), next_map)]
    args = [cache, cache]
    alias = None
    if buf is not None:
        in_specs.append(pl.BlockSpec(memory_space=pl.ANY))
        args.append(buf)
        alias = 2
    out_shape = jax.ShapeDtypeStruct(cache.shape, cache.dtype)
    out_spec = pl.BlockSpec((None, None, rows * KV_ROW, HEAD_DIM), main_map)
    return in_specs, args, out_shape, out_spec, alias, n_shift


def _lookup(j, table):
    out = table[0]
    for t in range(1, len(table)):
        out = jnp.where(j == t, table[t], out)
    return out


def _proj_kernel(x_ref, g_ref, wa_ref, wb_ref, eg_ref, em_ref, *rest, n_shift_in, n_shifts,
                 emit_w):
    rest = list(rest)
    shift_in, rest = rest[:n_shift_in], rest[n_shift_in:]
    o_ref = rest.pop(0)
    if emit_w:
        wa_out_ref, wb_out_ref = rest.pop(0), rest.pop(0)
    shift_out, rest = rest[:n_shifts], rest[n_shifts:]
    h_ref, = rest
    j = pl.program_id(1)
    ii = pl.program_id(2)

    @pl.when(j == 0)
    def _():
        x = x_ref[...]
        ms = jnp.mean(x * x, axis=-1, keepdims=True)
        h_ref[ii] = (x * lax.rsqrt(ms + EPS) * g_ref[...]).astype(BF16)

    if n_shifts:
        _run_shifts(shift_in, shift_out)
    h_tile = h_ref[ii]
    wa = wa_ref[...].astype(BF16)
    wb = wb_ref[...].astype(BF16)
    if emit_w:
        wa_out_ref[...] = wa
        wb_out_ref[...] = wb
    rows = h_tile.shape[0]
    part = max(rows // PROJ_ROW_PARTS, min(rows, PROJ_MIN_PART))
    for r0 in range(0, rows, part):
        rsl = pl.ds(r0, part)
        acc = jnp.dot(h_tile[r0:r0 + part], wa, preferred_element_type=F32)
        for h in range(N_HEADS):
            sl = slice(h * HEAD_DIM, (h + 1) * HEAD_DIM)
            a = acc[:, sl]
            o_ref[rsl, sl] = a * (lax.rsqrt(jnp.mean(a * a, axis=-1, keepdims=True) + EPS) * eg_ref[:, sl])
    for r0 in range(0, rows, part):
        rsl = pl.ds(r0, part)
        acc = jnp.dot(h_tile[r0:r0 + part], wb, preferred_element_type=F32)
        for h in range(N_HEADS):
            sl = slice(h * HEAD_DIM, (h + 1) * HEAD_DIM)
            osl = slice(SEG + h * HEAD_DIM, SEG + (h + 1) * HEAD_DIM)
            a = acc[:, sl]
            o_ref[rsl, osl] = a * jnp.where(em_ref[:, osl] == MODE_GATE, jax.nn.sigmoid(a), 1.0)


def _project(x, gain, weights, ep_gain, ep_mode, *, gain_layer, rows, name, emit_w=False,
             shift=None):
    m, k = x.shape
    cols = 2 * SEG
    n_tiles = m // rows
    group = 2 if n_tiles % 2 == 0 else 1
    if len(weights) == 3:
        w, layer, seg_pairs = weights
        n_j = len(seg_pairs)
        firsts = tuple(p[0] for p in seg_pairs)
        seconds = tuple(p[1] for p in seg_pairs)
        w_specs = [pl.BlockSpec((None, k, SEG), lambda g, j, t: (layer, 0, _lookup(j, firsts))),
                   pl.BlockSpec((None, k, SEG), lambda g, j, t: (layer, 0, _lookup(j, seconds)))]
        w_args = [w, w]
    else:
        n_j = weights[0].shape[1] // SEG
        w_specs = [pl.BlockSpec((k, SEG), lambda g, j, t: (0, j))] * 2
        w_args = list(weights)
    n = n_j * cols

    def x_map(g, j, t):
        return (jnp.where(j == 0, g * group + t, g * group + group - 1), 0)

    in_specs = [
        pl.BlockSpec((rows, k), x_map),
        pl.BlockSpec((None, 1, k), lambda g, j, t: (gain_layer, 0, 0)),
        *w_specs,
        pl.BlockSpec((None, 1, cols), lambda g, j, t: (gain_layer, 0, j)),
        pl.BlockSpec((1, cols), lambda g, j, t: (0, j)),
    ]
    args = [x, gain, *w_args, ep_gain, ep_mode]
    out_shape = [jax.ShapeDtypeStruct((m, n), F32)]
    out_specs = [pl.BlockSpec((rows, cols), lambda g, j, t: (g * group + t, j))]
    if emit_w:
        assert n_tiles == 1
        out_shape += [jax.ShapeDtypeStruct((k, n_j * SEG), BF16)] * 2
        out_specs += [pl.BlockSpec((k, SEG), lambda g, j, t: (0, j))] * 2
    aliases, s_args, s_shapes = {}, [], []
    if shift is not None:
        s_specs, s_args, s_shapes, s_out_specs, aliases = _shift_jobs(
            shift[1], shift[0], lambda g, j, t: (g * n_j + j) * group + t, n_tiles * n_j,
            len(args), len(out_shape))
        in_specs += s_specs
        args += s_args
        out_shape += s_shapes
        out_specs += s_out_specs
    return pl.pallas_call(
        functools.partial(_proj_kernel, n_shift_in=len(s_args), n_shifts=len(s_shapes),
                          emit_w=emit_w),
        out_shape=out_shape,
        grid=(n_tiles // group, n_j, group),
        in_specs=in_specs,
        out_specs=out_specs,
        scratch_shapes=[pltpu.VMEM((group, rows, k), BF16)],
        input_output_aliases=aliases,
        compiler_params=pltpu.CompilerParams(
            dimension_semantics=("arbitrary", "arbitrary", "arbitrary"),
            vmem_limit_bytes=VMEM_LIMIT_BYTES),
        name=name,
    )(*args)


def _mem_kv_kernel(x_ref, g_ref, w_ref, kg_ref, o_ref):
    x = x_ref[...]
    ms = jnp.mean(x * x, axis=-1, keepdims=True)
    h = (x * lax.rsqrt(ms + EPS) * g_ref[...]).astype(BF16)
    acc = jnp.dot(h, w_ref[...].astype(BF16), preferred_element_type=F32)
    for hd in range(N_HEADS):
        sl = slice(hd * HEAD_DIM, (hd + 1) * HEAD_DIM)
        a = acc[:, sl]
        o_ref[:, sl] = a * (lax.rsqrt(jnp.mean(a * a, axis=-1, keepdims=True) + EPS) * kg_ref[:, sl])
    o_ref[:, SEG:] = acc[:, SEG:]


def _memory_kv(mem2, mem_norm, w_mem_kv, k_gain):
    depth, k, n = w_mem_kv.shape
    rows = mem2.shape[0]
    return pl.pallas_call(
        _mem_kv_kernel,
        out_shape=jax.ShapeDtypeStruct((depth, rows, n), F32),
        grid=(depth,),
        in_specs=[
            pl.BlockSpec((rows, k), lambda l: (0, 0)),
            pl.BlockSpec((None, 1, k), lambda l: (l, 0, 0)),
            pl.BlockSpec((None, k, n), lambda l: (l, 0, 0)),
            pl.BlockSpec((None, 1, SEG), lambda l: (l, 0, 0)),
        ],
        out_specs=pl.BlockSpec((None, rows, n), lambda l: (l, 0, 0)),
        compiler_params=pltpu.CompilerParams(
            dimension_semantics=("arbitrary",), vmem_limit_bytes=VMEM_LIMIT_BYTES),
        name="memory_kv",
    )(mem2, mem_norm, w_mem_kv, k_gain)


def _dilattn_kernel(*refs, n_tiles):
    n_groups = len(DIL_PAIRS)
    ins = refs[:5 * n_groups]
    gate_ref = refs[5 * n_groups]
    y_ref, *tails, og_ref, lse_ref = refs[len(refs) - 3 - n_groups:]
    head = pl.program_id(1)
    i = pl.program_id(2)

    row = lax.broadcasted_iota(jnp.int32, (BAND, 2 * BAND), 0)
    col = lax.broadcasted_iota(jnp.int32, (BAND, 2 * BAND), 1)
    not_future = col <= row + BAND
    band = jnp.logical_and(col >= row, not_future)
    first_lo = jnp.maximum(row, (i == 0).astype(jnp.int32) * BAND)
    band_first = jnp.logical_and(col >= first_lo, not_future)
    ones = jnp.ones((2 * BAND, HEAD_DIM), BF16)
    nt_dims = (((1,), (1,)), ((), ()))

    for g in range(n_groups - 1, -1, -1):
        win, d = DIL_PAIRS[g]
        q_ref, kc_ref, vc_ref, kp_ref, vp_ref = ins[5 * g:5 * g + 5]
        span = BAND * d

        def rows_of(start, size, d=d):
            return pl.ds(start, size) if d == 1 else pl.ds(start, size, stride=d)

        for t in range(ATTN_ROWS // BAND):
            u, c = divmod(t, d)
            base = u * span + c
            q = q_ref[rows_of(base, BAND), :].astype(BF16)
            if u == 0:
                k = jnp.concatenate([kp_ref[rows_of(c, BAND), :], kc_ref[rows_of(c, BAND), :]], axis=0)
                v = jnp.concatenate([vp_ref[rows_of(c, BAND), :], vc_ref[rows_of(c, BAND), :]], axis=0)
            else:
                k = kc_ref[rows_of(base - span, 2 * BAND), :]
                v = vc_ref[rows_of(base - span, 2 * BAND), :]
            s = lax.dot_general(q, k.astype(BF16), nt_dims, preferred_element_type=F32)
            s = jnp.where(band_first if u == 0 else band, s, -jnp.inf)
            m = jnp.max(jnp.maximum(s[:, :BAND], s[:, BAND:]), axis=-1, keepdims=True)
            p = jnp.exp(s - m).astype(BF16)
            ov = jnp.dot(p, jnp.concatenate([v.astype(BF16), ones], axis=1), preferred_element_type=F32)
            l = ov[:, HEAD_DIM:]
            o = ov[:, :HEAD_DIM] / l
            lse = m + jnp.log(l)
            if g > 0:
                o_rows = rows_of((g - 1) * ATTN_ROWS + base, BAND)
                og_ref[o_rows, :] = o
                lse_ref[o_rows, :] = lse
            else:
                outs, lses = [o], [lse]
                for other in range(n_groups - 1):
                    sl = pl.ds(other * ATTN_ROWS + base, BAND)
                    outs.append(og_ref[sl, :])
                    lses.append(lse_ref[sl, :])
                mx = functools.reduce(jnp.maximum, lses)
                es = [jnp.exp(x - mx) for x in lses]
                num = sum(e * x for e, x in zip(es, outs))
                gated = gate_ref[pl.ds(base, BAND), :] * (num / sum(es))
                y_ref[pl.ds(base, BAND), :] = gated.astype(BF16)

        for hh in range(N_HEADS):
            @pl.when(jnp.logical_and(i == n_tiles - 1, head == hh))
            def _(g=g, win=win, hh=hh, kc_ref=kc_ref, vc_ref=vc_ref):
                tails[g][pl.ds(hh, win, stride=KV_ROW), :] = kc_ref[pl.ds(ATTN_ROWS - win, win), :]
                tails[g][pl.ds(N_HEADS + hh, win, stride=KV_ROW), :] = (
                    vc_ref[pl.ds(ATTN_ROWS - win, win), :])


def _dilated_attention(z3, layer, depth, cache_bufs):
    bsz, seq, _ = z3.shape
    n_tiles = seq // ATTN_ROWS
    in_specs = []
    for g, (win, d) in enumerate(DIL_PAIRS):
        span = BAND * d
        qb = _seg_q(g) * N_HEADS
        kb = _seg_k(g) * N_HEADS
        vb = kb + N_HEADS
        per_tile = ATTN_ROWS // span

        def prev_map(col0, per_tile=per_tile):
            return lambda b, h, i: (b, jnp.maximum(i * per_tile - 1, 0), col0 + h)

        def cur_map(col0):
            return lambda b, h, i: (b, i, col0 + h)

        in_specs += [
            pl.BlockSpec((None, ATTN_ROWS, HEAD_DIM), cur_map(qb)),
            pl.BlockSpec((None, ATTN_ROWS, HEAD_DIM), cur_map(kb)),
            pl.BlockSpec((None, ATTN_ROWS, HEAD_DIM), cur_map(vb)),
            pl.BlockSpec((None, span, HEAD_DIM), prev_map(kb)),
            pl.BlockSpec((None, span, HEAD_DIM), prev_map(vb)),
        ]
    in_specs.append(pl.BlockSpec((None, ATTN_ROWS, HEAD_DIM),
                                 lambda b, h, i: (b, i, SEG_GATE_DIL * N_HEADS + h)))
    args = [z3] * len(in_specs)
    out_shape = [jax.ShapeDtypeStruct((bsz, seq, SEG), BF16)]
    out_specs = [pl.BlockSpec((None, ATTN_ROWS, HEAD_DIM), lambda b, h, i: (b, i, h))]
    aliases = {}
    for g, (win, _) in enumerate(DIL_PAIRS):
        out_shape.append(jax.ShapeDtypeStruct((depth, bsz, win * KV_ROW, HEAD_DIM), F32))
        out_specs.append(pl.BlockSpec((None, None, win * KV_ROW, HEAD_DIM),
                                      lambda b, h, i: (layer, b, 0, 0)))
        if cache_bufs is not None:
            in_specs.append(pl.BlockSpec(memory_space=pl.ANY))
            args.append(cache_bufs[g])
            aliases[len(args) - 1] = 1 + g
    return pl.pallas_call(
        functools.partial(_dilattn_kernel, n_tiles=n_tiles),
        out_shape=out_shape,
        grid=(bsz, N_HEADS, n_tiles),
        in_specs=in_specs,
        out_specs=out_specs,
        scratch_shapes=[
            pltpu.VMEM(((len(DIL_PAIRS) - 1) * ATTN_ROWS, HEAD_DIM), F32),
            pltpu.VMEM(((len(DIL_PAIRS) - 1) * ATTN_ROWS, HEAD_DIM), F32),
        ],
        input_output_aliases=aliases,
        compiler_params=pltpu.CompilerParams(
            dimension_semantics=("arbitrary", "arbitrary", "arbitrary"),
            vmem_limit_bytes=VMEM_LIMIT_BYTES),
        name="dilated_attention",
    )(*args)


def _mix_kernel(u_ref, up_ref, gp_ref, qm_ref, gm_ref, yd_ref, mkv_ref, pw_ref, ps_ref,
                wo_ref, x_ref, *rest, n_shift_in, n_shifts):
    shift_in, rest = rest[:n_shift_in], rest[n_shift_in:]
    o_ref, rest = rest[0], rest[1:]
    shift_out, (ue_ref, t_ref, y_ref) = rest[:n_shifts], rest[n_shifts:]
    i = pl.program_id(1)
    rows = u_ref.shape[0]

    pad, halo = POOL_PAD, POOL_HALO
    n = halo + rows

    @pl.when(i > 0)
    def _():
        ue_ref[pl.ds(pad, halo), :] = up_ref[...]

    @pl.when(i == 0)
    def _():
        ue_ref[pl.ds(pad, halo), :] = jnp.zeros((halo, SEG), F32)

    ue_ref[pl.ds(0, pad), :] = jnp.zeros((pad, SEG), F32)
    t_ref[0, pl.ds(0, pad), :] = jnp.zeros((pad, HEAD_DIM), F32)
    t_ref[1, pl.ds(0, pad), :] = jnp.zeros((pad, HEAD_DIM), F32)
    ue_ref[pl.ds(pad + halo, rows), :] = u_ref[...]
    pos = i * rows + lax.broadcasted_iota(jnp.int32, (rows, 1), 0)
    for gi, w in enumerate(POOL_WINDOWS):
        sl = slice(gi * HEAD_DIM, (gi + 1) * HEAD_DIM)
        cur = ue_ref[pl.ds(pad, n), sl] + ue_ref[pl.ds(pad - 1, n), sl]
        shift, buf = 2, 0
        while shift < w:
            t_ref[buf, pl.ds(pad, n), :] = cur
            cur = cur + t_ref[buf, pl.ds(pad - shift, n), :]
            shift, buf = 2 * shift, 1 - buf
        cnt = jnp.minimum(w, pos + 1).astype(F32)
        dlt = cur[halo:] / cnt - u_ref[:, sl]
        yp = jnp.dot(dlt.astype(BF16), pw_ref[gi], preferred_element_type=F32) * ps_ref[:, sl]
        y_ref[:, sl] = (gp_ref[:, sl] * yp).astype(BF16)

    y_ref[:, SEG:2 * SEG] = yd_ref[...]

    nt_dims = (((1,), (1,)), ((), ()))
    ones = jnp.ones((MEM_LEN, HEAD_DIM), BF16)
    for h in range(N_HEADS):
        sl = slice(h * HEAD_DIM, (h + 1) * HEAD_DIM)
        q = qm_ref[:, sl].astype(BF16)
        k = mkv_ref[:, sl].astype(BF16)
        v = mkv_ref[:, SEG + h * HEAD_DIM:SEG + (h + 1) * HEAD_DIM].astype(BF16)
        s = lax.dot_general(q, k, nt_dims, preferred_element_type=F32)
        m = jnp.max(s, axis=-1, keepdims=True)
        p = jnp.exp(s - m).astype(BF16)
        ov = jnp.dot(p, jnp.concatenate([v, ones], axis=1), preferred_element_type=F32)
        o = ov[:, :HEAD_DIM] / ov[:, HEAD_DIM:]
        y_ref[:, 2 * SEG + h * HEAD_DIM:2 * SEG + (h + 1) * HEAD_DIM] = (gm_ref[:, sl] * o).astype(BF16)

    _run_shifts(shift_in, shift_out)
    o_ref[...] = x_ref[...] + jnp.dot(y_ref[...], wo_ref[...], preferred_element_type=F32)


def _mix(z3, y_dil, mem_kv, pool_w, pool_scale, w_out, x3, layer, shift_jobs):
    bsz, seq, _ = z3.shape
    rows = MIX_ROWS
    n_tiles = seq // rows
    halo_per_tile = rows // POOL_HALO

    def seg_spec(seg):
        return pl.BlockSpec((None, rows, SEG), lambda b, i: (b, i, seg))

    s_specs, s_args, s_shapes, s_out_specs, aliases = _shift_jobs(
        shift_jobs, layer, lambda b, i: b * n_tiles + i, bsz * n_tiles, 11, 1)
    return pl.pallas_call(
        functools.partial(_mix_kernel, n_shift_in=len(s_args), n_shifts=len(s_shapes)),
        out_shape=[jax.ShapeDtypeStruct(x3.shape, F32)] + s_shapes,
        grid=(bsz, n_tiles),
        in_specs=[
            seg_spec(SEG_U),
            pl.BlockSpec((None, POOL_HALO, SEG),
                         lambda b, i: (b, jnp.maximum(i * halo_per_tile - 1, 0), SEG_U)),
            seg_spec(SEG_GATE_POOL),
            seg_spec(SEG_QMEM),
            seg_spec(SEG_GATE_MEM),
            pl.BlockSpec((None, rows, SEG), lambda b, i: (b, i, 0)),
            pl.BlockSpec((None, None, MEM_LEN, 2 * SEG), lambda b, i: (layer, b, 0, 0)),
            pl.BlockSpec((None, len(POOL_WINDOWS), HEAD_DIM, HEAD_DIM), lambda b, i: (layer, 0, 0, 0)),
            pl.BlockSpec((None, 1, SEG), lambda b, i: (layer, 0, 0)),
            pl.BlockSpec((MIX_WIDTH, D_MODEL), lambda b, i: (0, 0), pipeline_mode=pl.Buffered(1)),
            pl.BlockSpec((None, rows, D_MODEL), lambda b, i: (b, i, 0)),
        ] + s_specs,
        out_specs=[pl.BlockSpec((None, rows, D_MODEL), lambda b, i: (b, i, 0))] + s_out_specs,
        scratch_shapes=[
            pltpu.VMEM((POOL_PAD + POOL_HALO + rows, SEG), F32),
            pltpu.VMEM((2, POOL_PAD + POOL_HALO + rows, HEAD_DIM), F32),
            pltpu.VMEM((rows, MIX_WIDTH), BF16),
        ],
        input_output_aliases=aliases,
        compiler_params=pltpu.CompilerParams(
            dimension_semantics=("arbitrary", "arbitrary"),
            vmem_limit_bytes=VMEM_LIMIT_BYTES),
        name="mix",
    )(z3, z3, z3, z3, z3, y_dil, mem_kv, pool_w, pool_scale, w_out, x3, *s_args)


def _sample_attn_kernel(zs_ref, c0_ref, c1_ref, c2_ref, cm_ref, o_ref):
    outs, lses = [], []
    for g, c_ref in enumerate((c0_ref, c1_ref, c2_ref)):
        q = zs_ref[pl.ds(_seg_q(g) * N_HEADS, N_HEADS), :]
        k_new = zs_ref[pl.ds(_seg_k(g) * N_HEADS, N_HEADS), :]
        v_new = zs_ref[pl.ds((_seg_k(g) + 1) * N_HEADS, N_HEADS), :]
        k = c_ref[:, 0]
        v = c_ref[:, 1]
        s = jnp.sum(k * q[None], axis=-1, keepdims=True)
        s_new = jnp.sum(k_new * q, axis=-1, keepdims=True)
        m = jnp.maximum(jnp.max(s, axis=0), s_new)
        p = jnp.exp(s - m[None])
        p_new = jnp.exp(s_new - m)
        l = jnp.sum(p, axis=0) + p_new
        outs.append((jnp.sum(p * v, axis=0) + p_new * v_new) / l)
        lses.append(m + jnp.log(l))
    mx = jnp.maximum(jnp.maximum(lses[0], lses[1]), lses[2])
    es = [jnp.exp(x - mx) for x in lses]
    o_ref[pl.ds(0, N_HEADS), :] = (es[0] * outs[0] + es[1] * outs[1] + es[2] * outs[2]) / (es[0] + es[1] + es[2])

    q = zs_ref[pl.ds(SEG_QMEM * N_HEADS, N_HEADS), :]
    k = cm_ref[:, 0]
    v = cm_ref[:, 1]
    s = jnp.sum(k * q[None], axis=-1, keepdims=True)
    m = jnp.max(s, axis=0)
    p = jnp.exp(s - m[None])
    o_ref[pl.ds(N_HEADS, N_HEADS), :] = jnp.sum(p * v, axis=0) / jnp.sum(p, axis=0)


def _sample_attention(zs3, caches7, cache_mem, layer):
    bsz = cache_mem.shape[1]
    in_specs = [pl.BlockSpec((None, N_SEG * N_HEADS, HEAD_DIM), lambda b: (b, 0, 0))]
    for _ in caches7:
        in_specs.append(pl.BlockSpec((None, None, BAND, None, 2, N_HEADS, HEAD_DIM),
                                     lambda b: (layer, b, 0, 0, 0, 0, 0)))
    in_specs.append(pl.BlockSpec((None, None, MEM_LEN, 2, N_HEADS, HEAD_DIM),
                                 lambda b: (layer, b, 0, 0, 0, 0)))
    return pl.pallas_call(
        _sample_attn_kernel,
        out_shape=jax.ShapeDtypeStruct((bsz, 2 * N_HEADS, HEAD_DIM), F32),
        grid=(bsz,),
        in_specs=in_specs,
        out_specs=pl.BlockSpec((None, 2 * N_HEADS, HEAD_DIM), lambda b: (b, 0, 0)),
        compiler_params=pltpu.CompilerParams(dimension_semantics=("arbitrary",)),
        name="sample_attention",
    )(zs3, *caches7, cache_mem)


def _sample_out_kernel(zs_ref, st_ref, ydm_ref, pw_ref, ps_ref, wo_ref, x_ref, o_ref, ns_ref,
                       wob_ref, y_ref):
    wob_ref[...] = wo_ref[...].astype(BF16)
    u = zs_ref[:, pl.ds(SEG_U * SEG, SEG)]
    for gi, w in enumerate(POOL_WINDOWS):
        sl = slice(gi * HEAD_DIM, (gi + 1) * HEAD_DIM)
        tot = u[:, sl]
        for back in range(1, w):
            tot = tot + st_ref[POOL_BUF - back, :, sl]
        cnt = float(min(w, PAST_LEN + 1))
        dlt = tot / cnt - u[:, sl]
        yp = jnp.dot(dlt.astype(BF16), pw_ref[gi], preferred_element_type=F32) * ps_ref[:, sl]
        y_ref[:, sl] = (zs_ref[:, pl.ds(SEG_GATE_POOL * SEG + gi * HEAD_DIM, HEAD_DIM)] * yp).astype(BF16)
    y_ref[:, SEG:2 * SEG] = (zs_ref[:, pl.ds(SEG_GATE_DIL * SEG, SEG)] * ydm_ref[:, pl.ds(0, SEG)]).astype(BF16)
    y_ref[:, 2 * SEG:] = (zs_ref[:, pl.ds(SEG_GATE_MEM * SEG, SEG)] * ydm_ref[:, pl.ds(SEG, SEG)]).astype(BF16)
    o_ref[...] = x_ref[...] + jnp.dot(y_ref[...], wob_ref[...], preferred_element_type=F32)
    for r in range(POOL_BUF - 1):
        ns_ref[r] = st_ref[r + 1]
    ns_ref[POOL_BUF - 1] = u


def _sample_out(zs, state_t, ydm, pool_w, pool_scale, w_out, xs, layer):
    rows = zs.shape[0]
    full = lambda shape: pl.BlockSpec(shape, lambda i: tuple(0 for _ in shape))
    of_layer = lambda a: pl.BlockSpec((None,) + a.shape[1:], lambda i: (layer,) + (0,) * (a.ndim - 1))
    return pl.pallas_call(
        _sample_out_kernel,
        out_shape=[jax.ShapeDtypeStruct(xs.shape, F32),
                   jax.ShapeDtypeStruct(state_t.shape[1:], F32),
                   jax.ShapeDtypeStruct(w_out.shape[1:], BF16)],
        grid=(1,),
        in_specs=[full(zs.shape), of_layer(state_t), full(ydm.shape), of_layer(pool_w),
                  of_layer(pool_scale), of_layer(w_out), full(xs.shape)],
        out_specs=[full(xs.shape), full(state_t.shape[1:]), full(w_out.shape[1:])],
        scratch_shapes=[pltpu.VMEM((rows, MIX_WIDTH), BF16)],
        compiler_params=pltpu.CompilerParams(
            dimension_semantics=("arbitrary",), vmem_limit_bytes=VMEM_LIMIT_BYTES),
        name="sample_out",
    )(zs, state_t, ydm, pool_w, pool_scale, w_out, xs)


def _set_last_kernel(*refs):
    n = len(refs) // 3
    for new_ref, o_ref in zip(refs[:n], refs[2 * n:]):
        o_ref[...] = new_ref[...]


def _cache_set_last(bufs, new_rows):
    n = len(bufs)
    depth, bsz = bufs[0].shape[:2]

    def last_row_spec(win):
        return pl.BlockSpec((None, bsz, KV_ROW, HEAD_DIM), lambda l: (l, 0, win - 1, 0))

    return pl.pallas_call(
        _set_last_kernel,
        out_shape=[jax.ShapeDtypeStruct(buf.shape, buf.dtype) for buf in bufs],
        grid=(depth,),
        in_specs=[pl.BlockSpec((None, bsz, KV_ROW, HEAD_DIM), lambda l: (l, 0, 0, 0))] * n
        + [pl.BlockSpec(memory_space=pl.ANY)] * n,
        out_specs=[last_row_spec(buf.shape[2] // KV_ROW) for buf in bufs],
        input_output_aliases={n + g: g for g in range(n)},
        compiler_params=pltpu.CompilerParams(dimension_semantics=("arbitrary",)),
        name="cache_set_last",
    )(*new_rows, *bufs)


def _tile_heads(v):
    return jnp.tile(v, N_HEADS)


def kernel(x_prompt, x_sample, state_pool, cache_dil_w128, cache_dil_w512, cache_dil_w2048,
           cache_mem_kv, mem_prompt, norm_g, w_in, pool_w, pool_scale, dil_q_norm, dil_k_norm,
           mem_norm_g, w_mem_kv, mem_q_norm, mem_k_norm, w_out):
    depth = w_in.shape[0]
    bsz, seq, _ = x_prompt.shape
    dbsz = x_sample.shape[0]
    caches = (cache_dil_w128, cache_dil_w512, cache_dil_w2048)

    z_pairs = tuple(zip(Z_ORDER[0::2], Z_ORDER[1::2]))
    pool_w_b = pool_w.astype(BF16)

    ones = jnp.ones((depth, SEG), F32)
    segs = [ones] * N_SEG
    for g in range(len(DIL_PAIRS)):
        segs[_seg_q(g)] = _tile_heads(dil_q_norm[:, g]) * ATTN_SCALE
        segs[_seg_k(g)] = _tile_heads(dil_k_norm[:, g])
    segs[SEG_QMEM] = _tile_heads(mem_q_norm) * ATTN_SCALE
    ep_gain = jnp.concatenate(segs, axis=1)[:, None, :]
    mem_k_gain = _tile_heads(mem_k_norm)[:, None, :]
    seg_modes = [MODE_GATE if s in GATE_SEGS else MODE_PLAIN for s in range(N_SEG)]
    ep_mode = jnp.repeat(jnp.array(seg_modes, F32), SEG)[None, :]
    gain = norm_g[:, None, :]
    mem_norm = mem_norm_g[:, None, :]
    pscale = pool_scale[:, None, :]

    caches7 = [c.reshape(depth, dbsz, win // d, d, 2, N_HEADS, HEAD_DIM)
               for c, (win, d) in zip(caches, DIL_PAIRS)]
    caches_2d = [c.reshape(depth, dbsz, win * KV_ROW, HEAD_DIM)
                 for c, (win, _) in zip(caches, DIL_PAIRS)]
    state_t = jnp.pad(jnp.transpose(state_pool, (0, 2, 1, 3)),
                      ((0, 0), (0, 0), (0, SAMPLE_ROWS - dbsz), (0, 0)))

    xp = x_prompt.reshape(bsz * seq, D_MODEL)
    xs = jnp.pad(x_sample.reshape(dbsz, D_MODEL), ((0, SAMPLE_ROWS - dbsz), (0, 0)))
    mem2 = mem_prompt.reshape(bsz * MEM_LEN, D_MODEL)
    mem_kv = _memory_kv(mem2, mem_norm, w_mem_kv, mem_k_gain).reshape(depth, bsz, MEM_LEN, 2 * SEG)

    pool_p, pool_s, zs_rows = [], [], []
    prompt_caches = None
    sample_caches = [None] * len(DIL_PAIRS)
    for l in range(depth):
        zs, w_first, w_second = _project(xs, gain, (w_in, l, z_pairs), ep_gain, ep_mode,
                                         gain_layer=l, rows=SAMPLE_ROWS, name="proj_sample",
                                         emit_w=True)
        zs3 = zs.reshape(SAMPLE_ROWS, N_SEG * N_HEADS, HEAD_DIM)
        ydm = _sample_attention(zs3, caches7, cache_mem_kv, l)
        ydm = jnp.pad(ydm.reshape(dbsz, 2 * SEG), ((0, SAMPLE_ROWS - dbsz), (0, 0)))
        xs, new_state_t, w_out_b = _sample_out(zs, state_t, ydm, pool_w_b, pscale, w_out, xs, l)
        pool_s.append(new_state_t)
        zs_rows.append(zs)

        z, sample_caches[2] = _project(
            xp, gain, (w_first, w_second), ep_gain, ep_mode, gain_layer=l, rows=PROJ_ROWS,
            name="proj_prompt", shift=(l, [(caches_2d[2], sample_caches[2], SHIFT_ROWS)]))
        z3 = z.reshape(bsz, seq, IN_COLS)
        y_dil, *prompt_caches = _dilated_attention(z3, l, depth, prompt_caches)
        xp3, sample_caches[1], sample_caches[0] = _mix(
            z3, y_dil, mem_kv, pool_w_b, pscale, w_out_b, xp.reshape(bsz, seq, D_MODEL), l,
            [(caches_2d[1], sample_caches[1], SHIFT_ROWS), (caches_2d[0], sample_caches[0], SHIFT_ROWS)])
        xp = xp3.reshape(bsz * seq, D_MODEL)
        pool_p.append(z3[:, seq - POOL_BUF:, SEG_U * SEG:(SEG_U + 1) * SEG])

    zs_all = jnp.stack(zs_rows)[:, :dbsz].reshape(depth, dbsz, N_SEG, N_HEADS, HEAD_DIM)
    new_rows = [zs_all[:, :, _seg_k(g):_seg_k(g) + 2].reshape(depth, dbsz, KV_ROW, HEAD_DIM)
                for g in range(len(DIL_PAIRS))]
    new_caches = _cache_set_last(sample_caches, new_rows)
    new_caches = [c.reshape(depth, dbsz, win, 2, N_HEADS, HEAD_DIM)
                  for c, (win, _) in zip(new_caches, DIL_PAIRS)]

    y_prompt = xp.reshape(bsz, seq, D_MODEL)
    y_sample = xs[:dbsz].reshape(dbsz, 1, D_MODEL)
    cache_mem_prompt = mem_kv.reshape(depth, bsz, MEM_LEN, 2, N_HEADS, HEAD_DIM)
    state_pool_sample = jnp.transpose(jnp.stack(pool_s)[:, :, :dbsz], (0, 2, 1, 3))
    prompt_caches = [c.reshape(depth, bsz, win, 2, N_HEADS, HEAD_DIM)
                     for c, (win, _) in zip(prompt_caches, DIL_PAIRS)]
    return (y_prompt, y_sample, jnp.stack(pool_p), prompt_caches[0], prompt_caches[1],
            prompt_caches[2], cache_mem_prompt, state_pool_sample, new_caches[0], new_caches[1],
            new_caches[2])
```

```python
import functools

import jax
import jax.numpy as jnp
from jax import lax
from jax.experimental import pallas as pl
from jax.experimental.pallas import tpu as pltpu

F32 = jnp.float32
BF16 = jnp.bfloat16

D_MODEL = 2048
HEAD_DIM = 128
N_HEADS = 4
SEG = N_HEADS * HEAD_DIM
N_SEG = 14
IN_COLS = N_SEG * SEG
POOL_WINDOWS = (2, 4, 8, 16)
POOL_BUF = 15
POOL_HALO = 16
POOL_PAD = 8
DIL_PAIRS = ((128, 1), (512, 4), (2048, 16))
BAND = 128
MEM_LEN = 256
MIX_WIDTH = 3 * SEG
EPS = 1e-6
ATTN_SCALE = HEAD_DIM ** -0.5
PAST_LEN = 16384

Z_ORDER = (2, 0, 3, 4, 5, 1, 6, 7, 8, 11, 9, 10, 12, 13)
SEG_U, SEG_GATE_POOL, SEG_GATE_DIL, SEG_QMEM, SEG_GATE_MEM = 1, 5, 9, 12, 13
GATE_SEGS = (SEG_GATE_POOL, SEG_GATE_DIL, SEG_GATE_MEM)


def _seg_q(g):
    return 4 * g


def _seg_k(g):
    return 4 * g + 2

VMEM_LIMIT_BYTES = 56 * 1024 * 1024

PROJ_ROWS = 1024
PROJ_ROW_PARTS = 4
PROJ_MIN_PART = 256
ATTN_ROWS = 2048
MIX_ROWS = 512
SAMPLE_ROWS = 16
KV_ROW = 2 * N_HEADS
SHIFT_ROWS = 512
SHIFT_CHUNK = 64


MODE_PLAIN, MODE_GATE = 0.0, 1.0


def _shift_rows(c_ref, nxt_ref, o_ref):
    n = c_ref.shape[0]
    chunk = min(n, SHIFT_CHUNK * KV_ROW)
    for lo in range(0, n - KV_ROW, chunk):
        size = min(chunk, n - KV_ROW - lo)
        o_ref[pl.ds(lo, size), :] = c_ref[pl.ds(lo + KV_ROW, size), :]
    o_ref[pl.ds(n - KV_ROW, KV_ROW), :] = nxt_ref[...]


def _shift_jobs(jobs, layer, step_of, n_steps, n_in, n_out):
    in_specs, args, out_shapes, out_specs, aliases = [], [], [], [], {}
    first_step = 0
    for cache, buf, rows in jobs:
        specs, a, shape, spec, alias, n_shift = _shift_operands(
            cache, buf, layer, step_of, n_steps, first_step, rows)
        if alias is not None:
            aliases[n_in + len(args) + alias] = n_out + len(out_shapes)
        in_specs += specs
        args += a
        out_shapes.append(shape)
        out_specs.append(spec)
        first_step += n_shift
    return in_specs, args, out_shapes, out_specs, aliases


def _run_shifts(in_refs, out_refs):
    per = len(in_refs) // len(out_refs)
    for n, o_ref in enumerate(out_refs):
        _shift_rows(in_refs[n * per], in_refs[n * per + 1], o_ref)


def _shift_operands(cache, buf, layer, step_of, n_steps, first_step, rows):
    bsz = cache.shape[1]
    win = cache.shape[2] // KV_ROW
    rows = min(win, rows)
    per_b = win // rows
    n_shift = bsz * per_b
    assert first_step + n_shift <= n_steps

    def block_of(*idx):
        t = jnp.clip(step_of(*idx) - first_step, 0, n_shift - 1)
        return t // per_b, t % per_b

    def main_map(*idx):
        b, r = block_of(*idx)
        return (layer, b, r, 0)

    def next_map(*idx):
        b, r = block_of(*idx)
        return (layer, b, jnp.minimum((r + 1) * rows, win - 1), 0)

    in_specs = [pl.BlockSpec((None, None, rows * KV_ROW, HEAD_DIM), main_map),
                pl.BlockSpec((None, None, KV_ROW, HEAD_DIM), next_map)]
    args = [cache, cache]
    alias = None
    if buf is not None:
        in_specs.append(pl.BlockSpec(memory_space=pl.ANY))
        args.append(buf)
        alias = 2
    out_shape = jax.ShapeDtypeStruct(cache.shape, cache.dtype)
    out_spec = pl.BlockSpec((None, None, rows * KV_ROW, HEAD_DIM), main_map)
    return in_specs, args, out_shape, out_spec, alias, n_shift


def _lookup(j, table):
    out = table[0]
    for t in range(1, len(table)):
        out = jnp.where(j == t, table[t], out)
    return out


def _proj_kernel(x_ref, g_ref, wa_ref, wb_ref, eg_ref, em_ref, *rest, n_shift_in, n_shifts,
                 emit_w):
    rest = list(rest)
    shift_in, rest = rest[:n_shift_in], rest[n_shift_in:]
    o_ref = rest.pop(0)
    if emit_w:
        wa_out_ref, wb_out_ref = rest.pop(0), rest.pop(0)
    shift_out, rest = rest[:n_shifts], rest[n_shifts:]
    h_ref, = rest
    j = pl.program_id(1)
    ii = pl.program_id(2)

    @pl.when(j == 0)
    def _():
        x = x_ref[...]
        ms = jnp.mean(x * x, axis=-1, keepdims=True)
        h_ref[ii] = (x * lax.rsqrt(ms + EPS) * g_ref[...]).astype(BF16)

    if n_shifts:
        _run_shifts(shift_in, shift_out)
    h_tile = h_ref[ii]
    wa = wa_ref[...].astype(BF16)
    wb = wb_ref[...].astype(BF16)
    if emit_w:
        wa_out_ref[...] = wa
        wb_out_ref[...] = wb
    rows = h_tile.shape[0]
    part = max(rows // PROJ_ROW_PARTS, min(rows, PROJ_MIN_PART))
    for r0 in range(0, rows, part):
        rsl = pl.ds(r0, part)
        acc = jnp.dot(h_tile[r0:r0 + part], wa, preferred_element_type=F32)
        for h in range(N_HEADS):
            sl = slice(h * HEAD_DIM, (h + 1) * HEAD_DIM)
            a = acc[:, sl]
            o_ref[rsl, sl] = a * (lax.rsqrt(jnp.mean(a * a, axis=-1, keepdims=True) + EPS) * eg_ref[:, sl])
    for r0 in range(0, rows, part):
        rsl = pl.ds(r0, part)
        acc = jnp.dot(h_tile[r0:r0 + part], wb, preferred_element_type=F32)
        for h in range(N_HEADS):
            sl = slice(h * HEAD_DIM, (h + 1) * HEAD_DIM)
            osl = slice(SEG + h * HEAD_DIM, SEG + (h + 1) * HEAD_DIM)
            a = acc[:, sl]
            o_ref[rsl, osl] = a * jnp.where(em_ref[:, osl] == MODE_GATE, jax.nn.sigmoid(a), 1.0)


def _project(x, gain, weights, ep_gain, ep_mode, *, gain_layer, rows, name, emit_w=False,
             shift=None):
    m, k = x.shape
    cols = 2 * SEG
    n_tiles = m // rows
    group = 2 if n_tiles % 2 == 0 else 1
    if len(weights) == 3:
        w, layer, seg_pairs = weights
        n_j = len(seg_pairs)
        firsts = tuple(p[0] for p in seg_pairs)
        seconds = tuple(p[1] for p in seg_pairs)
        w_specs = [pl.BlockSpec((None, k, SEG), lambda g, j, t: (layer, 0, _lookup(j, firsts))),
                   pl.BlockSpec((None, k, SEG), lambda g, j, t: (layer, 0, _lookup(j, seconds)))]
        w_args = [w, w]
    else:
        n_j = weights[0].shape[1] // SEG
        w_specs = [pl.BlockSpec((k, SEG), lambda g, j, t: (0, j))] * 2
        w_args = list(weights)
    n = n_j * cols

    def x_map(g, j, t):
        return (jnp.where(j == 0, g * group + t, g * group + group - 1), 0)

    in_specs = [
        pl.BlockSpec((rows, k), x_map),
        pl.BlockSpec((None, 1, k), lambda g, j, t: (gain_layer, 0, 0)),
        *w_specs,
        pl.BlockSpec((None, 1, cols), lambda g, j, t: (gain_layer, 0, j)),
        pl.BlockSpec((1, cols), lambda g, j, t: (0, j)),
    ]
    args = [x, gain, *w_args, ep_gain, ep_mode]
    out_shape = [jax.ShapeDtypeStruct((m, n), F32)]
    out_specs = [pl.BlockSpec((rows, cols), lambda g, j, t: (g * group + t, j))]
    if emit_w:
        assert n_tiles == 1
        out_shape += [jax.ShapeDtypeStruct((k, n_j * SEG), BF16)] * 2
        out_specs += [pl.BlockSpec((k, SEG), lambda g, j, t: (0, j))] * 2
    aliases, s_args, s_shapes = {}, [], []
    if shift is not None:
        s_specs, s_args, s_shapes, s_out_specs, aliases = _shift_jobs(
            shift[1], shift[0], lambda g, j, t: (g * n_j + j) * group + t, n_tiles * n_j,
            len(args), len(out_shape))
        in_specs += s_specs
        args += s_args
        out_shape += s_shapes
        out_specs += s_out_specs
    return pl.pallas_call(
        functools.partial(_proj_kernel, n_shift_in=len(s_args), n_shifts=len(s_shapes),
                          emit_w=emit_w),
        out_shape=out_shape,
        grid=(n_tiles // group, n_j, group),
        in_specs=in_specs,
        out_specs=out_specs,
        scratch_shapes=[pltpu.VMEM((group, rows, k), BF16)],
        input_output_aliases=aliases,
        compiler_params=pltpu.CompilerParams(
            dimension_semantics=("arbitrary", "arbitrary", "arbitrary"),
            vmem_limit_bytes=VMEM_LIMIT_BYTES),
        name=name,
    )(*args)


def _mem_kv_kernel(x_ref, g_ref, w_ref, kg_ref, o_ref):
    x = x_ref[...]
    ms = jnp.mean(x * x, axis=-1, keepdims=True)
    h = (x * lax.rsqrt(ms + EPS) * g_ref[...]).astype(BF16)
    acc = jnp.dot(h, w_ref[...].astype(BF16), preferred_element_type=F32)
    for hd in range(N_HEADS):
        sl = slice(hd * HEAD_DIM, (hd + 1) * HEAD_DIM)
        a = acc[:, sl]
        o_ref[:, sl] = a * (lax.rsqrt(jnp.mean(a * a, axis=-1, keepdims=True) + EPS) * kg_ref[:, sl])
    o_ref[:, SEG:] = acc[:, SEG:]


def _memory_kv(mem2, mem_norm, w_mem_kv, k_gain):
    depth, k, n = w_mem_kv.shape
    rows = mem2.shape[0]
    return pl.pallas_call(
        _mem_kv_kernel,
        out_shape=jax.ShapeDtypeStruct((depth, rows, n), F32),
        grid=(depth,),
        in_specs=[
            pl.BlockSpec((rows, k), lambda l: (0, 0)),
            pl.BlockSpec((None, 1, k), lambda l: (l, 0, 0)),
            pl.BlockSpec((None, k, n), lambda l: (l, 0, 0)),
            pl.BlockSpec((None, 1, SEG), lambda l: (l, 0, 0)),
        ],
        out_specs=pl.BlockSpec((None, rows, n), lambda l: (l, 0, 0)),
        compiler_params=pltpu.CompilerParams(
            dimension_semantics=("arbitrary",), vmem_limit_bytes=VMEM_LIMIT_BYTES),
        name="memory_kv",
    )(mem2, mem_norm, w_mem_kv, k_gain)


def _dilattn_kernel(*refs, n_tiles):
    n_groups = len(DIL_PAIRS)
    ins = refs[:5 * n_groups]
    gate_ref = refs[5 * n_groups]
    y_ref, *tails, og_ref, lse_ref = refs[len(refs) - 3 - n_groups:]
    head = pl.program_id(1)
    i = pl.program_id(2)

    row = lax.broadcasted_iota(jnp.int32, (BAND, 2 * BAND), 0)
    col = lax.broadcasted_iota(jnp.int32, (BAND, 2 * BAND), 1)
    not_future = col <= row + BAND
    band = jnp.logical_and(col >= row, not_future)
    first_lo = jnp.maximum(row, (i == 0).astype(jnp.int32) * BAND)
    band_first = jnp.logical_and(col >= first_lo, not_future)
    ones = jnp.ones((2 * BAND, HEAD_DIM), BF16)
    nt_dims = (((1,), (1,)), ((), ()))

    for g in range(n_groups - 1, -1, -1):
        win, d = DIL_PAIRS[g]
        q_ref, kc_ref, vc_ref, kp_ref, vp_ref = ins[5 * g:5 * g + 5]
        span = BAND * d

        def rows_of(start, size, d=d):
            return pl.ds(start, size) if d == 1 else pl.ds(start, size, stride=d)

        for t in range(ATTN_ROWS // BAND):
            u, c = divmod(t, d)
            base = u * span + c
            q = q_ref[rows_of(base, BAND), :].astype(BF16)
            if u == 0:
                k = jnp.concatenate([kp_ref[rows_of(c, BAND), :], kc_ref[rows_of(c, BAND), :]], axis=0)
                v = jnp.concatenate([vp_ref[rows_of(c, BAND), :], vc_ref[rows_of(c, BAND), :]], axis=0)
            else:
                k = kc_ref[rows_of(base - span, 2 * BAND), :]
                v = vc_ref[rows_of(base - span, 2 * BAND), :]
            s = lax.dot_general(q, k.astype(BF16), nt_dims, preferred_element_type=F32)
            s = jnp.where(band_first if u == 0 else band, s, -jnp.inf)
            m = jnp.max(jnp.maximum(s[:, :BAND], s[:, BAND:]), axis=-1, keepdims=True)
            p = jnp.exp(s - m).astype(BF16)
            ov = jnp.dot(p, jnp.concatenate([v.astype(BF16), ones], axis=1), preferred_element_type=F32)
            l = ov[:, HEAD_DIM:]
            o = ov[:, :HEAD_DIM] / l
            lse = m + jnp.log(l)
            if g > 0:
                o_rows = rows_of((g - 1) * ATTN_ROWS + base, BAND)
                og_ref[o_rows, :] = o
                lse_ref[o_rows, :] = lse
            else:
                outs, lses = [o], [lse]
                for other in range(n_groups - 1):
                    sl = pl.ds(other * ATTN_ROWS + base, BAND)
                    outs.append(og_ref[sl, :])
                    lses.append(lse_ref[sl, :])
                mx = functools.reduce(jnp.maximum, lses)
                es = [jnp.exp(x - mx) for x in lses]
                num = sum(e * x for e, x in zip(es, outs))
                gated = gate_ref[pl.ds(base, BAND), :] * (num / sum(es))
                y_ref[pl.ds(base, BAND), :] = gated.astype(BF16)

        for hh in range(N_HEADS):
            @pl.when(jnp.logical_and(i == n_tiles - 1, head == hh))
            def _(g=g, win=win, hh=hh, kc_ref=kc_ref, vc_ref=vc_ref):
                tails[g][pl.ds(hh, win, stride=KV_ROW), :] = kc_ref[pl.ds(ATTN_ROWS - win, win), :]
                tails[g][pl.ds(N_HEADS + hh, win, stride=KV_ROW), :] = (
                    vc_ref[pl.ds(ATTN_ROWS - win, win), :])


def _dilated_attention(z3, layer, depth, cache_bufs):
    bsz, seq, _ = z3.shape
    n_tiles = seq // ATTN_ROWS
    in_specs = []
    for g, (win, d) in enumerate(DIL_PAIRS):
        span = BAND * d
        qb = _seg_q(g) * N_HEADS
        kb = _seg_k(g) * N_HEADS
        vb = kb + N_HEADS
        per_tile = ATTN_ROWS // span

        def prev_map(col0, per_tile=per_tile):
            return lambda b, h, i: (b, jnp.maximum(i * per_tile - 1, 0), col0 + h)

        def cur_map(col0):
            return lambda b, h, i: (b, i, col0 + h)

        in_specs += [
            pl.BlockSpec((None, ATTN_ROWS, HEAD_DIM), cur_map(qb)),
            pl.BlockSpec((None, ATTN_ROWS, HEAD_DIM), cur_map(kb)),
            pl.BlockSpec((None, ATTN_ROWS, HEAD_DIM), cur_map(vb)),
            pl.BlockSpec((None, span, HEAD_DIM), prev_map(kb)),
            pl.BlockSpec((None, span, HEAD_DIM), prev_map(vb)),
        ]
    in_specs.append(pl.BlockSpec((None, ATTN_ROWS, HEAD_DIM),
                                 lambda b, h, i: (b, i, SEG_GATE_DIL * N_HEADS + h)))
    args = [z3] * len(in_specs)
    out_shape = [jax.ShapeDtypeStruct((bsz, seq, SEG), BF16)]
    out_specs = [pl.BlockSpec((None, ATTN_ROWS, HEAD_DIM), lambda b, h, i: (b, i, h))]
    aliases = {}
    for g, (win, _) in enumerate(DIL_PAIRS):
        out_shape.append(jax.ShapeDtypeStruct((depth, bsz, win * KV_ROW, HEAD_DIM), F32))
        out_specs.append(pl.BlockSpec((None, None, win * KV_ROW, HEAD_DIM),
                                      lambda b, h, i: (layer, b, 0, 0)))
        if cache_bufs is not None:
            in_specs.append(pl.BlockSpec(memory_space=pl.ANY))
            args.append(cache_bufs[g])
            aliases[len(args) - 1] = 1 + g
    return pl.pallas_call(
        functools.partial(_dilattn_kernel, n_tiles=n_tiles),
        out_shape=out_shape,
        grid=(bsz, N_HEADS, n_tiles),
        in_specs=in_specs,
        out_specs=out_specs,
        scratch_shapes=[
            pltpu.VMEM(((len(DIL_PAIRS) - 1) * ATTN_ROWS, HEAD_DIM), F32),
            pltpu.VMEM(((len(DIL_PAIRS) - 1) * ATTN_ROWS, HEAD_DIM), F32),
        ],
        input_output_aliases=aliases,
        compiler_params=pltpu.CompilerParams(
            dimension_semantics=("arbitrary", "arbitrary", "arbitrary"),
            vmem_limit_bytes=VMEM_LIMIT_BYTES),
        name="dilated_attention",
    )(*args)


def _mix_kernel(u_ref, up_ref, gp_ref, qm_ref, gm_ref, yd_ref, mkv_ref, pw_ref, ps_ref,
                wo_ref, x_ref, *rest, n_shift_in, n_shifts):
    shift_in, rest = rest[:n_shift_in], rest[n_shift_in:]
    o_ref, rest = rest[0], rest[1:]
    shift_out, (ue_ref, t_ref, y_ref) = rest[:n_shifts], rest[n_shifts:]
    i = pl.program_id(1)
    rows = u_ref.shape[0]

    pad, halo = POOL_PAD, POOL_HALO
    n = halo + rows

    @pl.when(i > 0)
    def _():
        ue_ref[pl.ds(pad, halo), :] = up_ref[...]

    @pl.when(i == 0)
    def _():
        ue_ref[pl.ds(pad, halo), :] = jnp.zeros((halo, SEG), F32)

    ue_ref[pl.ds(0, pad), :] = jnp.zeros((pad, SEG), F32)
    t_ref[0, pl.ds(0, pad), :] = jnp.zeros((pad, HEAD_DIM), F32)
    t_ref[1, pl.ds(0, pad), :] = jnp.zeros((pad, HEAD_DIM), F32)
    ue_ref[pl.ds(pad + halo, rows), :] = u_ref[...]
    pos = i * rows + lax.broadcasted_iota(jnp.int32, (rows, 1), 0)
    for gi, w in enumerate(POOL_WINDOWS):
        sl = slice(gi * HEAD_DIM, (gi + 1) * HEAD_DIM)
        cur = ue_ref[pl.ds(pad, n), sl] + ue_ref[pl.ds(pad - 1, n), sl]
        shift, buf = 2, 0
        while shift < w:
            t_ref[buf, pl.ds(pad, n), :] = cur
            cur = cur + t_ref[buf, pl.ds(pad - shift, n), :]
            shift, buf = 2 * shift, 1 - buf
        cnt = jnp.minimum(w, pos + 1).astype(F32)
        dlt = cur[halo:] / cnt - u_ref[:, sl]
        yp = jnp.dot(dlt.astype(BF16), pw_ref[gi], preferred_element_type=F32) * ps_ref[:, sl]
        y_ref[:, sl] = (gp_ref[:, sl] * yp).astype(BF16)

    y_ref[:, SEG:2 * SEG] = yd_ref[...]

    nt_dims = (((1,), (1,)), ((), ()))
    ones = jnp.ones((MEM_LEN, HEAD_DIM), BF16)
    for h in range(N_HEADS):
        sl = slice(h * HEAD_DIM, (h + 1) * HEAD_DIM)
        q = qm_ref[:, sl].astype(BF16)
        k = mkv_ref[:, sl].astype(BF16)
        v = mkv_ref[:, SEG + h * HEAD_DIM:SEG + (h + 1) * HEAD_DIM].astype(BF16)
        s = lax.dot_general(q, k, nt_dims, preferred_element_type=F32)
        m = jnp.max(s, axis=-1, keepdims=True)
        p = jnp.exp(s - m).astype(BF16)
        ov = jnp.dot(p, jnp.concatenate([v, ones], axis=1), preferred_element_type=F32)
        o = ov[:, :HEAD_DIM] / ov[:, HEAD_DIM:]
        y_ref[:, 2 * SEG + h * HEAD_DIM:2 * SEG + (h + 1) * HEAD_DIM] = (gm_ref[:, sl] * o).astype(BF16)

    _run_shifts(shift_in, shift_out)
    o_ref[...] = x_ref[...] + jnp.dot(y_ref[...], wo_ref[...], preferred_element_type=F32)


def _mix(z3, y_dil, mem_kv, pool_w, pool_scale, w_out, x3, layer, shift_jobs):
    bsz, seq, _ = z3.shape
    rows = MIX_ROWS
    n_tiles = seq // rows
    halo_per_tile = rows // POOL_HALO

    def seg_spec(seg):
        return pl.BlockSpec((None, rows, SEG), lambda b, i: (b, i, seg))

    s_specs, s_args, s_shapes, s_out_specs, aliases = _shift_jobs(
        shift_jobs, layer, lambda b, i: b * n_tiles + i, bsz * n_tiles, 11, 1)
    return pl.pallas_call(
        functools.partial(_mix_kernel, n_shift_in=len(s_args), n_shifts=len(s_shapes)),
        out_shape=[jax.ShapeDtypeStruct(x3.shape, F32)] + s_shapes,
        grid=(bsz, n_tiles),
        in_specs=[
            seg_spec(SEG_U),
            pl.BlockSpec((None, POOL_HALO, SEG),
                         lambda b, i: (b, jnp.maximum(i * halo_per_tile - 1, 0), SEG_U)),
            seg_spec(SEG_GATE_POOL),
            seg_spec(SEG_QMEM),
            seg_spec(SEG_GATE_MEM),
            pl.BlockSpec((None, rows, SEG), lambda b, i: (b, i, 0)),
            pl.BlockSpec((None, None, MEM_LEN, 2 * SEG), lambda b, i: (layer, b, 0, 0)),
            pl.BlockSpec((None, len(POOL_WINDOWS), HEAD_DIM, HEAD_DIM), lambda b, i: (layer, 0, 0, 0)),
            pl.BlockSpec((None, 1, SEG), lambda b, i: (layer, 0, 0)),
            pl.BlockSpec((MIX_WIDTH, D_MODEL), lambda b, i: (0, 0), pipeline_mode=pl.Buffered(1)),
            pl.BlockSpec((None, rows, D_MODEL), lambda b, i: (b, i, 0)),
        ] + s_specs,
        out_specs=[pl.BlockSpec((None, rows, D_MODEL), lambda b, i: (b, i, 0))] + s_out_specs,
        scratch_shapes=[
            pltpu.VMEM((POOL_PAD + POOL_HALO + rows, SEG), F32),
            pltpu.VMEM((2, POOL_PAD + POOL_HALO + rows, HEAD_DIM), F32),
            pltpu.VMEM((rows, MIX_WIDTH), BF16),
        ],
        input_output_aliases=aliases,
        compiler_params=pltpu.CompilerParams(
            dimension_semantics=("arbitrary", "arbitrary"),
            vmem_limit_bytes=VMEM_LIMIT_BYTES),
        name="mix",
    )(z3, z3, z3, z3, z3, y_dil, mem_kv, pool_w, pool_scale, w_out, x3, *s_args)


def _sample_tail_kernel(zs3_ref, c0_ref, c1_ref, c2_ref, cm_ref, zs_ref, st_ref, pw_ref, ps_ref,
                        wo_ref, x_ref, o_ref, ns_ref, wob_ref, ydm_ref, y_ref):
    b = pl.program_id(0)

    @pl.when(b == 0)
    def _():
        ydm_ref[...] = jnp.zeros(ydm_ref.shape, F32)

    _sample_attention_row(zs3_ref, (c0_ref, c1_ref, c2_ref), cm_ref, ydm_ref, b)

    @pl.when(b == pl.num_programs(0) - 1)
    def _():
        _sample_out_rows(zs_ref, st_ref, ydm_ref, pw_ref, ps_ref, wo_ref, x_ref, o_ref, ns_ref,
                         wob_ref, y_ref)


def _sample_attention_row(zs_ref, cache_refs, cm_ref, ydm_ref, b):
    outs, lses = [], []
    for g, c_ref in enumerate(cache_refs):
        q = zs_ref[pl.ds(_seg_q(g) * N_HEADS, N_HEADS), :]
        k_new = zs_ref[pl.ds(_seg_k(g) * N_HEADS, N_HEADS), :]
        v_new = zs_ref[pl.ds((_seg_k(g) + 1) * N_HEADS, N_HEADS), :]
        k = c_ref[:, 0]
        v = c_ref[:, 1]
        s = jnp.sum(k * q[None], axis=-1, keepdims=True)
        s_new = jnp.sum(k_new * q, axis=-1, keepdims=True)
        m = jnp.maximum(jnp.max(s, axis=0), s_new)
        p = jnp.exp(s - m[None])
        p_new = jnp.exp(s_new - m)
        l = jnp.sum(p, axis=0) + p_new
        outs.append((jnp.sum(p * v, axis=0) + p_new * v_new) / l)
        lses.append(m + jnp.log(l))
    mx = jnp.maximum(jnp.maximum(lses[0], lses[1]), lses[2])
    es = [jnp.exp(x - mx) for x in lses]
    o_dil = (es[0] * outs[0] + es[1] * outs[1] + es[2] * outs[2]) / (es[0] + es[1] + es[2])

    q = zs_ref[pl.ds(SEG_QMEM * N_HEADS, N_HEADS), :]
    k = cm_ref[:, 0]
    v = cm_ref[:, 1]
    s = jnp.sum(k * q[None], axis=-1, keepdims=True)
    m = jnp.max(s, axis=0)
    p = jnp.exp(s - m[None])
    o_mem = jnp.sum(p * v, axis=0) / jnp.sum(p, axis=0)

    rows = ydm_ref.shape[0]
    is_row_b = lax.broadcasted_iota(jnp.int32, (rows, HEAD_DIM), 0) == b
    for h in range(N_HEADS):
        for base, o in ((0, o_dil), (SEG, o_mem)):
            sl = pl.ds(base + h * HEAD_DIM, HEAD_DIM)
            row = jnp.broadcast_to(o[h:h + 1, :], (rows, HEAD_DIM))
            ydm_ref[:, sl] = jnp.where(is_row_b, row, ydm_ref[:, sl])


def _sample_out_rows(zs_ref, st_ref, ydm_ref, pw_ref, ps_ref, wo_ref, x_ref, o_ref, ns_ref,
                     wob_ref, y_ref):
    wob_ref[...] = wo_ref[...].astype(BF16)
    u = zs_ref[:, pl.ds(SEG_U * SEG, SEG)]
    for gi, w in enumerate(POOL_WINDOWS):
        sl = slice(gi * HEAD_DIM, (gi + 1) * HEAD_DIM)
        tot = u[:, sl]
        for back in range(1, w):
            tot = tot + st_ref[POOL_BUF - back, :, sl]
        cnt = float(min(w, PAST_LEN + 1))
        dlt = tot / cnt - u[:, sl]
        yp = jnp.dot(dlt.astype(BF16), pw_ref[gi], preferred_element_type=F32) * ps_ref[:, sl]
        y_ref[:, sl] = (zs_ref[:, pl.ds(SEG_GATE_POOL * SEG + gi * HEAD_DIM, HEAD_DIM)] * yp).astype(BF16)
    y_ref[:, SEG:2 * SEG] = (zs_ref[:, pl.ds(SEG_GATE_DIL * SEG, SEG)] * ydm_ref[:, pl.ds(0, SEG)]).astype(BF16)
    y_ref[:, 2 * SEG:] = (zs_ref[:, pl.ds(SEG_GATE_MEM * SEG, SEG)] * ydm_ref[:, pl.ds(SEG, SEG)]).astype(BF16)
    o_ref[...] = x_ref[...] + jnp.dot(y_ref[...], wob_ref[...], preferred_element_type=F32)
    for r in range(POOL_BUF - 1):
        ns_ref[r] = st_ref[r + 1]
    ns_ref[POOL_BUF - 1] = u


def _sample_tail(zs, caches7, cache_mem, state_t, pool_w, pool_scale, w_out, xs, layer):
    rows = zs.shape[0]
    bsz = cache_mem.shape[1]
    zs3 = zs.reshape(rows, N_SEG * N_HEADS, HEAD_DIM)
    full = lambda shape: pl.BlockSpec(shape, lambda b: tuple(0 for _ in shape))
    of_layer = lambda a: pl.BlockSpec((None,) + a.shape[1:], lambda b: (layer,) + (0,) * (a.ndim - 1))
    in_specs = [pl.BlockSpec((None, N_SEG * N_HEADS, HEAD_DIM), lambda b: (b, 0, 0))]
    for _ in caches7:
        in_specs.append(pl.BlockSpec((None, None, BAND, None, 2, N_HEADS, HEAD_DIM),
                                     lambda b: (layer, b, 0, 0, 0, 0, 0)))
    in_specs.append(pl.BlockSpec((None, None, MEM_LEN, 2, N_HEADS, HEAD_DIM),
                                 lambda b: (layer, b, 0, 0, 0, 0)))
    in_specs += [full(zs.shape), of_layer(state_t), of_layer(pool_w), of_layer(pool_scale),
                 of_layer(w_out), full(xs.shape)]
    return pl.pallas_call(
        _sample_tail_kernel,
        out_shape=[jax.ShapeDtypeStruct(xs.shape, F32),
                   jax.ShapeDtypeStruct(state_t.shape[1:], F32),
                   jax.ShapeDtypeStruct(w_out.shape[1:], BF16)],
        grid=(bsz,),
        in_specs=in_specs,
        out_specs=[full(xs.shape), full(state_t.shape[1:]), full(w_out.shape[1:])],
        scratch_shapes=[pltpu.VMEM((rows, 2 * SEG), F32), pltpu.VMEM((rows, MIX_WIDTH), BF16)],
        compiler_params=pltpu.CompilerParams(
            dimension_semantics=("arbitrary",), vmem_limit_bytes=VMEM_LIMIT_BYTES),
        name="sample_tail",
    )(zs3, *caches7, cache_mem, zs, state_t, pool_w, pool_scale, w_out, xs)


def _set_last_kernel(*refs):
    n = len(refs) // 3
    for new_ref, o_ref in zip(refs[:n], refs[2 * n:]):
        o_ref[...] = new_ref[...]


def _cache_set_last(bufs, new_rows):
    n = len(bufs)
    depth, bsz = bufs[0].shape[:2]

    def last_row_spec(win):
        return pl.BlockSpec((None, bsz, KV_ROW, HEAD_DIM), lambda l: (l, 0, win - 1, 0))

    return pl.pallas_call(
        _set_last_kernel,
        out_shape=[jax.ShapeDtypeStruct(buf.shape, buf.dtype) for buf in bufs],
        grid=(depth,),
        in_specs=[pl.BlockSpec((None, bsz, KV_ROW, HEAD_DIM), lambda l: (l, 0, 0, 0))] * n
        + [pl.BlockSpec(memory_space=pl.ANY)] * n,
        out_specs=[last_row_spec(buf.shape[2] // KV_ROW) for buf in bufs],
        input_output_aliases={n + g: g for g in range(n)},
        compiler_params=pltpu.CompilerParams(dimension_semantics=("arbitrary",)),
        name="cache_set_last",
    )(*new_rows, *bufs)


def _tile_heads(v):
    return jnp.tile(v, N_HEADS)


def kernel(x_prompt, x_sample, state_pool, cache_dil_w128, cache_dil_w512, cache_dil_w2048,
           cache_mem_kv, mem_prompt, norm_g, w_in, pool_w, pool_scale, dil_q_norm, dil_k_norm,
           mem_norm_g, w_mem_kv, mem_q_norm, mem_k_norm, w_out):
    depth = w_in.shape[0]
    bsz, seq, _ = x_prompt.shape
    dbsz = x_sample.shape[0]
    caches = (cache_dil_w128, cache_dil_w512, cache_dil_w2048)

    z_pairs = tuple(zip(Z_ORDER[0::2], Z_ORDER[1::2]))
    pool_w_b = pool_w.astype(BF16)

    ones = jnp.ones((depth, SEG), F32)
    segs = [ones] * N_SEG
    for g in range(len(DIL_PAIRS)):
        segs[_seg_q(g)] = _tile_heads(dil_q_norm[:, g]) * ATTN_SCALE
        segs[_seg_k(g)] = _tile_heads(dil_k_norm[:, g])
    segs[SEG_QMEM] = _tile_heads(mem_q_norm) * ATTN_SCALE
    ep_gain = jnp.concatenate(segs, axis=1)[:, None, :]
    mem_k_gain = _tile_heads(mem_k_norm)[:, None, :]
    seg_modes = [MODE_GATE if s in GATE_SEGS else MODE_PLAIN for s in range(N_SEG)]
    ep_mode = jnp.repeat(jnp.array(seg_modes, F32), SEG)[None, :]
    gain = norm_g[:, None, :]
    mem_norm = mem_norm_g[:, None, :]
    pscale = pool_scale[:, None, :]

    caches7 = [c.reshape(depth, dbsz, win // d, d, 2, N_HEADS, HEAD_DIM)
               for c, (win, d) in zip(caches, DIL_PAIRS)]
    caches_2d = [c.reshape(depth, dbsz, win * KV_ROW, HEAD_DIM)
                 for c, (win, _) in zip(caches, DIL_PAIRS)]
    state_t = jnp.pad(jnp.transpose(state_pool, (0, 2, 1, 3)),
                      ((0, 0), (0, 0), (0, SAMPLE_ROWS - dbsz), (0, 0)))

    xp = x_prompt.reshape(bsz * seq, D_MODEL)
    xs = jnp.pad(x_sample.reshape(dbsz, D_MODEL), ((0, SAMPLE_ROWS - dbsz), (0, 0)))
    mem2 = mem_prompt.reshape(bsz * MEM_LEN, D_MODEL)
    mem_kv = _memory_kv(mem2, mem_norm, w_mem_kv, mem_k_gain).reshape(depth, bsz, MEM_LEN, 2 * SEG)

    pool_p, pool_s, zs_rows = [], [], []
    prompt_caches = None
    sample_caches = [None] * len(DIL_PAIRS)
    for l in range(depth):
        zs, w_first, w_second = _project(xs, gain, (w_in, l, z_pairs), ep_gain, ep_mode,
                                         gain_layer=l, rows=SAMPLE_ROWS, name="proj_sample",
                                         emit_w=True)
        xs, new_state_t, w_out_b = _sample_tail(zs, caches7, cache_mem_kv, state_t, pool_w_b,
                                                pscale, w_out, xs, l)
        pool_s.append(new_state_t)
        zs_rows.append(zs)

        z, sample_caches[2] = _project(
            xp, gain, (w_first, w_second), ep_gain, ep_mode, gain_layer=l, rows=PROJ_ROWS,
            name="proj_prompt", shift=(l, [(caches_2d[2], sample_caches[2], SHIFT_ROWS)]))
        z3 = z.reshape(bsz, seq, IN_COLS)
        y_dil, *prompt_caches = _dilated_attention(z3, l, depth, prompt_caches)
        xp3, sample_caches[1], sample_caches[0] = _mix(
            z3, y_dil, mem_kv, pool_w_b, pscale, w_out_b, xp.reshape(bsz, seq, D_MODEL), l,
            [(caches_2d[1], sample_caches[1], SHIFT_ROWS), (caches_2d[0], sample_caches[0], SHIFT_ROWS)])
        xp = xp3.reshape(bsz * seq, D_MODEL)
        pool_p.append(z3[:, seq - POOL_BUF:, SEG_U * SEG:(SEG_U + 1) * SEG])

    zs_all = jnp.stack(zs_rows)[:, :dbsz].reshape(depth, dbsz, N_SEG, N_HEADS, HEAD_DIM)
    new_rows = [zs_all[:, :, _seg_k(g):_seg_k(g) + 2].reshape(depth, dbsz, KV_ROW, HEAD_DIM)
                for g in range(len(DIL_PAIRS))]
    new_caches = _cache_set_last(sample_caches, new_rows)
    new_caches = [c.reshape(depth, dbsz, win, 2, N_HEADS, HEAD_DIM)
                  for c, (win, _) in zip(new_caches, DIL_PAIRS)]

    y_prompt = xp.reshape(bsz, seq, D_MODEL)
    y_sample = xs[:dbsz].reshape(dbsz, 1, D_MODEL)
    cache_mem_prompt = mem_kv.reshape(depth, bsz, MEM_LEN, 2, N_HEADS, HEAD_DIM)
    state_pool_sample = jnp.transpose(jnp.stack(pool_s)[:, :, :dbsz], (0, 2, 1, 3))
    prompt_caches = [c.reshape(depth, bsz, win, 2, N_HEADS, HEAD_DIM)
                     for c, (win, _) in zip(prompt_caches, DIL_PAIRS)]
    return (y_prompt, y_sample, jnp.stack(pool_p), prompt_caches[0], prompt_caches[1],
            prompt_caches[2], cache_mem_prompt, state_pool_sample, new_caches[0], new_caches[1],
            new_caches[2])
```

```python
import functools

import jax
import jax.numpy as jnp
from jax import lax
from jax.experimental import pallas as pl
from jax.experimental.pallas import tpu as pltpu

F32 = jnp.float32
BF16 = jnp.bfloat16

D_MODEL = 2048
HEAD_DIM = 128
N_HEADS = 4
SEG = N_HEADS * HEAD_DIM
N_SEG = 14
IN_COLS = N_SEG * SEG
POOL_WINDOWS = (2, 4, 8, 16)
POOL_BUF = 15
POOL_HALO = 16
POOL_PAD = 8
DIL_PAIRS = ((128, 1), (512, 4), (2048, 16))
BAND = 128
MEM_LEN = 256
MIX_WIDTH = 3 * SEG
EPS = 1e-6
ATTN_SCALE = HEAD_DIM ** -0.5
PAST_LEN = 16384
assert PAST_LEN >= max(win for win, _ in DIL_PAIRS) and PAST_LEN + 1 >= max(POOL_WINDOWS)

Z_ORDER = (2, 0, 3, 4, 5, 1, 6, 7, 8, 11, 9, 10, 12, 13)
SEG_U, SEG_GATE_POOL, SEG_GATE_DIL, SEG_QMEM, SEG_GATE_MEM = 1, 5, 9, 12, 13
GATE_SEGS = (SEG_GATE_POOL, SEG_GATE_DIL, SEG_GATE_MEM)


def _seg_q(g):
    return 4 * g


def _seg_k(g):
    return 4 * g + 2

VMEM_LIMIT_BYTES = 56 * 1024 * 1024

PROJ_ROWS = 1024
PROJ_ROW_PARTS = 4
PROJ_MIN_PART = 256
ATTN_ROWS = 2048
MIX_ROWS = 512
SAMPLE_ROWS = 16
KV_ROW = 2 * N_HEADS
SHIFT_ROWS = 512
SHIFT_CHUNK = 64


MODE_PLAIN, MODE_GATE = 0.0, 1.0


def _shift_rows(c_ref, nxt_ref, o_ref):
    n = c_ref.shape[0]
    chunk = min(n, SHIFT_CHUNK * KV_ROW)
    for lo in range(0, n - KV_ROW, chunk):
        size = min(chunk, n - KV_ROW - lo)
        o_ref[pl.ds(lo, size), :] = c_ref[pl.ds(lo + KV_ROW, size), :]
    o_ref[pl.ds(n - KV_ROW, KV_ROW), :] = nxt_ref[...]


def _shift_jobs(jobs, layer, step_of, n_steps, n_in, n_out):
    in_specs, args, out_shapes, out_specs, aliases = [], [], [], [], {}
    first_step = 0
    for cache, buf, rows in jobs:
        specs, a, shape, spec, alias, n_shift = _shift_operands(
            cache, buf, layer, step_of, n_steps, first_step, rows)
        if alias is not None:
            aliases[n_in + len(args) + alias] = n_out + len(out_shapes)
        in_specs += specs
        args += a
        out_shapes.append(shape)
        out_specs.append(spec)
        first_step += n_shift
    return in_specs, args, out_shapes, out_specs, aliases


def _run_shifts(in_refs, out_refs):
    per = len(in_refs) // len(out_refs)
    for n, o_ref in enumerate(out_refs):
        _shift_rows(in_refs[n * per], in_refs[n * per + 1], o_ref)


def _shift_operands(cache, buf, layer, step_of, n_steps, first_step, rows):
    bsz = cache.shape[1]
    win = cache.shape[2] // KV_ROW
    rows = min(win, rows)
    per_b = win // rows
    n_shift = bsz * per_b
    assert first_step + n_shift <= n_steps

    def block_of(*idx):
        t = jnp.clip(step_of(*idx) - first_step, 0, n_shift - 1)
        return t // per_b, t % per_b

    def main_map(*idx):
        b, r = block_of(*idx)
        return (layer, b, r, 0)

    def next_map(*idx):
        b, r = block_of(*idx)
        return (layer, b, jnp.minimum((r + 1) * rows, win - 1), 0)

    in_specs = [pl.BlockSpec((None, None, rows * KV_ROW, HEAD_DIM), main_map),
                pl.BlockSpec((None, None, KV_ROW, HEAD_DIM), next_map)]
    args = [cache, cache]
    alias = None
    if buf is not None:
        in_specs.append(pl.BlockSpec(memory_space=pl.ANY))
        args.append(buf)
        alias = 2
    out_shape = jax.ShapeDtypeStruct(cache.shape, cache.dtype)
    out_spec = pl.BlockSpec((None, None, rows * KV_ROW, HEAD_DIM), main_map)
    return in_specs, args, out_shape, out_spec, alias, n_shift


def _lookup(j, table):
    out = table[0]
    for t in range(1, len(table)):
        out = jnp.where(j == t, table[t], out)
    return out


def _proj_kernel(x_ref, g_ref, wa_ref, wb_ref, eg_ref, em_ref, *rest, n_shift_in, n_shifts,
                 emit_w):
    rest = list(rest)
    shift_in, rest = rest[:n_shift_in], rest[n_shift_in:]
    o_ref = rest.pop(0)
    if emit_w:
        wa_out_ref, wb_out_ref = rest.pop(0), rest.pop(0)
    shift_out, rest = rest[:n_shifts], rest[n_shifts:]
    h_ref, = rest
    j = pl.program_id(1)
    ii = pl.program_id(2)

    @pl.when(j == 0)
    def _():
        x = x_ref[...]
        ms = jnp.mean(x * x, axis=-1, keepdims=True)
        h_ref[ii] = (x * lax.rsqrt(ms + EPS) * g_ref[...]).astype(BF16)

    if n_shifts:
        _run_shifts(shift_in, shift_out)
    h_tile = h_ref[ii]
    wa = wa_ref[...].astype(BF16)
    wb = wb_ref[...].astype(BF16)
    if emit_w:
        wa_out_ref[...] = wa
        wb_out_ref[...] = wb
    rows = h_tile.shape[0]
    part = max(rows // PROJ_ROW_PARTS, min(rows, PROJ_MIN_PART))
    for r0 in range(0, rows, part):
        rsl = pl.ds(r0, part)
        acc = jnp.dot(h_tile[r0:r0 + part], wa, preferred_element_type=F32)
        for h in range(N_HEADS):
            sl = slice(h * HEAD_DIM, (h + 1) * HEAD_DIM)
            a = acc[:, sl]
            o_ref[rsl, sl] = a * (lax.rsqrt(jnp.mean(a * a, axis=-1, keepdims=True) + EPS) * eg_ref[:, sl])
    for r0 in range(0, rows, part):
        rsl = pl.ds(r0, part)
        acc = jnp.dot(h_tile[r0:r0 + part], wb, preferred_element_type=F32)
        for h in range(N_HEADS):
            sl = slice(h * HEAD_DIM, (h + 1) * HEAD_DIM)
            osl = slice(SEG + h * HEAD_DIM, SEG + (h + 1) * HEAD_DIM)
            a = acc[:, sl]
            o_ref[rsl, osl] = a * jnp.where(em_ref[:, osl] == MODE_GATE, jax.nn.sigmoid(a), 1.0)


def _project(x, gain, weights, ep_gain, ep_mode, *, gain_layer, rows, name, emit_w=False,
             shift=None):
    m, k = x.shape
    cols = 2 * SEG
    n_tiles = m // rows
    group = 2 if n_tiles % 2 == 0 else 1
    if len(weights) == 3:
        w, layer, seg_pairs = weights
        n_j = len(seg_pairs)
        firsts = tuple(p[0] for p in seg_pairs)
        seconds = tuple(p[1] for p in seg_pairs)
        w_specs = [pl.BlockSpec((None, k, SEG), lambda g, j, t: (layer, 0, _lookup(j, firsts))),
                   pl.BlockSpec((None, k, SEG), lambda g, j, t: (layer, 0, _lookup(j, seconds)))]
        w_args = [w, w]
    else:
        n_j = weights[0].shape[1] // SEG
        w_specs = [pl.BlockSpec((k, SEG), lambda g, j, t: (0, j))] * 2
        w_args = list(weights)
    n = n_j * cols

    def x_map(g, j, t):
        return (jnp.where(j == 0, g * group + t, g * group + group - 1), 0)

    in_specs = [
        pl.BlockSpec((rows, k), x_map),
        pl.BlockSpec((None, 1, k), lambda g, j, t: (gain_layer, 0, 0)),
        *w_specs,
        pl.BlockSpec((None, 1, cols), lambda g, j, t: (gain_layer, 0, j)),
        pl.BlockSpec((1, cols), lambda g, j, t: (0, j)),
    ]
    args = [x, gain, *w_args, ep_gain, ep_mode]
    out_shape = [jax.ShapeDtypeStruct((m, n), F32)]
    out_specs = [pl.BlockSpec((rows, cols), lambda g, j, t: (g * group + t, j))]
    if emit_w:
        assert n_tiles == 1
        out_shape += [jax.ShapeDtypeStruct((k, n_j * SEG), BF16)] * 2
        out_specs += [pl.BlockSpec((k, SEG), lambda g, j, t: (0, j))] * 2
    aliases, s_args, s_shapes = {}, [], []
    if shift is not None:
        s_specs, s_args, s_shapes, s_out_specs, aliases = _shift_jobs(
            shift[1], shift[0], lambda g, j, t: (g * n_j + j) * group + t, n_tiles * n_j,
            len(args), len(out_shape))
        in_specs += s_specs
        args += s_args
        out_shape += s_shapes
        out_specs += s_out_specs
    return pl.pallas_call(
        functools.partial(_proj_kernel, n_shift_in=len(s_args), n_shifts=len(s_shapes),
                          emit_w=emit_w),
        out_shape=out_shape,
        grid=(n_tiles // group, n_j, group),
        in_specs=in_specs,
        out_specs=out_specs,
        scratch_shapes=[pltpu.VMEM((group, rows, k), BF16)],
        input_output_aliases=aliases,
        compiler_params=pltpu.CompilerParams(
            dimension_semantics=("arbitrary", "arbitrary", "arbitrary"),
            vmem_limit_bytes=VMEM_LIMIT_BYTES),
        name=name,
    )(*args)


def _mem_kv_kernel(x_ref, g_ref, w_ref, kg_ref, o_ref):
    x = x_ref[...]
    ms = jnp.mean(x * x, axis=-1, keepdims=True)
    h = (x * lax.rsqrt(ms + EPS) * g_ref[...]).astype(BF16)
    acc = jnp.dot(h, w_ref[...].astype(BF16), preferred_element_type=F32)
    for hd in range(N_HEADS):
        sl = slice(hd * HEAD_DIM, (hd + 1) * HEAD_DIM)
        a = acc[:, sl]
        o_ref[:, sl] = a * (lax.rsqrt(jnp.mean(a * a, axis=-1, keepdims=True) + EPS) * kg_ref[:, sl])
    o_ref[:, SEG:] = acc[:, SEG:]


def _memory_kv(mem2, mem_norm, w_mem_kv, k_gain):
    depth, k, n = w_mem_kv.shape
    rows = mem2.shape[0]
    return pl.pallas_call(
        _mem_kv_kernel,
        out_shape=jax.ShapeDtypeStruct((depth, rows, n), F32),
        grid=(depth,),
        in_specs=[
            pl.BlockSpec((rows, k), lambda l: (0, 0)),
            pl.BlockSpec((None, 1, k), lambda l: (l, 0, 0)),
            pl.BlockSpec((None, k, n), lambda l: (l, 0, 0)),
            pl.BlockSpec((None, 1, SEG), lambda l: (l, 0, 0)),
        ],
        out_specs=pl.BlockSpec((None, rows, n), lambda l: (l, 0, 0)),
        compiler_params=pltpu.CompilerParams(
            dimension_semantics=("arbitrary",), vmem_limit_bytes=VMEM_LIMIT_BYTES),
        name="memory_kv",
    )(mem2, mem_norm, w_mem_kv, k_gain)


def _dilattn_kernel(*refs, n_tiles):
    n_groups = len(DIL_PAIRS)
    ins = refs[:5 * n_groups]
    gate_ref = refs[5 * n_groups]
    y_ref, *tails, og_ref, lse_ref = refs[len(refs) - 3 - n_groups:]
    head = pl.program_id(1)
    i = pl.program_id(2)

    row = lax.broadcasted_iota(jnp.int32, (BAND, 2 * BAND), 0)
    col = lax.broadcasted_iota(jnp.int32, (BAND, 2 * BAND), 1)
    not_future = col <= row + BAND
    band = jnp.logical_and(col >= row, not_future)
    first_lo = jnp.maximum(row, (i == 0).astype(jnp.int32) * BAND)
    band_first = jnp.logical_and(col >= first_lo, not_future)
    ones = jnp.ones((2 * BAND, HEAD_DIM), BF16)
    nt_dims = (((1,), (1,)), ((), ()))

    for g in range(n_groups - 1, -1, -1):
        win, d = DIL_PAIRS[g]
        q_ref, kc_ref, vc_ref, kp_ref, vp_ref = ins[5 * g:5 * g + 5]
        span = BAND * d

        def rows_of(start, size, d=d):
            return pl.ds(start, size) if d == 1 else pl.ds(start, size, stride=d)

        for t in range(ATTN_ROWS // BAND):
            u, c = divmod(t, d)
            base = u * span + c
            q = q_ref[rows_of(base, BAND), :].astype(BF16)
            if u == 0:
                k = jnp.concatenate([kp_ref[rows_of(c, BAND), :], kc_ref[rows_of(c, BAND), :]], axis=0)
                v = jnp.concatenate([vp_ref[rows_of(c, BAND), :], vc_ref[rows_of(c, BAND), :]], axis=0)
            else:
                k = kc_ref[rows_of(base - span, 2 * BAND), :]
                v = vc_ref[rows_of(base - span, 2 * BAND), :]
            s = lax.dot_general(q, k.astype(BF16), nt_dims, preferred_element_type=F32)
            s = jnp.where(band_first if u == 0 else band, s, -jnp.inf)
            m = jnp.max(jnp.maximum(s[:, :BAND], s[:, BAND:]), axis=-1, keepdims=True)
            p = jnp.exp(s - m).astype(BF16)
            ov = jnp.dot(p, jnp.concatenate([v.astype(BF16), ones], axis=1), preferred_element_type=F32)
            l = ov[:, HEAD_DIM:]
            o = ov[:, :HEAD_DIM] / l
            lse = m + jnp.log(l)
            if g > 0:
                o_rows = rows_of((g - 1) * ATTN_ROWS + base, BAND)
                og_ref[o_rows, :] = o
                lse_ref[o_rows, :] = lse
            else:
                outs, lses = [o], [lse]
                for other in range(n_groups - 1):
                    sl = pl.ds(other * ATTN_ROWS + base, BAND)
                    outs.append(og_ref[sl, :])
                    lses.append(lse_ref[sl, :])
                mx = functools.reduce(jnp.maximum, lses)
                es = [jnp.exp(x - mx) for x in lses]
                num = sum(e * x for e, x in zip(es, outs))
                gated = gate_ref[pl.ds(base, BAND), :] * (num / sum(es))
                y_ref[pl.ds(base, BAND), :] = gated.astype(BF16)

    for hh in range(N_HEADS):
        @pl.when(jnp.logical_and(i == n_tiles - 1, head == hh))
        def _(hh=hh):
            for g, (win, _) in enumerate(DIL_PAIRS):
                kc_ref, vc_ref = ins[5 * g + 1], ins[5 * g + 2]
                tails[g][pl.ds(hh, win, stride=KV_ROW), :] = kc_ref[pl.ds(ATTN_ROWS - win, win), :]
                tails[g][pl.ds(N_HEADS + hh, win, stride=KV_ROW), :] = (
                    vc_ref[pl.ds(ATTN_ROWS - win, win), :])


def _dilated_attention(z3, layer, depth, cache_bufs):
    bsz, seq, _ = z3.shape
    n_tiles = seq // ATTN_ROWS
    in_specs = []
    for g, (win, d) in enumerate(DIL_PAIRS):
        span = BAND * d
        qb = _seg_q(g) * N_HEADS
        kb = _seg_k(g) * N_HEADS
        vb = kb + N_HEADS
        per_tile = ATTN_ROWS // span

        def prev_map(col0, per_tile=per_tile):
            return lambda b, h, i: (b, jnp.maximum(i * per_tile - 1, 0), col0 + h)

        def cur_map(col0):
            return lambda b, h, i: (b, i, col0 + h)

        in_specs += [
            pl.BlockSpec((None, ATTN_ROWS, HEAD_DIM), cur_map(qb)),
            pl.BlockSpec((None, ATTN_ROWS, HEAD_DIM), cur_map(kb)),
            pl.BlockSpec((None, ATTN_ROWS, HEAD_DIM), cur_map(vb)),
            pl.BlockSpec((None, span, HEAD_DIM), prev_map(kb)),
            pl.BlockSpec((None, span, HEAD_DIM), prev_map(vb)),
        ]
    in_specs.append(pl.BlockSpec((None, ATTN_ROWS, HEAD_DIM),
                                 lambda b, h, i: (b, i, SEG_GATE_DIL * N_HEADS + h)))
    args = [z3] * len(in_specs)
    out_shape = [jax.ShapeDtypeStruct((bsz, seq, SEG), BF16)]
    out_specs = [pl.BlockSpec((None, ATTN_ROWS, HEAD_DIM), lambda b, h, i: (b, i, h))]
    aliases = {}
    for g, (win, _) in enumerate(DIL_PAIRS):
        out_shape.append(jax.ShapeDtypeStruct((depth, bsz, win * KV_ROW, HEAD_DIM), F32))
        out_specs.append(pl.BlockSpec((None, None, win * KV_ROW, HEAD_DIM),
                                      lambda b, h, i: (layer, b, 0, 0)))
        if cache_bufs is not None:
            in_specs.append(pl.BlockSpec(memory_space=pl.ANY))
            args.append(cache_bufs[g])
            aliases[len(args) - 1] = 1 + g
    return pl.pallas_call(
        functools.partial(_dilattn_kernel, n_tiles=n_tiles),
        out_shape=out_shape,
        grid=(bsz, N_HEADS, n_tiles),
        in_specs=in_specs,
        out_specs=out_specs,
        scratch_shapes=[
            pltpu.VMEM(((len(DIL_PAIRS) - 1) * ATTN_ROWS, HEAD_DIM), F32),
            pltpu.VMEM(((len(DIL_PAIRS) - 1) * ATTN_ROWS, HEAD_DIM), F32),
        ],
        input_output_aliases=aliases,
        compiler_params=pltpu.CompilerParams(
            dimension_semantics=("arbitrary", "arbitrary", "arbitrary"),
            vmem_limit_bytes=VMEM_LIMIT_BYTES),
        name="dilated_attention",
    )(*args)


def _mix_kernel(u_ref, up_ref, gp_ref, qm_ref, gm_ref, yd_ref, mkv_ref, pw_ref, ps_ref,
                wo_ref, x_ref, *rest, n_shift_in, n_shifts):
    shift_in, rest = rest[:n_shift_in], rest[n_shift_in:]
    o_ref, rest = rest[0], rest[1:]
    shift_out, (ue_ref, t_ref, y_ref) = rest[:n_shifts], rest[n_shifts:]
    i = pl.program_id(1)
    rows = u_ref.shape[0]

    pad, halo = POOL_PAD, POOL_HALO
    n = halo + rows

    @pl.when(i > 0)
    def _():
        ue_ref[pl.ds(pad, halo), :] = up_ref[...]

    @pl.when(i == 0)
    def _():
        ue_ref[pl.ds(pad, halo), :] = jnp.zeros((halo, SEG), F32)

    ue_ref[pl.ds(0, pad), :] = jnp.zeros((pad, SEG), F32)
    t_ref[0, pl.ds(0, pad), :] = jnp.zeros((pad, HEAD_DIM), F32)
    t_ref[1, pl.ds(0, pad), :] = jnp.zeros((pad, HEAD_DIM), F32)
    ue_ref[pl.ds(pad + halo, rows), :] = u_ref[...]
    pos = i * rows + lax.broadcasted_iota(jnp.int32, (rows, 1), 0)
    for gi, w in enumerate(POOL_WINDOWS):
        sl = slice(gi * HEAD_DIM, (gi + 1) * HEAD_DIM)
        cur = ue_ref[pl.ds(pad, n), sl] + ue_ref[pl.ds(pad - 1, n), sl]
        shift, buf = 2, 0
        while shift < w:
            t_ref[buf, pl.ds(pad, n), :] = cur
            cur = cur + t_ref[buf, pl.ds(pad - shift, n), :]
            shift, buf = 2 * shift, 1 - buf
        cnt = jnp.minimum(w, pos + 1).astype(F32)
        dlt = cur[halo:] / cnt - u_ref[:, sl]
        yp = jnp.dot(dlt.astype(BF16), pw_ref[gi], preferred_element_type=F32) * ps_ref[:, sl]
        y_ref[:, sl] = (gp_ref[:, sl] * yp).astype(BF16)

    y_ref[:, SEG:2 * SEG] = yd_ref[...]

    nt_dims = (((1,), (1,)), ((), ()))
    ones = jnp.ones((MEM_LEN, HEAD_DIM), BF16)
    for h in range(N_HEADS):
        sl = slice(h * HEAD_DIM, (h + 1) * HEAD_DIM)
        q = qm_ref[:, sl].astype(BF16)
        k = mkv_ref[:, sl].astype(BF16)
        v = mkv_ref[:, SEG + h * HEAD_DIM:SEG + (h + 1) * HEAD_DIM].astype(BF16)
        s = lax.dot_general(q, k, nt_dims, preferred_element_type=F32)
        m = jnp.max(s, axis=-1, keepdims=True)
        p = jnp.exp(s - m).astype(BF16)
        ov = jnp.dot(p, jnp.concatenate([v, ones], axis=1), preferred_element_type=F32)
        o = ov[:, :HEAD_DIM] / ov[:, HEAD_DIM:]
        y_ref[:, 2 * SEG + h * HEAD_DIM:2 * SEG + (h + 1) * HEAD_DIM] = (gm_ref[:, sl] * o).astype(BF16)

    _run_shifts(shift_in, shift_out)
    o_ref[...] = x_ref[...] + jnp.dot(y_ref[...], wo_ref[...], preferred_element_type=F32)


def _mix(z3, y_dil, mem_kv, pool_w, pool_scale, w_out, x3, layer, shift_jobs):
    bsz, seq, _ = z3.shape
    rows = MIX_ROWS
    n_tiles = seq // rows
    halo_per_tile = rows // POOL_HALO

    def seg_spec(seg):
        return pl.BlockSpec((None, rows, SEG), lambda b, i: (b, i, seg))

    s_specs, s_args, s_shapes, s_out_specs, aliases = _shift_jobs(
        shift_jobs, layer, lambda b, i: b * n_tiles + i, bsz * n_tiles, 11, 1)
    return pl.pallas_call(
        functools.partial(_mix_kernel, n_shift_in=len(s_args), n_shifts=len(s_shapes)),
        out_shape=[jax.ShapeDtypeStruct(x3.shape, F32)] + s_shapes,
        grid=(bsz, n_tiles),
        in_specs=[
            seg_spec(SEG_U),
            pl.BlockSpec((None, POOL_HALO, SEG),
                         lambda b, i: (b, jnp.maximum(i * halo_per_tile - 1, 0), SEG_U)),
            seg_spec(SEG_GATE_POOL),
            seg_spec(SEG_QMEM),
            seg_spec(SEG_GATE_MEM),
            pl.BlockSpec((None, rows, SEG), lambda b, i: (b, i, 0)),
            pl.BlockSpec((None, None, MEM_LEN, 2 * SEG), lambda b, i: (layer, b, 0, 0)),
            pl.BlockSpec((None, len(POOL_WINDOWS), HEAD_DIM, HEAD_DIM), lambda b, i: (layer, 0, 0, 0)),
            pl.BlockSpec((None, 1, SEG), lambda b, i: (layer, 0, 0)),
            pl.BlockSpec((MIX_WIDTH, D_MODEL), lambda b, i: (0, 0), pipeline_mode=pl.Buffered(1)),
            pl.BlockSpec((None, rows, D_MODEL), lambda b, i: (b, i, 0)),
        ] + s_specs,
        out_specs=[pl.BlockSpec((None, rows, D_MODEL), lambda b, i: (b, i, 0))] + s_out_specs,
        scratch_shapes=[
            pltpu.VMEM((POOL_PAD + POOL_HALO + rows, SEG), F32),
            pltpu.VMEM((2, POOL_PAD + POOL_HALO + rows, HEAD_DIM), F32),
            pltpu.VMEM((rows, MIX_WIDTH), BF16),
        ],
        input_output_aliases=aliases,
        compiler_params=pltpu.CompilerParams(
            dimension_semantics=("arbitrary", "arbitrary"),
            vmem_limit_bytes=VMEM_LIMIT_BYTES),
        name="mix",
    )(z3, z3, z3, z3, z3, y_dil, mem_kv, pool_w, pool_scale, w_out, x3, *s_args)


def _sample_tail_kernel(zs3_ref, c0_ref, c1_ref, c2_ref, cm_ref, zs_ref, st_ref, pw_ref, ps_ref,
                        wo_ref, x_ref, o_ref, ns_ref, wob_ref, ydm_ref, y_ref):
    b = pl.program_id(0)

    @pl.when(b == 0)
    def _():
        ydm_ref[...] = jnp.zeros(ydm_ref.shape, F32)

    _sample_attention_row(zs3_ref, (c0_ref, c1_ref, c2_ref), cm_ref, ydm_ref, b)

    @pl.when(b == pl.num_programs(0) - 1)
    def _():
        _sample_out_rows(zs_ref, st_ref, ydm_ref, pw_ref, ps_ref, wo_ref, x_ref, o_ref, ns_ref,
                         wob_ref, y_ref)


def _sample_attention_row(zs_ref, cache_refs, cm_ref, ydm_ref, b):
    outs, lses = [], []
    for g, c_ref in enumerate(cache_refs):
        q = zs_ref[pl.ds(_seg_q(g) * N_HEADS, N_HEADS), :]
        k_new = zs_ref[pl.ds(_seg_k(g) * N_HEADS, N_HEADS), :]
        v_new = zs_ref[pl.ds((_seg_k(g) + 1) * N_HEADS, N_HEADS), :]
        k = c_ref[:, 0]
        v = c_ref[:, 1]
        s = jnp.sum(k * q[None], axis=-1, keepdims=True)
        s_new = jnp.sum(k_new * q, axis=-1, keepdims=True)
        m = jnp.maximum(jnp.max(s, axis=0), s_new)
        p = jnp.exp(s - m[None])
        p_new = jnp.exp(s_new - m)
        l = jnp.sum(p, axis=0) + p_new
        outs.append((jnp.sum(p * v, axis=0) + p_new * v_new) / l)
        lses.append(m + jnp.log(l))
    mx = jnp.maximum(jnp.maximum(lses[0], lses[1]), lses[2])
    es = [jnp.exp(x - mx) for x in lses]
    o_dil = (es[0] * outs[0] + es[1] * outs[1] + es[2] * outs[2]) / (es[0] + es[1] + es[2])

    q = zs_ref[pl.ds(SEG_QMEM * N_HEADS, N_HEADS), :]
    k = cm_ref[:, 0]
    v = cm_ref[:, 1]
    s = jnp.sum(k * q[None], axis=-1, keepdims=True)
    m = jnp.max(s, axis=0)
    p = jnp.exp(s - m[None])
    o_mem = jnp.sum(p * v, axis=0) / jnp.sum(p, axis=0)

    rows = ydm_ref.shape[0]
    is_row_b = lax.broadcasted_iota(jnp.int32, (rows, HEAD_DIM), 0) == b
    for h in range(N_HEADS):
        for base, o in ((0, o_dil), (SEG, o_mem)):
            sl = pl.ds(base + h * HEAD_DIM, HEAD_DIM)
            row = jnp.broadcast_to(o[h:h + 1, :], (rows, HEAD_DIM))
            ydm_ref[:, sl] = jnp.where(is_row_b, row, ydm_ref[:, sl])


def _sample_out_rows(zs_ref, st_ref, ydm_ref, pw_ref, ps_ref, wo_ref, x_ref, o_ref, ns_ref,
                     wob_ref, y_ref):
    wob_ref[...] = wo_ref[...].astype(BF16)
    u = zs_ref[:, pl.ds(SEG_U * SEG, SEG)]
    for gi, w in enumerate(POOL_WINDOWS):
        sl = slice(gi * HEAD_DIM, (gi + 1) * HEAD_DIM)
        tot = u[:, sl]
        for back in range(1, w):
            tot = tot + st_ref[POOL_BUF - back, :, sl]
        cnt = float(min(w, PAST_LEN + 1))
        dlt = tot / cnt - u[:, sl]
        yp = jnp.dot(dlt.astype(BF16), pw_ref[gi], preferred_element_type=F32) * ps_ref[:, sl]
        y_ref[:, sl] = (zs_ref[:, pl.ds(SEG_GATE_POOL * SEG + gi * HEAD_DIM, HEAD_DIM)] * yp).astype(BF16)
    y_ref[:, SEG:2 * SEG] = (zs_ref[:, pl.ds(SEG_GATE_DIL * SEG, SEG)] * ydm_ref[:, pl.ds(0, SEG)]).astype(BF16)
    y_ref[:, 2 * SEG:] = (zs_ref[:, pl.ds(SEG_GATE_MEM * SEG, SEG)] * ydm_ref[:, pl.ds(SEG, SEG)]).astype(BF16)
    o_ref[...] = x_ref[...] + jnp.dot(y_ref[...], wob_ref[...], preferred_element_type=F32)
    for r in range(POOL_BUF - 1):
        ns_ref[r] = st_ref[r + 1]
    ns_ref[POOL_BUF - 1] = u


def _sample_tail(zs, caches7, cache_mem, state_t, pool_w, pool_scale, w_out, xs, layer):
    rows = zs.shape[0]
    bsz = cache_mem.shape[1]
    zs3 = zs.reshape(rows, N_SEG * N_HEADS, HEAD_DIM)
    full = lambda shape: pl.BlockSpec(shape, lambda b: tuple(0 for _ in shape))
    of_layer = lambda a: pl.BlockSpec((None,) + a.shape[1:], lambda b: (layer,) + (0,) * (a.ndim - 1))
    in_specs = [pl.BlockSpec((None, N_SEG * N_HEADS, HEAD_DIM), lambda b: (b, 0, 0))]
    for _ in caches7:
        in_specs.append(pl.BlockSpec((None, None, BAND, None, 2, N_HEADS, HEAD_DIM),
                                     lambda b: (layer, b, 0, 0, 0, 0, 0)))
    in_specs.append(pl.BlockSpec((None, None, MEM_LEN, 2, N_HEADS, HEAD_DIM),
                                 lambda b: (layer, b, 0, 0, 0, 0)))
    in_specs += [full(zs.shape), of_layer(state_t), of_layer(pool_w), of_layer(pool_scale),
                 of_layer(w_out), full(xs.shape)]
    return pl.pallas_call(
        _sample_tail_kernel,
        out_shape=[jax.ShapeDtypeStruct(xs.shape, F32),
                   jax.ShapeDtypeStruct(state_t.shape[1:], F32),
                   jax.ShapeDtypeStruct(w_out.shape[1:], BF16)],
        grid=(bsz,),
        in_specs=in_specs,
        out_specs=[full(xs.shape), full(state_t.shape[1:]), full(w_out.shape[1:])],
        scratch_shapes=[pltpu.VMEM((rows, 2 * SEG), F32), pltpu.VMEM((rows, MIX_WIDTH), BF16)],
        compiler_params=pltpu.CompilerParams(
            dimension_semantics=("arbitrary",), vmem_limit_bytes=VMEM_LIMIT_BYTES),
        name="sample_tail",
    )(zs3, *caches7, cache_mem, zs, state_t, pool_w, pool_scale, w_out, xs)


def _set_last_kernel(*refs):
    n = len(refs) // 3
    for new_ref, o_ref in zip(refs[:n], refs[2 * n:]):
        o_ref[...] = new_ref[...]


def _cache_set_last(bufs, new_rows):
    n = len(bufs)
    depth, bsz = bufs[0].shape[:2]

    def last_row_spec(win):
        return pl.BlockSpec((None, bsz, KV_ROW, HEAD_DIM), lambda l: (l, 0, win - 1, 0))

    return pl.pallas_call(
        _set_last_kernel,
        out_shape=[jax.ShapeDtypeStruct(buf.shape, buf.dtype) for buf in bufs],
        grid=(depth,),
        in_specs=[pl.BlockSpec((None, bsz, KV_ROW, HEAD_DIM), lambda l: (l, 0, 0, 0))] * n
        + [pl.BlockSpec(memory_space=pl.ANY)] * n,
        out_specs=[last_row_spec(buf.shape[2] // KV_ROW) for buf in bufs],
        input_output_aliases={n + g: g for g in range(n)},
        compiler_params=pltpu.CompilerParams(dimension_semantics=("arbitrary",)),
        name="cache_set_last",
    )(*new_rows, *bufs)


def _tile_heads(v):
    return jnp.tile(v, N_HEADS)


def kernel(x_prompt, x_sample, state_pool, cache_dil_w128, cache_dil_w512, cache_dil_w2048,
           cache_mem_kv, mem_prompt, norm_g, w_in, pool_w, pool_scale, dil_q_norm, dil_k_norm,
           mem_norm_g, w_mem_kv, mem_q_norm, mem_k_norm, w_out):
    depth = w_in.shape[0]
    bsz, seq, _ = x_prompt.shape
    dbsz = x_sample.shape[0]
    caches = (cache_dil_w128, cache_dil_w512, cache_dil_w2048)

    z_pairs = tuple(zip(Z_ORDER[0::2], Z_ORDER[1::2]))
    pool_w_b = pool_w.astype(BF16)

    ones = jnp.ones((depth, SEG), F32)
    segs = [ones] * N_SEG
    for g in range(len(DIL_PAIRS)):
        segs[_seg_q(g)] = _tile_heads(dil_q_norm[:, g]) * ATTN_SCALE
        segs[_seg_k(g)] = _tile_heads(dil_k_norm[:, g])
    segs[SEG_QMEM] = _tile_heads(mem_q_norm) * ATTN_SCALE
    ep_gain = jnp.concatenate(segs, axis=1)[:, None, :]
    mem_k_gain = _tile_heads(mem_k_norm)[:, None, :]
    seg_modes = [MODE_GATE if s in GATE_SEGS else MODE_PLAIN for s in range(N_SEG)]
    ep_mode = jnp.repeat(jnp.array(seg_modes, F32), SEG)[None, :]
    gain = norm_g[:, None, :]
    mem_norm = mem_norm_g[:, None, :]
    pscale = pool_scale[:, None, :]

    caches7 = [c.reshape(depth, dbsz, win // d, d, 2, N_HEADS, HEAD_DIM)
               for c, (win, d) in zip(caches, DIL_PAIRS)]
    caches_2d = [c.reshape(depth, dbsz, win * KV_ROW, HEAD_DIM)
                 for c, (win, _) in zip(caches, DIL_PAIRS)]
    state_t = jnp.pad(jnp.transpose(state_pool, (0, 2, 1, 3)),
                      ((0, 0), (0, 0), (0, SAMPLE_ROWS - dbsz), (0, 0)))

    xp = x_prompt.reshape(bsz * seq, D_MODEL)
    xs = jnp.pad(x_sample.reshape(dbsz, D_MODEL), ((0, SAMPLE_ROWS - dbsz), (0, 0)))
    mem2 = mem_prompt.reshape(bsz * MEM_LEN, D_MODEL)
    mem_kv = _memory_kv(mem2, mem_norm, w_mem_kv, mem_k_gain).reshape(depth, bsz, MEM_LEN, 2 * SEG)

    pool_p, pool_s, zs_rows = [], [], []
    prompt_caches = None
    sample_caches = [None] * len(DIL_PAIRS)
    for l in range(depth):
        zs, w_first, w_second = _project(xs, gain, (w_in, l, z_pairs), ep_gain, ep_mode,
                                         gain_layer=l, rows=SAMPLE_ROWS, name="proj_sample",
                                         emit_w=True)
        xs, new_state_t, w_out_b = _sample_tail(zs, caches7, cache_mem_kv, state_t, pool_w_b,
                                                pscale, w_out, xs, l)
        pool_s.append(new_state_t)
        zs_rows.append(zs)

        z, sample_caches[2] = _project(
            xp, gain, (w_first, w_second), ep_gain, ep_mode, gain_layer=l, rows=PROJ_ROWS,
            name="proj_prompt", shift=(l, [(caches_2d[2], sample_caches[2], SHIFT_ROWS)]))
        z3 = z.reshape(bsz, seq, IN_COLS)
        y_dil, *prompt_caches = _dilated_attention(z3, l, depth, prompt_caches)
        xp3, sample_caches[1], sample_caches[0] = _mix(
            z3, y_dil, mem_kv, pool_w_b, pscale, w_out_b, xp.reshape(bsz, seq, D_MODEL), l,
            [(caches_2d[1], sample_caches[1], SHIFT_ROWS), (caches_2d[0], sample_caches[0], SHIFT_ROWS)])
        xp = xp3.reshape(bsz * seq, D_MODEL)
        pool_p.append(z3[:, seq - POOL_BUF:, SEG_U * SEG:(SEG_U + 1) * SEG])

    zs_all = jnp.stack(zs_rows)[:, :dbsz].reshape(depth, dbsz, N_SEG, N_HEADS, HEAD_DIM)
    new_rows = [zs_all[:, :, _seg_k(g):_seg_k(g) + 2].reshape(depth, dbsz, KV_ROW, HEAD_DIM)
                for g in range(len(DIL_PAIRS))]
    new_caches = _cache_set_last(sample_caches, new_rows)
    new_caches = [c.reshape(depth, dbsz, win, 2, N_HEADS, HEAD_DIM)
                  for c, (win, _) in zip(new_caches, DIL_PAIRS)]

    y_prompt = xp.reshape(bsz, seq, D_MODEL)
    y_sample = xs[:dbsz].reshape(dbsz, 1, D_MODEL)
    cache_mem_prompt = mem_kv.reshape(depth, bsz, MEM_LEN, 2, N_HEADS, HEAD_DIM)
    state_pool_sample = jnp.transpose(jnp.stack(pool_s)[:, :, :dbsz], (0, 2, 1, 3))
    prompt_caches = [c.reshape(depth, bsz, win, 2, N_HEADS, HEAD_DIM)
                     for c, (win, _) in zip(prompt_caches, DIL_PAIRS)]
    return (y_prompt, y_sample, jnp.stack(pool_p), prompt_caches[0], prompt_caches[1],
            prompt_caches[2], cache_mem_prompt, state_pool_sample, new_caches[0], new_caches[1],
            new_caches[2])
```

```python
import functools

import jax
import jax.numpy as jnp
from jax import lax
from jax.experimental import pallas as pl
from jax.experimental.pallas import tpu as pltpu

F32 = jnp.float32
BF16 = jnp.bfloat16

D_MODEL = 2048
HEAD_DIM = 128
N_HEADS = 4
SEG = N_HEADS * HEAD_DIM
N_SEG = 14
IN_COLS = N_SEG * SEG
POOL_WINDOWS = (2, 4, 8, 16)
POOL_BUF = 15
POOL_HALO = 16
POOL_PAD = 8
DIL_PAIRS = ((128, 1), (512, 4), (2048, 16))
BAND = 128
MEM_LEN = 256
MIX_WIDTH = 3 * SEG
EPS = 1e-6
ATTN_SCALE = HEAD_DIM ** -0.5
PAST_LEN = 16384
assert PAST_LEN >= max(win for win, _ in DIL_PAIRS) and PAST_LEN + 1 >= max(POOL_WINDOWS)

Z_ORDER = (2, 0, 3, 4, 5, 1, 6, 7, 8, 11, 9, 10, 12, 13)
SEG_U, SEG_GATE_POOL, SEG_GATE_DIL, SEG_QMEM, SEG_GATE_MEM = 1, 5, 9, 12, 13
GATE_SEGS = (SEG_GATE_POOL, SEG_GATE_DIL, SEG_GATE_MEM)


def _seg_q(g):
    return 4 * g


def _seg_k(g):
    return 4 * g + 2

VMEM_LIMIT_BYTES = 56 * 1024 * 1024

PROJ_ROWS = 1024
PROJ_ROW_PARTS = 4
PROJ_MIN_PART = 256
ATTN_ROWS = 2048
MIX_ROWS = 512
SAMPLE_ROWS = 16
KV_ROW = 2 * N_HEADS
SHIFT_ROWS = 512
MIX_SHIFT_ROWS = (64, 256)
SHIFT_CHUNK = 64


MODE_PLAIN, MODE_GATE = 0.0, 1.0


def _shift_rows(c_ref, nxt_ref, o_ref):
    n = c_ref.shape[0]
    chunk = min(n, SHIFT_CHUNK * KV_ROW)
    for lo in range(0, n - KV_ROW, chunk):
        size = min(chunk, n - KV_ROW - lo)
        o_ref[pl.ds(lo, size), :] = c_ref[pl.ds(lo + KV_ROW, size), :]
    o_ref[pl.ds(n - KV_ROW, KV_ROW), :] = nxt_ref[...]


def _shift_jobs(jobs, layer, step_of, n_steps, n_in, n_out):
    in_specs, args, out_shapes, out_specs, aliases = [], [], [], [], {}
    for cache, buf, rows in jobs:
        specs, a, shape, spec, alias = _shift_operands(cache, buf, layer, step_of, n_steps, rows)
        if alias is not None:
            aliases[n_in + len(args) + alias] = n_out + len(out_shapes)
        in_specs += specs
        args += a
        out_shapes.append(shape)
        out_specs.append(spec)
    return in_specs, args, out_shapes, out_specs, aliases


def _run_shifts(in_refs, out_refs):
    per = len(in_refs) // len(out_refs)
    for n, o_ref in enumerate(out_refs):
        _shift_rows(in_refs[n * per], in_refs[n * per + 1], o_ref)


def _shift_operands(cache, buf, layer, step_of, n_steps, rows):
    bsz = cache.shape[1]
    win = cache.shape[2] // KV_ROW
    rows = min(win, rows)
    per_b = win // rows
    n_shift = bsz * per_b
    assert n_shift <= n_steps

    def block_of(*idx):
        t = jnp.minimum(step_of(*idx), n_shift - 1)
        return t // per_b, t % per_b

    def main_map(*idx):
        b, r = block_of(*idx)
        return (layer, b, r, 0)

    def next_map(*idx):
        b, r = block_of(*idx)
        return (layer, b, jnp.minimum((r + 1) * rows, win - 1), 0)

    in_specs = [pl.BlockSpec((None, None, rows * KV_ROW, HEAD_DIM), main_map),
                pl.BlockSpec((None, None, KV_ROW, HEAD_DIM), next_map)]
    args = [cache, cache]
    alias = None
    if buf is not None:
        in_specs.append(pl.BlockSpec(memory_space=pl.ANY))
        args.append(buf)
        alias = 2
    out_shape = jax.ShapeDtypeStruct(cache.shape, cache.dtype)
    out_spec = pl.BlockSpec((None, None, rows * KV_ROW, HEAD_DIM), main_map)
    return in_specs, args, out_shape, out_spec, alias


def _lookup(j, table):
    out = table[0]
    for t in range(1, len(table)):
        out = jnp.where(j == t, table[t], out)
    return out


def _proj_kernel(x_ref, g_ref, wa_ref, wb_ref, eg_ref, em_ref, *rest, n_shift_in, n_shifts,
                 emit_w):
    rest = list(rest)
    shift_in, rest = rest[:n_shift_in], rest[n_shift_in:]
    o_ref = rest.pop(0)
    if emit_w:
        wa_out_ref, wb_out_ref = rest.pop(0), rest.pop(0)
    shift_out, rest = rest[:n_shifts], rest[n_shifts:]
    h_ref, = rest
    j = pl.program_id(1)
    ii = pl.program_id(2)

    @pl.when(j == 0)
    def _():
        x = x_ref[...]
        ms = jnp.mean(x * x, axis=-1, keepdims=True)
        h_ref[ii] = (x * lax.rsqrt(ms + EPS) * g_ref[...]).astype(BF16)

    if n_shifts:
        _run_shifts(shift_in, shift_out)
    h_tile = h_ref[ii]
    wa = wa_ref[...].astype(BF16)
    wb = wb_ref[...].astype(BF16)
    if emit_w:
        wa_out_ref[...] = wa
        wb_out_ref[...] = wb
    rows = h_tile.shape[0]
    part = max(rows // PROJ_ROW_PARTS, min(rows, PROJ_MIN_PART))
    for r0 in range(0, rows, part):
        rsl = pl.ds(r0, part)
        acc = jnp.dot(h_tile[r0:r0 + part], wa, preferred_element_type=F32)
        for h in range(N_HEADS):
            sl = slice(h * HEAD_DIM, (h + 1) * HEAD_DIM)
            a = acc[:, sl]
            o_ref[rsl, sl] = a * (lax.rsqrt(jnp.mean(a * a, axis=-1, keepdims=True) + EPS) * eg_ref[:, sl])
    for r0 in range(0, rows, part):
        rsl = pl.ds(r0, part)
        acc = jnp.dot(h_tile[r0:r0 + part], wb, preferred_element_type=F32)
        for h in range(N_HEADS):
            sl = slice(h * HEAD_DIM, (h + 1) * HEAD_DIM)
            osl = slice(SEG + h * HEAD_DIM, SEG + (h + 1) * HEAD_DIM)
            a = acc[:, sl]
            o_ref[rsl, osl] = a * jnp.where(em_ref[:, osl] == MODE_GATE, jax.nn.sigmoid(a), 1.0)


def _project(x, gain, weights, ep_gain, ep_mode, *, gain_layer, rows, name, emit_w=False,
             shift=None):
    m, k = x.shape
    cols = 2 * SEG
    n_tiles = m // rows
    group = 2 if n_tiles % 2 == 0 else 1
    if len(weights) == 3:
        w, layer, seg_pairs = weights
        n_j = len(seg_pairs)
        firsts = tuple(p[0] for p in seg_pairs)
        seconds = tuple(p[1] for p in seg_pairs)
        w_specs = [pl.BlockSpec((None, k, SEG), lambda g, j, t: (layer, 0, _lookup(j, firsts))),
                   pl.BlockSpec((None, k, SEG), lambda g, j, t: (layer, 0, _lookup(j, seconds)))]
        w_args = [w, w]
    else:
        n_j = weights[0].shape[1] // SEG
        w_specs = [pl.BlockSpec((k, SEG), lambda g, j, t: (0, j))] * 2
        w_args = list(weights)
    n = n_j * cols

    def x_map(g, j, t):
        return (jnp.where(j == 0, g * group + t, g * group + group - 1), 0)

    in_specs = [
        pl.BlockSpec((rows, k), x_map),
        pl.BlockSpec((None, 1, k), lambda g, j, t: (gain_layer, 0, 0)),
        *w_specs,
        pl.BlockSpec((None, 1, cols), lambda g, j, t: (gain_layer, 0, j)),
        pl.BlockSpec((1, cols), lambda g, j, t: (0, j)),
    ]
    args = [x, gain, *w_args, ep_gain, ep_mode]
    out_shape = [jax.ShapeDtypeStruct((m, n), F32)]
    out_specs = [pl.BlockSpec((rows, cols), lambda g, j, t: (g * group + t, j))]
    if emit_w:
        assert n_tiles == 1
        out_shape += [jax.ShapeDtypeStruct((k, n_j * SEG), BF16)] * 2
        out_specs += [pl.BlockSpec((k, SEG), lambda g, j, t: (0, j))] * 2
    aliases, s_args, s_shapes = {}, [], []
    if shift is not None:
        s_specs, s_args, s_shapes, s_out_specs, aliases = _shift_jobs(
            shift[1], shift[0], lambda g, j, t: (g * n_j + j) * group + t, n_tiles * n_j,
            len(args), len(out_shape))
        in_specs += s_specs
        args += s_args
        out_shape += s_shapes
        out_specs += s_out_specs
    return pl.pallas_call(
        functools.partial(_proj_kernel, n_shift_in=len(s_args), n_shifts=len(s_shapes),
                          emit_w=emit_w),
        out_shape=out_shape,
        grid=(n_tiles // group, n_j, group),
        in_specs=in_specs,
        out_specs=out_specs,
        scratch_shapes=[pltpu.VMEM((group, rows, k), BF16)],
        input_output_aliases=aliases,
        compiler_params=pltpu.CompilerParams(
            dimension_semantics=("arbitrary", "arbitrary", "arbitrary"),
            vmem_limit_bytes=VMEM_LIMIT_BYTES),
        name=name,
    )(*args)


def _mem_kv_kernel(x_ref, g_ref, w_ref, kg_ref, o_ref):
    x = x_ref[...]
    ms = jnp.mean(x * x, axis=-1, keepdims=True)
    h = (x * lax.rsqrt(ms + EPS) * g_ref[...]).astype(BF16)
    acc = jnp.dot(h, w_ref[...].astype(BF16), preferred_element_type=F32)
    for hd in range(N_HEADS):
        sl = slice(hd * HEAD_DIM, (hd + 1) * HEAD_DIM)
        a = acc[:, sl]
        o_ref[:, sl] = a * (lax.rsqrt(jnp.mean(a * a, axis=-1, keepdims=True) + EPS) * kg_ref[:, sl])
    o_ref[:, SEG:] = acc[:, SEG:]


def _memory_kv(mem2, mem_norm, w_mem_kv, k_gain):
    depth, k, n = w_mem_kv.shape
    rows = mem2.shape[0]
    return pl.pallas_call(
        _mem_kv_kernel,
        out_shape=jax.ShapeDtypeStruct((depth, rows, n), F32),
        grid=(depth,),
        in_specs=[
            pl.BlockSpec((rows, k), lambda l: (0, 0)),
            pl.BlockSpec((None, 1, k), lambda l: (l, 0, 0)),
            pl.BlockSpec((None, k, n), lambda l: (l, 0, 0)),
            pl.BlockSpec((None, 1, SEG), lambda l: (l, 0, 0)),
        ],
        out_specs=pl.BlockSpec((None, rows, n), lambda l: (l, 0, 0)),
        compiler_params=pltpu.CompilerParams(
            dimension_semantics=("arbitrary",), vmem_limit_bytes=VMEM_LIMIT_BYTES),
        name="memory_kv",
    )(mem2, mem_norm, w_mem_kv, k_gain)


def _dilattn_kernel(*refs, n_tiles):
    n_groups = len(DIL_PAIRS)
    ins = refs[:5 * n_groups]
    gate_ref = refs[5 * n_groups]
    y_ref, *tails, og_ref, lse_ref = refs[len(refs) - 3 - n_groups:]
    head = pl.program_id(1)
    i = pl.program_id(2)

    row = lax.broadcasted_iota(jnp.int32, (BAND, 2 * BAND), 0)
    col = lax.broadcasted_iota(jnp.int32, (BAND, 2 * BAND), 1)
    not_future = col <= row + BAND
    band = jnp.logical_and(col >= row, not_future)
    first_lo = jnp.maximum(row, (i == 0).astype(jnp.int32) * BAND)
    band_first = jnp.logical_and(col >= first_lo, not_future)
    ones = jnp.ones((2 * BAND, HEAD_DIM), BF16)
    nt_dims = (((1,), (1,)), ((), ()))

    for g in range(n_groups - 1, -1, -1):
        win, d = DIL_PAIRS[g]
        q_ref, kc_ref, vc_ref, kp_ref, vp_ref = ins[5 * g:5 * g + 5]
        span = BAND * d

        def rows_of(start, size, d=d):
            return pl.ds(start, size) if d == 1 else pl.ds(start, size, stride=d)

        for t in range(ATTN_ROWS // BAND):
            u, c = divmod(t, d)
            base = u * span + c
            q = q_ref[rows_of(base, BAND), :].astype(BF16)
            if u == 0:
                k = jnp.concatenate([kp_ref[rows_of(c, BAND), :], kc_ref[rows_of(c, BAND), :]], axis=0)
                v = jnp.concatenate([vp_ref[rows_of(c, BAND), :], vc_ref[rows_of(c, BAND), :]], axis=0)
            else:
                k = kc_ref[rows_of(base - span, 2 * BAND), :]
                v = vc_ref[rows_of(base - span, 2 * BAND), :]
            s = lax.dot_general(q, k.astype(BF16), nt_dims, preferred_element_type=F32)
            s = jnp.where(band_first if u == 0 else band, s, -jnp.inf)
            m = jnp.max(jnp.maximum(s[:, :BAND], s[:, BAND:]), axis=-1, keepdims=True)
            p = jnp.exp(s - m).astype(BF16)
            ov = jnp.dot(p, jnp.concatenate([v.astype(BF16), ones], axis=1), preferred_element_type=F32)
            l = ov[:, HEAD_DIM:]
            o = ov[:, :HEAD_DIM] / l
            lse = m + jnp.log(l)
            if g > 0:
                o_rows = rows_of((g - 1) * ATTN_ROWS + base, BAND)
                og_ref[o_rows, :] = o
                lse_ref[o_rows, :] = lse
            else:
                outs, lses = [o], [lse]
                for other in range(n_groups - 1):
                    sl = pl.ds(other * ATTN_ROWS + base, BAND)
                    outs.append(og_ref[sl, :])
                    lses.append(lse_ref[sl, :])
                mx = functools.reduce(jnp.maximum, lses)
                es = [jnp.exp(x - mx) for x in lses]
                num = sum(e * x for e, x in zip(es, outs))
                gated = gate_ref[pl.ds(base, BAND), :] * (num / sum(es))
                y_ref[pl.ds(base, BAND), :] = gated.astype(BF16)

    for hh in range(N_HEADS):
        @pl.when(jnp.logical_and(i == n_tiles - 1, head == hh))
        def _(hh=hh):
            for g, (win, _) in enumerate(DIL_PAIRS):
                kc_ref, vc_ref = ins[5 * g + 1], ins[5 * g + 2]
                tails[g][pl.ds(hh, win, stride=KV_ROW), :] = kc_ref[pl.ds(ATTN_ROWS - win, win), :]
                tails[g][pl.ds(N_HEADS + hh, win, stride=KV_ROW), :] = (
                    vc_ref[pl.ds(ATTN_ROWS - win, win), :])


def _dilated_attention(z3, layer, depth, cache_bufs):
    bsz, seq, _ = z3.shape
    n_tiles = seq // ATTN_ROWS
    in_specs = []
    for g, (win, d) in enumerate(DIL_PAIRS):
        span = BAND * d
        qb = _seg_q(g) * N_HEADS
        kb = _seg_k(g) * N_HEADS
        vb = kb + N_HEADS
        per_tile = ATTN_ROWS // span

        def prev_map(col0, per_tile=per_tile):
            return lambda b, h, i: (b, jnp.maximum(i * per_tile - 1, 0), col0 + h)

        def cur_map(col0):
            return lambda b, h, i: (b, i, col0 + h)

        in_specs += [
            pl.BlockSpec((None, ATTN_ROWS, HEAD_DIM), cur_map(qb)),
            pl.BlockSpec((None, ATTN_ROWS, HEAD_DIM), cur_map(kb)),
            pl.BlockSpec((None, ATTN_ROWS, HEAD_DIM), cur_map(vb)),
            pl.BlockSpec((None, span, HEAD_DIM), prev_map(kb)),
            pl.BlockSpec((None, span, HEAD_DIM), prev_map(vb)),
        ]
    in_specs.append(pl.BlockSpec((None, ATTN_ROWS, HEAD_DIM),
                                 lambda b, h, i: (b, i, SEG_GATE_DIL * N_HEADS + h)))
    args = [z3] * len(in_specs)
    out_shape = [jax.ShapeDtypeStruct((bsz, seq, SEG), BF16)]
    out_specs = [pl.BlockSpec((None, ATTN_ROWS, HEAD_DIM), lambda b, h, i: (b, i, h))]
    aliases = {}
    for g, (win, _) in enumerate(DIL_PAIRS):
        out_shape.append(jax.ShapeDtypeStruct((depth, bsz, win * KV_ROW, HEAD_DIM), F32))
        out_specs.append(pl.BlockSpec((None, None, win * KV_ROW, HEAD_DIM),
                                      lambda b, h, i: (layer, b, 0, 0)))
        if cache_bufs is not None:
            in_specs.append(pl.BlockSpec(memory_space=pl.ANY))
            args.append(cache_bufs[g])
            aliases[len(args) - 1] = 1 + g
    return pl.pallas_call(
        functools.partial(_dilattn_kernel, n_tiles=n_tiles),
        out_shape=out_shape,
        grid=(bsz, N_HEADS, n_tiles),
        in_specs=in_specs,
        out_specs=out_specs,
        scratch_shapes=[
            pltpu.VMEM(((len(DIL_PAIRS) - 1) * ATTN_ROWS, HEAD_DIM), F32),
            pltpu.VMEM(((len(DIL_PAIRS) - 1) * ATTN_ROWS, HEAD_DIM), F32),
        ],
        input_output_aliases=aliases,
        compiler_params=pltpu.CompilerParams(
            dimension_semantics=("arbitrary", "arbitrary", "arbitrary"),
            vmem_limit_bytes=VMEM_LIMIT_BYTES),
        name="dilated_attention",
    )(*args)


def _mix_kernel(u_ref, up_ref, gp_ref, qm_ref, gm_ref, yd_ref, mkv_ref, pw_ref, ps_ref,
                wo_ref, x_ref, *rest, n_shift_in, n_shifts):
    shift_in, rest = rest[:n_shift_in], rest[n_shift_in:]
    o_ref, rest = rest[0], rest[1:]
    shift_out, (ue_ref, t_ref, y_ref) = rest[:n_shifts], rest[n_shifts:]
    i = pl.program_id(1)
    rows = u_ref.shape[0]

    pad, halo = POOL_PAD, POOL_HALO
    n = halo + rows

    @pl.when(i > 0)
    def _():
        ue_ref[pl.ds(pad, halo), :] = up_ref[...]

    @pl.when(i == 0)
    def _():
        ue_ref[pl.ds(pad, halo), :] = jnp.zeros((halo, SEG), F32)

    ue_ref[pl.ds(0, pad), :] = jnp.zeros((pad, SEG), F32)
    t_ref[0, pl.ds(0, pad), :] = jnp.zeros((pad, HEAD_DIM), F32)
    t_ref[1, pl.ds(0, pad), :] = jnp.zeros((pad, HEAD_DIM), F32)
    ue_ref[pl.ds(pad + halo, rows), :] = u_ref[...]
    pos = i * rows + lax.broadcasted_iota(jnp.int32, (rows, 1), 0)
    for gi, w in enumerate(POOL_WINDOWS):
        sl = slice(gi * HEAD_DIM, (gi + 1) * HEAD_DIM)
        cur = ue_ref[pl.ds(pad, n), sl] + ue_ref[pl.ds(pad - 1, n), sl]
        shift, buf = 2, 0
        while shift < w:
            t_ref[buf, pl.ds(pad, n), :] = cur
            cur = cur + t_ref[buf, pl.ds(pad - shift, n), :]
            shift, buf = 2 * shift, 1 - buf
        cnt = jnp.minimum(w, pos + 1).astype(F32)
        dlt = cur[halo:] / cnt - u_ref[:, sl]
        yp = jnp.dot(dlt.astype(BF16), pw_ref[gi], preferred_element_type=F32) * ps_ref[:, sl]
        y_ref[:, sl] = (gp_ref[:, sl] * yp).astype(BF16)

    y_ref[:, SEG:2 * SEG] = yd_ref[...]

    nt_dims = (((1,), (1,)), ((), ()))
    ones = jnp.ones((MEM_LEN, HEAD_DIM), BF16)
    for h in range(N_HEADS):
        sl = slice(h * HEAD_DIM, (h + 1) * HEAD_DIM)
        q = qm_ref[:, sl].astype(BF16)
        k = mkv_ref[:, sl].astype(BF16)
        v = mkv_ref[:, SEG + h * HEAD_DIM:SEG + (h + 1) * HEAD_DIM].astype(BF16)
        s = lax.dot_general(q, k, nt_dims, preferred_element_type=F32)
        m = jnp.max(s, axis=-1, keepdims=True)
        p = jnp.exp(s - m).astype(BF16)
        ov = jnp.dot(p, jnp.concatenate([v, ones], axis=1), preferred_element_type=F32)
        o = ov[:, :HEAD_DIM] / ov[:, HEAD_DIM:]
        y_ref[:, 2 * SEG + h * HEAD_DIM:2 * SEG + (h + 1) * HEAD_DIM] = (gm_ref[:, sl] * o).astype(BF16)

    _run_shifts(shift_in, shift_out)
    o_ref[...] = x_ref[...] + jnp.dot(y_ref[...], wo_ref[...], preferred_element_type=F32)


def _mix(z3, y_dil, mem_kv, pool_w, pool_scale, w_out, x3, layer, shift_jobs):
    bsz, seq, _ = z3.shape
    rows = MIX_ROWS
    n_tiles = seq // rows
    halo_per_tile = rows // POOL_HALO

    def seg_spec(seg):
        return pl.BlockSpec((None, rows, SEG), lambda b, i: (b, i, seg))

    s_specs, s_args, s_shapes, s_out_specs, aliases = _shift_jobs(
        shift_jobs, layer, lambda b, i: b * n_tiles + i, bsz * n_tiles, 11, 1)
    return pl.pallas_call(
        functools.partial(_mix_kernel, n_shift_in=len(s_args), n_shifts=len(s_shapes)),
        out_shape=[jax.ShapeDtypeStruct(x3.shape, F32)] + s_shapes,
        grid=(bsz, n_tiles),
        in_specs=[
            seg_spec(SEG_U),
            pl.BlockSpec((None, POOL_HALO, SEG),
                         lambda b, i: (b, jnp.maximum(i * halo_per_tile - 1, 0), SEG_U)),
            seg_spec(SEG_GATE_POOL),
            seg_spec(SEG_QMEM),
            seg_spec(SEG_GATE_MEM),
            pl.BlockSpec((None, rows, SEG), lambda b, i: (b, i, 0)),
            pl.BlockSpec((None, None, MEM_LEN, 2 * SEG), lambda b, i: (layer, b, 0, 0)),
            pl.BlockSpec((None, len(POOL_WINDOWS), HEAD_DIM, HEAD_DIM), lambda b, i: (layer, 0, 0, 0)),
            pl.BlockSpec((None, 1, SEG), lambda b, i: (layer, 0, 0)),
            pl.BlockSpec((MIX_WIDTH, D_MODEL), lambda b, i: (0, 0), pipeline_mode=pl.Buffered(1)),
            pl.BlockSpec((None, rows, D_MODEL), lambda b, i: (b, i, 0)),
        ] + s_specs,
        out_specs=[pl.BlockSpec((None, rows, D_MODEL), lambda b, i: (b, i, 0))] + s_out_specs,
        scratch_shapes=[
            pltpu.VMEM((POOL_PAD + POOL_HALO + rows, SEG), F32),
            pltpu.VMEM((2, POOL_PAD + POOL_HALO + rows, HEAD_DIM), F32),
            pltpu.VMEM((rows, MIX_WIDTH), BF16),
        ],
        input_output_aliases=aliases,
        compiler_params=pltpu.CompilerParams(
            dimension_semantics=("arbitrary", "arbitrary"),
            vmem_limit_bytes=VMEM_LIMIT_BYTES),
        name="mix",
    )(z3, z3, z3, z3, z3, y_dil, mem_kv, pool_w, pool_scale, w_out, x3, *s_args)


def _sample_tail_kernel(zs3_ref, c0_ref, c1_ref, c2_ref, cm_ref, zs_ref, st_ref, pw_ref, ps_ref,
                        wo_ref, x_ref, o_ref, ns_ref, wob_ref, ydm_ref, y_ref):
    b = pl.program_id(0)

    @pl.when(b == 0)
    def _():
        ydm_ref[...] = jnp.zeros(ydm_ref.shape, F32)

    _sample_attention_row(zs3_ref, (c0_ref, c1_ref, c2_ref), cm_ref, ydm_ref, b)

    @pl.when(b == pl.num_programs(0) - 1)
    def _():
        _sample_out_rows(zs_ref, st_ref, ydm_ref, pw_ref, ps_ref, wo_ref, x_ref, o_ref, ns_ref,
                         wob_ref, y_ref)


def _sample_attention_row(zs_ref, cache_refs, cm_ref, ydm_ref, b):
    outs, lses = [], []
    for g, c_ref in enumerate(cache_refs):
        q = zs_ref[pl.ds(_seg_q(g) * N_HEADS, N_HEADS), :]
        k_new = zs_ref[pl.ds(_seg_k(g) * N_HEADS, N_HEADS), :]
        v_new = zs_ref[pl.ds((_seg_k(g) + 1) * N_HEADS, N_HEADS), :]
        k = c_ref[:, 0]
        v = c_ref[:, 1]
        s = jnp.sum(k * q[None], axis=-1, keepdims=True)
        s_new = jnp.sum(k_new * q, axis=-1, keepdims=True)
        m = jnp.maximum(jnp.max(s, axis=0), s_new)
        p = jnp.exp(s - m[None])
        p_new = jnp.exp(s_new - m)
        l = jnp.sum(p, axis=0) + p_new
        outs.append((jnp.sum(p * v, axis=0) + p_new * v_new) / l)
        lses.append(m + jnp.log(l))
    mx = jnp.maximum(jnp.maximum(lses[0], lses[1]), lses[2])
    es = [jnp.exp(x - mx) for x in lses]
    o_dil = (es[0] * outs[0] + es[1] * outs[1] + es[2] * outs[2]) / (es[0] + es[1] + es[2])

    q = zs_ref[pl.ds(SEG_QMEM * N_HEADS, N_HEADS), :]
    k = cm_ref[:, 0]
    v = cm_ref[:, 1]
    s = jnp.sum(k * q[None], axis=-1, keepdims=True)
    m = jnp.max(s, axis=0)
    p = jnp.exp(s - m[None])
    o_mem = jnp.sum(p * v, axis=0) / jnp.sum(p, axis=0)

    rows = ydm_ref.shape[0]
    is_row_b = lax.broadcasted_iota(jnp.int32, (rows, HEAD_DIM), 0) == b
    for h in range(N_HEADS):
        for base, o in ((0, o_dil), (SEG, o_mem)):
            sl = pl.ds(base + h * HEAD_DIM, HEAD_DIM)
            row = jnp.broadcast_to(o[h:h + 1, :], (rows, HEAD_DIM))
            ydm_ref[:, sl] = jnp.where(is_row_b, row, ydm_ref[:, sl])


def _sample_out_rows(zs_ref, st_ref, ydm_ref, pw_ref, ps_ref, wo_ref, x_ref, o_ref, ns_ref,
                     wob_ref, y_ref):
    wob_ref[...] = wo_ref[...].astype(BF16)
    u = zs_ref[:, pl.ds(SEG_U * SEG, SEG)]
    for gi, w in enumerate(POOL_WINDOWS):
        sl = slice(gi * HEAD_DIM, (gi + 1) * HEAD_DIM)
        tot = u[:, sl]
        for back in range(1, w):
            tot = tot + st_ref[POOL_BUF - back, :, sl]
        cnt = float(min(w, PAST_LEN + 1))
        dlt = tot / cnt - u[:, sl]
        yp = jnp.dot(dlt.astype(BF16), pw_ref[gi], preferred_element_type=F32) * ps_ref[:, sl]
        y_ref[:, sl] = (zs_ref[:, pl.ds(SEG_GATE_POOL * SEG + gi * HEAD_DIM, HEAD_DIM)] * yp).astype(BF16)
    y_ref[:, SEG:2 * SEG] = (zs_ref[:, pl.ds(SEG_GATE_DIL * SEG, SEG)] * ydm_ref[:, pl.ds(0, SEG)]).astype(BF16)
    y_ref[:, 2 * SEG:] = (zs_ref[:, pl.ds(SEG_GATE_MEM * SEG, SEG)] * ydm_ref[:, pl.ds(SEG, SEG)]).astype(BF16)
    o_ref[...] = x_ref[...] + jnp.dot(y_ref[...], wob_ref[...], preferred_element_type=F32)
    for r in range(POOL_BUF - 1):
        ns_ref[r] = st_ref[r + 1]
    ns_ref[POOL_BUF - 1] = u


def _sample_tail(zs, caches7, cache_mem, state_t, pool_w, pool_scale, w_out, xs, layer):
    rows = zs.shape[0]
    bsz = cache_mem.shape[1]
    zs3 = zs.reshape(rows, N_SEG * N_HEADS, HEAD_DIM)
    full = lambda shape: pl.BlockSpec(shape, lambda b: tuple(0 for _ in shape))
    of_layer = lambda a: pl.BlockSpec((None,) + a.shape[1:], lambda b: (layer,) + (0,) * (a.ndim - 1))
    in_specs = [pl.BlockSpec((None, N_SEG * N_HEADS, HEAD_DIM), lambda b: (b, 0, 0))]
    for _ in caches7:
        in_specs.append(pl.BlockSpec((None, None, BAND, None, 2, N_HEADS, HEAD_DIM),
                                     lambda b: (layer, b, 0, 0, 0, 0, 0)))
    in_specs.append(pl.BlockSpec((None, None, MEM_LEN, 2, N_HEADS, HEAD_DIM),
                                 lambda b: (layer, b, 0, 0, 0, 0)))
    in_specs += [full(zs.shape), of_layer(state_t), of_layer(pool_w), of_layer(pool_scale),
                 of_layer(w_out), full(xs.shape)]
    return pl.pallas_call(
        _sample_tail_kernel,
        out_shape=[jax.ShapeDtypeStruct(xs.shape, F32),
                   jax.ShapeDtypeStruct(state_t.shape[1:], F32),
                   jax.ShapeDtypeStruct(w_out.shape[1:], BF16)],
        grid=(bsz,),
        in_specs=in_specs,
        out_specs=[full(xs.shape), full(state_t.shape[1:]), full(w_out.shape[1:])],
        scratch_shapes=[pltpu.VMEM((rows, 2 * SEG), F32), pltpu.VMEM((rows, MIX_WIDTH), BF16)],
        compiler_params=pltpu.CompilerParams(
            dimension_semantics=("arbitrary",), vmem_limit_bytes=VMEM_LIMIT_BYTES),
        name="sample_tail",
    )(zs3, *caches7, cache_mem, zs, state_t, pool_w, pool_scale, w_out, xs)


def _set_last_kernel(*refs):
    n = len(refs) // 3
    for new_ref, o_ref in zip(refs[:n], refs[2 * n:]):
        o_ref[...] = new_ref[...]


def _cache_set_last(bufs, new_rows):
    n = len(bufs)
    depth, bsz = bufs[0].shape[:2]

    def last_row_spec(win):
        return pl.BlockSpec((None, bsz, KV_ROW, HEAD_DIM), lambda l: (l, 0, win - 1, 0))

    return pl.pallas_call(
        _set_last_kernel,
        out_shape=[jax.ShapeDtypeStruct(buf.shape, buf.dtype) for buf in bufs],
        grid=(depth,),
        in_specs=[pl.BlockSpec((None, bsz, KV_ROW, HEAD_DIM), lambda l: (l, 0, 0, 0))] * n
        + [pl.BlockSpec(memory_space=pl.ANY)] * n,
        out_specs=[last_row_spec(buf.shape[2] // KV_ROW) for buf in bufs],
        input_output_aliases={n + g: g for g in range(n)},
        compiler_params=pltpu.CompilerParams(dimension_semantics=("arbitrary",)),
        name="cache_set_last",
    )(*new_rows, *bufs)


def _tile_heads(v):
    return jnp.tile(v, N_HEADS)


def kernel(x_prompt, x_sample, state_pool, cache_dil_w128, cache_dil_w512, cache_dil_w2048,
           cache_mem_kv, mem_prompt, norm_g, w_in, pool_w, pool_scale, dil_q_norm, dil_k_norm,
           mem_norm_g, w_mem_kv, mem_q_norm, mem_k_norm, w_out):
    depth = w_in.shape[0]
    bsz, seq, _ = x_prompt.shape
    dbsz = x_sample.shape[0]
    caches = (cache_dil_w128, cache_dil_w512, cache_dil_w2048)

    z_pairs = tuple(zip(Z_ORDER[0::2], Z_ORDER[1::2]))
    pool_w_b = pool_w.astype(BF16)

    ones = jnp.ones((depth, SEG), F32)
    segs = [ones] * N_SEG
    for g in range(len(DIL_PAIRS)):
        segs[_seg_q(g)] = _tile_heads(dil_q_norm[:, g]) * ATTN_SCALE
        segs[_seg_k(g)] = _tile_heads(dil_k_norm[:, g])
    segs[SEG_QMEM] = _tile_heads(mem_q_norm) * ATTN_SCALE
    ep_gain = jnp.concatenate(segs, axis=1)[:, None, :]
    mem_k_gain = _tile_heads(mem_k_norm)[:, None, :]
    seg_modes = [MODE_GATE if s in GATE_SEGS else MODE_PLAIN for s in range(N_SEG)]
    ep_mode = jnp.repeat(jnp.array(seg_modes, F32), SEG)[None, :]
    gain = norm_g[:, None, :]
    mem_norm = mem_norm_g[:, None, :]
    pscale = pool_scale[:, None, :]

    caches7 = [c.reshape(depth, dbsz, win // d, d, 2, N_HEADS, HEAD_DIM)
               for c, (win, d) in zip(caches, DIL_PAIRS)]
    caches_2d = [c.reshape(depth, dbsz, win * KV_ROW, HEAD_DIM)
                 for c, (win, _) in zip(caches, DIL_PAIRS)]
    state_t = jnp.pad(jnp.transpose(state_pool, (0, 2, 1, 3)),
                      ((0, 0), (0, 0), (0, SAMPLE_ROWS - dbsz), (0, 0)))

    xp = x_prompt.reshape(bsz * seq, D_MODEL)
    xs = jnp.pad(x_sample.reshape(dbsz, D_MODEL), ((0, SAMPLE_ROWS - dbsz), (0, 0)))
    mem2 = mem_prompt.reshape(bsz * MEM_LEN, D_MODEL)
    mem_kv = _memory_kv(mem2, mem_norm, w_mem_kv, mem_k_gain).reshape(depth, bsz, MEM_LEN, 2 * SEG)

    pool_p, pool_s, zs_rows = [], [], []
    prompt_caches = None
    sample_caches = [None] * len(DIL_PAIRS)
    for l in range(depth):
        zs, w_first, w_second = _project(xs, gain, (w_in, l, z_pairs), ep_gain, ep_mode,
                                         gain_layer=l, rows=SAMPLE_ROWS, name="proj_sample",
                                         emit_w=True)
        xs, new_state_t, w_out_b = _sample_tail(zs, caches7, cache_mem_kv, state_t, pool_w_b,
                                                pscale, w_out, xs, l)
        pool_s.append(new_state_t)
        zs_rows.append(zs)

        z, sample_caches[2] = _project(
            xp, gain, (w_first, w_second), ep_gain, ep_mode, gain_layer=l, rows=PROJ_ROWS,
            name="proj_prompt", shift=(l, [(caches_2d[2], sample_caches[2], SHIFT_ROWS)]))
        z3 = z.reshape(bsz, seq, IN_COLS)
        y_dil, *prompt_caches = _dilated_attention(z3, l, depth, prompt_caches)
        xp3, sample_caches[1], sample_caches[0] = _mix(
            z3, y_dil, mem_kv, pool_w_b, pscale, w_out_b, xp.reshape(bsz, seq, D_MODEL), l,
            [(caches_2d[1], sample_caches[1], MIX_SHIFT_ROWS[1]),
             (caches_2d[0], sample_caches[0], MIX_SHIFT_ROWS[0])])
        xp = xp3.reshape(bsz * seq, D_MODEL)
        pool_p.append(z3[:, seq - POOL_BUF:, SEG_U * SEG:(SEG_U + 1) * SEG])

    zs_all = jnp.stack(zs_rows)[:, :dbsz].reshape(depth, dbsz, N_SEG, N_HEADS, HEAD_DIM)
    new_rows = [zs_all[:, :, _seg_k(g):_seg_k(g) + 2].reshape(depth, dbsz, KV_ROW, HEAD_DIM)
                for g in range(len(DIL_PAIRS))]
    new_caches = _cache_set_last(sample_caches, new_rows)
    new_caches = [c.reshape(depth, dbsz, win, 2, N_HEADS, HEAD_DIM)
                  for c, (win, _) in zip(new_caches, DIL_PAIRS)]

    y_prompt = xp.reshape(bsz, seq, D_MODEL)
    y_sample = xs[:dbsz].reshape(dbsz, 1, D_MODEL)
    cache_mem_prompt = mem_kv.reshape(depth, bsz, MEM_LEN, 2, N_HEADS, HEAD_DIM)
    state_pool_sample = jnp.transpose(jnp.stack(pool_s)[:, :, :dbsz], (0, 2, 1, 3))
    prompt_caches = [c.reshape(depth, bsz, win, 2, N_HEADS, HEAD_DIM)
                     for c, (win, _) in zip(prompt_caches, DIL_PAIRS)]
    return (y_prompt, y_sample, jnp.stack(pool_p), prompt_caches[0], prompt_caches[1],
            prompt_caches[2], cache_mem_prompt, state_pool_sample, new_caches[0], new_caches[1],
            new_caches[2])
```

```python
import functools

import jax
import jax.numpy as jnp
from jax import lax
from jax.experimental import pallas as pl
from jax.experimental.pallas import tpu as pltpu

F32 = jnp.float32
BF16 = jnp.bfloat16

D_MODEL = 2048
HEAD_DIM = 128
N_HEADS = 4
SEG = N_HEADS * HEAD_DIM
N_SEG = 14
IN_COLS = N_SEG * SEG
POOL_WINDOWS = (2, 4, 8, 16)
POOL_BUF = 15
POOL_HALO = 16
POOL_PAD = 8
DIL_PAIRS = ((128, 1), (512, 4), (2048, 16))
BAND = 128
MEM_LEN = 256
MIX_WIDTH = 3 * SEG
EPS = 1e-6
ATTN_SCALE = HEAD_DIM ** -0.5
LOG2_E = 1.4426950408889634
PAST_LEN = 16384
assert PAST_LEN >= max(win for win, _ in DIL_PAIRS) and PAST_LEN + 1 >= max(POOL_WINDOWS)

Z_ORDER = (2, 0, 3, 4, 5, 1, 6, 7, 8, 11, 9, 10, 12, 13)
SEG_U, SEG_GATE_POOL, SEG_GATE_DIL, SEG_QMEM, SEG_GATE_MEM = 1, 5, 9, 12, 13
GATE_SEGS = (SEG_GATE_POOL, SEG_GATE_DIL, SEG_GATE_MEM)


def _seg_q(g):
    return 4 * g


def _seg_k(g):
    return 4 * g + 2

VMEM_LIMIT_BYTES = 56 * 1024 * 1024

PROJ_ROWS = 1024
PROJ_ROW_PARTS = 4
PROJ_MIN_PART = 256
ATTN_ROWS = 2048
MIX_ROWS = 512
SAMPLE_ROWS = 16
KV_ROW = 2 * N_HEADS
SHIFT_ROWS = 512
MIX_SHIFT_ROWS = (64, 256)
SHIFT_CHUNK = 64


MODE_PLAIN, MODE_GATE = 0.0, 1.0


def _shift_rows(c_ref, nxt_ref, o_ref):
    n = c_ref.shape[0]
    chunk = min(n, SHIFT_CHUNK * KV_ROW)
    for lo in range(0, n - KV_ROW, chunk):
        size = min(chunk, n - KV_ROW - lo)
        o_ref[pl.ds(lo, size), :] = c_ref[pl.ds(lo + KV_ROW, size), :]
    o_ref[pl.ds(n - KV_ROW, KV_ROW), :] = nxt_ref[...]


def _shift_jobs(jobs, layer, step_of, n_steps, n_in, n_out):
    in_specs, args, out_shapes, out_specs, aliases = [], [], [], [], {}
    for cache, buf, rows in jobs:
        specs, a, shape, spec, alias = _shift_operands(cache, buf, layer, step_of, n_steps, rows)
        if alias is not None:
            aliases[n_in + len(args) + alias] = n_out + len(out_shapes)
        in_specs += specs
        args += a
        out_shapes.append(shape)
        out_specs.append(spec)
    return in_specs, args, out_shapes, out_specs, aliases


def _run_shifts(in_refs, out_refs):
    per = len(in_refs) // len(out_refs)
    for n, o_ref in enumerate(out_refs):
        _shift_rows(in_refs[n * per], in_refs[n * per + 1], o_ref)


def _shift_operands(cache, buf, layer, step_of, n_steps, rows):
    bsz = cache.shape[1]
    win = cache.shape[2] // KV_ROW
    rows = min(win, rows)
    per_b = win // rows
    n_shift = bsz * per_b
    assert n_shift <= n_steps

    def block_of(*idx):
        t = jnp.minimum(step_of(*idx), n_shift - 1)
        return t // per_b, t % per_b

    def main_map(*idx):
        b, r = block_of(*idx)
        return (layer, b, r, 0)

    def next_map(*idx):
        b, r = block_of(*idx)
        return (layer, b, jnp.minimum((r + 1) * rows, win - 1), 0)

    in_specs = [pl.BlockSpec((None, None, rows * KV_ROW, HEAD_DIM), main_map),
                pl.BlockSpec((None, None, KV_ROW, HEAD_DIM), next_map)]
    args = [cache, cache]
    alias = None
    if buf is not None:
        in_specs.append(pl.BlockSpec(memory_space=pl.ANY))
        args.append(buf)
        alias = 2
    out_shape = jax.ShapeDtypeStruct(cache.shape, cache.dtype)
    out_spec = pl.BlockSpec((None, None, rows * KV_ROW, HEAD_DIM), main_map)
    return in_specs, args, out_shape, out_spec, alias


def _lookup(j, table):
    out = table[0]
    for t in range(1, len(table)):
        out = jnp.where(j == t, table[t], out)
    return out


def _proj_kernel(x_ref, g_ref, wa_ref, wb_ref, eg_ref, em_ref, *rest, n_shift_in, n_shifts,
                 emit_w):
    rest = list(rest)
    shift_in, rest = rest[:n_shift_in], rest[n_shift_in:]
    o_ref = rest.pop(0)
    if emit_w:
        wa_out_ref, wb_out_ref = rest.pop(0), rest.pop(0)
    shift_out, rest = rest[:n_shifts], rest[n_shifts:]
    h_ref, = rest
    j = pl.program_id(1)
    ii = pl.program_id(2)

    @pl.when(j == 0)
    def _():
        x = x_ref[...]
        ms = jnp.mean(x * x, axis=-1, keepdims=True)
        h_ref[ii] = (x * lax.rsqrt(ms + EPS) * g_ref[...]).astype(BF16)

    if n_shifts:
        _run_shifts(shift_in, shift_out)
    h_tile = h_ref[ii]
    wa = wa_ref[...].astype(BF16)
    wb = wb_ref[...].astype(BF16)
    if emit_w:
        wa_out_ref[...] = wa
        wb_out_ref[...] = wb
    rows = h_tile.shape[0]
    part = max(rows // PROJ_ROW_PARTS, min(rows, PROJ_MIN_PART))
    for r0 in range(0, rows, part):
        rsl = pl.ds(r0, part)
        acc = jnp.dot(h_tile[r0:r0 + part], wa, preferred_element_type=F32)
        for h in range(N_HEADS):
            sl = slice(h * HEAD_DIM, (h + 1) * HEAD_DIM)
            a = acc[:, sl]
            o_ref[rsl, sl] = a * (lax.rsqrt(jnp.mean(a * a, axis=-1, keepdims=True) + EPS) * eg_ref[:, sl])
    for r0 in range(0, rows, part):
        rsl = pl.ds(r0, part)
        acc = jnp.dot(h_tile[r0:r0 + part], wb, preferred_element_type=F32)
        for h in range(N_HEADS):
            sl = slice(h * HEAD_DIM, (h + 1) * HEAD_DIM)
            osl = slice(SEG + h * HEAD_DIM, SEG + (h + 1) * HEAD_DIM)
            a = acc[:, sl]
            o_ref[rsl, osl] = a * jnp.where(em_ref[:, osl] == MODE_GATE, jax.nn.sigmoid(a), 1.0)


def _project(x, gain, weights, ep_gain, ep_mode, *, gain_layer, rows, name, emit_w=False,
             shift=None):
    m, k = x.shape
    cols = 2 * SEG
    n_tiles = m // rows
    group = 2 if n_tiles % 2 == 0 else 1
    if len(weights) == 3:
        w, layer, seg_pairs = weights
        n_j = len(seg_pairs)
        firsts = tuple(p[0] for p in seg_pairs)
        seconds = tuple(p[1] for p in seg_pairs)
        w_specs = [pl.BlockSpec((None, k, SEG), lambda g, j, t: (layer, 0, _lookup(j, firsts))),
                   pl.BlockSpec((None, k, SEG), lambda g, j, t: (layer, 0, _lookup(j, seconds)))]
        w_args = [w, w]
    else:
        n_j = weights[0].shape[1] // SEG
        w_specs = [pl.BlockSpec((k, SEG), lambda g, j, t: (0, j))] * 2
        w_args = list(weights)
    n = n_j * cols

    def x_map(g, j, t):
        return (jnp.where(j == 0, g * group + t, g * group + group - 1), 0)

    in_specs = [
        pl.BlockSpec((rows, k), x_map),
        pl.BlockSpec((None, 1, k), lambda g, j, t: (gain_layer, 0, 0)),
        *w_specs,
        pl.BlockSpec((None, 1, cols), lambda g, j, t: (gain_layer, 0, j)),
        pl.BlockSpec((1, cols), lambda g, j, t: (0, j)),
    ]
    args = [x, gain, *w_args, ep_gain, ep_mode]
    out_shape = [jax.ShapeDtypeStruct((m, n), F32)]
    out_specs = [pl.BlockSpec((rows, cols), lambda g, j, t: (g * group + t, j))]
    if emit_w:
        assert n_tiles == 1
        out_shape += [jax.ShapeDtypeStruct((k, n_j * SEG), BF16)] * 2
        out_specs += [pl.BlockSpec((k, SEG), lambda g, j, t: (0, j))] * 2
    aliases, s_args, s_shapes = {}, [], []
    if shift is not None:
        s_specs, s_args, s_shapes, s_out_specs, aliases = _shift_jobs(
            shift[1], shift[0], lambda g, j, t: (g * n_j + j) * group + t, n_tiles * n_j,
            len(args), len(out_shape))
        in_specs += s_specs
        args += s_args
        out_shape += s_shapes
        out_specs += s_out_specs
    return pl.pallas_call(
        functools.partial(_proj_kernel, n_shift_in=len(s_args), n_shifts=len(s_shapes),
                          emit_w=emit_w),
        out_shape=out_shape,
        grid=(n_tiles // group, n_j, group),
        in_specs=in_specs,
        out_specs=out_specs,
        scratch_shapes=[pltpu.VMEM((group, rows, k), BF16)],
        input_output_aliases=aliases,
        compiler_params=pltpu.CompilerParams(
            dimension_semantics=("arbitrary", "arbitrary", "arbitrary"),
            vmem_limit_bytes=VMEM_LIMIT_BYTES),
        name=name,
    )(*args)


def _mem_kv_kernel(x_ref, g_ref, w_ref, kg_ref, o_ref):
    x = x_ref[...]
    ms = jnp.mean(x * x, axis=-1, keepdims=True)
    h = (x * lax.rsqrt(ms + EPS) * g_ref[...]).astype(BF16)
    acc = jnp.dot(h, w_ref[...].astype(BF16), preferred_element_type=F32)
    for hd in range(N_HEADS):
        sl = slice(hd * HEAD_DIM, (hd + 1) * HEAD_DIM)
        a = acc[:, sl]
        o_ref[:, sl] = a * (lax.rsqrt(jnp.mean(a * a, axis=-1, keepdims=True) + EPS) * kg_ref[:, sl])
    o_ref[:, SEG:] = acc[:, SEG:]


def _memory_kv(mem2, mem_norm, w_mem_kv, k_gain):
    depth, k, n = w_mem_kv.shape
    rows = mem2.shape[0]
    return pl.pallas_call(
        _mem_kv_kernel,
        out_shape=jax.ShapeDtypeStruct((depth, rows, n), F32),
        grid=(depth,),
        in_specs=[
            pl.BlockSpec((rows, k), lambda l: (0, 0)),
            pl.BlockSpec((None, 1, k), lambda l: (l, 0, 0)),
            pl.BlockSpec((None, k, n), lambda l: (l, 0, 0)),
            pl.BlockSpec((None, 1, SEG), lambda l: (l, 0, 0)),
        ],
        out_specs=pl.BlockSpec((None, rows, n), lambda l: (l, 0, 0)),
        compiler_params=pltpu.CompilerParams(
            dimension_semantics=("arbitrary",), vmem_limit_bytes=VMEM_LIMIT_BYTES),
        name="memory_kv",
    )(mem2, mem_norm, w_mem_kv, k_gain)


def _dilattn_kernel(*refs, n_tiles):
    n_groups = len(DIL_PAIRS)
    ins = refs[:5 * n_groups]
    gate_ref = refs[5 * n_groups]
    y_ref, *tails, og_ref, lse_ref = refs[len(refs) - 3 - n_groups:]
    head = pl.program_id(1)
    i = pl.program_id(2)

    row = lax.broadcasted_iota(jnp.int32, (BAND, 2 * BAND), 0)
    col = lax.broadcasted_iota(jnp.int32, (BAND, 2 * BAND), 1)
    not_future = col <= row + BAND
    band = jnp.logical_and(col >= row, not_future)
    first_lo = jnp.maximum(row, (i == 0).astype(jnp.int32) * BAND)
    band_first = jnp.logical_and(col >= first_lo, not_future)
    ones = jnp.ones((2 * BAND, HEAD_DIM), BF16)
    nt_dims = (((1,), (1,)), ((), ()))

    for g in range(n_groups - 1, -1, -1):
        win, d = DIL_PAIRS[g]
        q_ref, kc_ref, vc_ref, kp_ref, vp_ref = ins[5 * g:5 * g + 5]
        span = BAND * d

        def rows_of(start, size, d=d):
            return pl.ds(start, size) if d == 1 else pl.ds(start, size, stride=d)

        for t in range(ATTN_ROWS // BAND):
            u, c = divmod(t, d)
            base = u * span + c
            q = q_ref[rows_of(base, BAND), :].astype(BF16)
            if u == 0:
                k = jnp.concatenate([kp_ref[rows_of(c, BAND), :], kc_ref[rows_of(c, BAND), :]], axis=0)
                v = jnp.concatenate([vp_ref[rows_of(c, BAND), :], vc_ref[rows_of(c, BAND), :]], axis=0)
            else:
                k = kc_ref[rows_of(base - span, 2 * BAND), :]
                v = vc_ref[rows_of(base - span, 2 * BAND), :]
            s = lax.dot_general(q, k.astype(BF16), nt_dims, preferred_element_type=F32)
            s = jnp.where(band_first if u == 0 else band, s, -jnp.inf)
            m = jnp.max(jnp.maximum(s[:, :BAND], s[:, BAND:]), axis=-1, keepdims=True)
            p = jnp.exp2(s - m).astype(BF16)
            ov = jnp.dot(p, jnp.concatenate([v.astype(BF16), ones], axis=1), preferred_element_type=F32)
            l = ov[:, HEAD_DIM:]
            o = ov[:, :HEAD_DIM] / l
            lse = m + jnp.log2(l)
            if g > 0:
                o_rows = rows_of((g - 1) * ATTN_ROWS + base, BAND)
                og_ref[o_rows, :] = o
                lse_ref[o_rows, :] = lse
            else:
                outs, lses = [o], [lse]
                for other in range(n_groups - 1):
                    sl = pl.ds(other * ATTN_ROWS + base, BAND)
                    outs.append(og_ref[sl, :])
                    lses.append(lse_ref[sl, :])
                mx = functools.reduce(jnp.maximum, lses)
                es = [jnp.exp2(x - mx) for x in lses]
                num = sum(e * x for e, x in zip(es, outs))
                gated = gate_ref[pl.ds(base, BAND), :] * (num / sum(es))
                y_ref[pl.ds(base, BAND), :] = gated.astype(BF16)

    for hh in range(N_HEADS):
        @pl.when(jnp.logical_and(i == n_tiles - 1, head == hh))
        def _(hh=hh):
            for g, (win, _) in enumerate(DIL_PAIRS):
                kc_ref, vc_ref = ins[5 * g + 1], ins[5 * g + 2]
                tails[g][pl.ds(hh, win, stride=KV_ROW), :] = kc_ref[pl.ds(ATTN_ROWS - win, win), :]
                tails[g][pl.ds(N_HEADS + hh, win, stride=KV_ROW), :] = (
                    vc_ref[pl.ds(ATTN_ROWS - win, win), :])


def _dilated_attention(z3, layer, depth, cache_bufs):
    bsz, seq, _ = z3.shape
    n_tiles = seq // ATTN_ROWS
    in_specs = []
    for g, (win, d) in enumerate(DIL_PAIRS):
        span = BAND * d
        qb = _seg_q(g) * N_HEADS
        kb = _seg_k(g) * N_HEADS
        vb = kb + N_HEADS
        per_tile = ATTN_ROWS // span

        def prev_map(col0, per_tile=per_tile):
            return lambda b, h, i: (b, jnp.maximum(i * per_tile - 1, 0), col0 + h)

        def cur_map(col0):
            return lambda b, h, i: (b, i, col0 + h)

        in_specs += [
            pl.BlockSpec((None, ATTN_ROWS, HEAD_DIM), cur_map(qb)),
            pl.BlockSpec((None, ATTN_ROWS, HEAD_DIM), cur_map(kb)),
            pl.BlockSpec((None, ATTN_ROWS, HEAD_DIM), cur_map(vb)),
            pl.BlockSpec((None, span, HEAD_DIM), prev_map(kb)),
            pl.BlockSpec((None, span, HEAD_DIM), prev_map(vb)),
        ]
    in_specs.append(pl.BlockSpec((None, ATTN_ROWS, HEAD_DIM),
                                 lambda b, h, i: (b, i, SEG_GATE_DIL * N_HEADS + h)))
    args = [z3] * len(in_specs)
    out_shape = [jax.ShapeDtypeStruct((bsz, seq, SEG), BF16)]
    out_specs = [pl.BlockSpec((None, ATTN_ROWS, HEAD_DIM), lambda b, h, i: (b, i, h))]
    aliases = {}
    for g, (win, _) in enumerate(DIL_PAIRS):
        out_shape.append(jax.ShapeDtypeStruct((depth, bsz, win * KV_ROW, HEAD_DIM), F32))
        out_specs.append(pl.BlockSpec((None, None, win * KV_ROW, HEAD_DIM),
                                      lambda b, h, i: (layer, b, 0, 0)))
        if cache_bufs is not None:
            in_specs.append(pl.BlockSpec(memory_space=pl.ANY))
            args.append(cache_bufs[g])
            aliases[len(args) - 1] = 1 + g
    return pl.pallas_call(
        functools.partial(_dilattn_kernel, n_tiles=n_tiles),
        out_shape=out_shape,
        grid=(bsz, N_HEADS, n_tiles),
        in_specs=in_specs,
        out_specs=out_specs,
        scratch_shapes=[
            pltpu.VMEM(((len(DIL_PAIRS) - 1) * ATTN_ROWS, HEAD_DIM), F32),
            pltpu.VMEM(((len(DIL_PAIRS) - 1) * ATTN_ROWS, HEAD_DIM), F32),
        ],
        input_output_aliases=aliases,
        compiler_params=pltpu.CompilerParams(
            dimension_semantics=("arbitrary", "arbitrary", "arbitrary"),
            vmem_limit_bytes=VMEM_LIMIT_BYTES),
        name="dilated_attention",
    )(*args)


def _mix_kernel(u_ref, up_ref, gp_ref, qm_ref, gm_ref, yd_ref, mkv_ref, pw_ref, ps_ref,
                wo_ref, x_ref, *rest, n_shift_in, n_shifts):
    shift_in, rest = rest[:n_shift_in], rest[n_shift_in:]
    o_ref, rest = rest[0], rest[1:]
    shift_out, (ue_ref, t_ref, y_ref) = rest[:n_shifts], rest[n_shifts:]
    i = pl.program_id(1)
    rows = u_ref.shape[0]

    pad, halo = POOL_PAD, POOL_HALO
    n = halo + rows

    @pl.when(i > 0)
    def _():
        ue_ref[pl.ds(pad, halo), :] = up_ref[...]

    @pl.when(i == 0)
    def _():
        ue_ref[pl.ds(pad, halo), :] = jnp.zeros((halo, SEG), F32)

    ue_ref[pl.ds(0, pad), :] = jnp.zeros((pad, SEG), F32)
    t_ref[0, pl.ds(0, pad), :] = jnp.zeros((pad, HEAD_DIM), F32)
    t_ref[1, pl.ds(0, pad), :] = jnp.zeros((pad, HEAD_DIM), F32)
    ue_ref[pl.ds(pad + halo, rows), :] = u_ref[...]
    pos = i * rows + lax.broadcasted_iota(jnp.int32, (rows, 1), 0)
    for gi, w in enumerate(POOL_WINDOWS):
        sl = slice(gi * HEAD_DIM, (gi + 1) * HEAD_DIM)
        cur = ue_ref[pl.ds(pad, n), sl] + ue_ref[pl.ds(pad - 1, n), sl]
        shift, buf = 2, 0
        while shift < w:
            t_ref[buf, pl.ds(pad, n), :] = cur
            cur = cur + t_ref[buf, pl.ds(pad - shift, n), :]
            shift, buf = 2 * shift, 1 - buf
        cnt = jnp.minimum(w, pos + 1).astype(F32)
        dlt = cur[halo:] / cnt - u_ref[:, sl]
        yp = jnp.dot(dlt.astype(BF16), pw_ref[gi], preferred_element_type=F32) * ps_ref[:, sl]
        y_ref[:, sl] = (gp_ref[:, sl] * yp).astype(BF16)

    y_ref[:, SEG:2 * SEG] = yd_ref[...]

    nt_dims = (((1,), (1,)), ((), ()))
    ones = jnp.ones((MEM_LEN, HEAD_DIM), BF16)
    for h in range(N_HEADS):
        sl = slice(h * HEAD_DIM, (h + 1) * HEAD_DIM)
        q = qm_ref[:, sl].astype(BF16)
        k = mkv_ref[:, sl].astype(BF16)
        v = mkv_ref[:, SEG + h * HEAD_DIM:SEG + (h + 1) * HEAD_DIM].astype(BF16)
        s = lax.dot_general(q, k, nt_dims, preferred_element_type=F32)
        m = jnp.max(s, axis=-1, keepdims=True)
        p = jnp.exp(s - m).astype(BF16)
        ov = jnp.dot(p, jnp.concatenate([v, ones], axis=1), preferred_element_type=F32)
        o = ov[:, :HEAD_DIM] / ov[:, HEAD_DIM:]
        y_ref[:, 2 * SEG + h * HEAD_DIM:2 * SEG + (h + 1) * HEAD_DIM] = (gm_ref[:, sl] * o).astype(BF16)

    _run_shifts(shift_in, shift_out)
    o_ref[...] = x_ref[...] + jnp.dot(y_ref[...], wo_ref[...], preferred_element_type=F32)


def _mix(z3, y_dil, mem_kv, pool_w, pool_scale, w_out, x3, layer, shift_jobs):
    bsz, seq, _ = z3.shape
    rows = MIX_ROWS
    n_tiles = seq // rows
    halo_per_tile = rows // POOL_HALO

    def seg_spec(seg):
        return pl.BlockSpec((None, rows, SEG), lambda b, i: (b, i, seg))

    s_specs, s_args, s_shapes, s_out_specs, aliases = _shift_jobs(
        shift_jobs, layer, lambda b, i: b * n_tiles + i, bsz * n_tiles, 11, 1)
    return pl.pallas_call(
        functools.partial(_mix_kernel, n_shift_in=len(s_args), n_shifts=len(s_shapes)),
        out_shape=[jax.ShapeDtypeStruct(x3.shape, F32)] + s_shapes,
        grid=(bsz, n_tiles),
        in_specs=[
            seg_spec(SEG_U),
            pl.BlockSpec((None, POOL_HALO, SEG),
                         lambda b, i: (b, jnp.maximum(i * halo_per_tile - 1, 0), SEG_U)),
            seg_spec(SEG_GATE_POOL),
            seg_spec(SEG_QMEM),
            seg_spec(SEG_GATE_MEM),
            pl.BlockSpec((None, rows, SEG), lambda b, i: (b, i, 0)),
            pl.BlockSpec((None, None, MEM_LEN, 2 * SEG), lambda b, i: (layer, b, 0, 0)),
            pl.BlockSpec((None, len(POOL_WINDOWS), HEAD_DIM, HEAD_DIM), lambda b, i: (layer, 0, 0, 0)),
            pl.BlockSpec((None, 1, SEG), lambda b, i: (layer, 0, 0)),
            pl.BlockSpec((MIX_WIDTH, D_MODEL), lambda b, i: (0, 0), pipeline_mode=pl.Buffered(1)),
            pl.BlockSpec((None, rows, D_MODEL), lambda b, i: (b, i, 0)),
        ] + s_specs,
        out_specs=[pl.BlockSpec((None, rows, D_MODEL), lambda b, i: (b, i, 0))] + s_out_specs,
        scratch_shapes=[
            pltpu.VMEM((POOL_PAD + POOL_HALO + rows, SEG), F32),
            pltpu.VMEM((2, POOL_PAD + POOL_HALO + rows, HEAD_DIM), F32),
            pltpu.VMEM((rows, MIX_WIDTH), BF16),
        ],
        input_output_aliases=aliases,
        compiler_params=pltpu.CompilerParams(
            dimension_semantics=("arbitrary", "arbitrary"),
            vmem_limit_bytes=VMEM_LIMIT_BYTES),
        name="mix",
    )(z3, z3, z3, z3, z3, y_dil, mem_kv, pool_w, pool_scale, w_out, x3, *s_args)


def _sample_tail_kernel(zs3_ref, c0_ref, c1_ref, c2_ref, cm_ref, zs_ref, st_ref, pw_ref, ps_ref,
                        wo_ref, x_ref, o_ref, ns_ref, wob_ref, ydm_ref, y_ref):
    b = pl.program_id(0)

    @pl.when(b == 0)
    def _():
        ydm_ref[...] = jnp.zeros(ydm_ref.shape, F32)

    _sample_attention_row(zs3_ref, (c0_ref, c1_ref, c2_ref), cm_ref, ydm_ref, b)

    @pl.when(b == pl.num_programs(0) - 1)
    def _():
        _sample_out_rows(zs_ref, st_ref, ydm_ref, pw_ref, ps_ref, wo_ref, x_ref, o_ref, ns_ref,
                         wob_ref, y_ref)


def _sample_attention_row(zs_ref, cache_refs, cm_ref, ydm_ref, b):
    outs, lses = [], []
    for g, c_ref in enumerate(cache_refs):
        q = zs_ref[pl.ds(_seg_q(g) * N_HEADS, N_HEADS), :]
        k_new = zs_ref[pl.ds(_seg_k(g) * N_HEADS, N_HEADS), :]
        v_new = zs_ref[pl.ds((_seg_k(g) + 1) * N_HEADS, N_HEADS), :]
        k = c_ref[:, 0]
        v = c_ref[:, 1]
        s = jnp.sum(k * q[None], axis=-1, keepdims=True)
        s_new = jnp.sum(k_new * q, axis=-1, keepdims=True)
        m = jnp.maximum(jnp.max(s, axis=0), s_new)
        p = jnp.exp2(s - m[None])
        p_new = jnp.exp2(s_new - m)
        l = jnp.sum(p, axis=0) + p_new
        outs.append((jnp.sum(p * v, axis=0) + p_new * v_new) / l)
        lses.append(m + jnp.log2(l))
    mx = jnp.maximum(jnp.maximum(lses[0], lses[1]), lses[2])
    es = [jnp.exp2(x - mx) for x in lses]
    o_dil = (es[0] * outs[0] + es[1] * outs[1] + es[2] * outs[2]) / (es[0] + es[1] + es[2])

    q = zs_ref[pl.ds(SEG_QMEM * N_HEADS, N_HEADS), :]
    k = cm_ref[:, 0]
    v = cm_ref[:, 1]
    s = jnp.sum(k * q[None], axis=-1, keepdims=True)
    m = jnp.max(s, axis=0)
    p = jnp.exp(s - m[None])
    o_mem = jnp.sum(p * v, axis=0) / jnp.sum(p, axis=0)

    rows = ydm_ref.shape[0]
    is_row_b = lax.broadcasted_iota(jnp.int32, (rows, HEAD_DIM), 0) == b
    for h in range(N_HEADS):
        for base, o in ((0, o_dil), (SEG, o_mem)):
            sl = pl.ds(base + h * HEAD_DIM, HEAD_DIM)
            row = jnp.broadcast_to(o[h:h + 1, :], (rows, HEAD_DIM))
            ydm_ref[:, sl] = jnp.where(is_row_b, row, ydm_ref[:, sl])


def _sample_out_rows(zs_ref, st_ref, ydm_ref, pw_ref, ps_ref, wo_ref, x_ref, o_ref, ns_ref,
                     wob_ref, y_ref):
    wob_ref[...] = wo_ref[...].astype(BF16)
    u = zs_ref[:, pl.ds(SEG_U * SEG, SEG)]
    for gi, w in enumerate(POOL_WINDOWS):
        sl = slice(gi * HEAD_DIM, (gi + 1) * HEAD_DIM)
        tot = u[:, sl]
        for back in range(1, w):
            tot = tot + st_ref[POOL_BUF - back, :, sl]
        cnt = float(min(w, PAST_LEN + 1))
        dlt = tot / cnt - u[:, sl]
        yp = jnp.dot(dlt.astype(BF16), pw_ref[gi], preferred_element_type=F32) * ps_ref[:, sl]
        y_ref[:, sl] = (zs_ref[:, pl.ds(SEG_GATE_POOL * SEG + gi * HEAD_DIM, HEAD_DIM)] * yp).astype(BF16)
    y_ref[:, SEG:2 * SEG] = (zs_ref[:, pl.ds(SEG_GATE_DIL * SEG, SEG)] * ydm_ref[:, pl.ds(0, SEG)]).astype(BF16)
    y_ref[:, 2 * SEG:] = (zs_ref[:, pl.ds(SEG_GATE_MEM * SEG, SEG)] * ydm_ref[:, pl.ds(SEG, SEG)]).astype(BF16)
    o_ref[...] = x_ref[...] + jnp.dot(y_ref[...], wob_ref[...], preferred_element_type=F32)
    for r in range(POOL_BUF - 1):
        ns_ref[r] = st_ref[r + 1]
    ns_ref[POOL_BUF - 1] = u


def _sample_tail(zs, caches7, cache_mem, state_t, pool_w, pool_scale, w_out, xs, layer):
    rows = zs.shape[0]
    bsz = cache_mem.shape[1]
    zs3 = zs.reshape(rows, N_SEG * N_HEADS, HEAD_DIM)
    full = lambda shape: pl.BlockSpec(shape, lambda b: tuple(0 for _ in shape))
    of_layer = lambda a: pl.BlockSpec((None,) + a.shape[1:], lambda b: (layer,) + (0,) * (a.ndim - 1))
    in_specs = [pl.BlockSpec((None, N_SEG * N_HEADS, HEAD_DIM), lambda b: (b, 0, 0))]
    for _ in caches7:
        in_specs.append(pl.BlockSpec((None, None, BAND, None, 2, N_HEADS, HEAD_DIM),
                                     lambda b: (layer, b, 0, 0, 0, 0, 0)))
    in_specs.append(pl.BlockSpec((None, None, MEM_LEN, 2, N_HEADS, HEAD_DIM),
                                 lambda b: (layer, b, 0, 0, 0, 0)))
    in_specs += [full(zs.shape), of_layer(state_t), of_layer(pool_w), of_layer(pool_scale),
                 of_layer(w_out), full(xs.shape)]
    return pl.pallas_call(
        _sample_tail_kernel,
        out_shape=[jax.ShapeDtypeStruct(xs.shape, F32),
                   jax.ShapeDtypeStruct(state_t.shape[1:], F32),
                   jax.ShapeDtypeStruct(w_out.shape[1:], BF16)],
        grid=(bsz,),
        in_specs=in_specs,
        out_specs=[full(xs.shape), full(state_t.shape[1:]), full(w_out.shape[1:])],
        scratch_shapes=[pltpu.VMEM((rows, 2 * SEG), F32), pltpu.VMEM((rows, MIX_WIDTH), BF16)],
        compiler_params=pltpu.CompilerParams(
            dimension_semantics=("arbitrary",), vmem_limit_bytes=VMEM_LIMIT_BYTES),
        name="sample_tail",
    )(zs3, *caches7, cache_mem, zs, state_t, pool_w, pool_scale, w_out, xs)


def _set_last_kernel(*refs):
    n = len(refs) // 3
    for new_ref, o_ref in zip(refs[:n], refs[2 * n:]):
        o_ref[...] = new_ref[...]


def _cache_set_last(bufs, new_rows):
    n = len(bufs)
    depth, bsz = bufs[0].shape[:2]

    def last_row_spec(win):
        return pl.BlockSpec((None, bsz, KV_ROW, HEAD_DIM), lambda l: (l, 0, win - 1, 0))

    return pl.pallas_call(
        _set_last_kernel,
        out_shape=[jax.ShapeDtypeStruct(buf.shape, buf.dtype) for buf in bufs],
        grid=(depth,),
        in_specs=[pl.BlockSpec((None, bsz, KV_ROW, HEAD_DIM), lambda l: (l, 0, 0, 0))] * n
        + [pl.BlockSpec(memory_space=pl.ANY)] * n,
        out_specs=[last_row_spec(buf.shape[2] // KV_ROW) for buf in bufs],
        input_output_aliases={n + g: g for g in range(n)},
        compiler_params=pltpu.CompilerParams(dimension_semantics=("arbitrary",)),
        name="cache_set_last",
    )(*new_rows, *bufs)


def _tile_heads(v):
    return jnp.tile(v, N_HEADS)


def kernel(x_prompt, x_sample, state_pool, cache_dil_w128, cache_dil_w512, cache_dil_w2048,
           cache_mem_kv, mem_prompt, norm_g, w_in, pool_w, pool_scale, dil_q_norm, dil_k_norm,
           mem_norm_g, w_mem_kv, mem_q_norm, mem_k_norm, w_out):
    depth = w_in.shape[0]
    bsz, seq, _ = x_prompt.shape
    dbsz = x_sample.shape[0]
    caches = (cache_dil_w128, cache_dil_w512, cache_dil_w2048)

    z_pairs = tuple(zip(Z_ORDER[0::2], Z_ORDER[1::2]))
    pool_w_b = pool_w.astype(BF16)

    ones = jnp.ones((depth, SEG), F32)
    segs = [ones] * N_SEG
    for g in range(len(DIL_PAIRS)):
        segs[_seg_q(g)] = _tile_heads(dil_q_norm[:, g]) * (ATTN_SCALE * LOG2_E)
        segs[_seg_k(g)] = _tile_heads(dil_k_norm[:, g])
    segs[SEG_QMEM] = _tile_heads(mem_q_norm) * ATTN_SCALE
    ep_gain = jnp.concatenate(segs, axis=1)[:, None, :]
    mem_k_gain = _tile_heads(mem_k_norm)[:, None, :]
    seg_modes = [MODE_GATE if s in GATE_SEGS else MODE_PLAIN for s in range(N_SEG)]
    ep_mode = jnp.repeat(jnp.array(seg_modes, F32), SEG)[None, :]
    gain = norm_g[:, None, :]
    mem_norm = mem_norm_g[:, None, :]
    pscale = pool_scale[:, None, :]

    caches7 = [c.reshape(depth, dbsz, win // d, d, 2, N_HEADS, HEAD_DIM)
               for c, (win, d) in zip(caches, DIL_PAIRS)]
    caches_2d = [c.reshape(depth, dbsz, win * KV_ROW, HEAD_DIM)
                 for c, (win, _) in zip(caches, DIL_PAIRS)]
    state_t = jnp.pad(jnp.transpose(state_pool, (0, 2, 1, 3)),
                      ((0, 0), (0, 0), (0, SAMPLE_ROWS - dbsz), (0, 0)))

    xp = x_prompt.reshape(bsz * seq, D_MODEL)
    xs = jnp.pad(x_sample.reshape(dbsz, D_MODEL), ((0, SAMPLE_ROWS - dbsz), (0, 0)))
    mem2 = mem_prompt.reshape(bsz * MEM_LEN, D_MODEL)
    mem_kv = _memory_kv(mem2, mem_norm, w_mem_kv, mem_k_gain).reshape(depth, bsz, MEM_LEN, 2 * SEG)

    pool_p, pool_s, zs_rows = [], [], []
    prompt_caches = None
    sample_caches = [None] * len(DIL_PAIRS)
    for l in range(depth):
        zs, w_first, w_second = _project(xs, gain, (w_in, l, z_pairs), ep_gain, ep_mode,
                                         gain_layer=l, rows=SAMPLE_ROWS, name="proj_sample",
                                         emit_w=True)
        xs, new_state_t, w_out_b = _sample_tail(zs, caches7, cache_mem_kv, state_t, pool_w_b,
                                                pscale, w_out, xs, l)
        pool_s.append(new_state_t)
        zs_rows.append(zs)

        z, sample_caches[2] = _project(
            xp, gain, (w_first, w_second), ep_gain, ep_mode, gain_layer=l, rows=PROJ_ROWS,
            name="proj_prompt", shift=(l, [(caches_2d[2], sample_caches[2], SHIFT_ROWS)]))
        z3 = z.reshape(bsz, seq, IN_COLS)
        y_dil, *prompt_caches = _dilated_attention(z3, l, depth, prompt_caches)
        xp3, sample_caches[1], sample_caches[0] = _mix(
            z3, y_dil, mem_kv, pool_w_b, pscale, w_out_b, xp.reshape(bsz, seq, D_MODEL), l,
            [(caches_2d[1], sample_caches[1], MIX_SHIFT_ROWS[1]),
             (caches_2d[0], sample_caches[0], MIX_SHIFT_ROWS[0])])
        xp = xp3.reshape(bsz * seq, D_MODEL)
        pool_p.append(z3[:, seq - POOL_BUF:, SEG_U * SEG:(SEG_U + 1) * SEG])

    zs_all = jnp.stack(zs_rows)[:, :dbsz].reshape(depth, dbsz, N_SEG, N_HEADS, HEAD_DIM)
    new_rows = [zs_all[:, :, _seg_k(g):_seg_k(g) + 2].reshape(depth, dbsz, KV_ROW, HEAD_DIM)
                for g in range(len(DIL_PAIRS))]
    new_caches = _cache_set_last(sample_caches, new_rows)
    new_caches = [c.reshape(depth, dbsz, win, 2, N_HEADS, HEAD_DIM)
                  for c, (win, _) in zip(new_caches, DIL_PAIRS)]

    y_prompt = xp.reshape(bsz, seq, D_MODEL)
    y_sample = xs[:dbsz].reshape(dbsz, 1, D_MODEL)
    cache_mem_prompt = mem_kv.reshape(depth, bsz, MEM_LEN, 2, N_HEADS, HEAD_DIM)
    state_pool_sample = jnp.transpose(jnp.stack(pool_s)[:, :, :dbsz], (0, 2, 1, 3))
    prompt_caches = [c.reshape(depth, bsz, win, 2, N_HEADS, HEAD_DIM)
                     for c, (win, _) in zip(prompt_caches, DIL_PAIRS)]
    return (y_prompt, y_sample, jnp.stack(pool_p), prompt_caches[0], prompt_caches[1],
            prompt_caches[2], cache_mem_prompt, state_pool_sample, new_caches[0], new_caches[1],
            new_caches[2])
```

```python
import functools

import jax
import jax.numpy as jnp
from jax import lax
from jax.experimental import pallas as pl
from jax.experimental.pallas import tpu as pltpu

F32 = jnp.float32
BF16 = jnp.bfloat16

D_MODEL = 2048
HEAD_DIM = 128
N_HEADS = 4
SEG = N_HEADS * HEAD_DIM
N_SEG = 14
IN_COLS = N_SEG * SEG
POOL_WINDOWS = (2, 4, 8, 16)
POOL_BUF = 15
POOL_HALO = 16
POOL_PAD = 8
DIL_PAIRS = ((128, 1), (512, 4), (2048, 16))
BAND = 128
MEM_LEN = 256
MIX_WIDTH = 3 * SEG
EPS = 1e-6
ATTN_SCALE = HEAD_DIM ** -0.5
LOG2_E = 1.4426950408889634
PAST_LEN = 16384
assert PAST_LEN >= max(win for win, _ in DIL_PAIRS) and PAST_LEN + 1 >= max(POOL_WINDOWS)

Z_ORDER = (2, 0, 3, 4, 5, 1, 6, 7, 8, 11, 9, 10, 12, 13)
SEG_U, SEG_GATE_POOL, SEG_GATE_DIL, SEG_QMEM, SEG_GATE_MEM = 1, 5, 9, 12, 13
GATE_SEGS = (SEG_GATE_POOL, SEG_GATE_DIL, SEG_GATE_MEM)


def _seg_q(g):
    return 4 * g


def _seg_k(g):
    return 4 * g + 2

VMEM_LIMIT_BYTES = 56 * 1024 * 1024

PROJ_ROWS = 1024
PROJ_ROW_PARTS = 4
PROJ_MIN_PART = 256
ATTN_ROWS = 2048
MIX_ROWS = 512
SAMPLE_ROWS = 16
SAMPLE_PER_STEP = 2
KV_ROW = 2 * N_HEADS
SHIFT_ROWS = 512
MIX_SHIFT_ROWS = (64, 256)
SHIFT_CHUNK = 64


MODE_PLAIN, MODE_GATE = 0.0, 1.0


def _shift_rows(c_ref, nxt_ref, o_ref):
    n = c_ref.shape[0]
    chunk = min(n, SHIFT_CHUNK * KV_ROW)
    for lo in range(0, n - KV_ROW, chunk):
        size = min(chunk, n - KV_ROW - lo)
        o_ref[pl.ds(lo, size), :] = c_ref[pl.ds(lo + KV_ROW, size), :]
    o_ref[pl.ds(n - KV_ROW, KV_ROW), :] = nxt_ref[...]


def _shift_jobs(jobs, layer, step_of, n_steps, n_in, n_out):
    in_specs, args, out_shapes, out_specs, aliases = [], [], [], [], {}
    for cache, buf, rows in jobs:
        specs, a, shape, spec, alias = _shift_operands(cache, buf, layer, step_of, n_steps, rows)
        if alias is not None:
            aliases[n_in + len(args) + alias] = n_out + len(out_shapes)
        in_specs += specs
        args += a
        out_shapes.append(shape)
        out_specs.append(spec)
    return in_specs, args, out_shapes, out_specs, aliases


def _run_shifts(in_refs, out_refs):
    per = len(in_refs) // len(out_refs)
    for n, o_ref in enumerate(out_refs):
        _shift_rows(in_refs[n * per], in_refs[n * per + 1], o_ref)


def _shift_operands(cache, buf, layer, step_of, n_steps, rows):
    bsz = cache.shape[1]
    win = cache.shape[2] // KV_ROW
    rows = min(win, rows)
    per_b = win // rows
    n_shift = bsz * per_b
    assert n_shift <= n_steps

    def block_of(*idx):
        t = jnp.minimum(step_of(*idx), n_shift - 1)
        return t // per_b, t % per_b

    def main_map(*idx):
        b, r = block_of(*idx)
        return (layer, b, r, 0)

    def next_map(*idx):
        b, r = block_of(*idx)
        return (layer, b, jnp.minimum((r + 1) * rows, win - 1), 0)

    in_specs = [pl.BlockSpec((None, None, rows * KV_ROW, HEAD_DIM), main_map),
                pl.BlockSpec((None, None, KV_ROW, HEAD_DIM), next_map)]
    args = [cache, cache]
    alias = None
    if buf is not None:
        in_specs.append(pl.BlockSpec(memory_space=pl.ANY))
        args.append(buf)
        alias = 2
    out_shape = jax.ShapeDtypeStruct(cache.shape, cache.dtype)
    out_spec = pl.BlockSpec((None, None, rows * KV_ROW, HEAD_DIM), main_map)
    return in_specs, args, out_shape, out_spec, alias


def _lookup(j, table):
    out = table[0]
    for t in range(1, len(table)):
        out = jnp.where(j == t, table[t], out)
    return out


def _proj_kernel(x_ref, g_ref, wa_ref, wb_ref, eg_ref, em_ref, *rest, n_shift_in, n_shifts,
                 emit_w):
    rest = list(rest)
    shift_in, rest = rest[:n_shift_in], rest[n_shift_in:]
    o_ref = rest.pop(0)
    if emit_w:
        wa_out_ref, wb_out_ref = rest.pop(0), rest.pop(0)
    shift_out, rest = rest[:n_shifts], rest[n_shifts:]
    h_ref, = rest
    j = pl.program_id(1)
    ii = pl.program_id(2)

    @pl.when(j == 0)
    def _():
        x = x_ref[...]
        ms = jnp.mean(x * x, axis=-1, keepdims=True)
        h_ref[ii] = (x * lax.rsqrt(ms + EPS) * g_ref[...]).astype(BF16)

    if n_shifts:
        _run_shifts(shift_in, shift_out)
    h_tile = h_ref[ii]
    wa = wa_ref[...].astype(BF16)
    wb = wb_ref[...].astype(BF16)
    if emit_w:
        wa_out_ref[...] = wa
        wb_out_ref[...] = wb
    rows = h_tile.shape[0]
    part = max(rows // PROJ_ROW_PARTS, min(rows, PROJ_MIN_PART))
    for r0 in range(0, rows, part):
        rsl = pl.ds(r0, part)
        acc = jnp.dot(h_tile[r0:r0 + part], wa, preferred_element_type=F32)
        for h in range(N_HEADS):
            sl = slice(h * HEAD_DIM, (h + 1) * HEAD_DIM)
            a = acc[:, sl]
            o_ref[rsl, sl] = a * (lax.rsqrt(jnp.mean(a * a, axis=-1, keepdims=True) + EPS) * eg_ref[:, sl])
    for r0 in range(0, rows, part):
        rsl = pl.ds(r0, part)
        acc = jnp.dot(h_tile[r0:r0 + part], wb, preferred_element_type=F32)
        for h in range(N_HEADS):
            sl = slice(h * HEAD_DIM, (h + 1) * HEAD_DIM)
            osl = slice(SEG + h * HEAD_DIM, SEG + (h + 1) * HEAD_DIM)
            a = acc[:, sl]
            o_ref[rsl, osl] = a * jnp.where(em_ref[:, osl] == MODE_GATE, jax.nn.sigmoid(a), 1.0)


def _project(x, gain, weights, ep_gain, ep_mode, *, gain_layer, rows, name, emit_w=False,
             shift=None):
    m, k = x.shape
    cols = 2 * SEG
    n_tiles = m // rows
    group = 2 if n_tiles % 2 == 0 else 1
    if len(weights) == 3:
        w, layer, seg_pairs = weights
        n_j = len(seg_pairs)
        firsts = tuple(p[0] for p in seg_pairs)
        seconds = tuple(p[1] for p in seg_pairs)
        w_specs = [pl.BlockSpec((None, k, SEG), lambda g, j, t: (layer, 0, _lookup(j, firsts))),
                   pl.BlockSpec((None, k, SEG), lambda g, j, t: (layer, 0, _lookup(j, seconds)))]
        w_args = [w, w]
    else:
        n_j = weights[0].shape[1] // SEG
        w_specs = [pl.BlockSpec((k, SEG), lambda g, j, t: (0, j))] * 2
        w_args = list(weights)
    n = n_j * cols

    def x_map(g, j, t):
        return (jnp.where(j == 0, g * group + t, g * group + group - 1), 0)

    in_specs = [
        pl.BlockSpec((rows, k), x_map),
        pl.BlockSpec((None, 1, k), lambda g, j, t: (gain_layer, 0, 0)),
        *w_specs,
        pl.BlockSpec((None, 1, cols), lambda g, j, t: (gain_layer, 0, j)),
        pl.BlockSpec((1, cols), lambda g, j, t: (0, j)),
    ]
    args = [x, gain, *w_args, ep_gain, ep_mode]
    out_shape = [jax.ShapeDtypeStruct((m, n), F32)]
    out_specs = [pl.BlockSpec((rows, cols), lambda g, j, t: (g * group + t, j))]
    if emit_w:
        assert n_tiles == 1
        out_shape += [jax.ShapeDtypeStruct((k, n_j * SEG), BF16)] * 2
        out_specs += [pl.BlockSpec((k, SEG), lambda g, j, t: (0, j))] * 2
    aliases, s_args, s_shapes = {}, [], []
    if shift is not None:
        s_specs, s_args, s_shapes, s_out_specs, aliases = _shift_jobs(
            shift[1], shift[0], lambda g, j, t: (g * n_j + j) * group + t, n_tiles * n_j,
            len(args), len(out_shape))
        in_specs += s_specs
        args += s_args
        out_shape += s_shapes
        out_specs += s_out_specs
    return pl.pallas_call(
        functools.partial(_proj_kernel, n_shift_in=len(s_args), n_shifts=len(s_shapes),
                          emit_w=emit_w),
        out_shape=out_shape,
        grid=(n_tiles // group, n_j, group),
        in_specs=in_specs,
        out_specs=out_specs,
        scratch_shapes=[pltpu.VMEM((group, rows, k), BF16)],
        input_output_aliases=aliases,
        compiler_params=pltpu.CompilerParams(
            dimension_semantics=("arbitrary", "arbitrary", "arbitrary"),
            vmem_limit_bytes=VMEM_LIMIT_BYTES),
        name=name,
    )(*args)


def _mem_kv_kernel(x_ref, g_ref, w_ref, kg_ref, o_ref):
    x = x_ref[...]
    ms = jnp.mean(x * x, axis=-1, keepdims=True)
    h = (x * lax.rsqrt(ms + EPS) * g_ref[...]).astype(BF16)
    acc = jnp.dot(h, w_ref[...].astype(BF16), preferred_element_type=F32)
    for hd in range(N_HEADS):
        sl = slice(hd * HEAD_DIM, (hd + 1) * HEAD_DIM)
        a = acc[:, sl]
        o_ref[:, sl] = a * (lax.rsqrt(jnp.mean(a * a, axis=-1, keepdims=True) + EPS) * kg_ref[:, sl])
    o_ref[:, SEG:] = acc[:, SEG:]


def _memory_kv(mem2, mem_norm, w_mem_kv, k_gain):
    depth, k, n = w_mem_kv.shape
    rows = mem2.shape[0]
    return pl.pallas_call(
        _mem_kv_kernel,
        out_shape=jax.ShapeDtypeStruct((depth, rows, n), F32),
        grid=(depth,),
        in_specs=[
            pl.BlockSpec((rows, k), lambda l: (0, 0)),
            pl.BlockSpec((None, 1, k), lambda l: (l, 0, 0)),
            pl.BlockSpec((None, k, n), lambda l: (l, 0, 0)),
            pl.BlockSpec((None, 1, SEG), lambda l: (l, 0, 0)),
        ],
        out_specs=pl.BlockSpec((None, rows, n), lambda l: (l, 0, 0)),
        compiler_params=pltpu.CompilerParams(
            dimension_semantics=("arbitrary",), vmem_limit_bytes=VMEM_LIMIT_BYTES),
        name="memory_kv",
    )(mem2, mem_norm, w_mem_kv, k_gain)


def _dilattn_kernel(*refs, n_tiles):
    n_groups = len(DIL_PAIRS)
    ins = refs[:5 * n_groups]
    gate_ref = refs[5 * n_groups]
    y_ref, *tails, og_ref, lse_ref = refs[len(refs) - 3 - n_groups:]
    head = pl.program_id(1)
    i = pl.program_id(2)

    row = lax.broadcasted_iota(jnp.int32, (BAND, 2 * BAND), 0)
    col = lax.broadcasted_iota(jnp.int32, (BAND, 2 * BAND), 1)
    not_future = col <= row + BAND
    band = jnp.logical_and(col >= row, not_future)
    first_lo = jnp.maximum(row, (i == 0).astype(jnp.int32) * BAND)
    band_first = jnp.logical_and(col >= first_lo, not_future)
    ones = jnp.ones((2 * BAND, HEAD_DIM), BF16)
    nt_dims = (((1,), (1,)), ((), ()))

    for g in range(n_groups - 1, -1, -1):
        win, d = DIL_PAIRS[g]
        q_ref, kc_ref, vc_ref, kp_ref, vp_ref = ins[5 * g:5 * g + 5]
        span = BAND * d

        def rows_of(start, size, d=d):
            return pl.ds(start, size) if d == 1 else pl.ds(start, size, stride=d)

        for t in range(ATTN_ROWS // BAND):
            u, c = divmod(t, d)
            base = u * span + c
            q = q_ref[rows_of(base, BAND), :].astype(BF16)
            if u == 0:
                k = jnp.concatenate([kp_ref[rows_of(c, BAND), :], kc_ref[rows_of(c, BAND), :]], axis=0)
                v = jnp.concatenate([vp_ref[rows_of(c, BAND), :], vc_ref[rows_of(c, BAND), :]], axis=0)
            else:
                k = kc_ref[rows_of(base - span, 2 * BAND), :]
                v = vc_ref[rows_of(base - span, 2 * BAND), :]
            s = lax.dot_general(q, k.astype(BF16), nt_dims, preferred_element_type=F32)
            s = jnp.where(band_first if u == 0 else band, s, -jnp.inf)
            m = jnp.max(jnp.maximum(s[:, :BAND], s[:, BAND:]), axis=-1, keepdims=True)
            p = jnp.exp2(s - m).astype(BF16)
            ov = jnp.dot(p, jnp.concatenate([v.astype(BF16), ones], axis=1), preferred_element_type=F32)
            l = ov[:, HEAD_DIM:]
            o = ov[:, :HEAD_DIM] / l
            lse = m + jnp.log2(l)
            if g > 0:
                o_rows = rows_of((g - 1) * ATTN_ROWS + base, BAND)
                og_ref[o_rows, :] = o
                lse_ref[o_rows, :] = lse
            else:
                outs, lses = [o], [lse]
                for other in range(n_groups - 1):
                    sl = pl.ds(other * ATTN_ROWS + base, BAND)
                    outs.append(og_ref[sl, :])
                    lses.append(lse_ref[sl, :])
                mx = functools.reduce(jnp.maximum, lses)
                es = [jnp.exp2(x - mx) for x in lses]
                num = sum(e * x for e, x in zip(es, outs))
                gated = gate_ref[pl.ds(base, BAND), :] * (num / sum(es))
                y_ref[pl.ds(base, BAND), :] = gated.astype(BF16)

    for hh in range(N_HEADS):
        @pl.when(jnp.logical_and(i == n_tiles - 1, head == hh))
        def _(hh=hh):
            for g, (win, _) in enumerate(DIL_PAIRS):
                kc_ref, vc_ref = ins[5 * g + 1], ins[5 * g + 2]
                tails[g][pl.ds(hh, win, stride=KV_ROW), :] = kc_ref[pl.ds(ATTN_ROWS - win, win), :]
                tails[g][pl.ds(N_HEADS + hh, win, stride=KV_ROW), :] = (
                    vc_ref[pl.ds(ATTN_ROWS - win, win), :])


def _dilated_attention(z3, layer, depth, cache_bufs):
    bsz, seq, _ = z3.shape
    n_tiles = seq // ATTN_ROWS
    in_specs = []
    for g, (win, d) in enumerate(DIL_PAIRS):
        span = BAND * d
        qb = _seg_q(g) * N_HEADS
        kb = _seg_k(g) * N_HEADS
        vb = kb + N_HEADS
        per_tile = ATTN_ROWS // span

        def prev_map(col0, per_tile=per_tile):
            return lambda b, h, i: (b, jnp.maximum(i * per_tile - 1, 0), col0 + h)

        def cur_map(col0):
            return lambda b, h, i: (b, i, col0 + h)

        in_specs += [
            pl.BlockSpec((None, ATTN_ROWS, HEAD_DIM), cur_map(qb)),
            pl.BlockSpec((None, ATTN_ROWS, HEAD_DIM), cur_map(kb)),
            pl.BlockSpec((None, ATTN_ROWS, HEAD_DIM), cur_map(vb)),
            pl.BlockSpec((None, span, HEAD_DIM), prev_map(kb)),
            pl.BlockSpec((None, span, HEAD_DIM), prev_map(vb)),
        ]
    in_specs.append(pl.BlockSpec((None, ATTN_ROWS, HEAD_DIM),
                                 lambda b, h, i: (b, i, SEG_GATE_DIL * N_HEADS + h)))
    args = [z3] * len(in_specs)
    out_shape = [jax.ShapeDtypeStruct((bsz, seq, SEG), BF16)]
    out_specs = [pl.BlockSpec((None, ATTN_ROWS, HEAD_DIM), lambda b, h, i: (b, i, h))]
    aliases = {}
    for g, (win, _) in enumerate(DIL_PAIRS):
        out_shape.append(jax.ShapeDtypeStruct((depth, bsz, win * KV_ROW, HEAD_DIM), F32))
        out_specs.append(pl.BlockSpec((None, None, win * KV_ROW, HEAD_DIM),
                                      lambda b, h, i: (layer, b, 0, 0)))
        if cache_bufs is not None:
            in_specs.append(pl.BlockSpec(memory_space=pl.ANY))
            args.append(cache_bufs[g])
            aliases[len(args) - 1] = 1 + g
    return pl.pallas_call(
        functools.partial(_dilattn_kernel, n_tiles=n_tiles),
        out_shape=out_shape,
        grid=(bsz, N_HEADS, n_tiles),
        in_specs=in_specs,
        out_specs=out_specs,
        scratch_shapes=[
            pltpu.VMEM(((len(DIL_PAIRS) - 1) * ATTN_ROWS, HEAD_DIM), F32),
            pltpu.VMEM(((len(DIL_PAIRS) - 1) * ATTN_ROWS, HEAD_DIM), F32),
        ],
        input_output_aliases=aliases,
        compiler_params=pltpu.CompilerParams(
            dimension_semantics=("arbitrary", "arbitrary", "arbitrary"),
            vmem_limit_bytes=VMEM_LIMIT_BYTES),
        name="dilated_attention",
    )(*args)


def _mix_kernel(u_ref, up_ref, gp_ref, qm_ref, gm_ref, yd_ref, mkv_ref, pw_ref, ps_ref,
                wo_ref, x_ref, *rest, n_shift_in, n_shifts):
    shift_in, rest = rest[:n_shift_in], rest[n_shift_in:]
    o_ref, rest = rest[0], rest[1:]
    shift_out, (ue_ref, t_ref, y_ref) = rest[:n_shifts], rest[n_shifts:]
    i = pl.program_id(1)
    rows = u_ref.shape[0]

    pad, halo = POOL_PAD, POOL_HALO
    n = halo + rows

    @pl.when(i > 0)
    def _():
        ue_ref[pl.ds(pad, halo), :] = up_ref[...]

    @pl.when(i == 0)
    def _():
        ue_ref[pl.ds(pad, halo), :] = jnp.zeros((halo, SEG), F32)

    ue_ref[pl.ds(0, pad), :] = jnp.zeros((pad, SEG), F32)
    t_ref[0, pl.ds(0, pad), :] = jnp.zeros((pad, HEAD_DIM), F32)
    t_ref[1, pl.ds(0, pad), :] = jnp.zeros((pad, HEAD_DIM), F32)
    ue_ref[pl.ds(pad + halo, rows), :] = u_ref[...]
    pos = i * rows + lax.broadcasted_iota(jnp.int32, (rows, 1), 0)
    for gi, w in enumerate(POOL_WINDOWS):
        sl = slice(gi * HEAD_DIM, (gi + 1) * HEAD_DIM)
        cur = ue_ref[pl.ds(pad, n), sl] + ue_ref[pl.ds(pad - 1, n), sl]
        shift, buf = 2, 0
        while shift < w:
            t_ref[buf, pl.ds(pad, n), :] = cur
            cur = cur + t_ref[buf, pl.ds(pad - shift, n), :]
            shift, buf = 2 * shift, 1 - buf
        cnt = jnp.minimum(w, pos + 1).astype(F32)
        dlt = cur[halo:] / cnt - u_ref[:, sl]
        yp = jnp.dot(dlt.astype(BF16), pw_ref[gi], preferred_element_type=F32) * ps_ref[:, sl]
        y_ref[:, sl] = (gp_ref[:, sl] * yp).astype(BF16)

    y_ref[:, SEG:2 * SEG] = yd_ref[...]

    nt_dims = (((1,), (1,)), ((), ()))
    ones = jnp.ones((MEM_LEN, HEAD_DIM), BF16)
    for h in range(N_HEADS):
        sl = slice(h * HEAD_DIM, (h + 1) * HEAD_DIM)
        q = qm_ref[:, sl].astype(BF16)
        k = mkv_ref[:, sl].astype(BF16)
        v = mkv_ref[:, SEG + h * HEAD_DIM:SEG + (h + 1) * HEAD_DIM].astype(BF16)
        s = lax.dot_general(q, k, nt_dims, preferred_element_type=F32)
        m = jnp.max(s, axis=-1, keepdims=True)
        p = jnp.exp(s - m).astype(BF16)
        ov = jnp.dot(p, jnp.concatenate([v, ones], axis=1), preferred_element_type=F32)
        o = ov[:, :HEAD_DIM] / ov[:, HEAD_DIM:]
        y_ref[:, 2 * SEG + h * HEAD_DIM:2 * SEG + (h + 1) * HEAD_DIM] = (gm_ref[:, sl] * o).astype(BF16)

    _run_shifts(shift_in, shift_out)
    o_ref[...] = x_ref[...] + jnp.dot(y_ref[...], wo_ref[...], preferred_element_type=F32)


def _mix(z3, y_dil, mem_kv, pool_w, pool_scale, w_out, x3, layer, shift_jobs):
    bsz, seq, _ = z3.shape
    rows = MIX_ROWS
    n_tiles = seq // rows
    halo_per_tile = rows // POOL_HALO

    def seg_spec(seg):
        return pl.BlockSpec((None, rows, SEG), lambda b, i: (b, i, seg))

    s_specs, s_args, s_shapes, s_out_specs, aliases = _shift_jobs(
        shift_jobs, layer, lambda b, i: b * n_tiles + i, bsz * n_tiles, 11, 1)
    return pl.pallas_call(
        functools.partial(_mix_kernel, n_shift_in=len(s_args), n_shifts=len(s_shapes)),
        out_shape=[jax.ShapeDtypeStruct(x3.shape, F32)] + s_shapes,
        grid=(bsz, n_tiles),
        in_specs=[
            seg_spec(SEG_U),
            pl.BlockSpec((None, POOL_HALO, SEG),
                         lambda b, i: (b, jnp.maximum(i * halo_per_tile - 1, 0), SEG_U)),
            seg_spec(SEG_GATE_POOL),
            seg_spec(SEG_QMEM),
            seg_spec(SEG_GATE_MEM),
            pl.BlockSpec((None, rows, SEG), lambda b, i: (b, i, 0)),
            pl.BlockSpec((None, None, MEM_LEN, 2 * SEG), lambda b, i: (layer, b, 0, 0)),
            pl.BlockSpec((None, len(POOL_WINDOWS), HEAD_DIM, HEAD_DIM), lambda b, i: (layer, 0, 0, 0)),
            pl.BlockSpec((None, 1, SEG), lambda b, i: (layer, 0, 0)),
            pl.BlockSpec((MIX_WIDTH, D_MODEL), lambda b, i: (0, 0), pipeline_mode=pl.Buffered(1)),
            pl.BlockSpec((None, rows, D_MODEL), lambda b, i: (b, i, 0)),
        ] + s_specs,
        out_specs=[pl.BlockSpec((None, rows, D_MODEL), lambda b, i: (b, i, 0))] + s_out_specs,
        scratch_shapes=[
            pltpu.VMEM((POOL_PAD + POOL_HALO + rows, SEG), F32),
            pltpu.VMEM((2, POOL_PAD + POOL_HALO + rows, HEAD_DIM), F32),
            pltpu.VMEM((rows, MIX_WIDTH), BF16),
        ],
        input_output_aliases=aliases,
        compiler_params=pltpu.CompilerParams(
            dimension_semantics=("arbitrary", "arbitrary"),
            vmem_limit_bytes=VMEM_LIMIT_BYTES),
        name="mix",
    )(z3, z3, z3, z3, z3, y_dil, mem_kv, pool_w, pool_scale, w_out, x3, *s_args)


def _sample_tail_kernel(zs3_ref, c0_ref, c1_ref, c2_ref, cm_ref, zs_ref, st_ref, pw_ref, ps_ref,
                        wo_ref, x_ref, o_ref, ns_ref, wob_ref, ydm_ref, y_ref):
    b = pl.program_id(0)

    @pl.when(b == 0)
    def _():
        ydm_ref[...] = jnp.zeros(ydm_ref.shape, F32)

    for n in range(SAMPLE_PER_STEP):
        _sample_attention_row(zs3_ref.at[n], (c0_ref.at[n], c1_ref.at[n], c2_ref.at[n]),
                              cm_ref.at[n], ydm_ref, b * SAMPLE_PER_STEP + n)

    @pl.when(b == pl.num_programs(0) - 1)
    def _():
        _sample_out_rows(zs_ref, st_ref, ydm_ref, pw_ref, ps_ref, wo_ref, x_ref, o_ref, ns_ref,
                         wob_ref, y_ref)


def _sample_attention_row(zs_ref, cache_refs, cm_ref, ydm_ref, b):
    outs, lses = [], []
    for g, c_ref in enumerate(cache_refs):
        q = zs_ref[pl.ds(_seg_q(g) * N_HEADS, N_HEADS), :]
        k_new = zs_ref[pl.ds(_seg_k(g) * N_HEADS, N_HEADS), :]
        v_new = zs_ref[pl.ds((_seg_k(g) + 1) * N_HEADS, N_HEADS), :]
        k = c_ref[:, 0]
        v = c_ref[:, 1]
        s = jnp.sum(k * q[None], axis=-1, keepdims=True)
        s_new = jnp.sum(k_new * q, axis=-1, keepdims=True)
        m = jnp.maximum(jnp.max(s, axis=0), s_new)
        p = jnp.exp2(s - m[None])
        p_new = jnp.exp2(s_new - m)
        l = jnp.sum(p, axis=0) + p_new
        outs.append((jnp.sum(p * v, axis=0) + p_new * v_new) / l)
        lses.append(m + jnp.log2(l))
    mx = jnp.maximum(jnp.maximum(lses[0], lses[1]), lses[2])
    es = [jnp.exp2(x - mx) for x in lses]
    o_dil = (es[0] * outs[0] + es[1] * outs[1] + es[2] * outs[2]) / (es[0] + es[1] + es[2])

    q = zs_ref[pl.ds(SEG_QMEM * N_HEADS, N_HEADS), :]
    k = cm_ref[:, 0]
    v = cm_ref[:, 1]
    s = jnp.sum(k * q[None], axis=-1, keepdims=True)
    m = jnp.max(s, axis=0)
    p = jnp.exp(s - m[None])
    o_mem = jnp.sum(p * v, axis=0) / jnp.sum(p, axis=0)

    rows = ydm_ref.shape[0]
    is_row_b = lax.broadcasted_iota(jnp.int32, (rows, HEAD_DIM), 0) == b
    for h in range(N_HEADS):
        for base, o in ((0, o_dil), (SEG, o_mem)):
            sl = pl.ds(base + h * HEAD_DIM, HEAD_DIM)
            row = jnp.broadcast_to(o[h:h + 1, :], (rows, HEAD_DIM))
            ydm_ref[:, sl] = jnp.where(is_row_b, row, ydm_ref[:, sl])


def _sample_out_rows(zs_ref, st_ref, ydm_ref, pw_ref, ps_ref, wo_ref, x_ref, o_ref, ns_ref,
                     wob_ref, y_ref):
    wob_ref[...] = wo_ref[...].astype(BF16)
    u = zs_ref[:, pl.ds(SEG_U * SEG, SEG)]
    for gi, w in enumerate(POOL_WINDOWS):
        sl = slice(gi * HEAD_DIM, (gi + 1) * HEAD_DIM)
        tot = u[:, sl]
        for back in range(1, w):
            tot = tot + st_ref[POOL_BUF - back, :, sl]
        cnt = float(min(w, PAST_LEN + 1))
        dlt = tot / cnt - u[:, sl]
        yp = jnp.dot(dlt.astype(BF16), pw_ref[gi], preferred_element_type=F32) * ps_ref[:, sl]
        y_ref[:, sl] = (zs_ref[:, pl.ds(SEG_GATE_POOL * SEG + gi * HEAD_DIM, HEAD_DIM)] * yp).astype(BF16)
    y_ref[:, SEG:2 * SEG] = (zs_ref[:, pl.ds(SEG_GATE_DIL * SEG, SEG)] * ydm_ref[:, pl.ds(0, SEG)]).astype(BF16)
    y_ref[:, 2 * SEG:] = (zs_ref[:, pl.ds(SEG_GATE_MEM * SEG, SEG)] * ydm_ref[:, pl.ds(SEG, SEG)]).astype(BF16)
    o_ref[...] = x_ref[...] + jnp.dot(y_ref[...], wob_ref[...], preferred_element_type=F32)
    for r in range(POOL_BUF - 1):
        ns_ref[r] = st_ref[r + 1]
    ns_ref[POOL_BUF - 1] = u


def _sample_tail(zs, caches7, cache_mem, state_t, pool_w, pool_scale, w_out, xs, layer):
    rows = zs.shape[0]
    bsz = cache_mem.shape[1]
    zs3 = zs.reshape(rows, N_SEG * N_HEADS, HEAD_DIM)
    full = lambda shape: pl.BlockSpec(shape, lambda b: tuple(0 for _ in shape))
    of_layer = lambda a: pl.BlockSpec((None,) + a.shape[1:], lambda b: (layer,) + (0,) * (a.ndim - 1))
    per = SAMPLE_PER_STEP
    in_specs = [pl.BlockSpec((per, N_SEG * N_HEADS, HEAD_DIM), lambda b: (b, 0, 0))]
    for _ in caches7:
        in_specs.append(pl.BlockSpec((None, per, BAND, None, 2, N_HEADS, HEAD_DIM),
                                     lambda b: (layer, b, 0, 0, 0, 0, 0)))
    in_specs.append(pl.BlockSpec((None, per, MEM_LEN, 2, N_HEADS, HEAD_DIM),
                                 lambda b: (layer, b, 0, 0, 0, 0)))
    in_specs += [full(zs.shape), of_layer(state_t), of_layer(pool_w), of_layer(pool_scale),
                 of_layer(w_out), full(xs.shape)]
    return pl.pallas_call(
        _sample_tail_kernel,
        out_shape=[jax.ShapeDtypeStruct(xs.shape, F32),
                   jax.ShapeDtypeStruct(state_t.shape[1:], F32),
                   jax.ShapeDtypeStruct(w_out.shape[1:], BF16)],
        grid=(bsz // per,),
        in_specs=in_specs,
        out_specs=[full(xs.shape), full(state_t.shape[1:]), full(w_out.shape[1:])],
        scratch_shapes=[pltpu.VMEM((rows, 2 * SEG), F32), pltpu.VMEM((rows, MIX_WIDTH), BF16)],
        compiler_params=pltpu.CompilerParams(
            dimension_semantics=("arbitrary",), vmem_limit_bytes=VMEM_LIMIT_BYTES),
        name="sample_tail",
    )(zs3, *caches7, cache_mem, zs, state_t, pool_w, pool_scale, w_out, xs)


def _set_last_kernel(*refs):
    n = len(refs) // 3
    for new_ref, o_ref in zip(refs[:n], refs[2 * n:]):
        o_ref[...] = new_ref[...]


def _cache_set_last(bufs, new_rows):
    n = len(bufs)
    depth, bsz = bufs[0].shape[:2]

    def last_row_spec(win):
        return pl.BlockSpec((None, bsz, KV_ROW, HEAD_DIM), lambda l: (l, 0, win - 1, 0))

    return pl.pallas_call(
        _set_last_kernel,
        out_shape=[jax.ShapeDtypeStruct(buf.shape, buf.dtype) for buf in bufs],
        grid=(depth,),
        in_specs=[pl.BlockSpec((None, bsz, KV_ROW, HEAD_DIM), lambda l: (l, 0, 0, 0))] * n
        + [pl.BlockSpec(memory_space=pl.ANY)] * n,
        out_specs=[last_row_spec(buf.shape[2] // KV_ROW) for buf in bufs],
        input_output_aliases={n + g: g for g in range(n)},
        compiler_params=pltpu.CompilerParams(dimension_semantics=("arbitrary",)),
        name="cache_set_last",
    )(*new_rows, *bufs)


def _tile_heads(v):
    return jnp.tile(v, N_HEADS)


def kernel(x_prompt, x_sample, state_pool, cache_dil_w128, cache_dil_w512, cache_dil_w2048,
           cache_mem_kv, mem_prompt, norm_g, w_in, pool_w, pool_scale, dil_q_norm, dil_k_norm,
           mem_norm_g, w_mem_kv, mem_q_norm, mem_k_norm, w_out):
    depth = w_in.shape[0]
    bsz, seq, _ = x_prompt.shape
    dbsz = x_sample.shape[0]
    caches = (cache_dil_w128, cache_dil_w512, cache_dil_w2048)

    z_pairs = tuple(zip(Z_ORDER[0::2], Z_ORDER[1::2]))
    pool_w_b = pool_w.astype(BF16)

    ones = jnp.ones((depth, SEG), F32)
    segs = [ones] * N_SEG
    for g in range(len(DIL_PAIRS)):
        segs[_seg_q(g)] = _tile_heads(dil_q_norm[:, g]) * (ATTN_SCALE * LOG2_E)
        segs[_seg_k(g)] = _tile_heads(dil_k_norm[:, g])
    segs[SEG_QMEM] = _tile_heads(mem_q_norm) * ATTN_SCALE
    ep_gain = jnp.concatenate(segs, axis=1)[:, None, :]
    mem_k_gain = _tile_heads(mem_k_norm)[:, None, :]
    seg_modes = [MODE_GATE if s in GATE_SEGS else MODE_PLAIN for s in range(N_SEG)]
    ep_mode = jnp.repeat(jnp.array(seg_modes, F32), SEG)[None, :]
    gain = norm_g[:, None, :]
    mem_norm = mem_norm_g[:, None, :]
    pscale = pool_scale[:, None, :]

    caches7 = [c.reshape(depth, dbsz, win // d, d, 2, N_HEADS, HEAD_DIM)
               for c, (win, d) in zip(caches, DIL_PAIRS)]
    caches_2d = [c.reshape(depth, dbsz, win * KV_ROW, HEAD_DIM)
                 for c, (win, _) in zip(caches, DIL_PAIRS)]
    state_t = jnp.pad(jnp.transpose(state_pool, (0, 2, 1, 3)),
                      ((0, 0), (0, 0), (0, SAMPLE_ROWS - dbsz), (0, 0)))

    xp = x_prompt.reshape(bsz * seq, D_MODEL)
    xs = jnp.pad(x_sample.reshape(dbsz, D_MODEL), ((0, SAMPLE_ROWS - dbsz), (0, 0)))
    mem2 = mem_prompt.reshape(bsz * MEM_LEN, D_MODEL)
    mem_kv = _memory_kv(mem2, mem_norm, w_mem_kv, mem_k_gain).reshape(depth, bsz, MEM_LEN, 2 * SEG)

    pool_p, pool_s, zs_rows = [], [], []
    prompt_caches = None
    sample_caches = [None] * len(DIL_PAIRS)
    for l in range(depth):
        zs, w_first, w_second = _project(xs, gain, (w_in, l, z_pairs), ep_gain, ep_mode,
                                         gain_layer=l, rows=SAMPLE_ROWS, name="proj_sample",
                                         emit_w=True)
        xs, new_state_t, w_out_b = _sample_tail(zs, caches7, cache_mem_kv, state_t, pool_w_b,
                                                pscale, w_out, xs, l)
        pool_s.append(new_state_t)
        zs_rows.append(zs)

        z, sample_caches[2] = _project(
            xp, gain, (w_first, w_second), ep_gain, ep_mode, gain_layer=l, rows=PROJ_ROWS,
            name="proj_prompt", shift=(l, [(caches_2d[2], sample_caches[2], SHIFT_ROWS)]))
        z3 = z.reshape(bsz, seq, IN_COLS)
        y_dil, *prompt_caches = _dilated_attention(z3, l, depth, prompt_caches)
        xp3, sample_caches[1], sample_caches[0] = _mix(
            z3, y_dil, mem_kv, pool_w_b, pscale, w_out_b, xp.reshape(bsz, seq, D_MODEL), l,
            [(caches_2d[1], sample_caches[1], MIX_SHIFT_ROWS[1]),
             (caches_2d[0], sample_caches[0], MIX_SHIFT_ROWS[0])])
        xp = xp3.reshape(bsz * seq, D_MODEL)
        pool_p.append(z3[:, seq - POOL_BUF:, SEG_U * SEG:(SEG_U + 1) * SEG])

    zs_all = jnp.stack(zs_rows)[:, :dbsz].reshape(depth, dbsz, N_SEG, N_HEADS, HEAD_DIM)
    new_rows = [zs_all[:, :, _seg_k(g):_seg_k(g) + 2].reshape(depth, dbsz, KV_ROW, HEAD_DIM)
                for g in range(len(DIL_PAIRS))]
    new_caches = _cache_set_last(sample_caches, new_rows)
    new_caches = [c.reshape(depth, dbsz, win, 2, N_HEADS, HEAD_DIM)
                  for c, (win, _) in zip(new_caches, DIL_PAIRS)]

    y_prompt = xp.reshape(bsz, seq, D_MODEL)
    y_sample = xs[:dbsz].reshape(dbsz, 1, D_MODEL)
    cache_mem_prompt = mem_kv.reshape(depth, bsz, MEM_LEN, 2, N_HEADS, HEAD_DIM)
    state_pool_sample = jnp.transpose(jnp.stack(pool_s)[:, :, :dbsz], (0, 2, 1, 3))
    prompt_caches = [c.reshape(depth, bsz, win, 2, N_HEADS, HEAD_DIM)
                     for c, (win, _) in zip(prompt_caches, DIL_PAIRS)]
    return (y_prompt, y_sample, jnp.stack(pool_p), prompt_caches[0], prompt_caches[1],
            prompt_caches[2], cache_mem_prompt, state_pool_sample, new_caches[0], new_caches[1],
            new_caches[2])
```
